```python
import math
import jax, jax.numpy as jnp
from jax import lax
import numpy as np

D_MODEL = 1024
BATCH = 8
SEQ = 4096
DEPTH = 2

GRID_W = 64
CTX_LEN = 256
EPS = 1e-6
HEAD_DIM = 64
A_HEADS = 8
A_KV = 2
B_HEADS = 8
B_KV = 2
WINDOW = 128
Q_BLOCK = 128
ROPE_THETA = 10000.0
C_HEADS = 4
C_DK = 128
C_DV = 128
CONV_W = 5
CHUNK = 64
D_HEADS = 4
D_DK = 128
D_DV = 128
RET_DECAY_BASE = 5.0
PEER_HEADS = 8
N_KEYS = 128
N_EXPERTS = N_KEYS * N_KEYS
PEER_QDIM = 256
PEER_HALF = PEER_QDIM // 2
PEER_TOPK = 16
PEER_BLOCK = 128
ATT_SPLITS = (A_HEADS * HEAD_DIM, A_KV * HEAD_DIM, A_KV * HEAD_DIM, B_HEADS * HEAD_DIM, B_KV * HEAD_DIM, B_KV * HEAD_DIM)
ATT_IN = sum(ATT_SPLITS)
REC_SPLITS = (C_HEADS * (2 * C_DK + C_DV), C_HEADS * C_DV, 4 * C_HEADS, D_HEADS * D_DK, D_HEADS * D_DK, D_HEADS * D_DV, 2 * D_HEADS * D_DV)
REC_IN = sum(REC_SPLITS)
MIX_WIDTH = A_HEADS * HEAD_DIM + B_HEADS * HEAD_DIM

kernel_name = "hybrid_dit_gqa_swa_gdn_retnet_peer"

f32 = jnp.float32


def _split_cols(p, sizes):
    return jnp.split(p, [int(s) for s in np.cumsum(sizes)[:-1]], axis=-1)


def _rmsnorm(x, g):
    x32 = x.astype(f32)
    y = x32 * lax.rsqrt(jnp.mean(x32 * x32, axis=-1, keepdims=True) + EPS)
    return (y * g.astype(f32)).astype(x.dtype)


def _l2norm(x):
    x32 = x.astype(f32)
    return (x32 * lax.rsqrt(jnp.sum(x32 * x32, axis=-1, keepdims=True) + EPS)).astype(x.dtype)


def _group_norm(x, g):
    x32 = x.astype(f32)
    mu = jnp.mean(x32, axis=-1, keepdims=True)
    var = jnp.mean(jnp.square(x32 - mu), axis=-1, keepdims=True)
    y = (x32 - mu) * lax.rsqrt(var + EPS)
    return (y * g.astype(f32).reshape(x.shape[-2:])).astype(x.dtype)


def _modulate(h, shift, scale):
    return h * (1 + scale) + shift


def _rope_2d(x, row, col):
    d = x.shape[-1]
    half, quarter = d // 2, d // 4
    freqs = ROPE_THETA ** (-jnp.arange(quarter, dtype=f32) / quarter)

    def rot(xp, pos):
        ang = pos.astype(f32)[:, None] * freqs
        cos, sin = jnp.cos(ang)[None, :, None, :], jnp.sin(ang)[None, :, None, :]
        x1, x2 = xp[..., :quarter].astype(f32), xp[..., quarter:].astype(f32)
        return jnp.concatenate([x1 * cos - x2 * sin, x2 * cos + x1 * sin], axis=-1)

    return jnp.concatenate([rot(x[..., :half], row), rot(x[..., half:], col)], axis=-1).astype(x.dtype)


def _att_project(h, w_in, q_g, k_g):
    Bn, L, _ = h.shape
    qa, ka, va, qb, kb, vb = _split_cols(h @ w_in, ATT_SPLITS)
    qa = _rmsnorm(qa.reshape(Bn, L, A_HEADS, HEAD_DIM), q_g)
    ka = _rmsnorm(ka.reshape(Bn, L, A_KV, HEAD_DIM), k_g)
    va = va.reshape(Bn, L, A_KV, HEAD_DIM)
    qb = qb.reshape(Bn, L, B_HEADS, HEAD_DIM)
    kb = kb.reshape(Bn, L, B_KV, HEAD_DIM)
    vb = vb.reshape(Bn, L, B_KV, HEAD_DIM)
    return qa, ka, va, qb, kb, vb


def _group(q, n_kv):
    Bn, L, H, d = q.shape
    return (q * (d ** -0.5)).reshape(Bn, L, n_kv, H // n_kv, d)


def _gqa_attend(q, keys, vals, masks, sink):
    scores = []
    for k, m in zip(keys, masks):
        s = jnp.einsum("bqkgd,bskd->bkgqs", q, k).astype(f32)
        scores.append(s if m is None else jnp.where(m, s, -jnp.inf))
    sizes = [k.shape[1] for k in keys]
    if sink is not None:
        scores.append(jnp.broadcast_to(sink.astype(f32)[None, :, :, None, None], scores[0].shape[:-1] + (1,)))
    p = jax.nn.softmax(jnp.concatenate(scores, axis=-1), axis=-1).astype(vals[0].dtype)
    parts = _split_cols(p, sizes + [p.shape[-1] - sum(sizes)])
    out = 0
    for pi, v in zip(parts, vals):
        out = out + jnp.einsum("bkgqs,bskd->bqkgd", pi, v)
    return out


def _att_latent(qa, ka, va, qb, kb, vb, ka_c, va_c, kb_c, vb_c, sink):
    Bn, L = qa.shape[:2]
    nb = L // Q_BLOCK
    band = Q_BLOCK + 2 * WINDOW
    kb_pad = jnp.pad(kb, ((0, 0), (WINDOW, WINDOW), (0, 0), (0, 0)))
    vb_pad = jnp.pad(vb, ((0, 0), (WINDOW, WINDOW), (0, 0), (0, 0)))
    q_off = jnp.arange(Q_BLOCK)
    k_off = jnp.arange(band) - WINDOW

    def to_blocks(q):
        return jnp.moveaxis(q.reshape(Bn, nb, Q_BLOCK, *q.shape[2:]), 1, 0)

    def one_block(args):
        j, qa_j, qb_j = args
        start = j * Q_BLOCK
        oa = _gqa_attend(qa_j, [ka, ka_c], [va, va_c], [None, None], None)
        kb_j = lax.dynamic_slice_in_dim(kb_pad, start, band, axis=1)
        vb_j = lax.dynamic_slice_in_dim(vb_pad, start, band, axis=1)
        qpos, kpos = start + q_off, start + k_off
        mask = (jnp.abs(qpos[:, None] - kpos[None, :]) <= WINDOW) & (kpos >= 0)[None, :] & (kpos < L)[None, :]
        ob = _gqa_attend(qb_j, [kb_j, kb_c], [vb_j, vb_c], [mask, None], sink)
        return oa, ob

    oa, ob = lax.map(one_block, (jnp.arange(nb), to_blocks(qa), to_blocks(qb)))

    def from_blocks(o):
        return jnp.moveaxis(o, 0, 1).reshape(Bn, L, -1)

    return jnp.concatenate([from_blocks(oa), from_blocks(ob)], axis=-1)


def _att_mixer(h_lat, h_ctx, w_in, q_g, k_g, sink, w_out, row, col, with_ctx_out):
    Bn, Lc = h_ctx.shape[:2]
    qa, ka, va, qb, kb, vb = _att_project(h_lat, w_in, q_g, k_g)
    qa, ka, qb, kb = (_rope_2d(t, row, col) for t in (qa, ka, qb, kb))
    qa_c, ka_c, va_c, qb_c, kb_c, vb_c = _att_project(h_ctx, w_in, q_g, k_g)
    sink_g = sink.reshape(B_KV, B_HEADS // B_KV)
    y_lat = _att_latent(_group(qa, A_KV), ka, va, _group(qb, B_KV), kb, vb, ka_c, va_c, kb_c, vb_c, sink_g) @ w_out
    y_ctx = None
    if with_ctx_out:
        oa = _gqa_attend(_group(qa_c, A_KV), [ka_c], [va_c], [None], None)
        ob = _gqa_attend(_group(qb_c, B_KV), [kb_c], [vb_c], [None], sink_g)
        y_ctx = jnp.concatenate([oa.reshape(Bn, Lc, -1), ob.reshape(Bn, Lc, -1)], axis=-1) @ w_out
    return y_lat, y_ctx


def _short_conv(x, w):
    ch = x.shape[-1]
    return lax.conv_general_dilated(x, w[:, None, :], window_strides=(1,), padding=[(CONV_W // 2, CONV_W // 2)],
                                    dimension_numbers=("NWC", "WIO", "NWC"), feature_group_count=ch)


def _to_chunks(t):
    Bn, L, H = t.shape[:3]
    t = t.astype(f32).reshape(Bn, L // CHUNK, CHUNK, H, *t.shape[3:])
    return jnp.moveaxis(jnp.moveaxis(t, 3, 2), 1, 0)


def _from_chunks(o):
    n, Bn, H, C, d = o.shape
    return jnp.moveaxis(jnp.moveaxis(o, 0, 1), 2, 3).reshape(Bn, n * C, H, d)


def _gated_delta_chunked(q, k, v, beta, log_alpha, s0):
    dv = v.shape[-1]
    qc, kc, vc, bc, ac = (_to_chunks(t) for t in (q, k, v, beta, log_alpha))
    g = jnp.cumsum(ac, axis=-1)
    tri = jnp.tril(jnp.ones((CHUNK, CHUNK), bool))
    tri_strict = jnp.tril(jnp.ones((CHUNK, CHUNK), bool), -1)
    diff = g[..., :, None] - g[..., None, :]
    dec = jnp.exp(jnp.where(tri, diff, -jnp.inf))
    dec_strict = jnp.exp(jnp.where(tri_strict, diff, -jnp.inf))
    kk = jnp.einsum("nbhcd,nbhsd->nbhcs", kc, kc)
    a_mat = jnp.eye(CHUNK, dtype=f32) + bc[..., None] * kk * dec_strict
    rhs = jnp.concatenate([bc[..., None] * vc, (bc * jnp.exp(g))[..., None] * kc], axis=-1)
    sol = lax.linalg.triangular_solve(a_mat, rhs, left_side=True, lower=True)
    u, wk = sol[..., :dv], sol[..., dv:]
    qk = jnp.einsum("nbhcd,nbhsd->nbhcs", qc, kc) * dec
    g_last = g[..., -1:]
    k_end = kc * jnp.exp(g_last - g)[..., None]

    def step(s, xs):
        qj, uj, wkj, gj, qkj, kej, glj = xs
        w = uj - jnp.einsum("bhcd,bhde->bhce", wkj, s)
        o = jnp.exp(gj)[..., None] * jnp.einsum("bhcd,bhde->bhce", qj, s) + jnp.einsum("bhcs,bhse->bhce", qkj, w)
        s = jnp.exp(glj)[..., None] * s + jnp.einsum("bhcd,bhce->bhde", kej, w)
        return s, o

    s, o = lax.scan(step, s0.astype(f32), (qc, u, wk, g, qk, k_end, g_last))
    return _from_chunks(o).astype(v.dtype), s


def _retention_chunked(q, k, v, log_gamma, s0):
    qc, kc, vc = _to_chunks(q), _to_chunks(k), _to_chunks(v)
    lg = log_gamma.astype(f32)
    pos = jnp.arange(CHUNK, dtype=f32)
    tri = jnp.tril(jnp.ones((CHUNK, CHUNK), bool))
    dmat = jnp.exp(jnp.where(tri, lg[:, None, None] * (pos[:, None] - pos[None, :]), -jnp.inf))
    inner = jnp.einsum("nbhcs,nbhse->nbhce", jnp.einsum("nbhcd,nbhsd->nbhcs", qc, kc) * dmat, vc)
    q_dec = jnp.exp(lg[:, None] * (pos + 1.0))
    k_dec = jnp.exp(lg[:, None] * (CHUNK - 1.0 - pos))
    chunk_dec = jnp.exp(lg * CHUNK)
    kd = kc * k_dec[..., None]

    def step(s, xs):
        qj, kj, vj = xs
        o = jnp.einsum("bhcd,bhde->bhce", qj, s) * q_dec[..., None]
        s = chunk_dec[:, None, None] * s + jnp.einsum("bhcd,bhce->bhde", kj, vj)
        return s, o

    s, cross = lax.scan(step, s0.astype(f32), (qc, kd, vc))
    return _from_chunks(inner + cross).astype(v.dtype), s


def _directional(scan_fn, args, s0, reverse):
    if reverse:
        args = tuple(jnp.flip(a, axis=1) for a in args)
    o, s = scan_fn(*args, s0)
    return (jnp.flip(o, axis=1) if reverse else o), s


def _rec_features(h, w_in, conv_w, a_log, dt_bias, row, col):
    Bn, L, _ = h.shape
    qkv, z, gates, qd, kd, vd, gd = _split_cols(h @ w_in, REC_SPLITS)
    qkv = jax.nn.silu(_short_conv(qkv, conv_w))
    qc, kc, vc = _split_cols(qkv, (C_HEADS * C_DK, C_HEADS * C_DK, C_HEADS * C_DV))
    qc = _l2norm(qc.reshape(Bn, L, C_HEADS, C_DK)) * (C_DK ** -0.5)
    kc = _l2norm(kc.reshape(Bn, L, C_HEADS, C_DK))
    vc = vc.reshape(Bn, L, C_HEADS, C_DV)
    gates = gates.reshape(Bn, L, 2, 2, C_HEADS).astype(f32)
    log_alpha = -jnp.exp(a_log.astype(f32)) * jax.nn.softplus(gates[:, :, 0] + dt_bias.astype(f32))
    beta = jax.nn.sigmoid(gates[:, :, 1])
    qd = qd.reshape(Bn, L, D_HEADS, D_DK)
    kd = kd.reshape(Bn, L, D_HEADS, D_DK)
    if row is not None:
        qd, kd = _rope_2d(qd, row, col), _rope_2d(kd, row, col)
    kd = kd * (D_DK ** -0.5)
    vd = vd.reshape(Bn, L, D_HEADS, D_DV)
    gd = gd.reshape(Bn, L, 2, D_HEADS * D_DV)
    return qc, kc, vc, z, beta, log_alpha, qd, kd, vd, gd


def _rec_merge(delta_dirs, z, ret_dirs, gd, out_g, gn_g, w_out):
    Bn, L = z.shape[:2]
    yc = _rmsnorm(delta_dirs[0] + delta_dirs[1], out_g).reshape(Bn, L, -1) * jax.nn.silu(z)
    yd = 0
    for d in range(2):
        yd = yd + _group_norm(ret_dirs[d], gn_g).reshape(Bn, L, -1) * jax.nn.silu(gd[:, :, d])
    return jnp.concatenate([yc, yd], axis=-1) @ w_out


def _rec_mixer(h_lat, h_ctx, w_in, conv_w, a_log, dt_bias, out_g, gn_g, w_out, row, col, with_ctx_out):
    Bn = h_lat.shape[0]
    qc, kc, vc, z, beta, la, qd, kd, vd, gd = _rec_features(h_lat, w_in, conv_w, a_log, dt_bias, row, col)
    qc_c, kc_c, vc_c, z_c, beta_c, la_c, qd_c, kd_c, vd_c, gd_c = _rec_features(h_ctx, w_in, conv_w, a_log, dt_bias, None, None)
    log_gamma = jnp.log1p(-jnp.exp2(-(RET_DECAY_BASE + jnp.arange(D_HEADS, dtype=f32))))
    retention = lambda q, k, v, s0: _retention_chunked(q, k, v, log_gamma, s0)
    zero_c = jnp.zeros((Bn, C_HEADS, C_DK, C_DV), f32)
    zero_d = jnp.zeros((Bn, D_HEADS, D_DK, D_DV), f32)
    delta_lat, delta_ctx, ret_lat, ret_ctx = [], [], [], []
    for d, rev in ((0, False), (1, True)):
        o, s = _directional(_gated_delta_chunked, (qc_c, kc_c, vc_c, beta_c[:, :, d], la_c[:, :, d]), zero_c, rev)
        delta_ctx.append(o)
        delta_lat.append(_directional(_gated_delta_chunked, (qc, kc, vc, beta[:, :, d], la[:, :, d]), s, rev)[0])
        o, s = _directional(retention, (qd_c, kd_c, vd_c), zero_d, rev)
        ret_ctx.append(o)
        ret_lat.append(_directional(retention, (qd, kd, vd), s, rev)[0])
    y_lat = _rec_merge(delta_lat, z, ret_lat, gd, out_g, gn_g, w_out)
    y_ctx = _rec_merge(delta_ctx, z_c, ret_ctx, gd_c, out_g, gn_g, w_out) if with_ctx_out else None
    return y_lat, y_ctx


def _peer(h, w_q, sub_keys, u, v):
    shp = h.shape
    tokens = h.reshape(-1, PEER_BLOCK, shp[-1])

    def block(xb):
        T = xb.shape[0]
        q = (xb @ w_q).reshape(T, PEER_HEADS, 2, PEER_HALF)
        s = jnp.einsum("thpd,pkd->thpk", q, sub_keys).astype(f32)
        sv, si = lax.top_k(s, PEER_TOPK)
        cand_s = (sv[:, :, 0, :, None] + sv[:, :, 1, None, :]).reshape(T, PEER_HEADS, -1)
        cand_i = (si[:, :, 0, :, None] * N_KEYS + si[:, :, 1, None, :]).reshape(T, PEER_HEADS, -1)
        top_s, pos = lax.top_k(cand_s, PEER_TOPK)
        idx = jnp.take_along_axis(cand_i, pos, axis=-1)
        gate = jax.nn.softmax(top_s, axis=-1)
        act = jax.nn.gelu(jnp.einsum("td,thkd->thk", xb, u[idx]).astype(f32), approximate=False)
        return jnp.einsum("thk,thkd->td", (gate * act).astype(xb.dtype), v[idx])

    return lax.map(block, tokens).reshape(shp)


def setup_inputs(seed: int = 0) -> dict:
    key = jax.random.key(seed)
    k = jax.random.split(key, 26)
    D = D_MODEL
    n_even, n_odd = (DEPTH + 1) // 2, DEPTH // 2

    def nrm(i, shape, std):
        return jax.random.normal(k[i], shape, f32) * std

    dt = jnp.exp(jax.random.uniform(k[16], (n_odd, 2, C_HEADS), f32, math.log(1e-3), math.log(1e-1)))
    return {
        "x": nrm(0, (BATCH, SEQ, D), 1.0),
        "c": nrm(1, (BATCH, D), 1.0),
        "ctx": nrm(2, (BATCH, CTX_LEN, D), 1.0),
        "c_ctx": nrm(3, (D,), 1.0),
        "mod_w": nrm(4, (DEPTH, D, 6 * D), 0.5 * D ** -0.5),
        "mod_b": nrm(5, (DEPTH, 6 * D), 0.02),
        "norm1_g": 1.0 + nrm(6, (DEPTH, D), 0.02),
        "norm2_g": 1.0 + nrm(7, (DEPTH, D), 0.02),
        "att_w_in": nrm(8, (n_even, D, ATT_IN), D ** -0.5),
        "att_q_norm": 1.0 + nrm(9, (n_even, HEAD_DIM), 0.02),
        "att_k_norm": 1.0 + nrm(10, (n_even, HEAD_DIM), 0.02),
        "att_sink": nrm(11, (n_even, B_HEADS), 0.5),
        "att_w_out": nrm(12, (n_even, MIX_WIDTH, D), MIX_WIDTH ** -0.5),
        "rec_w_in": nrm(13, (n_odd, D, REC_IN), D ** -0.5),
        "rec_conv_w": nrm(14, (n_odd, CONV_W, C_HEADS * (2 * C_DK + C_DV)), CONV_W ** -0.5),
        "rec_a_log": jnp.log(jax.random.uniform(k[15], (n_odd, 2, C_HEADS), f32, 1.0, 16.0)),
        "rec_dt_bias": dt + jnp.log(-jnp.expm1(-dt)),
        "rec_out_norm": 1.0 + nrm(17, (n_odd, C_DV), 0.02),
        "rec_gn_g": 1.0 + nrm(18, (n_odd, D_HEADS * D_DV), 0.02),
        "rec_w_out": nrm(19, (n_odd, MIX_WIDTH, D), MIX_WIDTH ** -0.5),
        "peer_w_q": nrm(20, (DEPTH, D, PEER_HEADS * PEER_QDIM), D ** -0.5),
        "peer_sub_keys": nrm(21, (DEPTH, 2, N_KEYS, PEER_HALF), PEER_HALF ** -0.5),
        "peer_u": nrm(22, (DEPTH, N_EXPERTS, D), D ** -0.5),
        "peer_v": nrm(23, (DEPTH, N_EXPERTS, D), 0.5),
        "final_norm_g": 1.0 + nrm(24, (D,), 0.02),
    }


def reference(x, c, ctx, c_ctx, mod_w, mod_b, norm1_g, norm2_g, att_w_in, att_q_norm, att_k_norm, att_sink, att_w_out,
              rec_w_in, rec_conv_w, rec_a_log, rec_dt_bias, rec_out_norm, rec_gn_g, rec_w_out,
              peer_w_q, peer_sub_keys, peer_u, peer_v, final_norm_g):
    rows = x.shape[1] // GRID_W
    row = jnp.repeat(jnp.arange(rows, dtype=jnp.int32), GRID_W)
    col = jnp.tile(jnp.arange(GRID_W, dtype=jnp.int32), rows)
    silu_c, silu_cc = jax.nn.silu(c), jax.nn.silu(c_ctx)
    h, hc = x, ctx
    for layer in range(DEPTH):
        last = layer == DEPTH - 1
        mod = jnp.split((silu_c @ mod_w[layer] + mod_b[layer])[:, None, :], 6, axis=-1)
        mod_c = jnp.split((silu_cc @ mod_w[layer] + mod_b[layer])[None, None, :], 6, axis=-1)
        a = _modulate(_rmsnorm(h, norm1_g[layer]), mod[0], mod[1])
        ac = _modulate(_rmsnorm(hc, norm1_g[layer]), mod_c[0], mod_c[1])
        i = layer // 2
        if layer % 2 == 0:
            y, yc = _att_mixer(a, ac, att_w_in[i], att_q_norm[i], att_k_norm[i], att_sink[i], att_w_out[i],
                               row, col, not last)
        else:
            y, yc = _rec_mixer(a, ac, rec_w_in[i], rec_conv_w[i], rec_a_log[i], rec_dt_bias[i], rec_out_norm[i],
                               rec_gn_g[i], rec_w_out[i], row, col, not last)
        h = h + mod[2] * y
        h = h + mod[5] * _peer(_modulate(_rmsnorm(h, norm2_g[layer]), mod[3], mod[4]),
                               peer_w_q[layer], peer_sub_keys[layer], peer_u[layer], peer_v[layer])
        if not last:
            hc = hc + mod_c[2] * yc
            hc = hc + mod_c[5] * _peer(_modulate(_rmsnorm(hc, norm2_g[layer]), mod_c[3], mod_c[4]),
                                       peer_w_q[layer], peer_sub_keys[layer], peer_u[layer], peer_v[layer])
    return _rmsnorm(h, final_norm_g)
```

```python
import functools
import math

import numpy as np
import jax
import jax.numpy as jnp
from jax import lax
from jax.experimental import pallas as pl
from jax.experimental.pallas import tpu as pltpu

F32 = jnp.float32
BF16 = jnp.bfloat16

GRID_W = 64
EPS = 1e-6
HEAD_DIM = 64
A_HEADS = 8
A_KV = 2
B_HEADS = 8
B_KV = 2
WINDOW = 128
ROPE_THETA = 10000.0
C_HEADS = 4
C_DK = 128
CONV_W = 5
CHUNK = 64
D_HEADS = 4
D_DK = 128
RET_DECAY_BASE = 5.0
PEER_HEADS = 8
N_KEYS = 128
PEER_TOPK = 16

LANES = 128
VMEM_LIMIT = 56 * 1024 * 1024

ROW_TILE = 256
ATT_TQ = 128
PEER_TM = 256
PEER_EB = 1024

_NEG_INF = float("-inf")


def _cparams(sem):
    return pltpu.CompilerParams(dimension_semantics=sem, vmem_limit_bytes=VMEM_LIMIT)


def _bdot(a, b):
    return jnp.dot(a.astype(BF16), b.astype(BF16), preferred_element_type=F32)


def _bdot_nt(a, b):
    return lax.dot_general(a.astype(BF16), b.astype(BF16), (((1,), (1,)), ((), ())), preferred_element_type=F32)


def _bdot_tn(a, b):
    return jnp.dot(a.T.astype(BF16), b.astype(BF16), preferred_element_type=F32)


def _split3(a):
    hi = a.astype(BF16)
    r1 = a - hi.astype(F32)
    mid = r1.astype(BF16)
    lo = (r1 - mid.astype(F32)).astype(BF16)
    return hi, mid, lo


def _dot3(a, b):
    a_hi, a_lo, _ = _split3(a)
    b_hi, b_lo, _ = _split3(b)
    d = functools.partial(jnp.dot, preferred_element_type=F32)
    return d(a_hi, b_hi) + (d(a_hi, b_lo) + d(a_lo, b_hi))


def _sigmoid(x):
    return 1.0 / (1.0 + jnp.exp(-x))


def _silu(x):
    return x * _sigmoid(x)


def _norm_mod(x, g, shift, scale):
    r = lax.rsqrt(jnp.mean(x * x, axis=-1, keepdims=True) + EPS)
    return (x * r * g) * (1.0 + scale) + shift


def _rope(x, cos, sin_signed, head_dim):
    quarter = head_dim // 4
    lane = lax.broadcasted_iota(jnp.int32, x.shape, 1)
    first = (lane % (2 * quarter)) < quarter
    partner = jnp.where(first, pltpu.roll(x, LANES - quarter, 1), pltpu.roll(x, quarter, 1))
    return x * cos + partner * sin_signed


def _mod_kernel(c_ref, w_ref, b_ref, o_ref):
    o_ref[0] = _bdot(_silu(c_ref[...]), w_ref[0]) + b_ref[0]


def _modulation(c, c_ctx, mod_w, mod_b):
    depth, d, n = mod_w.shape
    bsz = c.shape[0]
    rows = 16
    cc = jnp.zeros((rows, d), F32).at[:bsz].set(c).at[bsz].set(c_ctx)
    tn = 1536
    out = pl.pallas_call(
        _mod_kernel,
        grid=(depth, n // tn),
        in_specs=[pl.BlockSpec((rows, d), lambda l, j: (0, 0)),
                  pl.BlockSpec((1, d, tn), lambda l, j: (l, 0, j)),
                  pl.BlockSpec((1, 1, tn), lambda l, j: (l, 0, j))],
        out_specs=pl.BlockSpec((1, rows, tn), lambda l, j: (l, 0, j)),
        out_shape=jax.ShapeDtypeStruct((depth, rows, n), F32),
        compiler_params=_cparams(("arbitrary", "arbitrary")),
        name="modulation",
    )(cc, mod_w, mod_b.reshape(depth, 1, n))
    lat = out[:, :bsz].reshape(depth, bsz, 1, 6, d)
    ctx = jnp.broadcast_to(out[:, bsz].reshape(depth, 1, 1, 6, d), (depth, bsz, 1, 6, d))
    return jnp.concatenate([lat, ctx], axis=2)


def _rope_tables(L, LC, head_dim):
    quarter, half = head_dim // 4, head_dim // 2
    freqs = ROPE_THETA ** (-jnp.arange(quarter, dtype=F32) / quarter)
    lane = np.arange(LANES)
    within = lane % head_dim
    use_col = within >= half
    fidx = within % quarter
    sign = np.where((within % half) < quarter, -1.0, 1.0).astype(np.float32)
    t = jnp.arange(L, dtype=jnp.int32)
    row, col = (t // GRID_W).astype(F32), (t % GRID_W).astype(F32)
    pos = jnp.where(use_col[None, :], col[:, None], row[:, None])
    ang = pos * freqs[fidx][None, :]
    cos = jnp.concatenate([jnp.cos(ang), jnp.ones((LC, LANES), F32)], axis=0)
    sin = jnp.concatenate([jnp.sin(ang) * sign[None, :], jnp.zeros((LC, LANES), F32)], axis=0)
    return cos, sin


def _att_proj_kernel(h_ref, mod_ref, g_ref, w_ref, qg_ref, kg_ref, cos_ref, sin_ref, gm_ref,
                     qa_ref, ka_ref, va_ref, qb_ref, kb_ref, vb_ref):
    m = mod_ref[0, 0]
    a = _norm_mod(h_ref[0], g_ref[...], m[0:1], m[1:2])
    o = _bdot(a, w_ref[...])
    cos, sin = cos_ref[...], sin_ref[...]
    gm = gm_ref[...]

    def head_norm(x, gain):
        sq = x * x
        hi = sq.astype(BF16)
        lo = (sq - hi.astype(F32)).astype(BF16)
        ms = jnp.dot(hi, gm, preferred_element_type=F32) + jnp.dot(lo, gm, preferred_element_type=F32)
        return x * lax.rsqrt(ms + EPS) * gain

    def put(ref, first_head, x):
        ref[0, first_head] = x[:, :HEAD_DIM].astype(ref.dtype)
        ref[0, first_head + 1] = x[:, HEAD_DIM:].astype(ref.dtype)

    scale = HEAD_DIM ** -0.5
    qa_w = A_HEADS * HEAD_DIM
    kv_w = A_KV * HEAD_DIM
    off = 0
    for c in range(qa_w // LANES):
        x = o[:, off + c * LANES: off + (c + 1) * LANES]
        put(qa_ref, 2 * c, _rope(head_norm(x, qg_ref[...]), cos, sin, HEAD_DIM) * scale)
    off += qa_w
    put(ka_ref, 0, _rope(head_norm(o[:, off: off + kv_w], kg_ref[...]), cos, sin, HEAD_DIM))
    off += kv_w
    put(va_ref, 0, o[:, off: off + kv_w])
    off += kv_w
    for c in range(qa_w // LANES):
        x = o[:, off + c * LANES: off + (c + 1) * LANES]
        put(qb_ref, 2 * c, _rope(x, cos, sin, HEAD_DIM) * scale)
    off += qa_w
    put(kb_ref, 0, _rope(o[:, off: off + kv_w], cos, sin, HEAD_DIM))
    off += kv_w
    put(vb_ref, 0, o[:, off: off + kv_w])


def _att_project(hh, mod, norm_g, w_in, q_g, k_g, cos, sin, L):
    bsz, lt, d = hh.shape
    tm = ROW_TILE
    n = w_in.shape[1]
    gm = jnp.asarray(np.kron(np.eye(LANES // HEAD_DIM), np.full((HEAD_DIM, HEAD_DIM), 1.0 / HEAD_DIM)), BF16)
    tile2 = lambda v: jnp.tile(v.reshape(1, HEAD_DIM), (1, LANES // HEAD_DIM))
    qshape = jax.ShapeDtypeStruct((bsz, A_HEADS, lt, HEAD_DIM), BF16)
    kshape = jax.ShapeDtypeStruct((bsz, A_KV, lt, HEAD_DIM), BF16)
    qspec = pl.BlockSpec((1, A_HEADS, tm, HEAD_DIM), lambda b, i: (b, 0, i, 0))
    kspec = pl.BlockSpec((1, A_KV, tm, HEAD_DIM), lambda b, i: (b, 0, i, 0))
    const = lambda shape: pl.BlockSpec(shape, lambda b, i: (0,) * len(shape))
    return pl.pallas_call(
        _att_proj_kernel,
        grid=(bsz, lt // tm),
        in_specs=[pl.BlockSpec((1, tm, d), lambda b, i: (b, i, 0)),
                  pl.BlockSpec((1, 1, 6, d), lambda b, i: (b, i // (L // tm), 0, 0)),
                  const((1, d)), const((d, n)), const((1, LANES)), const((1, LANES)),
                  pl.BlockSpec((tm, LANES), lambda b, i: (i, 0)),
                  pl.BlockSpec((tm, LANES), lambda b, i: (i, 0)),
                  const((LANES, LANES))],
        out_specs=[qspec, kspec, kspec, qspec, kspec, kspec],
        out_shape=[qshape, kshape, kshape, qshape, kshape, kshape],
        compiler_params=_cparams(("parallel", "parallel")),
        name="att_project",
    )(hh, mod, norm_g.reshape(1, d), w_in.astype(BF16), tile2(q_g), tile2(k_g), cos, sin, gm)


def _softmax_pv(score_parts, value_parts, extra_logit=None):
    m = functools.reduce(jnp.maximum, [jnp.max(s, axis=-1, keepdims=True) for s in score_parts])
    if extra_logit is not None:
        m = jnp.maximum(m, extra_logit)
    l = 0.0
    acc = 0.0
    for s, v in zip(score_parts, value_parts):
        p = jnp.exp(s - m)
        l = l + jnp.sum(p, axis=-1, keepdims=True)
        acc = acc + jnp.dot(p.astype(BF16), v, preferred_element_type=F32)
    if extra_logit is not None:
        l = l + jnp.exp(extra_logit - m)
    return acc / l


def _att_kernel(sink_ref, qa_ref, ka_ref, va_ref, qb_ref, kb_ref, vb_ref, oa_ref, ob_ref, *, L, LC):
    i = pl.program_id(1)
    tq = ATT_TQ
    group = A_HEADS // A_KV
    band = tq + 2 * WINDOW

    def stacked_q(ref, kvh):
        return ref[0, kvh * group:(kvh + 1) * group].reshape(group * tq, HEAD_DIM)

    def put(ref, kvh, o):
        for g in range(group):
            hd = kvh * group + g
            ref[0, :, hd * HEAD_DIM:(hd + 1) * HEAD_DIM] = o[g * tq:(g + 1) * tq].astype(ref.dtype)

    def sink_col(kvh):
        return jnp.concatenate([jnp.full((tq, 1), sink_ref[kvh * group + g], F32) for g in range(group)], axis=0)

    @pl.when(i < L // tq)
    def _latent():
        for kvh in range(A_KV):
            q = stacked_q(qa_ref, kvh)
            put(oa_ref, kvh, _softmax_pv([_bdot_nt(q, ka_ref[0, kvh])], [va_ref[0, kvh]]))
            q = stacked_q(qb_ref, kvh)
            start = pl.multiple_of(jnp.clip((i - 1) * tq, 0, L - band), tq)
            s_band = _bdot_nt(q, kb_ref[0, kvh, pl.ds(start, band), :])
            qpos = i * tq + (lax.broadcasted_iota(jnp.int32, s_band.shape, 0) % tq)
            kpos = start + lax.broadcasted_iota(jnp.int32, s_band.shape, 1)
            s_band = jnp.where(jnp.abs(qpos - kpos) <= WINDOW, s_band, _NEG_INF)
            s_ctx = _bdot_nt(q, kb_ref[0, kvh, L:L + LC, :])
            put(ob_ref, kvh, _softmax_pv([s_band, s_ctx],
                                         [vb_ref[0, kvh, pl.ds(start, band), :], vb_ref[0, kvh, L:L + LC, :]],
                                         sink_col(kvh)))

    @pl.when(i >= L // tq)
    def _context():
        for kvh in range(A_KV):
            q = stacked_q(qa_ref, kvh)
            put(oa_ref, kvh, _softmax_pv([_bdot_nt(q, ka_ref[0, kvh, L:L + LC, :])], [va_ref[0, kvh, L:L + LC, :]]))
            q = stacked_q(qb_ref, kvh)
            put(ob_ref, kvh, _softmax_pv([_bdot_nt(q, kb_ref[0, kvh, L:L + LC, :])], [vb_ref[0, kvh, L:L + LC, :]],
                                         sink_col(kvh)))


def _attention(qa, ka, va, qb, kb, vb, sink, L):
    bsz, _, lt, _ = qa.shape
    tq = ATT_TQ
    qspec = pl.BlockSpec((1, A_HEADS, tq, HEAD_DIM), lambda b, i: (b, 0, i, 0))
    kspec = pl.BlockSpec((1, A_KV, lt, HEAD_DIM), lambda b, i: (b, 0, 0, 0))
    ospec = pl.BlockSpec((1, tq, A_HEADS * HEAD_DIM), lambda b, i: (b, i, 0))
    oshape = jax.ShapeDtypeStruct((bsz, lt, A_HEADS * HEAD_DIM), BF16)
    return pl.pallas_call(
        functools.partial(_att_kernel, L=L, LC=lt - L),
        grid=(bsz, lt // tq),
        in_specs=[pl.BlockSpec(memory_space=pltpu.SMEM), qspec, kspec, kspec, qspec, kspec, kspec],
        out_specs=[ospec, ospec],
        out_shape=[oshape, oshape],
        compiler_params=_cparams(("parallel", "parallel")),
        name="attention",
    )(sink, qa, ka, va, qb, kb, vb)


def _att_out_kernel(oa_ref, ob_ref, w_ref, h_ref, mod_ref, out_ref):
    half = oa_ref.shape[-1]
    y = (jnp.dot(oa_ref[0], w_ref[:half, :], preferred_element_type=F32)
         + jnp.dot(ob_ref[0], w_ref[half:, :], preferred_element_type=F32))
    out_ref[0] = h_ref[0] + mod_ref[0, 0][2:3] * y


def _att_out(oa, ob, w_out, hh, mod, L):
    bsz, lt, d = hh.shape
    tm = ROW_TILE
    half = oa.shape[-1]
    return pl.pallas_call(
        _att_out_kernel,
        grid=(bsz, lt // tm),
        in_specs=[pl.BlockSpec((1, tm, half), lambda b, i: (b, i, 0)),
                  pl.BlockSpec((1, tm, half), lambda b, i: (b, i, 0)),
                  pl.BlockSpec((2 * half, d), lambda b, i: (0, 0)),
                  pl.BlockSpec((1, tm, d), lambda b, i: (b, i, 0)),
                  pl.BlockSpec((1, 1, 6, d), lambda b, i: (b, i // (L // tm), 0, 0))],
        out_specs=pl.BlockSpec((1, tm, d), lambda b, i: (b, i, 0)),
        out_shape=jax.ShapeDtypeStruct(hh.shape, F32),
        input_output_aliases={3: 0},
        compiler_params=_cparams(("parallel", "parallel")),
        name="att_out",
    )(oa, ob, w_out.astype(BF16), hh, mod)


def _top_values(x, n):
    vals = []
    for _ in range(n):
        m = jnp.max(x, axis=0, keepdims=True)
        vals.append(m)
        x = jnp.where(x == m, _NEG_INF, x)
    return vals


def _peer_route_kernel(h_ref, mod_ref, g_ref, wq_ref, keys_ref, a_ref, s0_ref, e0_ref, s1_ref, e1_ref, tau_ref, q_scr):
    m = mod_ref[0, 0]
    a = _norm_mod(h_ref[0], g_ref[...], m[3:4], m[4:5]).astype(BF16)
    a_ref[0] = a
    q_scr[...] = jnp.dot(a, wq_ref[...], preferred_element_type=F32)
    k = PEER_TOPK
    n_second = [min(k, -(-(k // (r + 1)) // 8) * 8) for r in range(k)]

    def head(hd, carry):
        col = pl.multiple_of(hd * 2 * N_KEYS, 2 * N_KEYS)
        s0 = _bdot_nt(keys_ref[0], q_scr[:, pl.ds(col, N_KEYS)])
        s1 = _bdot_nt(keys_ref[1], q_scr[:, pl.ds(col + N_KEYS, N_KEYS)])
        top0 = _top_values(s0, k)
        top1 = jnp.concatenate(_top_values(s1, k), axis=0)
        cand = jnp.concatenate([top0[r] + top1[:n_second[r]] for r in range(k)], axis=0)
        best = _top_values(cand, k)
        z = functools.reduce(lambda x, y: x + y, [jnp.exp(b - best[0]) for b in best])
        s0_ref[0, hd] = s0
        e0_ref[0, hd] = jnp.exp(s0 - top0[0])
        s1_ref[0, hd] = s1
        e1_ref[0, hd] = jnp.exp(s1 - top1[0:1]) / z
        tau_ref[0, hd] = jnp.broadcast_to(best[k - 1], (8, best[0].shape[1]))
        return carry

    lax.fori_loop(0, PEER_HEADS, head, 0)


def _peer_route(hh, mod, norm_g, w_q, sub_keys, L):
    bsz, lt, d = hh.shape
    tm = ROW_TILE
    nq = w_q.shape[1]
    tok = lambda: pl.BlockSpec((1, PEER_HEADS, N_KEYS, tm), lambda b, i: (b, 0, 0, i))
    tshape = jax.ShapeDtypeStruct((bsz, PEER_HEADS, N_KEYS, lt), F32)
    return pl.pallas_call(
        _peer_route_kernel,
        grid=(bsz, lt // tm),
        in_specs=[pl.BlockSpec((1, tm, d), lambda b, i: (b, i, 0)),
                  pl.BlockSpec((1, 1, 6, d), lambda b, i: (b, i // (L // tm), 0, 0)),
                  pl.BlockSpec((1, d), lambda b, i: (0, 0)),
                  pl.BlockSpec((d, nq), lambda b, i: (0, 0)),
                  pl.BlockSpec((2, N_KEYS, N_KEYS), lambda b, i: (0, 0, 0))],
        out_specs=[pl.BlockSpec((1, tm, d), lambda b, i: (b, i, 0)), tok(), tok(), tok(), tok(),
                   pl.BlockSpec((1, PEER_HEADS, 8, tm), lambda b, i: (b, 0, 0, i))],
        out_shape=[jax.ShapeDtypeStruct((bsz, lt, d), BF16), tshape, tshape, tshape, tshape,
                   jax.ShapeDtypeStruct((bsz, PEER_HEADS, 8, lt), F32)],
        scratch_shapes=[pltpu.VMEM((tm, nq), F32)],
        compiler_params=_cparams(("parallel", "parallel")),
        name="peer_route",
    )(hh, mod, norm_g.reshape(1, d), w_q.astype(BF16), sub_keys.astype(BF16))


def _gelu(x):
    return 0.5 * x * (1.0 + lax.erf(x * (1.0 / math.sqrt(2.0))))


def _peer_expert_kernel(a_ref, s0_ref, e0_ref, s1_ref, e1_ref, tau_ref, u_ref, vt_ref, h_ref, mod_ref, fg_ref,
                        out_ref, yt_scr, wa_scr, at_scr, *, final):
    e = pl.program_id(2)
    tm = a_ref.shape[1]

    @pl.when(e == 0)
    def _():
        yt_scr[...] = jnp.zeros_like(yt_scr)

    at_scr[...] = lax.dot_general(u_ref[...], a_ref[0], (((1,), (1,)), ((), ())), preferred_element_type=F32)

    def lane_tile(c, carry):
        cs = pl.ds(pl.multiple_of(c * LANES, LANES), LANES)
        for il in range(PEER_EB // N_KEYS):
            rows = slice(il * N_KEYS, (il + 1) * N_KEYS)
            acc = jnp.zeros((N_KEYS, LANES), F32)
            for hd in range(PEER_HEADS):
                score = s0_ref[0, hd, il:il + 1, cs] + s1_ref[0, hd, :, cs]
                weight = e0_ref[0, hd, il:il + 1, cs] * e1_ref[0, hd, :, cs]
                acc = acc + jnp.where(score >= tau_ref[0, hd, 0:1, cs], weight, 0.0)
            wa_scr[rows, cs] = (acc * _gelu(at_scr[rows, cs])).astype(BF16)
        return carry

    lax.fori_loop(0, tm // LANES, lane_tile, 0)
    yt_scr[...] += jnp.dot(vt_ref[...], wa_scr[...], preferred_element_type=F32)

    @pl.when(e == pl.num_programs(2) - 1)
    def _():
        hn = h_ref[0] + mod_ref[0, 0][5:6] * yt_scr[...].T
        if final:
            hn = hn * lax.rsqrt(jnp.mean(hn * hn, axis=-1, keepdims=True) + EPS) * fg_ref[...]
        out_ref[0] = hn


def _peer_expert(hh, mod, a, s0, e0, s1, e1, tau, u_bf, vt_bf, final_g, L, *, with_ctx, final):
    bsz, lt, d = hh.shape
    n_exp = u_bf.shape[0]
    tm = PEER_TM
    n_tiles = (lt if with_ctx else L) // tm
    tok = lambda r: pl.BlockSpec((1, PEER_HEADS, r, tm), (lambda b, t, e: (b, 0, e, t)) if r == 8
                                 else (lambda b, t, e: (b, 0, 0, t)))
    row_spec = pl.BlockSpec((1, tm, d), lambda b, t, e: (b, t, 0))
    if final:
        out_shape = jax.ShapeDtypeStruct((bsz, L, d), F32)
        aliases = {}
    else:
        out_shape = jax.ShapeDtypeStruct(hh.shape, F32)
        aliases = {8: 0}
    return pl.pallas_call(
        functools.partial(_peer_expert_kernel, final=final),
        grid=(bsz, n_tiles, n_exp // PEER_EB),
        in_specs=[row_spec, tok(8), tok(8), tok(N_KEYS), tok(N_KEYS),
                  pl.BlockSpec((1, PEER_HEADS, 8, tm), lambda b, t, e: (b, 0, 0, t)),
                  pl.BlockSpec((PEER_EB, d), lambda b, t, e: (e, 0)),
                  pl.BlockSpec((d, PEER_EB), lambda b, t, e: (0, e)),
                  row_spec,
                  pl.BlockSpec((1, 1, 6, d), lambda b, t, e: (b, t // (L // tm), 0, 0)),
                  pl.BlockSpec((1, d), lambda b, t, e: (0, 0))],
        out_specs=row_spec,
        out_shape=out_shape,
        input_output_aliases=aliases,
        scratch_shapes=[pltpu.VMEM((d, tm), F32), pltpu.VMEM((PEER_EB, tm), BF16), pltpu.VMEM((PEER_EB, tm), F32)],
        compiler_params=_cparams(("parallel", "parallel", "arbitrary")),
        name="peer_expert",
    )(a, s0, e0, s1, e1, tau, u_bf, vt_bf, hh, mod, final_g.reshape(1, d))


def _peer(hh, mod, norm_g, w_q, sub_keys, u, v, final_g, L, *, with_ctx, final):
    a, s0, e0, s1, e1, tau = _peer_route(hh, mod, norm_g, w_q, sub_keys, L)
    return _peer_expert(hh, mod, a, s0, e0, s1, e1, tau, u.astype(BF16), v.astype(BF16).T, final_g, L,
                        with_ctx=with_ctx, final=final)


REC_QKV = C_HEADS * 3 * C_DK
REC_Z = C_HEADS * C_DK
REC_GATES = 4 * C_HEADS
REC_HD = D_HEADS * D_DK


def _rec_proj_kernel(h_ref, mod_ref, g_ref, w_ref, alog_ref, dtb_ref, cos_ref, sin_ref,
                     qkv_ref, z_ref, qd_ref, kd_ref, vd_ref, gd_ref, gate_ref):
    m = mod_ref[0, 0]
    a = _norm_mod(h_ref[0], g_ref[...], m[0:1], m[1:2])
    o = _bdot(a, w_ref[...])
    cos, sin = cos_ref[...], sin_ref[...]
    off = 0
    qkv_ref[0] = o[:, off:off + REC_QKV]
    off += REC_QKV
    z_ref[0] = o[:, off:off + REC_Z]
    off += REC_Z
    for hd in range(D_HEADS):
        qd_ref[0, :, hd * D_DK:(hd + 1) * D_DK] = _rope(o[:, off + hd * D_DK: off + (hd + 1) * D_DK], cos, sin, D_DK)
    off += REC_HD
    for hd in range(D_HEADS):
        kd_ref[0, :, hd * D_DK:(hd + 1) * D_DK] = (
            _rope(o[:, off + hd * D_DK: off + (hd + 1) * D_DK], cos, sin, D_DK) * (D_DK ** -0.5))
    off += REC_HD
    vd_ref[0] = o[:, off:off + REC_HD]
    off += REC_HD
    gd_ref[0] = o[:, off:off + 2 * REC_HD]
    off += 2 * REC_HD
    x = o[:, off:off + LANES]
    lane = lax.broadcasted_iota(jnp.int32, x.shape, 1)
    xb = x + dtb_ref[...]
    softplus = jnp.maximum(xb, 0.0) + jnp.log(1.0 + jnp.exp(-jnp.abs(xb)))
    gate_ref[0] = jnp.where(lane < 2 * C_HEADS, -jnp.exp(alog_ref[...]) * softplus, _sigmoid(x))


def _rec_project(hh, mod, norm_g, w_in, a_log, dt_bias, cos, sin, L):
    bsz, lt, d = hh.shape
    tm = ROW_TILE
    parts = np.cumsum([REC_QKV, REC_Z, REC_GATES, REC_HD, REC_HD, REC_HD])
    qkv_w, z_w, gates_w, qd_w, kd_w, vd_w, gd_w = jnp.split(w_in, [int(p) for p in parts], axis=1)
    w = jnp.concatenate([qkv_w, z_w, qd_w, kd_w, vd_w, gd_w, gates_w, jnp.zeros((d, LANES - REC_GATES), F32)],
                        axis=1).astype(BF16)
    n = w.shape[1]
    pad = lambda p: jnp.zeros((1, LANES), F32).at[0, :2 * C_HEADS].set(p.reshape(-1))
    widths = [REC_QKV, REC_Z, REC_HD, REC_HD, REC_HD, 2 * REC_HD, LANES]
    const = lambda shape: pl.BlockSpec(shape, lambda b, i: (0,) * len(shape))
    return pl.pallas_call(
        _rec_proj_kernel,
        grid=(bsz, lt // tm),
        in_specs=[pl.BlockSpec((1, tm, d), lambda b, i: (b, i, 0)),
                  pl.BlockSpec((1, 1, 6, d), lambda b, i: (b, i // (L // tm), 0, 0)),
                  const((1, d)), const((d, n)), const((1, LANES)), const((1, LANES)),
                  pl.BlockSpec((tm, LANES), lambda b, i: (i, 0)),
                  pl.BlockSpec((tm, LANES), lambda b, i: (i, 0))],
        out_specs=[pl.BlockSpec((1, tm, wd), lambda b, i: (b, i, 0)) for wd in widths],
        out_shape=[jax.ShapeDtypeStruct((bsz, lt, wd), F32) for wd in widths],
        compiler_params=_cparams(("parallel", "parallel")),
        name="rec_project",
    )(hh, mod, norm_g.reshape(1, d), w, pad(a_log), pad(dt_bias), cos, sin)


def _rec_conv_kernel(x_ref, prev_ref, next_ref, w_ref, q_ref, k_ref, v_ref, *, L):
    i = pl.program_id(1)
    tl = x_ref.shape[1]
    n_lat = L // tl
    at_start = (i == 0) | (i == n_lat)
    at_end = (i == n_lat - 1) | (i == pl.num_programs(1) - 1)
    prev = jnp.where(at_start, 0.0, prev_ref[0])
    nxt = jnp.where(at_end, 0.0, next_ref[0])
    xx = jnp.concatenate([prev, x_ref[0], nxt], axis=0)
    n = tl + 16
    acc = 0.0
    for tap in range(CONV_W):
        shift = (CONV_W // 2 - tap) % n
        shifted = xx if shift == 0 else pltpu.roll(xx, shift, 0)
        acc = acc + shifted[8:8 + tl] * w_ref[tap:tap + 1, :]
    y = _silu(acc)
    hw = C_HEADS * C_DK
    for hd in range(C_HEADS):
        def l2(x):
            return x * lax.rsqrt(jnp.sum(x * x, axis=-1, keepdims=True) + EPS)
        sl = slice(hd * C_DK, (hd + 1) * C_DK)
        q_ref[0, :, sl] = l2(y[:, hd * C_DK:(hd + 1) * C_DK]) * (C_DK ** -0.5)
        k_ref[0, :, sl] = l2(y[:, hw + hd * C_DK: hw + (hd + 1) * C_DK])
    v_ref[0] = y[:, 2 * hw:]


def _rec_conv(qkv, conv_w, L):
    bsz, lt, ch = qkv.shape
    tl = ROW_TILE
    hb = tl // 8
    last = lt // 8 - 1
    hw = C_HEADS * C_DK
    return pl.pallas_call(
        functools.partial(_rec_conv_kernel, L=L),
        grid=(bsz, lt // tl),
        in_specs=[pl.BlockSpec((1, tl, ch), lambda b, i: (b, i, 0)),
                  pl.BlockSpec((1, 8, ch), lambda b, i: (b, jnp.maximum(i * hb - 1, 0), 0)),
                  pl.BlockSpec((1, 8, ch), lambda b, i: (b, jnp.minimum((i + 1) * hb, last), 0)),
                  pl.BlockSpec((8, ch), lambda b, i: (0, 0))],
        out_specs=[pl.BlockSpec((1, tl, hw), lambda b, i: (b, i, 0))] * 3,
        out_shape=[jax.ShapeDtypeStruct((bsz, lt, hw), F32)] * 3,
        compiler_params=_cparams(("parallel", "parallel")),
        name="rec_conv",
    )(qkv, qkv, qkv, jnp.zeros((8, ch), F32).at[:CONV_W].set(conv_w))


def _ret_log_gamma(hd):
    return float(np.log1p(-np.exp2(-(RET_DECAY_BASE + hd))))


def _rec_intra_kernel(qc_ref, kc_ref, vc_ref, qd_ref, kd_ref, vd_ref, gate_ref,
                      qs_ref, oi_ref, pp_ref, nn_ref, al_ref):
    c = CHUNK
    ci = lax.broadcasted_iota(jnp.int32, (c, c), 0)
    si = lax.broadcasted_iota(jnp.int32, (c, c), 1)
    pos = lax.broadcasted_iota(jnp.int32, (c, 1), 0).astype(F32)
    gates = gate_ref[0]
    la_parts = _split3(gates)
    ones_row = jnp.ones((1, LANES), F32)
    for d in range(2):
        incl = (si <= ci) if d == 0 else (si >= ci)
        strict = (si < ci) if d == 0 else (si > ci)
        tri = incl.astype(BF16)
        g_all = functools.reduce(lambda x, y: x + y, [jnp.dot(tri, p, preferred_element_type=F32) for p in la_parts])
        g_all_t = g_all.T
        last = c - 1 if d == 0 else 0
        for hd in range(C_HEADS):
            col = d * C_HEADS + hd
            sl = slice(hd * C_DK, (hd + 1) * C_DK)
            q, k, v = qc_ref[0, :, sl], kc_ref[0, :, sl], vc_ref[0, :, sl]
            gcol = g_all[:, col:col + 1]
            grow = g_all_t[col:col + 1, :]
            glast = g_all[last:last + 1, col:col + 1]
            beta = gates[:, 2 * C_HEADS + col: 2 * C_HEADS + col + 1]
            diff = gcol - grow
            dec = jnp.exp(jnp.where(incl, diff, _NEG_INF))
            dec_strict = jnp.exp(jnp.where(strict, diff, _NEG_INF))
            x = -(beta * _bdot_nt(k, k) * dec_strict)
            sol = jnp.concatenate([beta * v, (beta * jnp.exp(gcol)) * k], axis=1)
            levels = int(math.log2(c))
            for lvl in range(levels):
                sol = sol + _dot3(x, sol)
                if lvl < levels - 1:
                    x = _dot3(x, x)
            u, wk = sol[:, :C_DK], sol[:, C_DK:]
            qk = _bdot_nt(q, k) * dec
            kend = k * jnp.exp(glast - gcol)
            qs_ref[0, 0, d, hd] = (jnp.exp(gcol) * q - _bdot(qk, wk)).astype(qs_ref.dtype)
            oi_ref[0, 0, d, hd] = _bdot(qk, u)
            pp_ref[0, 0, d, hd] = _bdot_tn(kend, wk).astype(pp_ref.dtype)
            nn_ref[0, 0, d, hd] = _bdot_tn(kend, u)
            al_ref[0, 0, d, hd:hd + 1, :] = jnp.exp(glast) * ones_row
        for hd in range(D_HEADS):
            lg = _ret_log_gamma(hd)
            sl = slice(hd * D_DK, (hd + 1) * D_DK)
            q, k, v = qd_ref[0, :, sl], kd_ref[0, :, sl], vd_ref[0, :, sl]
            steps = pos if d == 0 else (c - 1.0) - pos
            dist = (ci - si) if d == 0 else (si - ci)
            dmat = jnp.exp(jnp.where(incl, lg * dist.astype(F32), _NEG_INF))
            qk = _bdot_nt(q, k) * dmat
            qs_ref[0, 0, d, C_HEADS + hd] = (q * jnp.exp(lg * (steps + 1.0))).astype(qs_ref.dtype)
            oi_ref[0, 0, d, C_HEADS + hd] = _bdot(qk, v)
            nn_ref[0, 0, d, C_HEADS + hd] = _bdot_tn(k * jnp.exp(lg * ((c - 1.0) - steps)), v)
            al_ref[0, 0, d, C_HEADS + hd:C_HEADS + hd + 1, :] = math.exp(lg * c) * ones_row


def _rec_intra(qc, kc, vc, qd, kd, vd, gate):
    bsz, lt, hw = qc.shape
    nc = lt // CHUNK
    nh = C_HEADS + D_HEADS
    row = lambda w: pl.BlockSpec((1, CHUNK, w), lambda b, n: (b, n, 0))
    lead = lambda *tail: pl.BlockSpec((1, 1, 2) + tail, lambda b, n: (b, n, 0) + (0,) * len(tail))
    return pl.pallas_call(
        _rec_intra_kernel,
        grid=(bsz, nc),
        in_specs=[row(hw)] * 6 + [row(LANES)],
        out_specs=[lead(nh, CHUNK, C_DK), lead(nh, CHUNK, C_DK), lead(C_HEADS, C_DK, C_DK), lead(nh, C_DK, C_DK),
                   lead(nh, LANES)],
        out_shape=[jax.ShapeDtypeStruct((bsz, nc, 2, nh, CHUNK, C_DK), BF16),
                   jax.ShapeDtypeStruct((bsz, nc, 2, nh, CHUNK, C_DK), F32),
                   jax.ShapeDtypeStruct((bsz, nc, 2, C_HEADS, C_DK, C_DK), BF16),
                   jax.ShapeDtypeStruct((bsz, nc, 2, nh, C_DK, C_DK), F32),
                   jax.ShapeDtypeStruct((bsz, nc, 2, nh, LANES), F32)],
        compiler_params=_cparams(("parallel", "parallel")),
        name="rec_intra",
    )(qc, kc, vc, qd, kd, vd, gate)


def _rec_scan_kernel(qs_ref, oi_ref, pp_ref, nn_ref, al_ref, o_ref, s_scr):
    @pl.when(pl.program_id(2) == 0)
    def _():
        s_scr[...] = jnp.zeros_like(s_scr)

    for hd in range(C_HEADS + D_HEADS):
        s = s_scr[hd]
        sb = s.astype(BF16)
        o_ref[0, 0, :, hd * C_DK:(hd + 1) * C_DK] = (
            jnp.dot(qs_ref[0, 0, 0, hd], sb, preferred_element_type=F32) + oi_ref[0, 0, 0, hd])
        new = al_ref[0, 0, 0, hd:hd + 1, :] * s + nn_ref[0, 0, 0, hd]
        if hd < C_HEADS:
            new = new - jnp.dot(pp_ref[0, 0, 0, hd], sb, preferred_element_type=F32)
        s_scr[hd] = new


def _rec_scan(qs, oi, pp, nn, al, L):
    bsz, nc = qs.shape[:2]
    nh = C_HEADS + D_HEADS
    n_lat = L // CHUNK
    n_ctx = nc - n_lat

    def chunk(d, s):
        fwd = jnp.where(s < n_ctx, n_lat + s, s - n_ctx)
        bwd = nc - 1 - s
        return jnp.where(d == 0, fwd, bwd)

    lead = lambda *tail: pl.BlockSpec((1, 1, 1) + tail, lambda b, d, s: (b, chunk(d, s), d) + (0,) * len(tail))
    return pl.pallas_call(
        _rec_scan_kernel,
        grid=(bsz, 2, nc),
        in_specs=[lead(nh, CHUNK, C_DK), lead(nh, CHUNK, C_DK), lead(C_HEADS, C_DK, C_DK), lead(nh, C_DK, C_DK),
                  lead(nh, LANES)],
        out_specs=pl.BlockSpec((1, 1, CHUNK, nh * C_DK), lambda b, d, s: (b, d, chunk(d, s), 0)),
        out_shape=jax.ShapeDtypeStruct((bsz, 2, nc * CHUNK, nh * C_DK), F32),
        scratch_shapes=[pltpu.VMEM((nh, C_DK, C_DK), F32)],
        compiler_params=_cparams(("parallel", "parallel", "arbitrary")),
        name="rec_scan",
    )(qs, oi, pp, nn, al)


def _rec_out_kernel(of_ref, ob_ref, z_ref, gd_ref, og_ref, gn_ref, w_ref, h_ref, mod_ref, out_ref):
    hw = C_HEADS * C_DK
    of, ob = of_ref[0, 0], ob_ref[0, 0]
    z, gd = z_ref[0], gd_ref[0]
    parts = []
    for hd in range(C_HEADS):
        sl = slice(hd * C_DK, (hd + 1) * C_DK)
        x = of[:, sl] + ob[:, sl]
        y = x * lax.rsqrt(jnp.mean(x * x, axis=-1, keepdims=True) + EPS) * og_ref[...]
        parts.append(y * _silu(z[:, sl]))
    for hd in range(D_HEADS):
        sl = slice(hd * D_DK, (hd + 1) * D_DK)
        y = 0.0
        for d, o in enumerate((of, ob)):
            x = o[:, hw + hd * D_DK: hw + (hd + 1) * D_DK]
            mu = jnp.mean(x, axis=-1, keepdims=True)
            xc = x - mu
            var = jnp.mean(xc * xc, axis=-1, keepdims=True)
            y = y + xc * lax.rsqrt(var + EPS) * gn_ref[:, sl] * _silu(gd[:, d * hw + hd * D_DK: d * hw + (hd + 1) * D_DK])
        parts.append(y)
    mix = jnp.concatenate(parts, axis=1)
    out_ref[0] = h_ref[0] + mod_ref[0, 0][2:3] * _bdot(mix, w_ref[...])


def _rec_out(o_scan, z, gd, out_g, gn_g, w_out, hh, mod, L):
    bsz, lt, d = hh.shape
    tm = ROW_TILE
    hw = C_HEADS * C_DK
    return pl.pallas_call(
        _rec_out_kernel,
        grid=(bsz, L // tm),
        in_specs=[pl.BlockSpec((1, 1, tm, 2 * hw), lambda b, i: (b, 0, i, 0)),
                  pl.BlockSpec((1, 1, tm, 2 * hw), lambda b, i: (b, 1, i, 0)),
                  pl.BlockSpec((1, tm, hw), lambda b, i: (b, i, 0)),
                  pl.BlockSpec((1, tm, 2 * hw), lambda b, i: (b, i, 0)),
                  pl.BlockSpec((1, C_DK), lambda b, i: (0, 0)),
                  pl.BlockSpec((1, hw), lambda b, i: (0, 0)),
                  pl.BlockSpec((2 * hw, d), lambda b, i: (0, 0)),
                  pl.BlockSpec((1, tm, d), lambda b, i: (b, i, 0)),
                  pl.BlockSpec((1, 1, 6, d), lambda b, i: (b, 0, 0, 0))],
        out_specs=pl.BlockSpec((1, tm, d), lambda b, i: (b, i, 0)),
        out_shape=jax.ShapeDtypeStruct(hh.shape, F32),
        input_output_aliases={7: 0},
        compiler_params=_cparams(("parallel", "parallel")),
        name="rec_out",
    )(o_scan, o_scan, z, gd, out_g.reshape(1, C_DK), gn_g.reshape(1, hw), w_out.astype(BF16), hh, mod)


def _att_layer(hh, mod, norm_g, w_in, q_g, k_g, sink, w_out, L):
    cos, sin = _rope_tables(L, hh.shape[1] - L, HEAD_DIM)
    qa, ka, va, qb, kb, vb = _att_project(hh, mod, norm_g, w_in, q_g, k_g, cos, sin, L)
    oa, ob = _attention(qa, ka, va, qb, kb, vb, sink, L)
    return _att_out(oa, ob, w_out, hh, mod, L)


def _rec_layer(hh, mod, norm_g, w_in, conv_w, a_log, dt_bias, out_g, gn_g, w_out, L):
    cos, sin = _rope_tables(L, hh.shape[1] - L, D_DK)
    qkv, z, qd, kd, vd, gd, gate = _rec_project(hh, mod, norm_g, w_in, a_log, dt_bias, cos, sin, L)
    qc, kc, vc = _rec_conv(qkv, conv_w, L)
    qs, oi, pp, nn, al = _rec_intra(qc, kc, vc, qd, kd, vd, gate)
    o_scan = _rec_scan(qs, oi, pp, nn, al, L)
    return _rec_out(o_scan, z, gd, out_g, gn_g, w_out, hh, mod, L)


def kernel(x, c, ctx, c_ctx, mod_w, mod_b, norm1_g, norm2_g, att_w_in, att_q_norm, att_k_norm, att_sink, att_w_out,
           rec_w_in, rec_conv_w, rec_a_log, rec_dt_bias, rec_out_norm, rec_gn_g, rec_w_out,
           peer_w_q, peer_sub_keys, peer_u, peer_v, final_norm_g):
    L = x.shape[1]
    depth = mod_w.shape[0]
    mods = _modulation(c, c_ctx, mod_w, mod_b)
    hh = jnp.concatenate([x, ctx], axis=1)
    for layer in range(depth):
        last = layer == depth - 1
        i = layer // 2
        if layer % 2 == 0:
            hh = _att_layer(hh, mods[layer], norm1_g[layer], att_w_in[i], att_q_norm[i], att_k_norm[i], att_sink[i],
                            att_w_out[i], L)
        else:
            hh = _rec_layer(hh, mods[layer], norm1_g[layer], rec_w_in[i], rec_conv_w[i], rec_a_log[i], rec_dt_bias[i],
                            rec_out_norm[i], rec_gn_g[i], rec_w_out[i], L)
        hh = _peer(hh, mods[layer], norm2_g[layer], peer_w_q[layer], peer_sub_keys[layer], peer_u[layer],
                   peer_v[layer], final_norm_g, L, with_ctx=not last, final=last)
    return hh
```

```python
import functools
import math

import numpy as np
import jax
import jax.numpy as jnp
from jax import lax
from jax.experimental import pallas as pl
from jax.experimental.pallas import tpu as pltpu

F32 = jnp.float32
BF16 = jnp.bfloat16

GRID_W = 64
EPS = 1e-6
HEAD_DIM = 64
A_HEADS = 8
A_KV = 2
B_HEADS = 8
B_KV = 2
WINDOW = 128
ROPE_THETA = 10000.0
C_HEADS = 4
C_DK = 128
CONV_W = 5
CHUNK = 64
D_HEADS = 4
D_DK = 128
RET_DECAY_BASE = 5.0
PEER_HEADS = 8
N_KEYS = 128
PEER_TOPK = 16

LANES = 128
VMEM_LIMIT = 56 * 1024 * 1024

ROW_TILE = 256
ATT_TQ = 128
PEER_TM = 256
PEER_EB = 1024
PEER_SUB = 256

_NEG_INF = float("-inf")


def _cparams(sem):
    return pltpu.CompilerParams(dimension_semantics=sem, vmem_limit_bytes=VMEM_LIMIT)


def _bdot(a, b):
    return jnp.dot(a.astype(BF16), b.astype(BF16), preferred_element_type=F32)


def _bdot_nt(a, b):
    return lax.dot_general(a.astype(BF16), b.astype(BF16), (((1,), (1,)), ((), ())), preferred_element_type=F32)


def _bdot_tn(a, b):
    return jnp.dot(a.T.astype(BF16), b.astype(BF16), preferred_element_type=F32)


def _split3(a):
    hi = a.astype(BF16)
    r1 = a - hi.astype(F32)
    mid = r1.astype(BF16)
    lo = (r1 - mid.astype(F32)).astype(BF16)
    return hi, mid, lo


def _dot3(a, b):
    a_hi, a_lo, _ = _split3(a)
    b_hi, b_lo, _ = _split3(b)
    d = functools.partial(jnp.dot, preferred_element_type=F32)
    return d(a_hi, b_hi) + (d(a_hi, b_lo) + d(a_lo, b_hi))


def _sigmoid(x):
    return 1.0 / (1.0 + jnp.exp(-x))


def _silu(x):
    return x * _sigmoid(x)


def _norm_mod(x, g, shift, scale):
    r = lax.rsqrt(jnp.mean(x * x, axis=-1, keepdims=True) + EPS)
    return (x * r * g) * (1.0 + scale) + shift


def _rope(x, cos, sin_signed, head_dim):
    quarter = head_dim // 4
    lane = lax.broadcasted_iota(jnp.int32, x.shape, 1)
    first = (lane % (2 * quarter)) < quarter
    partner = jnp.where(first, pltpu.roll(x, LANES - quarter, 1), pltpu.roll(x, quarter, 1))
    return x * cos + partner * sin_signed


def _mod_kernel(c_ref, w_ref, b_ref, o_ref):
    o_ref[0] = _bdot(_silu(c_ref[...]), w_ref[0]) + b_ref[0]


def _modulation(c, c_ctx, mod_w, mod_b):
    depth, d, n = mod_w.shape
    bsz = c.shape[0]
    rows = 16
    cc = jnp.zeros((rows, d), F32).at[:bsz].set(c).at[bsz].set(c_ctx)
    tn = 1536
    out = pl.pallas_call(
        _mod_kernel,
        grid=(depth, n // tn),
        in_specs=[pl.BlockSpec((rows, d), lambda l, j: (0, 0)),
                  pl.BlockSpec((1, d, tn), lambda l, j: (l, 0, j)),
                  pl.BlockSpec((1, 1, tn), lambda l, j: (l, 0, j))],
        out_specs=pl.BlockSpec((1, rows, tn), lambda l, j: (l, 0, j)),
        out_shape=jax.ShapeDtypeStruct((depth, rows, n), F32),
        compiler_params=_cparams(("arbitrary", "arbitrary")),
        name="modulation",
    )(cc, mod_w, mod_b.reshape(depth, 1, n))
    lat = out[:, :bsz].reshape(depth, bsz, 1, 6, d)
    ctx = jnp.broadcast_to(out[:, bsz].reshape(depth, 1, 1, 6, d), (depth, bsz, 1, 6, d))
    return jnp.concatenate([lat, ctx], axis=2)


def _rope_tables(L, LC, head_dim):
    quarter, half = head_dim // 4, head_dim // 2
    freqs = ROPE_THETA ** (-jnp.arange(quarter, dtype=F32) / quarter)
    lane = np.arange(LANES)
    within = lane % head_dim
    use_col = within >= half
    fidx = within % quarter
    sign = np.where((within % half) < quarter, -1.0, 1.0).astype(np.float32)
    t = jnp.arange(L, dtype=jnp.int32)
    row, col = (t // GRID_W).astype(F32), (t % GRID_W).astype(F32)
    pos = jnp.where(use_col[None, :], col[:, None], row[:, None])
    ang = pos * freqs[fidx][None, :]
    cos = jnp.concatenate([jnp.cos(ang), jnp.ones((LC, LANES), F32)], axis=0)
    sin = jnp.concatenate([jnp.sin(ang) * sign[None, :], jnp.zeros((LC, LANES), F32)], axis=0)
    return cos, sin


def _att_proj_kernel(h_ref, mod_ref, g_ref, w_ref, qg_ref, kg_ref, cos_ref, sin_ref, gm_ref,
                     qa_ref, ka_ref, va_ref, qb_ref, kb_ref, vb_ref):
    m = mod_ref[0, 0]
    a = _norm_mod(h_ref[0], g_ref[...], m[0:1], m[1:2])
    o = _bdot(a, w_ref[...])
    cos, sin = cos_ref[...], sin_ref[...]
    gm = gm_ref[...]

    def head_norm(x, gain):
        sq = x * x
        hi = sq.astype(BF16)
        lo = (sq - hi.astype(F32)).astype(BF16)
        ms = jnp.dot(hi, gm, preferred_element_type=F32) + jnp.dot(lo, gm, preferred_element_type=F32)
        return x * lax.rsqrt(ms + EPS) * gain

    def put(ref, first_head, x):
        ref[0, first_head] = x[:, :HEAD_DIM].astype(ref.dtype)
        ref[0, first_head + 1] = x[:, HEAD_DIM:].astype(ref.dtype)

    scale = HEAD_DIM ** -0.5
    qa_w = A_HEADS * HEAD_DIM
    kv_w = A_KV * HEAD_DIM
    off = 0
    for c in range(qa_w // LANES):
        x = o[:, off + c * LANES: off + (c + 1) * LANES]
        put(qa_ref, 2 * c, _rope(head_norm(x, qg_ref[...]), cos, sin, HEAD_DIM) * scale)
    off += qa_w
    put(ka_ref, 0, _rope(head_norm(o[:, off: off + kv_w], kg_ref[...]), cos, sin, HEAD_DIM))
    off += kv_w
    put(va_ref, 0, o[:, off: off + kv_w])
    off += kv_w
    for c in range(qa_w // LANES):
        x = o[:, off + c * LANES: off + (c + 1) * LANES]
        put(qb_ref, 2 * c, _rope(x, cos, sin, HEAD_DIM) * scale)
    off += qa_w
    put(kb_ref, 0, _rope(o[:, off: off + kv_w], cos, sin, HEAD_DIM))
    off += kv_w
    put(vb_ref, 0, o[:, off: off + kv_w])


def _att_project(hh, mod, norm_g, w_in, q_g, k_g, cos, sin, L):
    bsz, lt, d = hh.shape
    tm = ROW_TILE
    n = w_in.shape[1]
    gm = jnp.asarray(np.kron(np.eye(LANES // HEAD_DIM), np.full((HEAD_DIM, HEAD_DIM), 1.0 / HEAD_DIM)), BF16)
    tile2 = lambda v: jnp.tile(v.reshape(1, HEAD_DIM), (1, LANES // HEAD_DIM))
    qshape = jax.ShapeDtypeStruct((bsz, A_HEADS, lt, HEAD_DIM), BF16)
    kshape = jax.ShapeDtypeStruct((bsz, A_KV, lt, HEAD_DIM), BF16)
    qspec = pl.BlockSpec((1, A_HEADS, tm, HEAD_DIM), lambda b, i: (b, 0, i, 0))
    kspec = pl.BlockSpec((1, A_KV, tm, HEAD_DIM), lambda b, i: (b, 0, i, 0))
    const = lambda shape: pl.BlockSpec(shape, lambda b, i: (0,) * len(shape))
    return pl.pallas_call(
        _att_proj_kernel,
        grid=(bsz, lt // tm),
        in_specs=[pl.BlockSpec((1, tm, d), lambda b, i: (b, i, 0)),
                  pl.BlockSpec((1, 1, 6, d), lambda b, i: (b, i // (L // tm), 0, 0)),
                  const((1, d)), const((d, n)), const((1, LANES)), const((1, LANES)),
                  pl.BlockSpec((tm, LANES), lambda b, i: (i, 0)),
                  pl.BlockSpec((tm, LANES), lambda b, i: (i, 0)),
                  const((LANES, LANES))],
        out_specs=[qspec, kspec, kspec, qspec, kspec, kspec],
        out_shape=[qshape, kshape, kshape, qshape, kshape, kshape],
        compiler_params=_cparams(("parallel", "parallel")),
        name="att_project",
    )(hh, mod, norm_g.reshape(1, d), w_in.astype(BF16), tile2(q_g), tile2(k_g), cos, sin, gm)


def _softmax_pv(score_parts, value_parts, extra_logit=None):
    m = functools.reduce(jnp.maximum, [jnp.max(s, axis=-1, keepdims=True) for s in score_parts])
    if extra_logit is not None:
        m = jnp.maximum(m, extra_logit)
    l = 0.0
    acc = 0.0
    for s, v in zip(score_parts, value_parts):
        p = jnp.exp(s - m)
        l = l + jnp.sum(p, axis=-1, keepdims=True)
        acc = acc + jnp.dot(p.astype(BF16), v, preferred_element_type=F32)
    if extra_logit is not None:
        l = l + jnp.exp(extra_logit - m)
    return acc / l


def _att_kernel(sink_ref, qa_ref, ka_ref, va_ref, qb_ref, kb_ref, vb_ref, oa_ref, ob_ref, *, L, LC):
    i = pl.program_id(1)
    tq = ATT_TQ
    group = A_HEADS // A_KV
    band = tq + 2 * WINDOW

    def stacked_q(ref, kvh):
        return ref[0, kvh * group:(kvh + 1) * group].reshape(group * tq, HEAD_DIM)

    def put(ref, kvh, o):
        for g in range(group):
            hd = kvh * group + g
            ref[0, :, hd * HEAD_DIM:(hd + 1) * HEAD_DIM] = o[g * tq:(g + 1) * tq].astype(ref.dtype)

    def sink_col(kvh):
        return jnp.concatenate([jnp.full((tq, 1), sink_ref[kvh * group + g], F32) for g in range(group)], axis=0)

    @pl.when(i < L // tq)
    def _latent():
        for kvh in range(A_KV):
            q = stacked_q(qa_ref, kvh)
            put(oa_ref, kvh, _softmax_pv([_bdot_nt(q, ka_ref[0, kvh])], [va_ref[0, kvh]]))
            q = stacked_q(qb_ref, kvh)
            start = pl.multiple_of(jnp.clip((i - 1) * tq, 0, L - band), tq)
            s_band = _bdot_nt(q, kb_ref[0, kvh, pl.ds(start, band), :])
            qpos = i * tq + (lax.broadcasted_iota(jnp.int32, s_band.shape, 0) % tq)
            kpos = start + lax.broadcasted_iota(jnp.int32, s_band.shape, 1)
            s_band = jnp.where(jnp.abs(qpos - kpos) <= WINDOW, s_band, _NEG_INF)
            s_ctx = _bdot_nt(q, kb_ref[0, kvh, L:L + LC, :])
            put(ob_ref, kvh, _softmax_pv([s_band, s_ctx],
                                         [vb_ref[0, kvh, pl.ds(start, band), :], vb_ref[0, kvh, L:L + LC, :]],
                                         sink_col(kvh)))

    @pl.when(i >= L // tq)
    def _context():
        for kvh in range(A_KV):
            q = stacked_q(qa_ref, kvh)
            put(oa_ref, kvh, _softmax_pv([_bdot_nt(q, ka_ref[0, kvh, L:L + LC, :])], [va_ref[0, kvh, L:L + LC, :]]))
            q = stacked_q(qb_ref, kvh)
            put(ob_ref, kvh, _softmax_pv([_bdot_nt(q, kb_ref[0, kvh, L:L + LC, :])], [vb_ref[0, kvh, L:L + LC, :]],
                                         sink_col(kvh)))


def _attention(qa, ka, va, qb, kb, vb, sink, L):
    bsz, _, lt, _ = qa.shape
    tq = ATT_TQ
    qspec = pl.BlockSpec((1, A_HEADS, tq, HEAD_DIM), lambda b, i: (b, 0, i, 0))
    kspec = pl.BlockSpec((1, A_KV, lt, HEAD_DIM), lambda b, i: (b, 0, 0, 0))
    ospec = pl.BlockSpec((1, tq, A_HEADS * HEAD_DIM), lambda b, i: (b, i, 0))
    oshape = jax.ShapeDtypeStruct((bsz, lt, A_HEADS * HEAD_DIM), BF16)
    return pl.pallas_call(
        functools.partial(_att_kernel, L=L, LC=lt - L),
        grid=(bsz, lt // tq),
        in_specs=[pl.BlockSpec(memory_space=pltpu.SMEM), qspec, kspec, kspec, qspec, kspec, kspec],
        out_specs=[ospec, ospec],
        out_shape=[oshape, oshape],
        compiler_params=_cparams(("parallel", "parallel")),
        name="attention",
    )(sink, qa, ka, va, qb, kb, vb)


def _att_out_kernel(oa_ref, ob_ref, w_ref, h_ref, mod_ref, out_ref):
    half = oa_ref.shape[-1]
    y = (jnp.dot(oa_ref[0], w_ref[:half, :], preferred_element_type=F32)
         + jnp.dot(ob_ref[0], w_ref[half:, :], preferred_element_type=F32))
    out_ref[0] = h_ref[0] + mod_ref[0, 0][2:3] * y


def _att_out(oa, ob, w_out, hh, mod, L):
    bsz, lt, d = hh.shape
    tm = ROW_TILE
    half = oa.shape[-1]
    return pl.pallas_call(
        _att_out_kernel,
        grid=(bsz, lt // tm),
        in_specs=[pl.BlockSpec((1, tm, half), lambda b, i: (b, i, 0)),
                  pl.BlockSpec((1, tm, half), lambda b, i: (b, i, 0)),
                  pl.BlockSpec((2 * half, d), lambda b, i: (0, 0)),
                  pl.BlockSpec((1, tm, d), lambda b, i: (b, i, 0)),
                  pl.BlockSpec((1, 1, 6, d), lambda b, i: (b, i // (L // tm), 0, 0))],
        out_specs=pl.BlockSpec((1, tm, d), lambda b, i: (b, i, 0)),
        out_shape=jax.ShapeDtypeStruct(hh.shape, F32),
        input_output_aliases={3: 0},
        compiler_params=_cparams(("parallel", "parallel")),
        name="att_out",
    )(oa, ob, w_out.astype(BF16), hh, mod)


def _top_values(x, n):
    vals = []
    for _ in range(n):
        m = jnp.max(x, axis=0, keepdims=True)
        vals.append(m)
        x = jnp.where(x == m, _NEG_INF, x)
    return vals


def _peer_route_kernel(h_ref, mod_ref, g_ref, wq_ref, keys_ref, a_ref, thr_ref, e0_ref, s1_ref, e1_ref, q_scr):
    m = mod_ref[0, 0]
    a = _norm_mod(h_ref[0], g_ref[...], m[3:4], m[4:5]).astype(BF16)
    a_ref[0] = a
    q_scr[...] = jnp.dot(a, wq_ref[...], preferred_element_type=F32)
    k = PEER_TOPK
    n = k + 1
    n_second = [-(-(n // (r + 1)) // 8) * 8 for r in range(n)]

    def head(hd, carry):
        col = pl.multiple_of(hd * 2 * N_KEYS, 2 * N_KEYS)
        s0 = _bdot_nt(keys_ref[0], q_scr[:, pl.ds(col, N_KEYS)])
        s1 = _bdot_nt(keys_ref[1], q_scr[:, pl.ds(col + N_KEYS, N_KEYS)])
        tokens = s0.shape[1]
        top0 = _top_values(s0, n)
        top1 = jnp.concatenate(_top_values(s1, n) + [jnp.full((n_second[0] - n, tokens), _NEG_INF, F32)], axis=0)
        cand = jnp.concatenate([top0[r] + top1[:n_second[r]] for r in range(n)], axis=0)
        best = _top_values(cand, n)
        z = functools.reduce(lambda x, y: x + y, [jnp.exp(b - best[0]) for b in best[:k]])
        tau = 0.5 * (best[k - 1] + best[k])
        thr_ref[0, hd] = tau - s0
        e0_ref[0, hd] = jnp.exp(s0 - top0[0])
        s1_ref[0, hd] = s1
        e1_ref[0, hd] = jnp.exp(s1 - top1[0:1]) / z
        return carry

    lax.fori_loop(0, PEER_HEADS, head, 0)


def _peer_route(hh, mod, norm_g, w_q, sub_keys, L):
    bsz, lt, d = hh.shape
    tm = ROW_TILE
    nq = w_q.shape[1]
    tok = lambda: pl.BlockSpec((1, PEER_HEADS, N_KEYS, tm), lambda b, i: (b, 0, 0, i))
    tshape = jax.ShapeDtypeStruct((bsz, PEER_HEADS, N_KEYS, lt), F32)
    return pl.pallas_call(
        _peer_route_kernel,
        grid=(bsz, lt // tm),
        in_specs=[pl.BlockSpec((1, tm, d), lambda b, i: (b, i, 0)),
                  pl.BlockSpec((1, 1, 6, d), lambda b, i: (b, i // (L // tm), 0, 0)),
                  pl.BlockSpec((1, d), lambda b, i: (0, 0)),
                  pl.BlockSpec((d, nq), lambda b, i: (0, 0)),
                  pl.BlockSpec((2, N_KEYS, N_KEYS), lambda b, i: (0, 0, 0))],
        out_specs=[pl.BlockSpec((1, tm, d), lambda b, i: (b, i, 0)), tok(), tok(), tok(), tok()],
        out_shape=[jax.ShapeDtypeStruct((bsz, lt, d), BF16), tshape, tshape, tshape, tshape],
        scratch_shapes=[pltpu.VMEM((tm, nq), F32)],
        compiler_params=_cparams(("parallel", "parallel")),
        name="peer_route",
    )(hh, mod, norm_g.reshape(1, d), w_q.astype(BF16), sub_keys.astype(BF16))


def _gelu(x):
    return 0.5 * x * (1.0 + lax.erf(x * (1.0 / math.sqrt(2.0))))


def _peer_expert_kernel(a_ref, thr_ref, e0_ref, s1_ref, e1_ref, u_ref, vt_ref, h_ref, mod_ref, fg_ref,
                        out_ref, yt_scr, wa_scr, *, final):
    e = pl.program_id(2)
    tm = a_ref.shape[1]

    @pl.when(e == 0)
    def _():
        yt_scr[...] = jnp.zeros_like(yt_scr)

    a = a_ref[0]
    y = None
    for q in range(PEER_EB // PEER_SUB):
        rows = slice(q * PEER_SUB, (q + 1) * PEER_SUB)
        act = lax.dot_general(u_ref[rows, :], a, (((1,), (1,)), ((), ())), preferred_element_type=F32)
        for il in range(PEER_SUB // N_KEYS):
            key0 = q * (PEER_SUB // N_KEYS) + il
            for c in range(tm // LANES):
                cs = slice(c * LANES, (c + 1) * LANES)
                gate = jnp.zeros((N_KEYS, LANES), F32)
                for hd in range(PEER_HEADS):
                    weight = e0_ref[0, hd, key0:key0 + 1, cs] * e1_ref[0, hd, :, cs]
                    gate = gate + jnp.where(s1_ref[0, hd, :, cs] >= thr_ref[0, hd, key0:key0 + 1, cs], weight, 0.0)
                wa_scr[q * PEER_SUB + il * N_KEYS: q * PEER_SUB + (il + 1) * N_KEYS, cs] = (
                    gate * _gelu(act[il * N_KEYS:(il + 1) * N_KEYS, cs])).astype(BF16)
        part = jnp.dot(vt_ref[:, rows], wa_scr[rows, :], preferred_element_type=F32)
        y = part if y is None else y + part
    yt_scr[...] += y

    @pl.when(e == pl.num_programs(2) - 1)
    def _():
        hn = h_ref[0] + mod_ref[0, 0][5:6] * yt_scr[...].T
        if final:
            hn = hn * lax.rsqrt(jnp.mean(hn * hn, axis=-1, keepdims=True) + EPS) * fg_ref[...]
        out_ref[0] = hn


def _peer_expert(hh, mod, a, thr, e0, s1, e1, u_bf, vt_bf, final_g, L, *, with_ctx, final):
    bsz, lt, d = hh.shape
    n_exp = u_bf.shape[0]
    tm = PEER_TM
    n_tiles = (lt if with_ctx else L) // tm
    tok = lambda r: pl.BlockSpec((1, PEER_HEADS, r, tm), (lambda b, t, e: (b, 0, e, t)) if r == 8
                                 else (lambda b, t, e: (b, 0, 0, t)))
    row_spec = pl.BlockSpec((1, tm, d), lambda b, t, e: (b, t, 0))
    if final:
        out_shape = jax.ShapeDtypeStruct((bsz, L, d), F32)
        aliases = {}
    else:
        out_shape = jax.ShapeDtypeStruct(hh.shape, F32)
        aliases = {7: 0}
    return pl.pallas_call(
        functools.partial(_peer_expert_kernel, final=final),
        grid=(bsz, n_tiles, n_exp // PEER_EB),
        in_specs=[row_spec, tok(8), tok(8), tok(N_KEYS), tok(N_KEYS),
                  pl.BlockSpec((PEER_EB, d), lambda b, t, e: (e, 0)),
                  pl.BlockSpec((d, PEER_EB), lambda b, t, e: (0, e)),
                  row_spec,
                  pl.BlockSpec((1, 1, 6, d), lambda b, t, e: (b, t // (L // tm), 0, 0)),
                  pl.BlockSpec((1, d), lambda b, t, e: (0, 0))],
        out_specs=row_spec,
        out_shape=out_shape,
        input_output_aliases=aliases,
        scratch_shapes=[pltpu.VMEM((d, tm), F32), pltpu.VMEM((PEER_EB, tm), BF16)],
        compiler_params=_cparams(("parallel", "parallel", "arbitrary")),
        name="peer_expert",
    )(a, thr, e0, s1, e1, u_bf, vt_bf, hh, mod, final_g.reshape(1, d))


def _peer(hh, mod, norm_g, w_q, sub_keys, u, v, final_g, L, *, with_ctx, final):
    a, thr, e0, s1, e1 = _peer_route(hh, mod, norm_g, w_q, sub_keys, L)
    return _peer_expert(hh, mod, a, thr, e0, s1, e1, u.astype(BF16), v.astype(BF16).T, final_g, L,
                        with_ctx=with_ctx, final=final)


REC_QKV = C_HEADS * 3 * C_DK
REC_Z = C_HEADS * C_DK
REC_GATES = 4 * C_HEADS
REC_HD = D_HEADS * D_DK


def _rec_proj_kernel(h_ref, mod_ref, g_ref, w_ref, alog_ref, dtb_ref, cos_ref, sin_ref,
                     qkv_ref, z_ref, qd_ref, kd_ref, vd_ref, gd_ref, gate_ref):
    m = mod_ref[0, 0]
    a = _norm_mod(h_ref[0], g_ref[...], m[0:1], m[1:2])
    o = _bdot(a, w_ref[...])
    cos, sin = cos_ref[...], sin_ref[...]
    off = 0
    qkv_ref[0] = o[:, off:off + REC_QKV]
    off += REC_QKV
    z_ref[0] = o[:, off:off + REC_Z]
    off += REC_Z
    for hd in range(D_HEADS):
        qd_ref[0, :, hd * D_DK:(hd + 1) * D_DK] = _rope(o[:, off + hd * D_DK: off + (hd + 1) * D_DK], cos, sin, D_DK)
    off += REC_HD
    for hd in range(D_HEADS):
        kd_ref[0, :, hd * D_DK:(hd + 1) * D_DK] = (
            _rope(o[:, off + hd * D_DK: off + (hd + 1) * D_DK], cos, sin, D_DK) * (D_DK ** -0.5))
    off += REC_HD
    vd_ref[0] = o[:, off:off + REC_HD]
    off += REC_HD
    gd_ref[0] = o[:, off:off + 2 * REC_HD]
    off += 2 * REC_HD
    x = o[:, off:off + LANES]
    lane = lax.broadcasted_iota(jnp.int32, x.shape, 1)
    xb = x + dtb_ref[...]
    softplus = jnp.maximum(xb, 0.0) + jnp.log(1.0 + jnp.exp(-jnp.abs(xb)))
    gate_ref[0] = jnp.where(lane < 2 * C_HEADS, -jnp.exp(alog_ref[...]) * softplus, _sigmoid(x))


def _rec_project(hh, mod, norm_g, w_in, a_log, dt_bias, cos, sin, L):
    bsz, lt, d = hh.shape
    tm = ROW_TILE
    parts = np.cumsum([REC_QKV, REC_Z, REC_GATES, REC_HD, REC_HD, REC_HD])
    qkv_w, z_w, gates_w, qd_w, kd_w, vd_w, gd_w = jnp.split(w_in, [int(p) for p in parts], axis=1)
    w = jnp.concatenate([qkv_w, z_w, qd_w, kd_w, vd_w, gd_w, gates_w, jnp.zeros((d, LANES - REC_GATES), F32)],
                        axis=1).astype(BF16)
    n = w.shape[1]
    pad = lambda p: jnp.zeros((1, LANES), F32).at[0, :2 * C_HEADS].set(p.reshape(-1))
    widths = [REC_QKV, REC_Z, REC_HD, REC_HD, REC_HD, 2 * REC_HD, LANES]
    const = lambda shape: pl.BlockSpec(shape, lambda b, i: (0,) * len(shape))
    return pl.pallas_call(
        _rec_proj_kernel,
        grid=(bsz, lt // tm),
        in_specs=[pl.BlockSpec((1, tm, d), lambda b, i: (b, i, 0)),
                  pl.BlockSpec((1, 1, 6, d), lambda b, i: (b, i // (L // tm), 0, 0)),
                  const((1, d)), const((d, n)), const((1, LANES)), const((1, LANES)),
                  pl.BlockSpec((tm, LANES), lambda b, i: (i, 0)),
                  pl.BlockSpec((tm, LANES), lambda b, i: (i, 0))],
        out_specs=[pl.BlockSpec((1, tm, wd), lambda b, i: (b, i, 0)) for wd in widths],
        out_shape=[jax.ShapeDtypeStruct((bsz, lt, wd), F32) for wd in widths],
        compiler_params=_cparams(("parallel", "parallel")),
        name="rec_project",
    )(hh, mod, norm_g.reshape(1, d), w, pad(a_log), pad(dt_bias), cos, sin)


def _rec_conv_kernel(x_ref, prev_ref, next_ref, w_ref, q_ref, k_ref, v_ref, *, L):
    i = pl.program_id(1)
    tl = x_ref.shape[1]
    n_lat = L // tl
    at_start = (i == 0) | (i == n_lat)
    at_end = (i == n_lat - 1) | (i == pl.num_programs(1) - 1)
    prev = jnp.where(at_start, 0.0, prev_ref[0])
    nxt = jnp.where(at_end, 0.0, next_ref[0])
    xx = jnp.concatenate([prev, x_ref[0], nxt], axis=0)
    n = tl + 16
    acc = 0.0
    for tap in range(CONV_W):
        shift = (CONV_W // 2 - tap) % n
        shifted = xx if shift == 0 else pltpu.roll(xx, shift, 0)
        acc = acc + shifted[8:8 + tl] * w_ref[tap:tap + 1, :]
    y = _silu(acc)
    hw = C_HEADS * C_DK
    for hd in range(C_HEADS):
        def l2(x):
            return x * lax.rsqrt(jnp.sum(x * x, axis=-1, keepdims=True) + EPS)
        sl = slice(hd * C_DK, (hd + 1) * C_DK)
        q_ref[0, :, sl] = l2(y[:, hd * C_DK:(hd + 1) * C_DK]) * (C_DK ** -0.5)
        k_ref[0, :, sl] = l2(y[:, hw + hd * C_DK: hw + (hd + 1) * C_DK])
    v_ref[0] = y[:, 2 * hw:]


def _rec_conv(qkv, conv_w, L):
    bsz, lt, ch = qkv.shape
    tl = ROW_TILE
    hb = tl // 8
    last = lt // 8 - 1
    hw = C_HEADS * C_DK
    return pl.pallas_call(
        functools.partial(_rec_conv_kernel, L=L),
        grid=(bsz, lt // tl),
        in_specs=[pl.BlockSpec((1, tl, ch), lambda b, i: (b, i, 0)),
                  pl.BlockSpec((1, 8, ch), lambda b, i: (b, jnp.maximum(i * hb - 1, 0), 0)),
                  pl.BlockSpec((1, 8, ch), lambda b, i: (b, jnp.minimum((i + 1) * hb, last), 0)),
                  pl.BlockSpec((8, ch), lambda b, i: (0, 0))],
        out_specs=[pl.BlockSpec((1, tl, hw), lambda b, i: (b, i, 0))] * 3,
        out_shape=[jax.ShapeDtypeStruct((bsz, lt, hw), F32)] * 3,
        compiler_params=_cparams(("parallel", "parallel")),
        name="rec_conv",
    )(qkv, qkv, qkv, jnp.zeros((8, ch), F32).at[:CONV_W].set(conv_w))


def _ret_log_gamma(hd):
    return float(np.log1p(-np.exp2(-(RET_DECAY_BASE + hd))))


def _rec_intra_kernel(qc_ref, kc_ref, vc_ref, qd_ref, kd_ref, vd_ref, gate_ref,
                      qs_ref, oi_ref, pp_ref, nn_ref, al_ref):
    c = CHUNK
    ci = lax.broadcasted_iota(jnp.int32, (c, c), 0)
    si = lax.broadcasted_iota(jnp.int32, (c, c), 1)
    pos = lax.broadcasted_iota(jnp.int32, (c, 1), 0).astype(F32)
    gates = gate_ref[0]
    la_parts = _split3(gates)
    ones_row = jnp.ones((1, LANES), F32)
    combos = []
    for d in range(2):
        incl = (si <= ci) if d == 0 else (si >= ci)
        strict = (si < ci) if d == 0 else (si > ci)
        tri = incl.astype(BF16)
        g_all = functools.reduce(lambda x, y: x + y, [jnp.dot(tri, p, preferred_element_type=F32) for p in la_parts])
        g_all_t = g_all.T
        last = c - 1 if d == 0 else 0
        for hd in range(C_HEADS):
            col = d * C_HEADS + hd
            sl = slice(hd * C_DK, (hd + 1) * C_DK)
            q, k, v = qc_ref[0, :, sl], kc_ref[0, :, sl], vc_ref[0, :, sl]
            gcol = g_all[:, col:col + 1]
            grow = g_all_t[col:col + 1, :]
            glast = g_all[last:last + 1, col:col + 1]
            beta = gates[:, 2 * C_HEADS + col: 2 * C_HEADS + col + 1]
            diff = gcol - grow
            dec_strict = jnp.exp(jnp.where(strict, diff, _NEG_INF))
            combos.append(dict(
                d=d, hd=hd, q=q, gcol=gcol, glast=glast,
                x=-(beta * _bdot_nt(k, k) * dec_strict),
                sol=jnp.concatenate([beta * v, (beta * jnp.exp(gcol)) * k], axis=1),
                qk=_bdot_nt(q, k) * jnp.exp(jnp.where(incl, diff, _NEG_INF)),
                kend=k * jnp.exp(glast - gcol)))
    levels = int(math.log2(c))
    for lvl in range(levels):
        for cb in combos:
            cb["sol"] = cb["sol"] + _dot3(cb["x"], cb["sol"])
        if lvl < levels - 1:
            for cb in combos:
                cb["x"] = _dot3(cb["x"], cb["x"])
    for cb in combos:
        d, hd, qk, kend = cb["d"], cb["hd"], cb["qk"], cb["kend"]
        u, wk = cb["sol"][:, :C_DK], cb["sol"][:, C_DK:]
        qs_ref[0, 0, d, hd] = (jnp.exp(cb["gcol"]) * cb["q"] - _bdot(qk, wk)).astype(qs_ref.dtype)
        oi_ref[0, 0, d, hd] = _bdot(qk, u)
        pp_ref[0, 0, d, hd] = _bdot_tn(kend, wk).astype(pp_ref.dtype)
        nn_ref[0, 0, d, hd] = _bdot_tn(kend, u)
        al_ref[0, 0, d, hd:hd + 1, :] = jnp.exp(cb["glast"]) * ones_row
    for d in range(2):
        incl = (si <= ci) if d == 0 else (si >= ci)
        for hd in range(D_HEADS):
            lg = _ret_log_gamma(hd)
            sl = slice(hd * D_DK, (hd + 1) * D_DK)
            q, k, v = qd_ref[0, :, sl], kd_ref[0, :, sl], vd_ref[0, :, sl]
            steps = pos if d == 0 else (c - 1.0) - pos
            dist = (ci - si) if d == 0 else (si - ci)
            dmat = jnp.exp(jnp.where(incl, lg * dist.astype(F32), _NEG_INF))
            qk = _bdot_nt(q, k) * dmat
            qs_ref[0, 0, d, C_HEADS + hd] = (q * jnp.exp(lg * (steps + 1.0))).astype(qs_ref.dtype)
            oi_ref[0, 0, d, C_HEADS + hd] = _bdot(qk, v)
            nn_ref[0, 0, d, C_HEADS + hd] = _bdot_tn(k * jnp.exp(lg * ((c - 1.0) - steps)), v)
            al_ref[0, 0, d, C_HEADS + hd:C_HEADS + hd + 1, :] = math.exp(lg * c) * ones_row


def _rec_intra(qc, kc, vc, qd, kd, vd, gate):
    bsz, lt, hw = qc.shape
    nc = lt // CHUNK
    nh = C_HEADS + D_HEADS
    row = lambda w: pl.BlockSpec((1, CHUNK, w), lambda b, n: (b, n, 0))
    lead = lambda *tail: pl.BlockSpec((1, 1, 2) + tail, lambda b, n: (b, n, 0) + (0,) * len(tail))
    return pl.pallas_call(
        _rec_intra_kernel,
        grid=(bsz, nc),
        in_specs=[row(hw)] * 6 + [row(LANES)],
        out_specs=[lead(nh, CHUNK, C_DK), lead(nh, CHUNK, C_DK), lead(C_HEADS, C_DK, C_DK), lead(nh, C_DK, C_DK),
                   lead(nh, LANES)],
        out_shape=[jax.ShapeDtypeStruct((bsz, nc, 2, nh, CHUNK, C_DK), BF16),
                   jax.ShapeDtypeStruct((bsz, nc, 2, nh, CHUNK, C_DK), F32),
                   jax.ShapeDtypeStruct((bsz, nc, 2, C_HEADS, C_DK, C_DK), BF16),
                   jax.ShapeDtypeStruct((bsz, nc, 2, nh, C_DK, C_DK), F32),
                   jax.ShapeDtypeStruct((bsz, nc, 2, nh, LANES), F32)],
        compiler_params=_cparams(("parallel", "parallel")),
        name="rec_intra",
    )(qc, kc, vc, qd, kd, vd, gate)


def _rec_scan_kernel(qs_ref, oi_ref, pp_ref, nn_ref, al_ref, o_ref, s_scr):
    @pl.when(pl.program_id(2) == 0)
    def _():
        s_scr[...] = jnp.zeros_like(s_scr)

    for hd in range(C_HEADS + D_HEADS):
        s = s_scr[hd]
        sb = s.astype(BF16)
        o_ref[0, 0, :, hd * C_DK:(hd + 1) * C_DK] = (
            jnp.dot(qs_ref[0, 0, 0, hd], sb, preferred_element_type=F32) + oi_ref[0, 0, 0, hd])
        new = al_ref[0, 0, 0, hd:hd + 1, :] * s + nn_ref[0, 0, 0, hd]
        if hd < C_HEADS:
            new = new - jnp.dot(pp_ref[0, 0, 0, hd], sb, preferred_element_type=F32)
        s_scr[hd] = new


def _rec_scan(qs, oi, pp, nn, al, L):
    bsz, nc = qs.shape[:2]
    nh = C_HEADS + D_HEADS
    n_lat = L // CHUNK
    n_ctx = nc - n_lat

    def chunk(d, s):
        fwd = jnp.where(s < n_ctx, n_lat + s, s - n_ctx)
        bwd = nc - 1 - s
        return jnp.where(d == 0, fwd, bwd)

    lead = lambda *tail: pl.BlockSpec((1, 1, 1) + tail, lambda b, d, s: (b, chunk(d, s), d) + (0,) * len(tail))
    return pl.pallas_call(
        _rec_scan_kernel,
        grid=(bsz, 2, nc),
        in_specs=[lead(nh, CHUNK, C_DK), lead(nh, CHUNK, C_DK), lead(C_HEADS, C_DK, C_DK), lead(nh, C_DK, C_DK),
                  lead(nh, LANES)],
        out_specs=pl.BlockSpec((1, 1, CHUNK, nh * C_DK), lambda b, d, s: (b, d, chunk(d, s), 0)),
        out_shape=jax.ShapeDtypeStruct((bsz, 2, nc * CHUNK, nh * C_DK), F32),
        scratch_shapes=[pltpu.VMEM((nh, C_DK, C_DK), F32)],
        compiler_params=_cparams(("parallel", "parallel", "arbitrary")),
        name="rec_scan",
    )(qs, oi, pp, nn, al)


def _rec_out_kernel(of_ref, ob_ref, z_ref, gd_ref, og_ref, gn_ref, w_ref, h_ref, mod_ref, out_ref):
    hw = C_HEADS * C_DK
    of, ob = of_ref[0, 0], ob_ref[0, 0]
    z, gd = z_ref[0], gd_ref[0]
    parts = []
    for hd in range(C_HEADS):
        sl = slice(hd * C_DK, (hd + 1) * C_DK)
        x = of[:, sl] + ob[:, sl]
        y = x * lax.rsqrt(jnp.mean(x * x, axis=-1, keepdims=True) + EPS) * og_ref[...]
        parts.append(y * _silu(z[:, sl]))
    for hd in range(D_HEADS):
        sl = slice(hd * D_DK, (hd + 1) * D_DK)
        y = 0.0
        for d, o in enumerate((of, ob)):
            x = o[:, hw + hd * D_DK: hw + (hd + 1) * D_DK]
            mu = jnp.mean(x, axis=-1, keepdims=True)
            xc = x - mu
            var = jnp.mean(xc * xc, axis=-1, keepdims=True)
            y = y + xc * lax.rsqrt(var + EPS) * gn_ref[:, sl] * _silu(gd[:, d * hw + hd * D_DK: d * hw + (hd + 1) * D_DK])
        parts.append(y)
    mix = jnp.concatenate(parts, axis=1)
    out_ref[0] = h_ref[0] + mod_ref[0, 0][2:3] * _bdot(mix, w_ref[...])


def _rec_out(o_scan, z, gd, out_g, gn_g, w_out, hh, mod, L):
    bsz, lt, d = hh.shape
    tm = ROW_TILE
    hw = C_HEADS * C_DK
    return pl.pallas_call(
        _rec_out_kernel,
        grid=(bsz, L // tm),
        in_specs=[pl.BlockSpec((1, 1, tm, 2 * hw), lambda b, i: (b, 0, i, 0)),
                  pl.BlockSpec((1, 1, tm, 2 * hw), lambda b, i: (b, 1, i, 0)),
                  pl.BlockSpec((1, tm, hw), lambda b, i: (b, i, 0)),
                  pl.BlockSpec((1, tm, 2 * hw), lambda b, i: (b, i, 0)),
                  pl.BlockSpec((1, C_DK), lambda b, i: (0, 0)),
                  pl.BlockSpec((1, hw), lambda b, i: (0, 0)),
                  pl.BlockSpec((2 * hw, d), lambda b, i: (0, 0)),
                  pl.BlockSpec((1, tm, d), lambda b, i: (b, i, 0)),
                  pl.BlockSpec((1, 1, 6, d), lambda b, i: (b, 0, 0, 0))],
        out_specs=pl.BlockSpec((1, tm, d), lambda b, i: (b, i, 0)),
        out_shape=jax.ShapeDtypeStruct(hh.shape, F32),
        input_output_aliases={7: 0},
        compiler_params=_cparams(("parallel", "parallel")),
        name="rec_out",
    )(o_scan, o_scan, z, gd, out_g.reshape(1, C_DK), gn_g.reshape(1, hw), w_out.astype(BF16), hh, mod)


def _att_layer(hh, mod, norm_g, w_in, q_g, k_g, sink, w_out, L):
    cos, sin = _rope_tables(L, hh.shape[1] - L, HEAD_DIM)
    qa, ka, va, qb, kb, vb = _att_project(hh, mod, norm_g, w_in, q_g, k_g, cos, sin, L)
    oa, ob = _attention(qa, ka, va, qb, kb, vb, sink, L)
    return _att_out(oa, ob, w_out, hh, mod, L)


def _rec_layer(hh, mod, norm_g, w_in, conv_w, a_log, dt_bias, out_g, gn_g, w_out, L):
    cos, sin = _rope_tables(L, hh.shape[1] - L, D_DK)
    qkv, z, qd, kd, vd, gd, gate = _rec_project(hh, mod, norm_g, w_in, a_log, dt_bias, cos, sin, L)
    qc, kc, vc = _rec_conv(qkv, conv_w, L)
    qs, oi, pp, nn, al = _rec_intra(qc, kc, vc, qd, kd, vd, gate)
    o_scan = _rec_scan(qs, oi, pp, nn, al, L)
    return _rec_out(o_scan, z, gd, out_g, gn_g, w_out, hh, mod, L)


def kernel(x, c, ctx, c_ctx, mod_w, mod_b, norm1_g, norm2_g, att_w_in, att_q_norm, att_k_norm, att_sink, att_w_out,
           rec_w_in, rec_conv_w, rec_a_log, rec_dt_bias, rec_out_norm, rec_gn_g, rec_w_out,
           peer_w_q, peer_sub_keys, peer_u, peer_v, final_norm_g):
    L = x.shape[1]
    depth = mod_w.shape[0]
    mods = _modulation(c, c_ctx, mod_w, mod_b)
    hh = jnp.concatenate([x, ctx], axis=1)
    for layer in range(depth):
        last = layer == depth - 1
        i = layer // 2
        if layer % 2 == 0:
            hh = _att_layer(hh, mods[layer], norm1_g[layer], att_w_in[i], att_q_norm[i], att_k_norm[i], att_sink[i],
                            att_w_out[i], L)
        else:
            hh = _rec_layer(hh, mods[layer], norm1_g[layer], rec_w_in[i], rec_conv_w[i], rec_a_log[i], rec_dt_bias[i],
                            rec_out_norm[i], rec_gn_g[i], rec_w_out[i], L)
        hh = _peer(hh, mods[layer], norm2_g[layer], peer_w_q[layer], peer_sub_keys[layer], peer_u[layer],
                   peer_v[layer], final_norm_g, L, with_ctx=not last, final=last)
    return hh
```

```python
import functools
import math

import numpy as np
import jax
import jax.numpy as jnp
from jax import lax
from jax.experimental import pallas as pl
from jax.experimental.pallas import tpu as pltpu

F32 = jnp.float32
BF16 = jnp.bfloat16

GRID_W = 64
EPS = 1e-6
HEAD_DIM = 64
A_HEADS = 8
A_KV = 2
B_HEADS = 8
B_KV = 2
WINDOW = 128
ROPE_THETA = 10000.0
C_HEADS = 4
C_DK = 128
CONV_W = 5
CHUNK = 64
D_HEADS = 4
D_DK = 128
RET_DECAY_BASE = 5.0
PEER_HEADS = 8
N_KEYS = 128
PEER_TOPK = 16

LANES = 128
VMEM_LIMIT = 56 * 1024 * 1024

ROW_TILE = 256
ATT_TQ = 128
PEER_TM = 256
PEER_EB = 2048
PEER_SUB = 256

_NEG_INF = float("-inf")


def _cparams(sem, flags=None):
    return pltpu.CompilerParams(dimension_semantics=sem, vmem_limit_bytes=VMEM_LIMIT, flags=flags)


def _bdot(a, b):
    return jnp.dot(a.astype(BF16), b.astype(BF16), preferred_element_type=F32)


def _bdot_nt(a, b):
    return lax.dot_general(a.astype(BF16), b.astype(BF16), (((1,), (1,)), ((), ())), preferred_element_type=F32)


def _bdot_tn(a, b):
    return jnp.dot(a.T.astype(BF16), b.astype(BF16), preferred_element_type=F32)


def _split3(a):
    hi = a.astype(BF16)
    r1 = a - hi.astype(F32)
    mid = r1.astype(BF16)
    lo = (r1 - mid.astype(F32)).astype(BF16)
    return hi, mid, lo


def _dot3(a, b):
    a_hi, a_lo, _ = _split3(a)
    b_hi, b_lo, _ = _split3(b)
    d = functools.partial(jnp.dot, preferred_element_type=F32)
    return d(a_hi, b_hi) + (d(a_hi, b_lo) + d(a_lo, b_hi))


def _sigmoid(x):
    return 1.0 / (1.0 + jnp.exp(-x))


def _silu(x):
    return x * _sigmoid(x)


def _norm_mod(x, g, shift, scale):
    r = lax.rsqrt(jnp.mean(x * x, axis=-1, keepdims=True) + EPS)
    return (x * r * g) * (1.0 + scale) + shift


def _rope(x, cos, sin_signed, head_dim):
    quarter = head_dim // 4
    lane = lax.broadcasted_iota(jnp.int32, x.shape, 1)
    first = (lane % (2 * quarter)) < quarter
    partner = jnp.where(first, pltpu.roll(x, LANES - quarter, 1), pltpu.roll(x, quarter, 1))
    return x * cos + partner * sin_signed


def _mod_kernel(c_ref, w_ref, b_ref, o_ref):
    o_ref[0] = _bdot(_silu(c_ref[...]), w_ref[0]) + b_ref[0]


def _modulation(c, c_ctx, mod_w, mod_b):
    depth, d, n = mod_w.shape
    bsz = c.shape[0]
    rows = 16
    cc = jnp.zeros((rows, d), F32).at[:bsz].set(c).at[bsz].set(c_ctx)
    tn = 1536
    out = pl.pallas_call(
        _mod_kernel,
        grid=(depth, n // tn),
        in_specs=[pl.BlockSpec((rows, d), lambda l, j: (0, 0)),
                  pl.BlockSpec((1, d, tn), lambda l, j: (l, 0, j)),
                  pl.BlockSpec((1, 1, tn), lambda l, j: (l, 0, j))],
        out_specs=pl.BlockSpec((1, rows, tn), lambda l, j: (l, 0, j)),
        out_shape=jax.ShapeDtypeStruct((depth, rows, n), F32),
        compiler_params=_cparams(("arbitrary", "arbitrary")),
        name="modulation",
    )(cc, mod_w, mod_b.reshape(depth, 1, n))
    lat = out[:, :bsz].reshape(depth, bsz, 1, 6, d)
    ctx = jnp.broadcast_to(out[:, bsz].reshape(depth, 1, 1, 6, d), (depth, bsz, 1, 6, d))
    return jnp.concatenate([lat, ctx], axis=2)


def _rope_tables(L, LC, head_dim):
    quarter, half = head_dim // 4, head_dim // 2
    freqs = ROPE_THETA ** (-jnp.arange(quarter, dtype=F32) / quarter)
    lane = np.arange(LANES)
    within = lane % head_dim
    use_col = within >= half
    fidx = within % quarter
    sign = np.where((within % half) < quarter, -1.0, 1.0).astype(np.float32)
    t = jnp.arange(L, dtype=jnp.int32)
    row, col = (t // GRID_W).astype(F32), (t % GRID_W).astype(F32)
    pos = jnp.where(use_col[None, :], col[:, None], row[:, None])
    ang = pos * freqs[fidx][None, :]
    cos = jnp.concatenate([jnp.cos(ang), jnp.ones((LC, LANES), F32)], axis=0)
    sin = jnp.concatenate([jnp.sin(ang) * sign[None, :], jnp.zeros((LC, LANES), F32)], axis=0)
    return cos, sin


def _att_proj_kernel(h_ref, mod_ref, g_ref, w_ref, qg_ref, kg_ref, cos_ref, sin_ref, gm_ref,
                     qa_ref, ka_ref, va_ref, qb_ref, kb_ref, vb_ref):
    m = mod_ref[0, 0]
    a = _norm_mod(h_ref[0], g_ref[...], m[0:1], m[1:2])
    o = _bdot(a, w_ref[...])
    cos, sin = cos_ref[...], sin_ref[...]
    gm = gm_ref[...]

    def head_norm(x, gain):
        sq = x * x
        hi = sq.astype(BF16)
        lo = (sq - hi.astype(F32)).astype(BF16)
        ms = jnp.dot(hi, gm, preferred_element_type=F32) + jnp.dot(lo, gm, preferred_element_type=F32)
        return x * lax.rsqrt(ms + EPS) * gain

    def put(ref, first_head, x):
        ref[0, first_head] = x[:, :HEAD_DIM].astype(ref.dtype)
        ref[0, first_head + 1] = x[:, HEAD_DIM:].astype(ref.dtype)

    scale = HEAD_DIM ** -0.5
    qa_w = A_HEADS * HEAD_DIM
    kv_w = A_KV * HEAD_DIM
    off = 0
    for c in range(qa_w // LANES):
        x = o[:, off + c * LANES: off + (c + 1) * LANES]
        put(qa_ref, 2 * c, _rope(head_norm(x, qg_ref[...]), cos, sin, HEAD_DIM) * scale)
    off += qa_w
    put(ka_ref, 0, _rope(head_norm(o[:, off: off + kv_w], kg_ref[...]), cos, sin, HEAD_DIM))
    off += kv_w
    put(va_ref, 0, o[:, off: off + kv_w])
    off += kv_w
    for c in range(qa_w // LANES):
        x = o[:, off + c * LANES: off + (c + 1) * LANES]
        put(qb_ref, 2 * c, _rope(x, cos, sin, HEAD_DIM) * scale)
    off += qa_w
    put(kb_ref, 0, _rope(o[:, off: off + kv_w], cos, sin, HEAD_DIM))
    off += kv_w
    put(vb_ref, 0, o[:, off: off + kv_w])


def _att_project(hh, mod, norm_g, w_in, q_g, k_g, cos, sin, L):
    bsz, lt, d = hh.shape
    tm = ROW_TILE
    n = w_in.shape[1]
    gm = jnp.asarray(np.kron(np.eye(LANES // HEAD_DIM), np.full((HEAD_DIM, HEAD_DIM), 1.0 / HEAD_DIM)), BF16)
    tile2 = lambda v: jnp.tile(v.reshape(1, HEAD_DIM), (1, LANES // HEAD_DIM))
    qshape = jax.ShapeDtypeStruct((bsz, A_HEADS, lt, HEAD_DIM), BF16)
    kshape = jax.ShapeDtypeStruct((bsz, A_KV, lt, HEAD_DIM), BF16)
    qspec = pl.BlockSpec((1, A_HEADS, tm, HEAD_DIM), lambda b, i: (b, 0, i, 0))
    kspec = pl.BlockSpec((1, A_KV, tm, HEAD_DIM), lambda b, i: (b, 0, i, 0))
    const = lambda shape: pl.BlockSpec(shape, lambda b, i: (0,) * len(shape))
    return pl.pallas_call(
        _att_proj_kernel,
        grid=(bsz, lt // tm),
        in_specs=[pl.BlockSpec((1, tm, d), lambda b, i: (b, i, 0)),
                  pl.BlockSpec((1, 1, 6, d), lambda b, i: (b, i // (L // tm), 0, 0)),
                  const((1, d)), const((d, n)), const((1, LANES)), const((1, LANES)),
                  pl.BlockSpec((tm, LANES), lambda b, i: (i, 0)),
                  pl.BlockSpec((tm, LANES), lambda b, i: (i, 0)),
                  const((LANES, LANES))],
        out_specs=[qspec, kspec, kspec, qspec, kspec, kspec],
        out_shape=[qshape, kshape, kshape, qshape, kshape, kshape],
        compiler_params=_cparams(("parallel", "parallel")),
        name="att_project",
    )(hh, mod, norm_g.reshape(1, d), w_in.astype(BF16), tile2(q_g), tile2(k_g), cos, sin, gm)


def _softmax_pv(score_parts, value_parts, extra_logit=None):
    m = functools.reduce(jnp.maximum, [jnp.max(s, axis=-1, keepdims=True) for s in score_parts])
    if extra_logit is not None:
        m = jnp.maximum(m, extra_logit)
    l = 0.0
    acc = 0.0
    for s, v in zip(score_parts, value_parts):
        p = jnp.exp(s - m)
        l = l + jnp.sum(p, axis=-1, keepdims=True)
        acc = acc + jnp.dot(p.astype(BF16), v, preferred_element_type=F32)
    if extra_logit is not None:
        l = l + jnp.exp(extra_logit - m)
    return acc / l


def _att_kernel(sink_ref, qa_ref, ka_ref, va_ref, qb_ref, kb_ref, vb_ref, oa_ref, ob_ref, *, L, LC):
    i = pl.program_id(1)
    tq = ATT_TQ
    group = A_HEADS // A_KV
    band = tq + 2 * WINDOW

    def stacked_q(ref, kvh):
        return ref[0, kvh * group:(kvh + 1) * group].reshape(group * tq, HEAD_DIM)

    def put(ref, kvh, o):
        for g in range(group):
            hd = kvh * group + g
            ref[0, :, hd * HEAD_DIM:(hd + 1) * HEAD_DIM] = o[g * tq:(g + 1) * tq].astype(ref.dtype)

    def sink_col(kvh):
        return jnp.concatenate([jnp.full((tq, 1), sink_ref[kvh * group + g], F32) for g in range(group)], axis=0)

    @pl.when(i < L // tq)
    def _latent():
        for kvh in range(A_KV):
            q = stacked_q(qa_ref, kvh)
            put(oa_ref, kvh, _softmax_pv([_bdot_nt(q, ka_ref[0, kvh])], [va_ref[0, kvh]]))
            q = stacked_q(qb_ref, kvh)
            start = pl.multiple_of(jnp.clip((i - 1) * tq, 0, L - band), tq)
            s_band = _bdot_nt(q, kb_ref[0, kvh, pl.ds(start, band), :])
            qpos = i * tq + (lax.broadcasted_iota(jnp.int32, s_band.shape, 0) % tq)
            kpos = start + lax.broadcasted_iota(jnp.int32, s_band.shape, 1)
            s_band = jnp.where(jnp.abs(qpos - kpos) <= WINDOW, s_band, _NEG_INF)
            s_ctx = _bdot_nt(q, kb_ref[0, kvh, L:L + LC, :])
            put(ob_ref, kvh, _softmax_pv([s_band, s_ctx],
                                         [vb_ref[0, kvh, pl.ds(start, band), :], vb_ref[0, kvh, L:L + LC, :]],
                                         sink_col(kvh)))

    @pl.when(i >= L // tq)
    def _context():
        for kvh in range(A_KV):
            q = stacked_q(qa_ref, kvh)
            put(oa_ref, kvh, _softmax_pv([_bdot_nt(q, ka_ref[0, kvh, L:L + LC, :])], [va_ref[0, kvh, L:L + LC, :]]))
            q = stacked_q(qb_ref, kvh)
            put(ob_ref, kvh, _softmax_pv([_bdot_nt(q, kb_ref[0, kvh, L:L + LC, :])], [vb_ref[0, kvh, L:L + LC, :]],
                                         sink_col(kvh)))


def _attention(qa, ka, va, qb, kb, vb, sink, L):
    bsz, _, lt, _ = qa.shape
    tq = ATT_TQ
    qspec = pl.BlockSpec((1, A_HEADS, tq, HEAD_DIM), lambda b, i: (b, 0, i, 0))
    kspec = pl.BlockSpec((1, A_KV, lt, HEAD_DIM), lambda b, i: (b, 0, 0, 0))
    ospec = pl.BlockSpec((1, tq, A_HEADS * HEAD_DIM), lambda b, i: (b, i, 0))
    oshape = jax.ShapeDtypeStruct((bsz, lt, A_HEADS * HEAD_DIM), BF16)
    return pl.pallas_call(
        functools.partial(_att_kernel, L=L, LC=lt - L),
        grid=(bsz, lt // tq),
        in_specs=[pl.BlockSpec(memory_space=pltpu.SMEM), qspec, kspec, kspec, qspec, kspec, kspec],
        out_specs=[ospec, ospec],
        out_shape=[oshape, oshape],
        compiler_params=_cparams(("parallel", "parallel")),
        name="attention",
    )(sink, qa, ka, va, qb, kb, vb)


def _att_out_kernel(oa_ref, ob_ref, w_ref, h_ref, mod_ref, out_ref):
    half = oa_ref.shape[-1]
    y = (jnp.dot(oa_ref[0], w_ref[:half, :], preferred_element_type=F32)
         + jnp.dot(ob_ref[0], w_ref[half:, :], preferred_element_type=F32))
    out_ref[0] = h_ref[0] + mod_ref[0, 0][2:3] * y


def _att_out(oa, ob, w_out, hh, mod, L):
    bsz, lt, d = hh.shape
    tm = ROW_TILE
    half = oa.shape[-1]
    return pl.pallas_call(
        _att_out_kernel,
        grid=(bsz, lt // tm),
        in_specs=[pl.BlockSpec((1, tm, half), lambda b, i: (b, i, 0)),
                  pl.BlockSpec((1, tm, half), lambda b, i: (b, i, 0)),
                  pl.BlockSpec((2 * half, d), lambda b, i: (0, 0)),
                  pl.BlockSpec((1, tm, d), lambda b, i: (b, i, 0)),
                  pl.BlockSpec((1, 1, 6, d), lambda b, i: (b, i // (L // tm), 0, 0))],
        out_specs=pl.BlockSpec((1, tm, d), lambda b, i: (b, i, 0)),
        out_shape=jax.ShapeDtypeStruct(hh.shape, F32),
        input_output_aliases={3: 0},
        compiler_params=_cparams(("parallel", "parallel")),
        name="att_out",
    )(oa, ob, w_out.astype(BF16), hh, mod)


def _sort_network(n):
    pairs = []
    p = 1
    while p < n:
        k = p
        while k >= 1:
            for j in range(k % p, n - k, 2 * k):
                for i in range(min(k, n - j - k)):
                    if (i + j) // (2 * p) == (i + j + k) // (2 * p):
                        pairs.append((i + j, i + j + k))
            k //= 2
        p *= 2
    return pairs


def _pop_sorted(lists, n, singles=()):
    lists, singles = list(lists), list(singles)
    out = []
    for rnd in range(n):
        head = functools.reduce(jnp.maximum, [lists[0]] + singles)
        m = jnp.max(head, axis=0, keepdims=True)
        out.append(m)
        if rnd == n - 1:
            break
        hit = lists[0] == m
        depth = min(len(lists), n - rnd - 1)
        for d in range(depth):
            nxt = lists[d + 1] if d + 1 < len(lists) else _NEG_INF
            lists[d] = jnp.where(hit, nxt, lists[d])
        singles = [jnp.where(s == m, _NEG_INF, s) for s in singles]
    return out


def _top_values(x, n):
    slabs = [x[r * 8:(r + 1) * 8] for r in range(x.shape[0] // 8)]
    for i, j in _sort_network(len(slabs)):
        slabs[i], slabs[j] = jnp.maximum(slabs[i], slabs[j]), jnp.minimum(slabs[i], slabs[j])
    return _pop_sorted(slabs, n)


def _peer_route_kernel(h_ref, mod_ref, g_ref, wq_ref, keys_ref, at_ref, thr_ref, e0_ref, s1_ref, e1_ref, q_scr):
    m = mod_ref[0, 0]
    a = _norm_mod(h_ref[0], g_ref[...], m[3:4], m[4:5])
    at_ref[0] = a.T.astype(BF16)
    q_scr[...] = jnp.dot(a.astype(BF16), wq_ref[...], preferred_element_type=F32)
    k = PEER_TOPK
    n = k + 1

    def head(hd, carry):
        col = pl.multiple_of(hd * 2 * N_KEYS, 2 * N_KEYS)
        s0 = _bdot_nt(keys_ref[0], q_scr[:, pl.ds(col, N_KEYS)])
        s1 = _bdot_nt(keys_ref[1], q_scr[:, pl.ds(col + N_KEYS, N_KEYS)])
        tokens = s0.shape[1]
        top0 = _top_values(s0, n)
        top1 = _top_values(s1, n)
        first = jnp.concatenate(top0 + [jnp.full((24 - n, tokens), _NEG_INF, F32)], axis=0)
        best = _pop_sorted([first[0:8] + t for t in top1], n, [first[8:16] + top1[0], first[16:24] + top1[0]])
        z = functools.reduce(lambda x, y: x + y, [jnp.exp(b - best[0]) for b in best[:k]])
        tau = 0.5 * (best[k - 1] + best[k])
        thr_ref[0, hd] = tau - s0
        e0_ref[0, hd] = jnp.exp(s0 - top0[0])
        s1_ref[0, hd] = s1
        e1_ref[0, hd] = jnp.exp(s1 - top1[0]) / z
        return carry

    lax.fori_loop(0, PEER_HEADS, head, 0)


def _peer_route(hh, mod, norm_g, w_q, sub_keys, L):
    bsz, lt, d = hh.shape
    tm = ROW_TILE
    nq = w_q.shape[1]
    tok = lambda: pl.BlockSpec((1, PEER_HEADS, N_KEYS, tm), lambda b, i: (b, 0, 0, i))
    tshape = jax.ShapeDtypeStruct((bsz, PEER_HEADS, N_KEYS, lt), F32)
    return pl.pallas_call(
        _peer_route_kernel,
        grid=(bsz, lt // tm),
        in_specs=[pl.BlockSpec((1, tm, d), lambda b, i: (b, i, 0)),
                  pl.BlockSpec((1, 1, 6, d), lambda b, i: (b, i // (L // tm), 0, 0)),
                  pl.BlockSpec((1, d), lambda b, i: (0, 0)),
                  pl.BlockSpec((d, nq), lambda b, i: (0, 0)),
                  pl.BlockSpec((2, N_KEYS, N_KEYS), lambda b, i: (0, 0, 0))],
        out_specs=[pl.BlockSpec((1, d, tm), lambda b, i: (b, 0, i)), tok(), tok(), tok(), tok()],
        out_shape=[jax.ShapeDtypeStruct((bsz, d, lt), BF16), tshape, tshape, tshape, tshape],
        scratch_shapes=[pltpu.VMEM((tm, nq), F32)],
        compiler_params=_cparams(("parallel", "parallel")),
        name="peer_route",
    )(hh, mod, norm_g.reshape(1, d), w_q.astype(BF16), sub_keys.astype(BF16))


def _gelu(x):
    return 0.5 * x * (1.0 + lax.erf(x * (1.0 / math.sqrt(2.0))))


def _peer_expert_kernel(a_ref, thr_ref, e0_ref, s1_ref, e1_ref, u_ref, vt_ref, h_ref, mod_ref, fg_ref,
                        out_ref, yt_scr, wa_scr, *, final):
    e = pl.program_id(2)
    tm = a_ref.shape[2]

    @pl.when(e == 0)
    def _():
        yt_scr[...] = jnp.zeros_like(yt_scr)

    y = None
    for q in range(PEER_EB // PEER_SUB):
        rows = slice(q * PEER_SUB, (q + 1) * PEER_SUB)
        act = jnp.dot(u_ref[rows, :], a_ref[0], preferred_element_type=F32)
        for il in range(PEER_SUB // N_KEYS):
            key0 = q * (PEER_SUB // N_KEYS) + il
            for c in range(tm // LANES):
                cs = slice(c * LANES, (c + 1) * LANES)
                gate = None
                for hd in range(PEER_HEADS):
                    weight = e0_ref[0, hd, key0:key0 + 1, cs] * e1_ref[0, hd, :, cs]
                    part = jnp.where(s1_ref[0, hd, :, cs] >= thr_ref[0, hd, key0:key0 + 1, cs], weight, 0.0)
                    gate = part if gate is None else gate + part
                wa_scr[q * PEER_SUB + il * N_KEYS: q * PEER_SUB + (il + 1) * N_KEYS, cs] = (
                    gate * _gelu(act[il * N_KEYS:(il + 1) * N_KEYS, cs])).astype(BF16)
        part = jnp.dot(vt_ref[:, rows], wa_scr[rows, :], preferred_element_type=F32)
        y = part if y is None else y + part
    yt_scr[...] += y

    @pl.when(e == pl.num_programs(2) - 1)
    def _():
        hn = h_ref[0] + mod_ref[0, 0][5:6] * yt_scr[...].T
        if final:
            hn = hn * lax.rsqrt(jnp.mean(hn * hn, axis=-1, keepdims=True) + EPS) * fg_ref[...]
        out_ref[0] = hn


def _peer_expert(hh, mod, a, thr, e0, s1, e1, u_bf, vt_bf, final_g, L, *, with_ctx, final):
    bsz, lt, d = hh.shape
    n_exp = u_bf.shape[0]
    tm = PEER_TM
    n_tiles = (lt if with_ctx else L) // tm
    keys_per_step = PEER_EB // N_KEYS
    tok = lambda r: pl.BlockSpec((1, PEER_HEADS, r, tm), (lambda b, t, e: (b, 0, e, t)) if r == keys_per_step
                                 else (lambda b, t, e: (b, 0, 0, t)))
    row_spec = pl.BlockSpec((1, tm, d), lambda b, t, e: (b, t, 0))
    if final:
        out_shape = jax.ShapeDtypeStruct((bsz, L, d), F32)
        aliases = {}
    else:
        out_shape = jax.ShapeDtypeStruct(hh.shape, F32)
        aliases = {7: 0}
    return pl.pallas_call(
        functools.partial(_peer_expert_kernel, final=final),
        grid=(bsz, n_tiles, n_exp // PEER_EB),
        in_specs=[pl.BlockSpec((1, d, tm), lambda b, t, e: (b, 0, t)),
                  tok(keys_per_step), tok(keys_per_step), tok(N_KEYS), tok(N_KEYS),
                  pl.BlockSpec((PEER_EB, d), lambda b, t, e: (e, 0)),
                  pl.BlockSpec((d, PEER_EB), lambda b, t, e: (0, e)),
                  row_spec,
                  pl.BlockSpec((1, 1, 6, d), lambda b, t, e: (b, t // (L // tm), 0, 0)),
                  pl.BlockSpec((1, d), lambda b, t, e: (0, 0))],
        out_specs=row_spec,
        out_shape=out_shape,
        input_output_aliases=aliases,
        scratch_shapes=[pltpu.VMEM((d, tm), F32), pltpu.VMEM((PEER_EB, tm), BF16)],
        compiler_params=_cparams(("parallel", "parallel", "arbitrary")),
        name="peer_expert",
    )(a, thr, e0, s1, e1, u_bf, vt_bf, hh, mod, final_g.reshape(1, d))


def _peer(hh, mod, norm_g, w_q, sub_keys, u, v, final_g, L, *, with_ctx, final):
    a, thr, e0, s1, e1 = _peer_route(hh, mod, norm_g, w_q, sub_keys, L)
    return _peer_expert(hh, mod, a, thr, e0, s1, e1, u.astype(BF16), v.astype(BF16).T, final_g, L,
                        with_ctx=with_ctx, final=final)


REC_QKV = C_HEADS * 3 * C_DK
REC_Z = C_HEADS * C_DK
REC_GATES = 4 * C_HEADS
REC_HD = D_HEADS * D_DK


def _rec_proj_kernel(h_ref, mod_ref, g_ref, w_ref, alog_ref, dtb_ref, cos_ref, sin_ref,
                     qkv_ref, z_ref, qd_ref, kd_ref, vd_ref, gd_ref, gate_ref):
    m = mod_ref[0, 0]
    a = _norm_mod(h_ref[0], g_ref[...], m[0:1], m[1:2])
    o = _bdot(a, w_ref[...])
    cos, sin = cos_ref[...], sin_ref[...]
    off = 0
    qkv_ref[0] = o[:, off:off + REC_QKV]
    off += REC_QKV
    z_ref[0] = o[:, off:off + REC_Z]
    off += REC_Z
    for hd in range(D_HEADS):
        qd_ref[0, :, hd * D_DK:(hd + 1) * D_DK] = _rope(o[:, off + hd * D_DK: off + (hd + 1) * D_DK], cos, sin, D_DK)
    off += REC_HD
    for hd in range(D_HEADS):
        kd_ref[0, :, hd * D_DK:(hd + 1) * D_DK] = (
            _rope(o[:, off + hd * D_DK: off + (hd + 1) * D_DK], cos, sin, D_DK) * (D_DK ** -0.5))
    off += REC_HD
    vd_ref[0] = o[:, off:off + REC_HD]
    off += REC_HD
    gd_ref[0] = o[:, off:off + 2 * REC_HD]
    off += 2 * REC_HD
    x = o[:, off:off + LANES]
    lane = lax.broadcasted_iota(jnp.int32, x.shape, 1)
    xb = x + dtb_ref[...]
    softplus = jnp.maximum(xb, 0.0) + jnp.log(1.0 + jnp.exp(-jnp.abs(xb)))
    gate_ref[0] = jnp.where(lane < 2 * C_HEADS, -jnp.exp(alog_ref[...]) * softplus, _sigmoid(x))


def _rec_project(hh, mod, norm_g, w_in, a_log, dt_bias, cos, sin, L):
    bsz, lt, d = hh.shape
    tm = ROW_TILE
    parts = np.cumsum([REC_QKV, REC_Z, REC_GATES, REC_HD, REC_HD, REC_HD])
    qkv_w, z_w, gates_w, qd_w, kd_w, vd_w, gd_w = jnp.split(w_in, [int(p) for p in parts], axis=1)
    w = jnp.concatenate([qkv_w, z_w, qd_w, kd_w, vd_w, gd_w, gates_w, jnp.zeros((d, LANES - REC_GATES), F32)],
                        axis=1).astype(BF16)
    n = w.shape[1]
    pad = lambda p: jnp.zeros((1, LANES), F32).at[0, :2 * C_HEADS].set(p.reshape(-1))
    widths = [REC_QKV, REC_Z, REC_HD, REC_HD, REC_HD, 2 * REC_HD, LANES]
    const = lambda shape: pl.BlockSpec(shape, lambda b, i: (0,) * len(shape))
    return pl.pallas_call(
        _rec_proj_kernel,
        grid=(bsz, lt // tm),
        in_specs=[pl.BlockSpec((1, tm, d), lambda b, i: (b, i, 0)),
                  pl.BlockSpec((1, 1, 6, d), lambda b, i: (b, i // (L // tm), 0, 0)),
                  const((1, d)), const((d, n)), const((1, LANES)), const((1, LANES)),
                  pl.BlockSpec((tm, LANES), lambda b, i: (i, 0)),
                  pl.BlockSpec((tm, LANES), lambda b, i: (i, 0))],
        out_specs=[pl.BlockSpec((1, tm, wd), lambda b, i: (b, i, 0)) for wd in widths],
        out_shape=[jax.ShapeDtypeStruct((bsz, lt, wd), F32) for wd in widths],
        compiler_params=_cparams(("parallel", "parallel")),
        name="rec_project",
    )(hh, mod, norm_g.reshape(1, d), w, pad(a_log), pad(dt_bias), cos, sin)


def _rec_conv_kernel(x_ref, prev_ref, next_ref, w_ref, q_ref, k_ref, v_ref, *, L):
    i = pl.program_id(1)
    tl = x_ref.shape[1]
    n_lat = L // tl
    at_start = (i == 0) | (i == n_lat)
    at_end = (i == n_lat - 1) | (i == pl.num_programs(1) - 1)
    prev = jnp.where(at_start, 0.0, prev_ref[0])
    nxt = jnp.where(at_end, 0.0, next_ref[0])
    xx = jnp.concatenate([prev, x_ref[0], nxt], axis=0)
    n = tl + 16
    acc = 0.0
    for tap in range(CONV_W):
        shift = (CONV_W // 2 - tap) % n
        shifted = xx if shift == 0 else pltpu.roll(xx, shift, 0)
        acc = acc + shifted[8:8 + tl] * w_ref[tap:tap + 1, :]
    y = _silu(acc)
    hw = C_HEADS * C_DK
    for hd in range(C_HEADS):
        def l2(x):
            return x * lax.rsqrt(jnp.sum(x * x, axis=-1, keepdims=True) + EPS)
        sl = slice(hd * C_DK, (hd + 1) * C_DK)
        q_ref[0, :, sl] = l2(y[:, hd * C_DK:(hd + 1) * C_DK]) * (C_DK ** -0.5)
        k_ref[0, :, sl] = l2(y[:, hw + hd * C_DK: hw + (hd + 1) * C_DK])
    v_ref[0] = y[:, 2 * hw:]


def _rec_conv(qkv, conv_w, L):
    bsz, lt, ch = qkv.shape
    tl = ROW_TILE
    hb = tl // 8
    last = lt // 8 - 1
    hw = C_HEADS * C_DK
    return pl.pallas_call(
        functools.partial(_rec_conv_kernel, L=L),
        grid=(bsz, lt // tl),
        in_specs=[pl.BlockSpec((1, tl, ch), lambda b, i: (b, i, 0)),
                  pl.BlockSpec((1, 8, ch), lambda b, i: (b, jnp.maximum(i * hb - 1, 0), 0)),
                  pl.BlockSpec((1, 8, ch), lambda b, i: (b, jnp.minimum((i + 1) * hb, last), 0)),
                  pl.BlockSpec((8, ch), lambda b, i: (0, 0))],
        out_specs=[pl.BlockSpec((1, tl, hw), lambda b, i: (b, i, 0))] * 3,
        out_shape=[jax.ShapeDtypeStruct((bsz, lt, hw), F32)] * 3,
        compiler_params=_cparams(("parallel", "parallel")),
        name="rec_conv",
    )(qkv, qkv, qkv, jnp.zeros((8, ch), F32).at[:CONV_W].set(conv_w))


def _ret_log_gamma(hd):
    return float(np.log1p(-np.exp2(-(RET_DECAY_BASE + hd))))


def _rec_intra_kernel(qc_ref, kc_ref, vc_ref, qd_ref, kd_ref, vd_ref, gate_ref,
                      qs_ref, oi_ref, pp_ref, nn_ref, al_ref):
    c = CHUNK
    ci = lax.broadcasted_iota(jnp.int32, (c, c), 0)
    si = lax.broadcasted_iota(jnp.int32, (c, c), 1)
    pos = lax.broadcasted_iota(jnp.int32, (c, 1), 0).astype(F32)
    gates = gate_ref[0]
    la_parts = _split3(gates)
    ones_row = jnp.ones((1, LANES), F32)
    combos = []
    for d in range(2):
        incl = (si <= ci) if d == 0 else (si >= ci)
        strict = (si < ci) if d == 0 else (si > ci)
        tri = incl.astype(BF16)
        g_all = functools.reduce(lambda x, y: x + y, [jnp.dot(tri, p, preferred_element_type=F32) for p in la_parts])
        g_all_t = g_all.T
        last = c - 1 if d == 0 else 0
        for hd in range(C_HEADS):
            col = d * C_HEADS + hd
            sl = slice(hd * C_DK, (hd + 1) * C_DK)
            q, k, v = qc_ref[0, :, sl], kc_ref[0, :, sl], vc_ref[0, :, sl]
            gcol = g_all[:, col:col + 1]
            grow = g_all_t[col:col + 1, :]
            glast = g_all[last:last + 1, col:col + 1]
            beta = gates[:, 2 * C_HEADS + col: 2 * C_HEADS + col + 1]
            diff = gcol - grow
            dec_strict = jnp.exp(jnp.where(strict, diff, _NEG_INF))
            combos.append(dict(
                d=d, hd=hd, q=q, gcol=gcol, glast=glast,
                x=-(beta * _bdot_nt(k, k) * dec_strict),
                sol=jnp.concatenate([beta * v, (beta * jnp.exp(gcol)) * k], axis=1),
                qk=_bdot_nt(q, k) * jnp.exp(jnp.where(incl, diff, _NEG_INF)),
                kend=k * jnp.exp(glast - gcol)))
    levels = int(math.log2(c))
    for lvl in range(levels):
        for cb in combos:
            cb["sol"] = cb["sol"] + _dot3(cb["x"], cb["sol"])
        if lvl < levels - 1:
            for cb in combos:
                cb["x"] = _dot3(cb["x"], cb["x"])
    for cb in combos:
        d, hd, qk, kend = cb["d"], cb["hd"], cb["qk"], cb["kend"]
        u, wk = cb["sol"][:, :C_DK], cb["sol"][:, C_DK:]
        qs_ref[0, 0, d, hd] = (jnp.exp(cb["gcol"]) * cb["q"] - _bdot(qk, wk)).astype(qs_ref.dtype)
        oi_ref[0, 0, d, hd] = _bdot(qk, u)
        pp_ref[0, 0, d, hd] = _bdot_tn(kend, wk).astype(pp_ref.dtype)
        nn_ref[0, 0, d, hd] = _bdot_tn(kend, u)
        al_ref[0, 0, d, hd:hd + 1, :] = jnp.exp(cb["glast"]) * ones_row
    for d in range(2):
        incl = (si <= ci) if d == 0 else (si >= ci)
        for hd in range(D_HEADS):
            lg = _ret_log_gamma(hd)
            sl = slice(hd * D_DK, (hd + 1) * D_DK)
            q, k, v = qd_ref[0, :, sl], kd_ref[0, :, sl], vd_ref[0, :, sl]
            steps = pos if d == 0 else (c - 1.0) - pos
            dist = (ci - si) if d == 0 else (si - ci)
            dmat = jnp.exp(jnp.where(incl, lg * dist.astype(F32), _NEG_INF))
            qk = _bdot_nt(q, k) * dmat
            qs_ref[0, 0, d, C_HEADS + hd] = (q * jnp.exp(lg * (steps + 1.0))).astype(qs_ref.dtype)
            oi_ref[0, 0, d, C_HEADS + hd] = _bdot(qk, v)
            nn_ref[0, 0, d, C_HEADS + hd] = _bdot_tn(k * jnp.exp(lg * ((c - 1.0) - steps)), v)
            al_ref[0, 0, d, C_HEADS + hd:C_HEADS + hd + 1, :] = math.exp(lg * c) * ones_row


def _rec_intra(qc, kc, vc, qd, kd, vd, gate):
    bsz, lt, hw = qc.shape
    nc = lt // CHUNK
    nh = C_HEADS + D_HEADS
    row = lambda w: pl.BlockSpec((1, CHUNK, w), lambda b, n: (b, n, 0))
    lead = lambda *tail: pl.BlockSpec((1, 1, 2) + tail, lambda b, n: (b, n, 0) + (0,) * len(tail))
    return pl.pallas_call(
        _rec_intra_kernel,
        grid=(bsz, nc),
        in_specs=[row(hw)] * 6 + [row(LANES)],
        out_specs=[lead(nh, CHUNK, C_DK), lead(nh, CHUNK, C_DK), lead(C_HEADS, C_DK, C_DK), lead(nh, C_DK, C_DK),
                   lead(nh, LANES)],
        out_shape=[jax.ShapeDtypeStruct((bsz, nc, 2, nh, CHUNK, C_DK), BF16),
                   jax.ShapeDtypeStruct((bsz, nc, 2, nh, CHUNK, C_DK), F32),
                   jax.ShapeDtypeStruct((bsz, nc, 2, C_HEADS, C_DK, C_DK), BF16),
                   jax.ShapeDtypeStruct((bsz, nc, 2, nh, C_DK, C_DK), F32),
                   jax.ShapeDtypeStruct((bsz, nc, 2, nh, LANES), F32)],
        compiler_params=_cparams(("parallel", "parallel")),
        name="rec_intra",
    )(qc, kc, vc, qd, kd, vd, gate)


def _rec_scan_kernel(qs_ref, oi_ref, pp_ref, nn_ref, al_ref, o_ref, s_scr):
    @pl.when(pl.program_id(2) == 0)
    def _():
        s_scr[...] = jnp.zeros_like(s_scr)

    for hd in range(C_HEADS + D_HEADS):
        s = s_scr[hd]
        sb = s.astype(BF16)
        o_ref[0, 0, :, hd * C_DK:(hd + 1) * C_DK] = (
            jnp.dot(qs_ref[0, 0, 0, hd], sb, preferred_element_type=F32) + oi_ref[0, 0, 0, hd])
        new = al_ref[0, 0, 0, hd:hd + 1, :] * s + nn_ref[0, 0, 0, hd]
        if hd < C_HEADS:
            new = new - jnp.dot(pp_ref[0, 0, 0, hd], sb, preferred_element_type=F32)
        s_scr[hd] = new


def _rec_scan(qs, oi, pp, nn, al, L):
    bsz, nc = qs.shape[:2]
    nh = C_HEADS + D_HEADS
    n_lat = L // CHUNK
    n_ctx = nc - n_lat

    def chunk(d, s):
        fwd = jnp.where(s < n_ctx, n_lat + s, s - n_ctx)
        bwd = nc - 1 - s
        return jnp.where(d == 0, fwd, bwd)

    lead = lambda *tail: pl.BlockSpec((1, 1, 1) + tail, lambda b, d, s: (b, chunk(d, s), d) + (0,) * len(tail))
    return pl.pallas_call(
        _rec_scan_kernel,
        grid=(bsz, 2, nc),
        in_specs=[lead(nh, CHUNK, C_DK), lead(nh, CHUNK, C_DK), lead(C_HEADS, C_DK, C_DK), lead(nh, C_DK, C_DK),
                  lead(nh, LANES)],
        out_specs=pl.BlockSpec((1, 1, CHUNK, nh * C_DK), lambda b, d, s: (b, d, chunk(d, s), 0)),
        out_shape=jax.ShapeDtypeStruct((bsz, 2, nc * CHUNK, nh * C_DK), F32),
        scratch_shapes=[pltpu.VMEM((nh, C_DK, C_DK), F32)],
        compiler_params=_cparams(("parallel", "parallel", "arbitrary")),
        name="rec_scan",
    )(qs, oi, pp, nn, al)


def _rec_out_kernel(of_ref, ob_ref, z_ref, gd_ref, og_ref, gn_ref, w_ref, h_ref, mod_ref, out_ref):
    hw = C_HEADS * C_DK
    of, ob = of_ref[0, 0], ob_ref[0, 0]
    z, gd = z_ref[0], gd_ref[0]
    parts = []
    for hd in range(C_HEADS):
        sl = slice(hd * C_DK, (hd + 1) * C_DK)
        x = of[:, sl] + ob[:, sl]
        y = x * lax.rsqrt(jnp.mean(x * x, axis=-1, keepdims=True) + EPS) * og_ref[...]
        parts.append(y * _silu(z[:, sl]))
    for hd in range(D_HEADS):
        sl = slice(hd * D_DK, (hd + 1) * D_DK)
        y = 0.0
        for d, o in enumerate((of, ob)):
            x = o[:, hw + hd * D_DK: hw + (hd + 1) * D_DK]
            mu = jnp.mean(x, axis=-1, keepdims=True)
            xc = x - mu
            var = jnp.mean(xc * xc, axis=-1, keepdims=True)
            y = y + xc * lax.rsqrt(var + EPS) * gn_ref[:, sl] * _silu(gd[:, d * hw + hd * D_DK: d * hw + (hd + 1) * D_DK])
        parts.append(y)
    mix = jnp.concatenate(parts, axis=1)
    out_ref[0] = h_ref[0] + mod_ref[0, 0][2:3] * _bdot(mix, w_ref[...])


def _rec_out(o_scan, z, gd, out_g, gn_g, w_out, hh, mod, L):
    bsz, lt, d = hh.shape
    tm = ROW_TILE
    hw = C_HEADS * C_DK
    return pl.pallas_call(
        _rec_out_kernel,
        grid=(bsz, L // tm),
        in_specs=[pl.BlockSpec((1, 1, tm, 2 * hw), lambda b, i: (b, 0, i, 0)),
                  pl.BlockSpec((1, 1, tm, 2 * hw), lambda b, i: (b, 1, i, 0)),
                  pl.BlockSpec((1, tm, hw), lambda b, i: (b, i, 0)),
                  pl.BlockSpec((1, tm, 2 * hw), lambda b, i: (b, i, 0)),
                  pl.BlockSpec((1, C_DK), lambda b, i: (0, 0)),
                  pl.BlockSpec((1, hw), lambda b, i: (0, 0)),
                  pl.BlockSpec((2 * hw, d), lambda b, i: (0, 0)),
                  pl.BlockSpec((1, tm, d), lambda b, i: (b, i, 0)),
                  pl.BlockSpec((1, 1, 6, d), lambda b, i: (b, 0, 0, 0))],
        out_specs=pl.BlockSpec((1, tm, d), lambda b, i: (b, i, 0)),
        out_shape=jax.ShapeDtypeStruct(hh.shape, F32),
        input_output_aliases={7: 0},
        compiler_params=_cparams(("parallel", "parallel")),
        name="rec_out",
    )(o_scan, o_scan, z, gd, out_g.reshape(1, C_DK), gn_g.reshape(1, hw), w_out.astype(BF16), hh, mod)


def _att_layer(hh, mod, norm_g, w_in, q_g, k_g, sink, w_out, L):
    cos, sin = _rope_tables(L, hh.shape[1] - L, HEAD_DIM)
    qa, ka, va, qb, kb, vb = _att_project(hh, mod, norm_g, w_in, q_g, k_g, cos, sin, L)
    oa, ob = _attention(qa, ka, va, qb, kb, vb, sink, L)
    return _att_out(oa, ob, w_out, hh, mod, L)


def _rec_layer(hh, mod, norm_g, w_in, conv_w, a_log, dt_bias, out_g, gn_g, w_out, L):
    cos, sin = _rope_tables(L, hh.shape[1] - L, D_DK)
    qkv, z, qd, kd, vd, gd, gate = _rec_project(hh, mod, norm_g, w_in, a_log, dt_bias, cos, sin, L)
    qc, kc, vc = _rec_conv(qkv, conv_w, L)
    qs, oi, pp, nn, al = _rec_intra(qc, kc, vc, qd, kd, vd, gate)
    o_scan = _rec_scan(qs, oi, pp, nn, al, L)
    return _rec_out(o_scan, z, gd, out_g, gn_g, w_out, hh, mod, L)


def kernel(x, c, ctx, c_ctx, mod_w, mod_b, norm1_g, norm2_g, att_w_in, att_q_norm, att_k_norm, att_sink, att_w_out,
           rec_w_in, rec_conv_w, rec_a_log, rec_dt_bias, rec_out_norm, rec_gn_g, rec_w_out,
           peer_w_q, peer_sub_keys, peer_u, peer_v, final_norm_g):
    L = x.shape[1]
    depth = mod_w.shape[0]
    mods = _modulation(c, c_ctx, mod_w, mod_b)
    hh = jnp.concatenate([x, ctx], axis=1)
    for layer in range(depth):
        last = layer == depth - 1
        i = layer // 2
        if layer % 2 == 0:
            hh = _att_layer(hh, mods[layer], norm1_g[layer], att_w_in[i], att_q_norm[i], att_k_norm[i], att_sink[i],
                            att_w_out[i], L)
        else:
            hh = _rec_layer(hh, mods[layer], norm1_g[layer], rec_w_in[i], rec_conv_w[i], rec_a_log[i], rec_dt_bias[i],
                            rec_out_norm[i], rec_gn_g[i], rec_w_out[i], L)
        hh = _peer(hh, mods[layer], norm2_g[layer], peer_w_q[layer], peer_sub_keys[layer], peer_u[layer],
                   peer_v[layer], final_norm_g, L, with_ctx=not last, final=last)
    return hh
```

```python
import functools
import math

import numpy as np
import jax
import jax.numpy as jnp
from jax import lax
from jax.experimental import pallas as pl
from jax.experimental.pallas import tpu as pltpu

F32 = jnp.float32
BF16 = jnp.bfloat16

GRID_W = 64
EPS = 1e-6
HEAD_DIM = 64
A_HEADS = 8
A_KV = 2
B_HEADS = 8
B_KV = 2
WINDOW = 128
ROPE_THETA = 10000.0
C_HEADS = 4
C_DK = 128
CONV_W = 5
CHUNK = 64
D_HEADS = 4
D_DK = 128
RET_DECAY_BASE = 5.0
PEER_HEADS = 8
N_KEYS = 128
PEER_TOPK = 16

LANES = 128
VMEM_LIMIT = 56 * 1024 * 1024

ROW_TILE = 256
ATT_TQ = 128
PEER_TM = 256
PEER_EB = 2048
PEER_SUB = 256
PEER_TOK = 256

_NEG_INF = float("-inf")


def _cparams(sem, flags=None):
    return pltpu.CompilerParams(dimension_semantics=sem, vmem_limit_bytes=VMEM_LIMIT, flags=flags)


def _bdot(a, b):
    return jnp.dot(a.astype(BF16), b.astype(BF16), preferred_element_type=F32)


def _bdot_nt(a, b):
    return lax.dot_general(a.astype(BF16), b.astype(BF16), (((1,), (1,)), ((), ())), preferred_element_type=F32)


def _bdot_tn(a, b):
    return jnp.dot(a.T.astype(BF16), b.astype(BF16), preferred_element_type=F32)


def _split3(a):
    hi = a.astype(BF16)
    r1 = a - hi.astype(F32)
    mid = r1.astype(BF16)
    lo = (r1 - mid.astype(F32)).astype(BF16)
    return hi, mid, lo


def _dot3(a, b):
    a_hi, a_lo, _ = _split3(a)
    b_hi, b_lo, _ = _split3(b)
    return jnp.dot(jnp.concatenate([a_hi, a_lo, a_hi], axis=1), jnp.concatenate([b_hi, b_hi, b_lo], axis=0),
                   preferred_element_type=F32)


def _sigmoid(x):
    return 1.0 / (1.0 + jnp.exp(-x))


def _silu(x):
    return x * _sigmoid(x)


def _norm_mod(x, g, shift, scale):
    r = lax.rsqrt(jnp.mean(x * x, axis=-1, keepdims=True) + EPS)
    return (x * r * g) * (1.0 + scale) + shift


def _rope(x, cos, sin_signed, head_dim):
    quarter = head_dim // 4
    lane = lax.broadcasted_iota(jnp.int32, x.shape, 1)
    first = (lane % (2 * quarter)) < quarter
    partner = jnp.where(first, pltpu.roll(x, LANES - quarter, 1), pltpu.roll(x, quarter, 1))
    return x * cos + partner * sin_signed


def _mod_kernel(c_ref, w_ref, b_ref, o_ref):
    o_ref[0] = _bdot(_silu(c_ref[...]), w_ref[0]) + b_ref[0]


def _modulation(c, c_ctx, mod_w, mod_b):
    depth, d, n = mod_w.shape
    bsz = c.shape[0]
    rows = 16
    cc = jnp.zeros((rows, d), F32).at[:bsz].set(c).at[bsz].set(c_ctx)
    tn = 1536
    out = pl.pallas_call(
        _mod_kernel,
        grid=(depth, n // tn),
        in_specs=[pl.BlockSpec((rows, d), lambda l, j: (0, 0)),
                  pl.BlockSpec((1, d, tn), lambda l, j: (l, 0, j)),
                  pl.BlockSpec((1, 1, tn), lambda l, j: (l, 0, j))],
        out_specs=pl.BlockSpec((1, rows, tn), lambda l, j: (l, 0, j)),
        out_shape=jax.ShapeDtypeStruct((depth, rows, n), F32),
        compiler_params=_cparams(("arbitrary", "arbitrary")),
        name="modulation",
    )(cc, mod_w, mod_b.reshape(depth, 1, n))
    lat = out[:, :bsz].reshape(depth, bsz, 1, 6, d)
    ctx = jnp.broadcast_to(out[:, bsz].reshape(depth, 1, 1, 6, d), (depth, bsz, 1, 6, d))
    return jnp.concatenate([lat, ctx], axis=2)


def _rope_tables(L, LC, head_dim):
    quarter, half = head_dim // 4, head_dim // 2
    freqs = ROPE_THETA ** (-jnp.arange(quarter, dtype=F32) / quarter)
    lane = np.arange(LANES)
    within = lane % head_dim
    use_col = within >= half
    fidx = within % quarter
    sign = np.where((within % half) < quarter, -1.0, 1.0).astype(np.float32)
    t = jnp.arange(L, dtype=jnp.int32)
    row, col = (t // GRID_W).astype(F32), (t % GRID_W).astype(F32)
    pos = jnp.where(use_col[None, :], col[:, None], row[:, None])
    ang = pos * freqs[fidx][None, :]
    cos = jnp.concatenate([jnp.cos(ang), jnp.ones((LC, LANES), F32)], axis=0)
    sin = jnp.concatenate([jnp.sin(ang) * sign[None, :], jnp.zeros((LC, LANES), F32)], axis=0)
    return cos, sin


def _att_proj_kernel(h_ref, mod_ref, g_ref, w_ref, qg_ref, kg_ref, cos_ref, sin_ref, gm_ref,
                     qa_ref, ka_ref, va_ref, qb_ref, kb_ref, vb_ref):
    m = mod_ref[0, 0]
    a = _norm_mod(h_ref[0], g_ref[...], m[0:1], m[1:2])
    o = _bdot(a, w_ref[...])
    cos, sin = cos_ref[...], sin_ref[...]
    gm = gm_ref[...]

    def head_norm(x, gain):
        sq = x * x
        hi = sq.astype(BF16)
        lo = (sq - hi.astype(F32)).astype(BF16)
        ms = jnp.dot(hi, gm, preferred_element_type=F32) + jnp.dot(lo, gm, preferred_element_type=F32)
        return x * lax.rsqrt(ms + EPS) * gain

    def put(ref, first_head, x):
        ref[0, first_head] = x[:, :HEAD_DIM].astype(ref.dtype)
        ref[0, first_head + 1] = x[:, HEAD_DIM:].astype(ref.dtype)

    scale = HEAD_DIM ** -0.5
    qa_w = A_HEADS * HEAD_DIM
    kv_w = A_KV * HEAD_DIM
    off = 0
    for c in range(qa_w // LANES):
        x = o[:, off + c * LANES: off + (c + 1) * LANES]
        put(qa_ref, 2 * c, _rope(head_norm(x, qg_ref[...]), cos, sin, HEAD_DIM) * scale)
    off += qa_w
    put(ka_ref, 0, _rope(head_norm(o[:, off: off + kv_w], kg_ref[...]), cos, sin, HEAD_DIM))
    off += kv_w
    put(va_ref, 0, o[:, off: off + kv_w])
    off += kv_w
    for c in range(qa_w // LANES):
        x = o[:, off + c * LANES: off + (c + 1) * LANES]
        put(qb_ref, 2 * c, _rope(x, cos, sin, HEAD_DIM) * scale)
    off += qa_w
    put(kb_ref, 0, _rope(o[:, off: off + kv_w], cos, sin, HEAD_DIM))
    off += kv_w
    put(vb_ref, 0, o[:, off: off + kv_w])


def _att_project(hh, mod, norm_g, w_in, q_g, k_g, cos, sin, L):
    bsz, lt, d = hh.shape
    tm = ROW_TILE
    n = w_in.shape[1]
    gm = jnp.asarray(np.kron(np.eye(LANES // HEAD_DIM), np.full((HEAD_DIM, HEAD_DIM), 1.0 / HEAD_DIM)), BF16)
    tile2 = lambda v: jnp.tile(v.reshape(1, HEAD_DIM), (1, LANES // HEAD_DIM))
    qshape = jax.ShapeDtypeStruct((bsz, A_HEADS, lt, HEAD_DIM), BF16)
    kshape = jax.ShapeDtypeStruct((bsz, A_KV, lt, HEAD_DIM), BF16)
    qspec = pl.BlockSpec((1, A_HEADS, tm, HEAD_DIM), lambda b, i: (b, 0, i, 0))
    kspec = pl.BlockSpec((1, A_KV, tm, HEAD_DIM), lambda b, i: (b, 0, i, 0))
    const = lambda shape: pl.BlockSpec(shape, lambda b, i: (0,) * len(shape))
    return pl.pallas_call(
        _att_proj_kernel,
        grid=(bsz, lt // tm),
        in_specs=[pl.BlockSpec((1, tm, d), lambda b, i: (b, i, 0)),
                  pl.BlockSpec((1, 1, 6, d), lambda b, i: (b, i // (L // tm), 0, 0)),
                  const((1, d)), const((d, n)), const((1, LANES)), const((1, LANES)),
                  pl.BlockSpec((tm, LANES), lambda b, i: (i, 0)),
                  pl.BlockSpec((tm, LANES), lambda b, i: (i, 0)),
                  const((LANES, LANES))],
        out_specs=[qspec, kspec, kspec, qspec, kspec, kspec],
        out_shape=[qshape, kshape, kshape, qshape, kshape, kshape],
        compiler_params=_cparams(("parallel", "parallel")),
        name="att_project",
    )(hh, mod, norm_g.reshape(1, d), w_in.astype(BF16), tile2(q_g), tile2(k_g), cos, sin, gm)


def _softmax_pv(score_parts, value_parts, extra_logit=None):
    m = functools.reduce(jnp.maximum, [jnp.max(s, axis=-1, keepdims=True) for s in score_parts])
    if extra_logit is not None:
        m = jnp.maximum(m, extra_logit)
    l = 0.0
    acc = 0.0
    for s, v in zip(score_parts, value_parts):
        p = jnp.exp(s - m)
        l = l + jnp.sum(p, axis=-1, keepdims=True)
        acc = acc + jnp.dot(p.astype(BF16), v, preferred_element_type=F32)
    if extra_logit is not None:
        l = l + jnp.exp(extra_logit - m)
    return acc / l


def _att_kernel(sink_ref, qa_ref, ka_ref, va_ref, qb_ref, kb_ref, vb_ref, oa_ref, ob_ref, *, L, LC):
    i = pl.program_id(1)
    tq = ATT_TQ
    group = A_HEADS // A_KV
    band = tq + 2 * WINDOW

    def stacked_q(ref, kvh):
        return ref[0, kvh * group:(kvh + 1) * group].reshape(group * tq, HEAD_DIM)

    def put(ref, kvh, o):
        for g in range(group):
            hd = kvh * group + g
            ref[0, :, hd * HEAD_DIM:(hd + 1) * HEAD_DIM] = o[g * tq:(g + 1) * tq].astype(ref.dtype)

    def sink_col(kvh):
        return jnp.concatenate([jnp.full((tq, 1), sink_ref[kvh * group + g], F32) for g in range(group)], axis=0)

    @pl.when(i < L // tq)
    def _latent():
        for kvh in range(A_KV):
            q = stacked_q(qa_ref, kvh)
            put(oa_ref, kvh, _softmax_pv([_bdot_nt(q, ka_ref[0, kvh])], [va_ref[0, kvh]]))
            q = stacked_q(qb_ref, kvh)
            start = pl.multiple_of(jnp.clip((i - 1) * tq, 0, L - band), tq)
            s_band = _bdot_nt(q, kb_ref[0, kvh, pl.ds(start, band), :])
            qpos = i * tq + (lax.broadcasted_iota(jnp.int32, s_band.shape, 0) % tq)
            kpos = start + lax.broadcasted_iota(jnp.int32, s_band.shape, 1)
            s_band = jnp.where(jnp.abs(qpos - kpos) <= WINDOW, s_band, _NEG_INF)
            s_ctx = _bdot_nt(q, kb_ref[0, kvh, L:L + LC, :])
            put(ob_ref, kvh, _softmax_pv([s_band, s_ctx],
                                         [vb_ref[0, kvh, pl.ds(start, band), :], vb_ref[0, kvh, L:L + LC, :]],
                                         sink_col(kvh)))

    @pl.when(i >= L // tq)
    def _context():
        for kvh in range(A_KV):
            q = stacked_q(qa_ref, kvh)
            put(oa_ref, kvh, _softmax_pv([_bdot_nt(q, ka_ref[0, kvh, L:L + LC, :])], [va_ref[0, kvh, L:L + LC, :]]))
            q = stacked_q(qb_ref, kvh)
            put(ob_ref, kvh, _softmax_pv([_bdot_nt(q, kb_ref[0, kvh, L:L + LC, :])], [vb_ref[0, kvh, L:L + LC, :]],
                                         sink_col(kvh)))


def _attention(qa, ka, va, qb, kb, vb, sink, L):
    bsz, _, lt, _ = qa.shape
    tq = ATT_TQ
    qspec = pl.BlockSpec((1, A_HEADS, tq, HEAD_DIM), lambda b, i: (b, 0, i, 0))
    kspec = pl.BlockSpec((1, A_KV, lt, HEAD_DIM), lambda b, i: (b, 0, 0, 0))
    ospec = pl.BlockSpec((1, tq, A_HEADS * HEAD_DIM), lambda b, i: (b, i, 0))
    oshape = jax.ShapeDtypeStruct((bsz, lt, A_HEADS * HEAD_DIM), BF16)
    return pl.pallas_call(
        functools.partial(_att_kernel, L=L, LC=lt - L),
        grid=(bsz, lt // tq),
        in_specs=[pl.BlockSpec(memory_space=pltpu.SMEM), qspec, kspec, kspec, qspec, kspec, kspec],
        out_specs=[ospec, ospec],
        out_shape=[oshape, oshape],
        compiler_params=_cparams(("parallel", "parallel")),
        name="attention",
    )(sink, qa, ka, va, qb, kb, vb)


def _att_out_kernel(oa_ref, ob_ref, w_ref, h_ref, mod_ref, out_ref):
    half = oa_ref.shape[-1]
    y = (jnp.dot(oa_ref[0], w_ref[:half, :], preferred_element_type=F32)
         + jnp.dot(ob_ref[0], w_ref[half:, :], preferred_element_type=F32))
    out_ref[0] = h_ref[0] + mod_ref[0, 0][2:3] * y


def _att_out(oa, ob, w_out, hh, mod, L):
    bsz, lt, d = hh.shape
    tm = ROW_TILE
    half = oa.shape[-1]
    return pl.pallas_call(
        _att_out_kernel,
        grid=(bsz, lt // tm),
        in_specs=[pl.BlockSpec((1, tm, half), lambda b, i: (b, i, 0)),
                  pl.BlockSpec((1, tm, half), lambda b, i: (b, i, 0)),
                  pl.BlockSpec((2 * half, d), lambda b, i: (0, 0)),
                  pl.BlockSpec((1, tm, d), lambda b, i: (b, i, 0)),
                  pl.BlockSpec((1, 1, 6, d), lambda b, i: (b, i // (L // tm), 0, 0))],
        out_specs=pl.BlockSpec((1, tm, d), lambda b, i: (b, i, 0)),
        out_shape=jax.ShapeDtypeStruct(hh.shape, F32),
        input_output_aliases={3: 0},
        compiler_params=_cparams(("parallel", "parallel")),
        name="att_out",
    )(oa, ob, w_out.astype(BF16), hh, mod)


def _sort_network(n):
    pairs = []
    p = 1
    while p < n:
        k = p
        while k >= 1:
            for j in range(k % p, n - k, 2 * k):
                for i in range(min(k, n - j - k)):
                    if (i + j) // (2 * p) == (i + j + k) // (2 * p):
                        pairs.append((i + j, i + j + k))
            k //= 2
        p *= 2
    return pairs


def _pop_sorted(lists, n, singles=()):
    lists, singles = list(lists), list(singles)
    out = []
    for rnd in range(n):
        head = functools.reduce(jnp.maximum, [lists[0]] + singles)
        m = jnp.max(head, axis=0, keepdims=True)
        out.append(m)
        if rnd == n - 1:
            break
        hit = lists[0] == m
        depth = min(len(lists), n - rnd - 1)
        for d in range(depth):
            nxt = lists[d + 1] if d + 1 < len(lists) else _NEG_INF
            lists[d] = jnp.where(hit, nxt, lists[d])
        singles = [jnp.where(s == m, _NEG_INF, s) for s in singles]
    return out


def _top_values(x, n):
    slabs = [x[r * 8:(r + 1) * 8] for r in range(x.shape[0] // 8)]
    for i, j in _sort_network(len(slabs)):
        slabs[i], slabs[j] = jnp.maximum(slabs[i], slabs[j]), jnp.minimum(slabs[i], slabs[j])
    return _pop_sorted(slabs, n)


def _peer_route_kernel(h_ref, mod_ref, g_ref, wq_ref, keys_ref, at_ref, thr_ref, e0_ref, s1_ref, e1_ref, q_scr):
    m = mod_ref[0, 0]
    a = _norm_mod(h_ref[0], g_ref[...], m[3:4], m[4:5])
    at_ref[0] = a.T.astype(BF16)
    q_scr[...] = jnp.dot(a.astype(BF16), wq_ref[...], preferred_element_type=F32)
    k = PEER_TOPK
    n = k + 1

    def head(hd, carry):
        col = pl.multiple_of(hd * 2 * N_KEYS, 2 * N_KEYS)
        s0 = _bdot_nt(keys_ref[0], q_scr[:, pl.ds(col, N_KEYS)])
        s1 = _bdot_nt(keys_ref[1], q_scr[:, pl.ds(col + N_KEYS, N_KEYS)])
        tokens = s0.shape[1]
        top0 = _top_values(s0, n)
        top1 = _top_values(s1, n)
        first = jnp.concatenate(top0 + [jnp.full((24 - n, tokens), _NEG_INF, F32)], axis=0)
        best = _pop_sorted([first[0:8] + t for t in top1], n, [first[8:16] + top1[0], first[16:24] + top1[0]])
        z = functools.reduce(lambda x, y: x + y, [jnp.exp(b - best[0]) for b in best[:k]])
        tau = 0.5 * (best[k - 1] + best[k])
        thr_ref[0, hd] = tau - s0
        e0_ref[0, hd] = jnp.exp(s0 - top0[0])
        s1_ref[0, hd] = s1
        e1_ref[0, hd] = jnp.exp(s1 - top1[0]) / z
        return carry

    lax.fori_loop(0, PEER_HEADS, head, 0, unroll=True)


def _peer_route(hh, mod, norm_g, w_q, sub_keys, L):
    bsz, lt, d = hh.shape
    tm = ROW_TILE
    nq = w_q.shape[1]
    tok = lambda: pl.BlockSpec((1, PEER_HEADS, N_KEYS, tm), lambda b, i: (b, 0, 0, i))
    tshape = jax.ShapeDtypeStruct((bsz, PEER_HEADS, N_KEYS, lt), F32)
    return pl.pallas_call(
        _peer_route_kernel,
        grid=(bsz, lt // tm),
        in_specs=[pl.BlockSpec((1, tm, d), lambda b, i: (b, i, 0)),
                  pl.BlockSpec((1, 1, 6, d), lambda b, i: (b, i // (L // tm), 0, 0)),
                  pl.BlockSpec((1, d), lambda b, i: (0, 0)),
                  pl.BlockSpec((d, nq), lambda b, i: (0, 0)),
                  pl.BlockSpec((2, N_KEYS, N_KEYS), lambda b, i: (0, 0, 0))],
        out_specs=[pl.BlockSpec((1, d, tm), lambda b, i: (b, 0, i)), tok(), tok(), tok(), tok()],
        out_shape=[jax.ShapeDtypeStruct((bsz, d, lt), BF16), tshape, tshape, tshape, tshape],
        scratch_shapes=[pltpu.VMEM((tm, nq), F32)],
        compiler_params=_cparams(("parallel", "parallel")),
        name="peer_route",
    )(hh, mod, norm_g.reshape(1, d), w_q.astype(BF16), sub_keys.astype(BF16))


def _gelu(x):
    return 0.5 * x * (1.0 + lax.erf(x * (1.0 / math.sqrt(2.0))))


def _peer_expert_kernel(a_ref, thr_ref, e0_ref, s1_ref, e1_ref, u_ref, vt_ref, h_ref, mod_ref, fg_ref,
                        out_ref, yt_scr, wa_scr, *, final):
    e = pl.program_id(1)
    t = pl.program_id(2)
    tm = a_ref.shape[2]

    @pl.when(e == 0)
    def _():
        yt_scr[t] = jnp.zeros(yt_scr.shape[1:], F32)

    for t0 in range(0, tm, PEER_TOK):
        ts = slice(t0, t0 + PEER_TOK)
        y = None
        for q in range(PEER_EB // PEER_SUB):
            rows = slice(q * PEER_SUB, (q + 1) * PEER_SUB)
            act = jnp.dot(u_ref[rows, :], a_ref[0, :, ts], preferred_element_type=F32)
            for il in range(PEER_SUB // N_KEYS):
                key0 = q * (PEER_SUB // N_KEYS) + il
                for c in range(PEER_TOK // LANES):
                    cs = slice(t0 + c * LANES, t0 + (c + 1) * LANES)
                    gate = None
                    for hd in range(PEER_HEADS):
                        weight = e0_ref[0, hd, key0:key0 + 1, cs] * e1_ref[0, hd, :, cs]
                        part = jnp.where(s1_ref[0, hd, :, cs] >= thr_ref[0, hd, key0:key0 + 1, cs], weight, 0.0)
                        gate = part if gate is None else gate + part
                    wa_scr[q * PEER_SUB + il * N_KEYS: q * PEER_SUB + (il + 1) * N_KEYS, cs] = (
                        gate * _gelu(act[il * N_KEYS:(il + 1) * N_KEYS, c * LANES:(c + 1) * LANES])).astype(BF16)
            part = jnp.dot(vt_ref[:, rows], wa_scr[rows, ts], preferred_element_type=F32)
            y = part if y is None else y + part
        yt_scr[t, :, ts] += y

    @pl.when(e == pl.num_programs(1) - 1)
    def _():
        hn = h_ref[0] + mod_ref[0, 0][5:6] * yt_scr[t].T
        if final:
            hn = hn * lax.rsqrt(jnp.mean(hn * hn, axis=-1, keepdims=True) + EPS) * fg_ref[...]
        out_ref[0] = hn


def _peer_expert(hh, mod, a, thr, e0, s1, e1, u_bf, vt_bf, final_g, L, *, with_ctx, final):
    bsz, lt, d = hh.shape
    n_exp = u_bf.shape[0]
    tm = PEER_TM
    n_tiles = (lt if with_ctx else L) // tm
    keys_per_step = PEER_EB // N_KEYS
    n_blocks = n_exp // PEER_EB
    tok = lambda r: pl.BlockSpec((1, PEER_HEADS, r, tm), (lambda b, e, t: (b, 0, e, t)) if r == keys_per_step
                                 else (lambda b, e, t: (b, 0, 0, t)))
    row_spec = pl.BlockSpec((1, tm, d), lambda b, e, t: (b, jnp.where(e == n_blocks - 1, t, 0), 0))
    if final:
        out_shape = jax.ShapeDtypeStruct((bsz, L, d), F32)
        aliases = {}
    else:
        out_shape = jax.ShapeDtypeStruct(hh.shape, F32)
        aliases = {7: 0}
    return pl.pallas_call(
        functools.partial(_peer_expert_kernel, final=final),
        grid=(bsz, n_blocks, n_tiles),
        in_specs=[pl.BlockSpec((1, d, tm), lambda b, e, t: (b, 0, t)),
                  tok(keys_per_step), tok(keys_per_step), tok(N_KEYS), tok(N_KEYS),
                  pl.BlockSpec((PEER_EB, d), lambda b, e, t: (e, 0)),
                  pl.BlockSpec((d, PEER_EB), lambda b, e, t: (0, e)),
                  row_spec,
                  pl.BlockSpec((1, 1, 6, d), lambda b, e, t: (b, t // (L // tm), 0, 0)),
                  pl.BlockSpec((1, d), lambda b, e, t: (0, 0))],
        out_specs=row_spec,
        out_shape=out_shape,
        input_output_aliases=aliases,
        scratch_shapes=[pltpu.VMEM((n_tiles, d, tm), F32), pltpu.VMEM((PEER_EB, tm), BF16)],
        compiler_params=_cparams(("parallel", "arbitrary", "arbitrary")),
        name="peer_expert",
    )(a, thr, e0, s1, e1, u_bf, vt_bf, hh, mod, final_g.reshape(1, d))


def _peer(hh, mod, norm_g, w_q, sub_keys, u, v, final_g, L, *, with_ctx, final):
    a, thr, e0, s1, e1 = _peer_route(hh, mod, norm_g, w_q, sub_keys, L)
    return _peer_expert(hh, mod, a, thr, e0, s1, e1, u.astype(BF16), v.astype(BF16).T, final_g, L,
                        with_ctx=with_ctx, final=final)


REC_QKV = C_HEADS * 3 * C_DK
REC_Z = C_HEADS * C_DK
REC_GATES = 4 * C_HEADS
REC_HD = D_HEADS * D_DK


def _rec_proj_kernel(h_ref, mod_ref, g_ref, w_ref, alog_ref, dtb_ref, cos_ref, sin_ref,
                     qkv_ref, z_ref, qd_ref, kd_ref, vd_ref, gd_ref, gate_ref):
    m = mod_ref[0, 0]
    a = _norm_mod(h_ref[0], g_ref[...], m[0:1], m[1:2])
    o = _bdot(a, w_ref[...])
    cos, sin = cos_ref[...], sin_ref[...]
    off = 0
    qkv_ref[0] = o[:, off:off + REC_QKV]
    off += REC_QKV
    z_ref[0] = o[:, off:off + REC_Z]
    off += REC_Z
    for hd in range(D_HEADS):
        qd_ref[0, :, hd * D_DK:(hd + 1) * D_DK] = _rope(o[:, off + hd * D_DK: off + (hd + 1) * D_DK], cos, sin, D_DK)
    off += REC_HD
    for hd in range(D_HEADS):
        kd_ref[0, :, hd * D_DK:(hd + 1) * D_DK] = (
            _rope(o[:, off + hd * D_DK: off + (hd + 1) * D_DK], cos, sin, D_DK) * (D_DK ** -0.5))
    off += REC_HD
    vd_ref[0] = o[:, off:off + REC_HD]
    off += REC_HD
    gd_ref[0] = o[:, off:off + 2 * REC_HD]
    off += 2 * REC_HD
    x = o[:, off:off + LANES]
    lane = lax.broadcasted_iota(jnp.int32, x.shape, 1)
    xb = x + dtb_ref[...]
    softplus = jnp.maximum(xb, 0.0) + jnp.log(1.0 + jnp.exp(-jnp.abs(xb)))
    gate_ref[0] = jnp.where(lane < 2 * C_HEADS, -jnp.exp(alog_ref[...]) * softplus, _sigmoid(x))


def _rec_project(hh, mod, norm_g, w_in, a_log, dt_bias, cos, sin, L):
    bsz, lt, d = hh.shape
    tm = ROW_TILE
    parts = np.cumsum([REC_QKV, REC_Z, REC_GATES, REC_HD, REC_HD, REC_HD])
    qkv_w, z_w, gates_w, qd_w, kd_w, vd_w, gd_w = jnp.split(w_in, [int(p) for p in parts], axis=1)
    w = jnp.concatenate([qkv_w, z_w, qd_w, kd_w, vd_w, gd_w, gates_w, jnp.zeros((d, LANES - REC_GATES), F32)],
                        axis=1).astype(BF16)
    n = w.shape[1]
    pad = lambda p: jnp.zeros((1, LANES), F32).at[0, :2 * C_HEADS].set(p.reshape(-1))
    widths = [REC_QKV, REC_Z, REC_HD, REC_HD, REC_HD, 2 * REC_HD, LANES]
    const = lambda shape: pl.BlockSpec(shape, lambda b, i: (0,) * len(shape))
    return pl.pallas_call(
        _rec_proj_kernel,
        grid=(bsz, lt // tm),
        in_specs=[pl.BlockSpec((1, tm, d), lambda b, i: (b, i, 0)),
                  pl.BlockSpec((1, 1, 6, d), lambda b, i: (b, i // (L // tm), 0, 0)),
                  const((1, d)), const((d, n)), const((1, LANES)), const((1, LANES)),
                  pl.BlockSpec((tm, LANES), lambda b, i: (i, 0)),
                  pl.BlockSpec((tm, LANES), lambda b, i: (i, 0))],
        out_specs=[pl.BlockSpec((1, tm, wd), lambda b, i: (b, i, 0)) for wd in widths],
        out_shape=[jax.ShapeDtypeStruct((bsz, lt, wd), F32) for wd in widths],
        compiler_params=_cparams(("parallel", "parallel")),
        name="rec_project",
    )(hh, mod, norm_g.reshape(1, d), w, pad(a_log), pad(dt_bias), cos, sin)


def _rec_conv_kernel(x_ref, prev_ref, next_ref, w_ref, q_ref, k_ref, v_ref, *, L):
    i = pl.program_id(1)
    tl = x_ref.shape[1]
    n_lat = L // tl
    at_start = (i == 0) | (i == n_lat)
    at_end = (i == n_lat - 1) | (i == pl.num_programs(1) - 1)
    prev = jnp.where(at_start, 0.0, prev_ref[0])
    nxt = jnp.where(at_end, 0.0, next_ref[0])
    xx = jnp.concatenate([prev, x_ref[0], nxt], axis=0)
    n = tl + 16
    acc = 0.0
    for tap in range(CONV_W):
        shift = (CONV_W // 2 - tap) % n
        shifted = xx if shift == 0 else pltpu.roll(xx, shift, 0)
        acc = acc + shifted[8:8 + tl] * w_ref[tap:tap + 1, :]
    y = _silu(acc)
    hw = C_HEADS * C_DK
    for hd in range(C_HEADS):
        def l2(x):
            return x * lax.rsqrt(jnp.sum(x * x, axis=-1, keepdims=True) + EPS)
        sl = slice(hd * C_DK, (hd + 1) * C_DK)
        q_ref[0, :, sl] = l2(y[:, hd * C_DK:(hd + 1) * C_DK]) * (C_DK ** -0.5)
        k_ref[0, :, sl] = l2(y[:, hw + hd * C_DK: hw + (hd + 1) * C_DK])
    v_ref[0] = y[:, 2 * hw:]


def _rec_conv(qkv, conv_w, L):
    bsz, lt, ch = qkv.shape
    tl = ROW_TILE
    hb = tl // 8
    last = lt // 8 - 1
    hw = C_HEADS * C_DK
    return pl.pallas_call(
        functools.partial(_rec_conv_kernel, L=L),
        grid=(bsz, lt // tl),
        in_specs=[pl.BlockSpec((1, tl, ch), lambda b, i: (b, i, 0)),
                  pl.BlockSpec((1, 8, ch), lambda b, i: (b, jnp.maximum(i * hb - 1, 0), 0)),
                  pl.BlockSpec((1, 8, ch), lambda b, i: (b, jnp.minimum((i + 1) * hb, last), 0)),
                  pl.BlockSpec((8, ch), lambda b, i: (0, 0))],
        out_specs=[pl.BlockSpec((1, tl, hw), lambda b, i: (b, i, 0))] * 3,
        out_shape=[jax.ShapeDtypeStruct((bsz, lt, hw), F32)] * 3,
        compiler_params=_cparams(("parallel", "parallel")),
        name="rec_conv",
    )(qkv, qkv, qkv, jnp.zeros((8, ch), F32).at[:CONV_W].set(conv_w))


def _ret_log_gamma(hd):
    return float(np.log1p(-np.exp2(-(RET_DECAY_BASE + hd))))


def _rec_intra_kernel(qc_ref, kc_ref, vc_ref, qd_ref, kd_ref, vd_ref, gate_ref,
                      qs_ref, oi_ref, pp_ref, nn_ref, al_ref):
    c = CHUNK
    ci = lax.broadcasted_iota(jnp.int32, (c, c), 0)
    si = lax.broadcasted_iota(jnp.int32, (c, c), 1)
    pos = lax.broadcasted_iota(jnp.int32, (c, 1), 0).astype(F32)
    gates = gate_ref[0]
    la_parts = _split3(gates)
    ones_row = jnp.ones((1, LANES), F32)
    combos = []
    for d in range(2):
        incl = (si <= ci) if d == 0 else (si >= ci)
        strict = (si < ci) if d == 0 else (si > ci)
        tri = incl.astype(BF16)
        g_all = jnp.dot(jnp.concatenate([tri] * 3, axis=1), jnp.concatenate(la_parts, axis=0),
                        preferred_element_type=F32)
        g_all_t = g_all.T
        last = c - 1 if d == 0 else 0
        for hd in range(C_HEADS):
            col = d * C_HEADS + hd
            sl = slice(hd * C_DK, (hd + 1) * C_DK)
            q, k, v = qc_ref[0, :, sl], kc_ref[0, :, sl], vc_ref[0, :, sl]
            gcol = g_all[:, col:col + 1]
            grow = g_all_t[col:col + 1, :]
            glast = g_all[last:last + 1, col:col + 1]
            beta = gates[:, 2 * C_HEADS + col: 2 * C_HEADS + col + 1]
            diff = gcol - grow
            dec_strict = jnp.exp(jnp.where(strict, diff, _NEG_INF))
            combos.append(dict(
                d=d, hd=hd, q=q, gcol=gcol, glast=glast,
                x=-(beta * _bdot_nt(k, k) * dec_strict),
                sol=jnp.concatenate([beta * v, (beta * jnp.exp(gcol)) * k], axis=1),
                qk=_bdot_nt(q, k) * jnp.exp(jnp.where(incl, diff, _NEG_INF)),
                kend=k * jnp.exp(glast - gcol)))
    for d in range(2):
        incl = (si <= ci) if d == 0 else (si >= ci)
        for hd in range(D_HEADS):
            lg = _ret_log_gamma(hd)
            sl = slice(hd * D_DK, (hd + 1) * D_DK)
            q, k, v = qd_ref[0, :, sl], kd_ref[0, :, sl], vd_ref[0, :, sl]
            steps = pos if d == 0 else (c - 1.0) - pos
            dist = (ci - si) if d == 0 else (si - ci)
            dmat = jnp.exp(jnp.where(incl, lg * dist.astype(F32), _NEG_INF))
            qk = _bdot_nt(q, k) * dmat
            qs_ref[0, 0, d, C_HEADS + hd] = (q * jnp.exp(lg * (steps + 1.0))).astype(qs_ref.dtype)
            oi_ref[0, 0, d, C_HEADS + hd] = _bdot(qk, v)
            nn_ref[0, 0, d, C_HEADS + hd] = _bdot_tn(k * jnp.exp(lg * ((c - 1.0) - steps)), v)
            al_ref[0, 0, d, C_HEADS + hd:C_HEADS + hd + 1, :] = math.exp(lg * c) * ones_row
    for cb in combos:
        cb["kend_t"] = cb["kend"].T.astype(BF16)
    levels = int(math.log2(c))
    for lvl in range(levels):
        for cb in combos:
            cb["sol"] = cb["sol"] + _dot3(cb["x"], cb["sol"])
        if lvl < levels - 1:
            for cb in combos:
                cb["x"] = _dot3(cb["x"], cb["x"])
    for cb in combos:
        cb["out"] = _bdot(cb["qk"], cb["sol"])
    for cb in combos:
        cb["state"] = jnp.dot(cb["kend_t"], cb["sol"].astype(BF16), preferred_element_type=F32)
    for cb in combos:
        d, hd = cb["d"], cb["hd"]
        qs_ref[0, 0, d, hd] = (jnp.exp(cb["gcol"]) * cb["q"] - cb["out"][:, C_DK:]).astype(qs_ref.dtype)
        oi_ref[0, 0, d, hd] = cb["out"][:, :C_DK]
        pp_ref[0, 0, d, hd] = cb["state"][:, C_DK:].astype(pp_ref.dtype)
        nn_ref[0, 0, d, hd] = cb["state"][:, :C_DK]
        al_ref[0, 0, d, hd:hd + 1, :] = jnp.exp(cb["glast"]) * ones_row


def _rec_intra(qc, kc, vc, qd, kd, vd, gate):
    bsz, lt, hw = qc.shape
    nc = lt // CHUNK
    nh = C_HEADS + D_HEADS
    row = lambda w: pl.BlockSpec((1, CHUNK, w), lambda b, n: (b, n, 0))
    lead = lambda *tail: pl.BlockSpec((1, 1, 2) + tail, lambda b, n: (b, n, 0) + (0,) * len(tail))
    return pl.pallas_call(
        _rec_intra_kernel,
        grid=(bsz, nc),
        in_specs=[row(hw)] * 6 + [row(LANES)],
        out_specs=[lead(nh, CHUNK, C_DK), lead(nh, CHUNK, C_DK), lead(C_HEADS, C_DK, C_DK), lead(nh, C_DK, C_DK),
                   lead(nh, LANES)],
        out_shape=[jax.ShapeDtypeStruct((bsz, nc, 2, nh, CHUNK, C_DK), BF16),
                   jax.ShapeDtypeStruct((bsz, nc, 2, nh, CHUNK, C_DK), F32),
                   jax.ShapeDtypeStruct((bsz, nc, 2, C_HEADS, C_DK, C_DK), BF16),
                   jax.ShapeDtypeStruct((bsz, nc, 2, nh, C_DK, C_DK), F32),
                   jax.ShapeDtypeStruct((bsz, nc, 2, nh, LANES), F32)],
        compiler_params=_cparams(("parallel", "parallel")),
        name="rec_intra",
    )(qc, kc, vc, qd, kd, vd, gate)


def _rec_scan_kernel(qs_ref, oi_ref, pp_ref, nn_ref, al_ref, o_ref, s_scr):
    @pl.when(pl.program_id(2) == 0)
    def _():
        s_scr[...] = jnp.zeros_like(s_scr)

    for hd in range(C_HEADS + D_HEADS):
        s = s_scr[hd]
        sb = s.astype(BF16)
        o_ref[0, 0, :, hd * C_DK:(hd + 1) * C_DK] = (
            jnp.dot(qs_ref[0, 0, 0, hd], sb, preferred_element_type=F32) + oi_ref[0, 0, 0, hd])
        new = al_ref[0, 0, 0, hd:hd + 1, :] * s + nn_ref[0, 0, 0, hd]
        if hd < C_HEADS:
            new = new - jnp.dot(pp_ref[0, 0, 0, hd], sb, preferred_element_type=F32)
        s_scr[hd] = new


def _rec_scan(qs, oi, pp, nn, al, L):
    bsz, nc = qs.shape[:2]
    nh = C_HEADS + D_HEADS
    n_lat = L // CHUNK
    n_ctx = nc - n_lat

    def chunk(d, s):
        fwd = jnp.where(s < n_ctx, n_lat + s, s - n_ctx)
        bwd = nc - 1 - s
        return jnp.where(d == 0, fwd, bwd)

    lead = lambda *tail: pl.BlockSpec((1, 1, 1) + tail, lambda b, d, s: (b, chunk(d, s), d) + (0,) * len(tail))
    return pl.pallas_call(
        _rec_scan_kernel,
        grid=(bsz, 2, nc),
        in_specs=[lead(nh, CHUNK, C_DK), lead(nh, CHUNK, C_DK), lead(C_HEADS, C_DK, C_DK), lead(nh, C_DK, C_DK),
                  lead(nh, LANES)],
        out_specs=pl.BlockSpec((1, 1, CHUNK, nh * C_DK), lambda b, d, s: (b, d, chunk(d, s), 0)),
        out_shape=jax.ShapeDtypeStruct((bsz, 2, nc * CHUNK, nh * C_DK), F32),
        scratch_shapes=[pltpu.VMEM((nh, C_DK, C_DK), F32)],
        compiler_params=_cparams(("parallel", "parallel", "arbitrary")),
        name="rec_scan",
    )(qs, oi, pp, nn, al)


def _rec_out_kernel(of_ref, ob_ref, z_ref, gd_ref, og_ref, gn_ref, w_ref, h_ref, mod_ref, out_ref):
    hw = C_HEADS * C_DK
    of, ob = of_ref[0, 0], ob_ref[0, 0]
    z, gd = z_ref[0], gd_ref[0]
    parts = []
    for hd in range(C_HEADS):
        sl = slice(hd * C_DK, (hd + 1) * C_DK)
        x = of[:, sl] + ob[:, sl]
        y = x * lax.rsqrt(jnp.mean(x * x, axis=-1, keepdims=True) + EPS) * og_ref[...]
        parts.append(y * _silu(z[:, sl]))
    for hd in range(D_HEADS):
        sl = slice(hd * D_DK, (hd + 1) * D_DK)
        y = 0.0
        for d, o in enumerate((of, ob)):
            x = o[:, hw + hd * D_DK: hw + (hd + 1) * D_DK]
            mu = jnp.mean(x, axis=-1, keepdims=True)
            xc = x - mu
            var = jnp.mean(xc * xc, axis=-1, keepdims=True)
            y = y + xc * lax.rsqrt(var + EPS) * gn_ref[:, sl] * _silu(gd[:, d * hw + hd * D_DK: d * hw + (hd + 1) * D_DK])
        parts.append(y)
    mix = jnp.concatenate(parts, axis=1)
    out_ref[0] = h_ref[0] + mod_ref[0, 0][2:3] * _bdot(mix, w_ref[...])


def _rec_out(o_scan, z, gd, out_g, gn_g, w_out, hh, mod, L):
    bsz, lt, d = hh.shape
    tm = ROW_TILE
    hw = C_HEADS * C_DK
    return pl.pallas_call(
        _rec_out_kernel,
        grid=(bsz, L // tm),
        in_specs=[pl.BlockSpec((1, 1, tm, 2 * hw), lambda b, i: (b, 0, i, 0)),
                  pl.BlockSpec((1, 1, tm, 2 * hw), lambda b, i: (b, 1, i, 0)),
                  pl.BlockSpec((1, tm, hw), lambda b, i: (b, i, 0)),
                  pl.BlockSpec((1, tm, 2 * hw), lambda b, i: (b, i, 0)),
                  pl.BlockSpec((1, C_DK), lambda b, i: (0, 0)),
                  pl.BlockSpec((1, hw), lambda b, i: (0, 0)),
                  pl.BlockSpec((2 * hw, d), lambda b, i: (0, 0)),
                  pl.BlockSpec((1, tm, d), lambda b, i: (b, i, 0)),
                  pl.BlockSpec((1, 1, 6, d), lambda b, i: (b, 0, 0, 0))],
        out_specs=pl.BlockSpec((1, tm, d), lambda b, i: (b, i, 0)),
        out_shape=jax.ShapeDtypeStruct(hh.shape, F32),
        input_output_aliases={7: 0},
        compiler_params=_cparams(("parallel", "parallel")),
        name="rec_out",
    )(o_scan, o_scan, z, gd, out_g.reshape(1, C_DK), gn_g.reshape(1, hw), w_out.astype(BF16), hh, mod)


def _att_layer(hh, mod, norm_g, w_in, q_g, k_g, sink, w_out, L):
    cos, sin = _rope_tables(L, hh.shape[1] - L, HEAD_DIM)
    qa, ka, va, qb, kb, vb = _att_project(hh, mod, norm_g, w_in, q_g, k_g, cos, sin, L)
    oa, ob = _attention(qa, ka, va, qb, kb, vb, sink, L)
    return _att_out(oa, ob, w_out, hh, mod, L)


def _rec_layer(hh, mod, norm_g, w_in, conv_w, a_log, dt_bias, out_g, gn_g, w_out, L):
    cos, sin = _rope_tables(L, hh.shape[1] - L, D_DK)
    qkv, z, qd, kd, vd, gd, gate = _rec_project(hh, mod, norm_g, w_in, a_log, dt_bias, cos, sin, L)
    qc, kc, vc = _rec_conv(qkv, conv_w, L)
    qs, oi, pp, nn, al = _rec_intra(qc, kc, vc, qd, kd, vd, gate)
    o_scan = _rec_scan(qs, oi, pp, nn, al, L)
    return _rec_out(o_scan, z, gd, out_g, gn_g, w_out, hh, mod, L)


def kernel(x, c, ctx, c_ctx, mod_w, mod_b, norm1_g, norm2_g, att_w_in, att_q_norm, att_k_norm, att_sink, att_w_out,
           rec_w_in, rec_conv_w, rec_a_log, rec_dt_bias, rec_out_norm, rec_gn_g, rec_w_out,
           peer_w_q, peer_sub_keys, peer_u, peer_v, final_norm_g):
    L = x.shape[1]
    depth = mod_w.shape[0]
    mods = _modulation(c, c_ctx, mod_w, mod_b)
    hh = jnp.concatenate([x, ctx], axis=1)
    for layer in range(depth):
        last = layer == depth - 1
        i = layer // 2
        if layer % 2 == 0:
            hh = _att_layer(hh, mods[layer], norm1_g[layer], att_w_in[i], att_q_norm[i], att_k_norm[i], att_sink[i],
                            att_w_out[i], L)
        else:
            hh = _rec_layer(hh, mods[layer], norm1_g[layer], rec_w_in[i], rec_conv_w[i], rec_a_log[i], rec_dt_bias[i],
                            rec_out_norm[i], rec_gn_g[i], rec_w_out[i], L)
        hh = _peer(hh, mods[layer], norm2_g[layer], peer_w_q[layer], peer_sub_keys[layer], peer_u[layer],
                   peer_v[layer], final_norm_g, L, with_ctx=not last, final=last)
    return hh
```

```python
import functools
import math

import numpy as np
import jax
import jax.numpy as jnp
from jax import lax
from jax.experimental import pallas as pl
from jax.experimental.pallas import tpu as pltpu

F32 = jnp.float32
BF16 = jnp.bfloat16

GRID_W = 64
EPS = 1e-6
HEAD_DIM = 64
A_HEADS = 8
A_KV = 2
B_HEADS = 8
B_KV = 2
WINDOW = 128
ROPE_THETA = 10000.0
C_HEADS = 4
C_DK = 128
CONV_W = 5
CHUNK = 64
D_HEADS = 4
D_DK = 128
RET_DECAY_BASE = 5.0
PEER_HEADS = 8
N_KEYS = 128
PEER_TOPK = 16

LANES = 128
BF16_ROWS = 16
VMEM_LIMIT = 56 * 1024 * 1024

ROW_TILE = 256
ATT_TQ = 128
PEER_TM = 256
PEER_EB = 2048
PEER_SUB = 256
PEER_TOK = 256

_NEG_INF = float("-inf")


def _cparams(sem, flags=None):
    return pltpu.CompilerParams(dimension_semantics=sem, vmem_limit_bytes=VMEM_LIMIT, flags=flags)


def _bdot(a, b):
    return jnp.dot(a.astype(BF16), b.astype(BF16), preferred_element_type=F32)


def _bdot_nt(a, b):
    return lax.dot_general(a.astype(BF16), b.astype(BF16), (((1,), (1,)), ((), ())), preferred_element_type=F32)


def _bdot_tn(a, b):
    return jnp.dot(a.T.astype(BF16), b.astype(BF16), preferred_element_type=F32)


def _split3(a):
    hi = a.astype(BF16)
    r1 = a - hi.astype(F32)
    mid = r1.astype(BF16)
    lo = (r1 - mid.astype(F32)).astype(BF16)
    return hi, mid, lo


def _dot3(a, b):
    a_hi, a_lo, _ = _split3(a)
    b_hi, b_lo, _ = _split3(b)
    return jnp.dot(jnp.concatenate([a_hi, a_lo, a_hi], axis=1), jnp.concatenate([b_hi, b_hi, b_lo], axis=0),
                   preferred_element_type=F32)


def _sigmoid(x):
    return 1.0 / (1.0 + jnp.exp(-x))


def _silu(x):
    return x * _sigmoid(x)


def _norm_mod(x, g, shift, scale):
    r = lax.rsqrt(jnp.mean(x * x, axis=-1, keepdims=True) + EPS)
    return (x * r * g) * (1.0 + scale) + shift


def _rope(x, cos, sin_signed, head_dim):
    quarter = head_dim // 4
    lane = lax.broadcasted_iota(jnp.int32, x.shape, 1)
    first = (lane % (2 * quarter)) < quarter
    partner = jnp.where(first, pltpu.roll(x, LANES - quarter, 1), pltpu.roll(x, quarter, 1))
    return x * cos + partner * sin_signed


def _mod_kernel(c_ref, w_ref, b_ref, o_ref):
    o_ref[0] = _bdot(_silu(c_ref[...]), w_ref[0]) + b_ref[0]


def _modulation(c, c_ctx, mod_w, mod_b):
    depth, d, n = mod_w.shape
    bsz = c.shape[0]
    rows = 16
    cc = jnp.zeros((rows, d), F32).at[:bsz].set(c).at[bsz].set(c_ctx)
    tn = 1536
    out = pl.pallas_call(
        _mod_kernel,
        grid=(depth, n // tn),
        in_specs=[pl.BlockSpec((rows, d), lambda l, j: (0, 0)),
                  pl.BlockSpec((1, d, tn), lambda l, j: (l, 0, j)),
                  pl.BlockSpec((1, 1, tn), lambda l, j: (l, 0, j))],
        out_specs=pl.BlockSpec((1, rows, tn), lambda l, j: (l, 0, j)),
        out_shape=jax.ShapeDtypeStruct((depth, rows, n), F32),
        compiler_params=_cparams(("arbitrary", "arbitrary")),
        name="modulation",
    )(cc, mod_w, mod_b.reshape(depth, 1, n))
    lat = out[:, :bsz].reshape(depth, bsz, 1, 6, d)
    ctx = jnp.broadcast_to(out[:, bsz].reshape(depth, 1, 1, 6, d), (depth, bsz, 1, 6, d))
    return jnp.concatenate([lat, ctx], axis=2)


def _rope_tables(L, LC, head_dim):
    quarter, half = head_dim // 4, head_dim // 2
    freqs = ROPE_THETA ** (-jnp.arange(quarter, dtype=F32) / quarter)
    lane = np.arange(LANES)
    within = lane % head_dim
    use_col = within >= half
    fidx = within % quarter
    sign = np.where((within % half) < quarter, -1.0, 1.0).astype(np.float32)
    t = jnp.arange(L, dtype=jnp.int32)
    row, col = (t // GRID_W).astype(F32), (t % GRID_W).astype(F32)
    pos = jnp.where(use_col[None, :], col[:, None], row[:, None])
    ang = pos * freqs[fidx][None, :]
    cos = jnp.concatenate([jnp.cos(ang), jnp.ones((LC, LANES), F32)], axis=0)
    sin = jnp.concatenate([jnp.sin(ang) * sign[None, :], jnp.zeros((LC, LANES), F32)], axis=0)
    return cos, sin


def _att_proj_kernel(h_ref, mod_ref, g_ref, w_ref, qg_ref, kg_ref, cos_ref, sin_ref, gm_ref,
                     qa_ref, ka_ref, va_ref, qb_ref, kb_ref, vb_ref):
    m = mod_ref[0, 0]
    a = _norm_mod(h_ref[0], g_ref[...], m[0:1], m[1:2])
    o = _bdot(a, w_ref[...])
    cos, sin = cos_ref[...], sin_ref[...]
    gm = gm_ref[...]

    def head_norm(x, gain):
        sq = x * x
        hi = sq.astype(BF16)
        lo = (sq - hi.astype(F32)).astype(BF16)
        ms = jnp.dot(hi, gm, preferred_element_type=F32) + jnp.dot(lo, gm, preferred_element_type=F32)
        return x * lax.rsqrt(ms + EPS) * gain

    def put(ref, first_head, x):
        ref[0, first_head] = x[:, :HEAD_DIM].astype(ref.dtype)
        ref[0, first_head + 1] = x[:, HEAD_DIM:].astype(ref.dtype)

    scale = HEAD_DIM ** -0.5
    qa_w = A_HEADS * HEAD_DIM
    kv_w = A_KV * HEAD_DIM
    off = 0
    for c in range(qa_w // LANES):
        x = o[:, off + c * LANES: off + (c + 1) * LANES]
        put(qa_ref, 2 * c, _rope(head_norm(x, qg_ref[...]), cos, sin, HEAD_DIM) * scale)
    off += qa_w
    put(ka_ref, 0, _rope(head_norm(o[:, off: off + kv_w], kg_ref[...]), cos, sin, HEAD_DIM))
    off += kv_w
    put(va_ref, 0, o[:, off: off + kv_w])
    off += kv_w
    for c in range(qa_w // LANES):
        x = o[:, off + c * LANES: off + (c + 1) * LANES]
        put(qb_ref, 2 * c, _rope(x, cos, sin, HEAD_DIM) * scale)
    off += qa_w
    put(kb_ref, 0, _rope(o[:, off: off + kv_w], cos, sin, HEAD_DIM))
    off += kv_w
    put(vb_ref, 0, o[:, off: off + kv_w])


def _att_project(hh, mod, norm_g, w_in, q_g, k_g, cos, sin, L):
    bsz, lt, d = hh.shape
    tm = ROW_TILE
    n = w_in.shape[1]
    gm = jnp.asarray(np.kron(np.eye(LANES // HEAD_DIM), np.full((HEAD_DIM, HEAD_DIM), 1.0 / HEAD_DIM)), BF16)
    tile2 = lambda v: jnp.tile(v.reshape(1, HEAD_DIM), (1, LANES // HEAD_DIM))
    qshape = jax.ShapeDtypeStruct((bsz, A_HEADS, lt, HEAD_DIM), BF16)
    kshape = jax.ShapeDtypeStruct((bsz, A_KV, lt, HEAD_DIM), BF16)
    qspec = pl.BlockSpec((1, A_HEADS, tm, HEAD_DIM), lambda b, i: (b, 0, i, 0))
    kspec = pl.BlockSpec((1, A_KV, tm, HEAD_DIM), lambda b, i: (b, 0, i, 0))
    const = lambda shape: pl.BlockSpec(shape, lambda b, i: (0,) * len(shape))
    return pl.pallas_call(
        _att_proj_kernel,
        grid=(bsz, lt // tm),
        in_specs=[pl.BlockSpec((1, tm, d), lambda b, i: (b, i, 0)),
                  pl.BlockSpec((1, 1, 6, d), lambda b, i: (b, i // (L // tm), 0, 0)),
                  const((1, d)), const((d, n)), const((1, LANES)), const((1, LANES)),
                  pl.BlockSpec((tm, LANES), lambda b, i: (i, 0)),
                  pl.BlockSpec((tm, LANES), lambda b, i: (i, 0)),
                  const((LANES, LANES))],
        out_specs=[qspec, kspec, kspec, qspec, kspec, kspec],
        out_shape=[qshape, kshape, kshape, qshape, kshape, kshape],
        compiler_params=_cparams(("parallel", "parallel")),
        name="att_project",
    )(hh, mod, norm_g.reshape(1, d), w_in.astype(BF16), tile2(q_g), tile2(k_g), cos, sin, gm)


def _softmax_pv(score_parts, value_parts, extra_logit=None):
    m = functools.reduce(jnp.maximum, [jnp.max(s, axis=-1, keepdims=True) for s in score_parts])
    if extra_logit is not None:
        m = jnp.maximum(m, extra_logit)
    l = 0.0
    acc = 0.0
    for s, v in zip(score_parts, value_parts):
        p = jnp.exp(s - m)
        l = l + jnp.sum(p, axis=-1, keepdims=True)
        acc = acc + jnp.dot(p.astype(BF16), v, preferred_element_type=F32)
    if extra_logit is not None:
        l = l + jnp.exp(extra_logit - m)
    return acc / l


def _att_kernel(sink_ref, qa_ref, ka_ref, va_ref, qb_ref, kb_ref, vb_ref, oa_ref, ob_ref, *, L, LC):
    i = pl.program_id(1)
    tq = ATT_TQ
    group = A_HEADS // A_KV
    band = tq + 2 * WINDOW

    def stacked_q(ref, kvh):
        return ref[0, kvh * group:(kvh + 1) * group].reshape(group * tq, HEAD_DIM)

    def put(ref, kvh, o):
        for g in range(group):
            hd = kvh * group + g
            ref[0, :, hd * HEAD_DIM:(hd + 1) * HEAD_DIM] = o[g * tq:(g + 1) * tq].astype(ref.dtype)

    def sink_col(kvh):
        return jnp.concatenate([jnp.full((tq, 1), sink_ref[kvh * group + g], F32) for g in range(group)], axis=0)

    @pl.when(i < L // tq)
    def _latent():
        for kvh in range(A_KV):
            q = stacked_q(qa_ref, kvh)
            put(oa_ref, kvh, _softmax_pv([_bdot_nt(q, ka_ref[0, kvh])], [va_ref[0, kvh]]))
            q = stacked_q(qb_ref, kvh)
            start = pl.multiple_of(jnp.clip((i - 1) * tq, 0, L - band), tq)
            s_band = _bdot_nt(q, kb_ref[0, kvh, pl.ds(start, band), :])
            qpos = i * tq + (lax.broadcasted_iota(jnp.int32, s_band.shape, 0) % tq)
            kpos = start + lax.broadcasted_iota(jnp.int32, s_band.shape, 1)
            s_band = jnp.where(jnp.abs(qpos - kpos) <= WINDOW, s_band, _NEG_INF)
            s_ctx = _bdot_nt(q, kb_ref[0, kvh, L:L + LC, :])
            put(ob_ref, kvh, _softmax_pv([s_band, s_ctx],
                                         [vb_ref[0, kvh, pl.ds(start, band), :], vb_ref[0, kvh, L:L + LC, :]],
                                         sink_col(kvh)))

    @pl.when(i >= L // tq)
    def _context():
        for kvh in range(A_KV):
            q = stacked_q(qa_ref, kvh)
            put(oa_ref, kvh, _softmax_pv([_bdot_nt(q, ka_ref[0, kvh, L:L + LC, :])], [va_ref[0, kvh, L:L + LC, :]]))
            q = stacked_q(qb_ref, kvh)
            put(ob_ref, kvh, _softmax_pv([_bdot_nt(q, kb_ref[0, kvh, L:L + LC, :])], [vb_ref[0, kvh, L:L + LC, :]],
                                         sink_col(kvh)))


def _attention(qa, ka, va, qb, kb, vb, sink, L):
    bsz, _, lt, _ = qa.shape
    tq = ATT_TQ
    qspec = pl.BlockSpec((1, A_HEADS, tq, HEAD_DIM), lambda b, i: (b, 0, i, 0))
    kspec = pl.BlockSpec((1, A_KV, lt, HEAD_DIM), lambda b, i: (b, 0, 0, 0))
    ospec = pl.BlockSpec((1, tq, A_HEADS * HEAD_DIM), lambda b, i: (b, i, 0))
    oshape = jax.ShapeDtypeStruct((bsz, lt, A_HEADS * HEAD_DIM), BF16)
    return pl.pallas_call(
        functools.partial(_att_kernel, L=L, LC=lt - L),
        grid=(bsz, lt // tq),
        in_specs=[pl.BlockSpec(memory_space=pltpu.SMEM), qspec, kspec, kspec, qspec, kspec, kspec],
        out_specs=[ospec, ospec],
        out_shape=[oshape, oshape],
        compiler_params=_cparams(("parallel", "parallel")),
        name="attention",
    )(sink, qa, ka, va, qb, kb, vb)


def _att_out_kernel(oa_ref, ob_ref, w_ref, h_ref, mod_ref, out_ref):
    half = oa_ref.shape[-1]
    y = (jnp.dot(oa_ref[0], w_ref[:half, :], preferred_element_type=F32)
         + jnp.dot(ob_ref[0], w_ref[half:, :], preferred_element_type=F32))
    out_ref[0] = h_ref[0] + mod_ref[0, 0][2:3] * y


def _att_out(oa, ob, w_out, hh, mod, L):
    bsz, lt, d = hh.shape
    tm = ROW_TILE
    half = oa.shape[-1]
    return pl.pallas_call(
        _att_out_kernel,
        grid=(bsz, lt // tm),
        in_specs=[pl.BlockSpec((1, tm, half), lambda b, i: (b, i, 0)),
                  pl.BlockSpec((1, tm, half), lambda b, i: (b, i, 0)),
                  pl.BlockSpec((2 * half, d), lambda b, i: (0, 0)),
                  pl.BlockSpec((1, tm, d), lambda b, i: (b, i, 0)),
                  pl.BlockSpec((1, 1, 6, d), lambda b, i: (b, i // (L // tm), 0, 0))],
        out_specs=pl.BlockSpec((1, tm, d), lambda b, i: (b, i, 0)),
        out_shape=jax.ShapeDtypeStruct(hh.shape, F32),
        input_output_aliases={3: 0},
        compiler_params=_cparams(("parallel", "parallel")),
        name="att_out",
    )(oa, ob, w_out.astype(BF16), hh, mod)


def _sort_network(n):
    pairs = []
    p = 1
    while p < n:
        k = p
        while k >= 1:
            for j in range(k % p, n - k, 2 * k):
                for i in range(min(k, n - j - k)):
                    if (i + j) // (2 * p) == (i + j + k) // (2 * p):
                        pairs.append((i + j, i + j + k))
            k //= 2
        p *= 2
    return pairs


def _pop_sorted(lists, n, singles=()):
    lists, singles = list(lists), list(singles)
    out = []
    for rnd in range(n):
        head = functools.reduce(jnp.maximum, [lists[0]] + singles)
        m = jnp.max(head, axis=0, keepdims=True)
        out.append(m)
        if rnd == n - 1:
            break
        hit = lists[0] == m
        depth = min(len(lists), n - rnd - 1)
        for d in range(depth):
            nxt = lists[d + 1] if d + 1 < len(lists) else _NEG_INF
            lists[d] = jnp.where(hit, nxt, lists[d])
        singles = [jnp.where(s == m, _NEG_INF, s) for s in singles]
    return out


def _top_values(x, n):
    slabs = [x[r * 8:(r + 1) * 8] for r in range(x.shape[0] // 8)]
    for i, j in _sort_network(len(slabs)):
        slabs[i], slabs[j] = jnp.maximum(slabs[i], slabs[j]), jnp.minimum(slabs[i], slabs[j])
    return _pop_sorted(slabs, n)


def _peer_route_kernel(h_ref, mod_ref, g_ref, wq_ref, keys_ref, at_ref, thr_ref, e0_ref, r1_ref, e1_ref, q_scr):
    m = mod_ref[0, 0]
    a = _norm_mod(h_ref[0], g_ref[...], m[3:4], m[4:5])
    at_ref[0] = a.T.astype(BF16)
    q_scr[...] = jnp.dot(a.astype(BF16), wq_ref[...], preferred_element_type=F32)
    k = PEER_TOPK
    n = k + 1

    def head(hd, carry):
        col = pl.multiple_of(hd * 2 * N_KEYS, 2 * N_KEYS)
        s0 = _bdot_nt(keys_ref[0], q_scr[:, pl.ds(col, N_KEYS)])
        s1 = _bdot_nt(keys_ref[1], q_scr[:, pl.ds(col + N_KEYS, N_KEYS)])
        tokens = s0.shape[1]
        top0 = _top_values(s0, n)
        top1 = _top_values(s1, n)
        first = jnp.concatenate(top0 + [jnp.full((24 - n, tokens), _NEG_INF, F32)], axis=0)
        best = _pop_sorted([first[0:8] + t for t in top1], n, [first[8:16] + top1[0], first[16:24] + top1[0]])
        z = functools.reduce(lambda x, y: x + y, [jnp.exp(b - best[0]) for b in best[:k]])
        tau = 0.5 * (best[k - 1] + best[k])
        r1 = jnp.zeros_like(s1)
        need = jnp.full_like(s0, n + 1.0)
        for b in range(n):
            r1 = jnp.where(s1 >= top1[n - 1 - b], b + 1.0, r1)
            need = jnp.where(s0 >= tau - top1[b], float(n - b), need)
        thr_ref[0, hd] = need
        e0_ref[0, hd] = jnp.exp(s0 - top0[0])
        r1_ref[0, hd] = pltpu.bitcast(r1.astype(BF16), jnp.uint32)
        e1_ref[0, hd] = pltpu.bitcast((jnp.exp(s1 - top1[0]) / z).astype(BF16), jnp.uint32)
        return carry

    lax.fori_loop(0, PEER_HEADS, head, 0, unroll=True)


def _peer_route(hh, mod, norm_g, w_q, sub_keys, L):
    bsz, lt, d = hh.shape
    tm = ROW_TILE
    nq = w_q.shape[1]
    tok = lambda: pl.BlockSpec((1, PEER_HEADS, N_KEYS, tm), lambda b, i: (b, 0, 0, i))
    tshape = jax.ShapeDtypeStruct((bsz, PEER_HEADS, N_KEYS, lt), F32)
    pshape = jax.ShapeDtypeStruct((bsz, PEER_HEADS, N_KEYS // 2, lt), jnp.uint32)
    ptok = pl.BlockSpec((1, PEER_HEADS, N_KEYS // 2, tm), lambda b, i: (b, 0, 0, i))
    return pl.pallas_call(
        _peer_route_kernel,
        grid=(bsz, lt // tm),
        in_specs=[pl.BlockSpec((1, tm, d), lambda b, i: (b, i, 0)),
                  pl.BlockSpec((1, 1, 6, d), lambda b, i: (b, i // (L // tm), 0, 0)),
                  pl.BlockSpec((1, d), lambda b, i: (0, 0)),
                  pl.BlockSpec((d, nq), lambda b, i: (0, 0)),
                  pl.BlockSpec((2, N_KEYS, N_KEYS), lambda b, i: (0, 0, 0))],
        out_specs=[pl.BlockSpec((1, d, tm), lambda b, i: (b, 0, i)), tok(), tok(), ptok, ptok],
        out_shape=[jax.ShapeDtypeStruct((bsz, d, lt), BF16), tshape, tshape, pshape, pshape],
        scratch_shapes=[pltpu.VMEM((tm, nq), F32)],
        compiler_params=_cparams(("parallel", "parallel")),
        name="peer_route",
    )(hh, mod, norm_g.reshape(1, d), w_q.astype(BF16), sub_keys.astype(BF16))


def _gelu(x):
    return 0.5 * x * (1.0 + lax.erf(x * (1.0 / math.sqrt(2.0))))


def _peer_expert_kernel(a_ref, thr_ref, e0_ref, r1_ref, e1_ref, u_ref, vt_ref, h_ref, mod_ref, fg_ref,
                        out_ref, yt_scr, wa_scr, *, final):
    e = pl.program_id(1)
    t = pl.program_id(2)
    tm = a_ref.shape[2]

    @pl.when(e == 0)
    def _():
        yt_scr[t] = jnp.zeros(yt_scr.shape[1:], F32)

    for t0 in range(0, tm, PEER_TOK):
        ts = slice(t0, t0 + PEER_TOK)
        y = None
        for q in range(PEER_EB // PEER_SUB):
            rows = slice(q * PEER_SUB, (q + 1) * PEER_SUB)
            act = jnp.dot(u_ref[rows, :], a_ref[0, :, ts], preferred_element_type=F32)
            for il in range(PEER_SUB // N_KEYS):
                key0 = q * (PEER_SUB // N_KEYS) + il
                for c in range(PEER_TOK // LANES):
                    cs = slice(t0 + c * LANES, t0 + (c + 1) * LANES)
                    row = lambda ref, hd: jnp.broadcast_to(ref[0, hd, key0:key0 + 1, cs], (BF16_ROWS, LANES)).astype(BF16)
                    thr = [row(thr_ref, hd) for hd in range(PEER_HEADS)]
                    e0 = [row(e0_ref, hd) for hd in range(PEER_HEADS)]
                    zero = jnp.zeros((BF16_ROWS, LANES), BF16)
                    for r in range(N_KEYS // BF16_ROWS):
                        ws = slice(r * BF16_ROWS // 2, (r + 1) * BF16_ROWS // 2)
                        gate = None
                        for hd in range(PEER_HEADS):
                            rank = pltpu.bitcast(r1_ref[0, hd, ws, cs], BF16)
                            weight = pltpu.bitcast(e1_ref[0, hd, ws, cs], BF16)
                            part = jnp.where(rank >= thr[hd], weight * e0[hd], zero)
                            gate = part if gate is None else gate + part
                        ars = slice(il * N_KEYS + r * BF16_ROWS, il * N_KEYS + (r + 1) * BF16_ROWS)
                        wa_scr[q * PEER_SUB + ars.start: q * PEER_SUB + ars.stop, cs] = (
                            gate * _gelu(act[ars, c * LANES:(c + 1) * LANES]).astype(BF16))
            part = jnp.dot(vt_ref[:, rows], wa_scr[rows, ts], preferred_element_type=F32)
            y = part if y is None else y + part
        yt_scr[t, :, ts] += y

    @pl.when(e == pl.num_programs(1) - 1)
    def _():
        hn = h_ref[0] + mod_ref[0, 0][5:6] * yt_scr[t].T
        if final:
            hn = hn * lax.rsqrt(jnp.mean(hn * hn, axis=-1, keepdims=True) + EPS) * fg_ref[...]
        out_ref[0] = hn


def _peer_expert(hh, mod, a, thr, e0, r1, e1, u_bf, vt_bf, final_g, L, *, with_ctx, final):
    bsz, lt, d = hh.shape
    n_exp = u_bf.shape[0]
    tm = PEER_TM
    n_tiles = (lt if with_ctx else L) // tm
    keys_per_step = PEER_EB // N_KEYS
    n_blocks = n_exp // PEER_EB
    tok = lambda r: pl.BlockSpec((1, PEER_HEADS, r, tm), (lambda b, e, t: (b, 0, e, t)) if r == keys_per_step
                                 else (lambda b, e, t: (b, 0, 0, t)))
    row_spec = pl.BlockSpec((1, tm, d), lambda b, e, t: (b, jnp.where(e == n_blocks - 1, t, 0), 0))
    if final:
        out_shape = jax.ShapeDtypeStruct((bsz, L, d), F32)
        aliases = {}
    else:
        out_shape = jax.ShapeDtypeStruct(hh.shape, F32)
        aliases = {7: 0}
    return pl.pallas_call(
        functools.partial(_peer_expert_kernel, final=final),
        grid=(bsz, n_blocks, n_tiles),
        in_specs=[pl.BlockSpec((1, d, tm), lambda b, e, t: (b, 0, t)),
                  tok(keys_per_step), tok(keys_per_step), tok(N_KEYS // 2), tok(N_KEYS // 2),
                  pl.BlockSpec((PEER_EB, d), lambda b, e, t: (e, 0)),
                  pl.BlockSpec((d, PEER_EB), lambda b, e, t: (0, e)),
                  row_spec,
                  pl.BlockSpec((1, 1, 6, d), lambda b, e, t: (b, t // (L // tm), 0, 0)),
                  pl.BlockSpec((1, d), lambda b, e, t: (0, 0))],
        out_specs=row_spec,
        out_shape=out_shape,
        input_output_aliases=aliases,
        scratch_shapes=[pltpu.VMEM((n_tiles, d, tm), F32), pltpu.VMEM((PEER_EB, tm), BF16)],
        compiler_params=_cparams(("parallel", "arbitrary", "arbitrary")),
        name="peer_expert",
    )(a, thr, e0, r1, e1, u_bf, vt_bf, hh, mod, final_g.reshape(1, d))


def _peer(hh, mod, norm_g, w_q, sub_keys, u, v, final_g, L, *, with_ctx, final):
    a, thr, e0, r1, e1 = _peer_route(hh, mod, norm_g, w_q, sub_keys, L)
    return _peer_expert(hh, mod, a, thr, e0, r1, e1, u.astype(BF16), v.astype(BF16).T, final_g, L,
                        with_ctx=with_ctx, final=final)


REC_QKV = C_HEADS * 3 * C_DK
REC_Z = C_HEADS * C_DK
REC_GATES = 4 * C_HEADS
REC_HD = D_HEADS * D_DK


def _rec_proj_kernel(h_ref, mod_ref, g_ref, w_ref, alog_ref, dtb_ref, cos_ref, sin_ref,
                     qkv_ref, z_ref, qd_ref, kd_ref, vd_ref, gd_ref, gate_ref):
    m = mod_ref[0, 0]
    a = _norm_mod(h_ref[0], g_ref[...], m[0:1], m[1:2])
    o = _bdot(a, w_ref[...])
    cos, sin = cos_ref[...], sin_ref[...]
    off = 0
    qkv_ref[0] = o[:, off:off + REC_QKV]
    off += REC_QKV
    z_ref[0] = o[:, off:off + REC_Z]
    off += REC_Z
    for hd in range(D_HEADS):
        qd_ref[0, :, hd * D_DK:(hd + 1) * D_DK] = _rope(o[:, off + hd * D_DK: off + (hd + 1) * D_DK], cos, sin, D_DK)
    off += REC_HD
    for hd in range(D_HEADS):
        kd_ref[0, :, hd * D_DK:(hd + 1) * D_DK] = (
            _rope(o[:, off + hd * D_DK: off + (hd + 1) * D_DK], cos, sin, D_DK) * (D_DK ** -0.5))
    off += REC_HD
    vd_ref[0] = o[:, off:off + REC_HD]
    off += REC_HD
    gd_ref[0] = o[:, off:off + 2 * REC_HD]
    off += 2 * REC_HD
    x = o[:, off:off + LANES]
    lane = lax.broadcasted_iota(jnp.int32, x.shape, 1)
    xb = x + dtb_ref[...]
    softplus = jnp.maximum(xb, 0.0) + jnp.log(1.0 + jnp.exp(-jnp.abs(xb)))
    gate_ref[0] = jnp.where(lane < 2 * C_HEADS, -jnp.exp(alog_ref[...]) * softplus, _sigmoid(x))


def _rec_project(hh, mod, norm_g, w_in, a_log, dt_bias, cos, sin, L):
    bsz, lt, d = hh.shape
    tm = ROW_TILE
    parts = np.cumsum([REC_QKV, REC_Z, REC_GATES, REC_HD, REC_HD, REC_HD])
    qkv_w, z_w, gates_w, qd_w, kd_w, vd_w, gd_w = jnp.split(w_in, [int(p) for p in parts], axis=1)
    w = jnp.concatenate([qkv_w, z_w, qd_w, kd_w, vd_w, gd_w, gates_w, jnp.zeros((d, LANES - REC_GATES), F32)],
                        axis=1).astype(BF16)
    n = w.shape[1]
    pad = lambda p: jnp.zeros((1, LANES), F32).at[0, :2 * C_HEADS].set(p.reshape(-1))
    widths = [REC_QKV, REC_Z, REC_HD, REC_HD, REC_HD, 2 * REC_HD, LANES]
    const = lambda shape: pl.BlockSpec(shape, lambda b, i: (0,) * len(shape))
    return pl.pallas_call(
        _rec_proj_kernel,
        grid=(bsz, lt // tm),
        in_specs=[pl.BlockSpec((1, tm, d), lambda b, i: (b, i, 0)),
                  pl.BlockSpec((1, 1, 6, d), lambda b, i: (b, i // (L // tm), 0, 0)),
                  const((1, d)), const((d, n)), const((1, LANES)), const((1, LANES)),
                  pl.BlockSpec((tm, LANES), lambda b, i: (i, 0)),
                  pl.BlockSpec((tm, LANES), lambda b, i: (i, 0))],
        out_specs=[pl.BlockSpec((1, tm, wd), lambda b, i: (b, i, 0)) for wd in widths],
        out_shape=[jax.ShapeDtypeStruct((bsz, lt, wd), F32) for wd in widths],
        compiler_params=_cparams(("parallel", "parallel")),
        name="rec_project",
    )(hh, mod, norm_g.reshape(1, d), w, pad(a_log), pad(dt_bias), cos, sin)


def _rec_conv_kernel(x_ref, prev_ref, next_ref, w_ref, q_ref, k_ref, v_ref, *, L):
    i = pl.program_id(1)
    tl = x_ref.shape[1]
    n_lat = L // tl
    at_start = (i == 0) | (i == n_lat)
    at_end = (i == n_lat - 1) | (i == pl.num_programs(1) - 1)
    prev = jnp.where(at_start, 0.0, prev_ref[0])
    nxt = jnp.where(at_end, 0.0, next_ref[0])
    xx = jnp.concatenate([prev, x_ref[0], nxt], axis=0)
    n = tl + 16
    acc = 0.0
    for tap in range(CONV_W):
        shift = (CONV_W // 2 - tap) % n
        shifted = xx if shift == 0 else pltpu.roll(xx, shift, 0)
        acc = acc + shifted[8:8 + tl] * w_ref[tap:tap + 1, :]
    y = _silu(acc)
    hw = C_HEADS * C_DK
    for hd in range(C_HEADS):
        def l2(x):
            return x * lax.rsqrt(jnp.sum(x * x, axis=-1, keepdims=True) + EPS)
        sl = slice(hd * C_DK, (hd + 1) * C_DK)
        q_ref[0, :, sl] = l2(y[:, hd * C_DK:(hd + 1) * C_DK]) * (C_DK ** -0.5)
        k_ref[0, :, sl] = l2(y[:, hw + hd * C_DK: hw + (hd + 1) * C_DK])
    v_ref[0] = y[:, 2 * hw:]


def _rec_conv(qkv, conv_w, L):
    bsz, lt, ch = qkv.shape
    tl = ROW_TILE
    hb = tl // 8
    last = lt // 8 - 1
    hw = C_HEADS * C_DK
    return pl.pallas_call(
        functools.partial(_rec_conv_kernel, L=L),
        grid=(bsz, lt // tl),
        in_specs=[pl.BlockSpec((1, tl, ch), lambda b, i: (b, i, 0)),
                  pl.BlockSpec((1, 8, ch), lambda b, i: (b, jnp.maximum(i * hb - 1, 0), 0)),
                  pl.BlockSpec((1, 8, ch), lambda b, i: (b, jnp.minimum((i + 1) * hb, last), 0)),
                  pl.BlockSpec((8, ch), lambda b, i: (0, 0))],
        out_specs=[pl.BlockSpec((1, tl, hw), lambda b, i: (b, i, 0))] * 3,
        out_shape=[jax.ShapeDtypeStruct((bsz, lt, hw), F32)] * 3,
        compiler_params=_cparams(("parallel", "parallel")),
        name="rec_conv",
    )(qkv, qkv, qkv, jnp.zeros((8, ch), F32).at[:CONV_W].set(conv_w))


def _ret_log_gamma(hd):
    return float(np.log1p(-np.exp2(-(RET_DECAY_BASE + hd))))


def _rec_intra_kernel(qc_ref, kc_ref, vc_ref, qd_ref, kd_ref, vd_ref, gate_ref,
                      qs_ref, oi_ref, pp_ref, nn_ref, al_ref):
    c = CHUNK
    ci = lax.broadcasted_iota(jnp.int32, (c, c), 0)
    si = lax.broadcasted_iota(jnp.int32, (c, c), 1)
    pos = lax.broadcasted_iota(jnp.int32, (c, 1), 0).astype(F32)
    gates = gate_ref[0]
    la_parts = _split3(gates)
    ones_row = jnp.ones((1, LANES), F32)
    combos = []
    for d in range(2):
        incl = (si <= ci) if d == 0 else (si >= ci)
        strict = (si < ci) if d == 0 else (si > ci)
        tri = incl.astype(BF16)
        g_all = jnp.dot(jnp.concatenate([tri] * 3, axis=1), jnp.concatenate(la_parts, axis=0),
                        preferred_element_type=F32)
        g_all_t = g_all.T
        last = c - 1 if d == 0 else 0
        for hd in range(C_HEADS):
            col = d * C_HEADS + hd
            sl = slice(hd * C_DK, (hd + 1) * C_DK)
            q, k, v = qc_ref[0, :, sl], kc_ref[0, :, sl], vc_ref[0, :, sl]
            gcol = g_all[:, col:col + 1]
            grow = g_all_t[col:col + 1, :]
            glast = g_all[last:last + 1, col:col + 1]
            beta = gates[:, 2 * C_HEADS + col: 2 * C_HEADS + col + 1]
            diff = gcol - grow
            dec_strict = jnp.exp(jnp.where(strict, diff, _NEG_INF))
            combos.append(dict(
                d=d, hd=hd, q=q, gcol=gcol, glast=glast,
                x=-(beta * _bdot_nt(k, k) * dec_strict),
                sol=jnp.concatenate([beta * v, (beta * jnp.exp(gcol)) * k], axis=1),
                qk=_bdot_nt(q, k) * jnp.exp(jnp.where(incl, diff, _NEG_INF)),
                kend=k * jnp.exp(glast - gcol)))
    for d in range(2):
        incl = (si <= ci) if d == 0 else (si >= ci)
        for hd in range(D_HEADS):
            lg = _ret_log_gamma(hd)
            sl = slice(hd * D_DK, (hd + 1) * D_DK)
            q, k, v = qd_ref[0, :, sl], kd_ref[0, :, sl], vd_ref[0, :, sl]
            steps = pos if d == 0 else (c - 1.0) - pos
            dist = (ci - si) if d == 0 else (si - ci)
            dmat = jnp.exp(jnp.where(incl, lg * dist.astype(F32), _NEG_INF))
            qk = _bdot_nt(q, k) * dmat
            qs_ref[0, 0, d, C_HEADS + hd] = (q * jnp.exp(lg * (steps + 1.0))).astype(qs_ref.dtype)
            oi_ref[0, 0, d, C_HEADS + hd] = _bdot(qk, v)
            nn_ref[0, 0, d, C_HEADS + hd] = _bdot_tn(k * jnp.exp(lg * ((c - 1.0) - steps)), v)
            al_ref[0, 0, d, C_HEADS + hd:C_HEADS + hd + 1, :] = math.exp(lg * c) * ones_row
    for cb in combos:
        cb["kend_t"] = cb["kend"].T.astype(BF16)
    levels = int(math.log2(c))
    for lvl in range(levels):
        for cb in combos:
            cb["sol"] = cb["sol"] + _dot3(cb["x"], cb["sol"])
        if lvl < levels - 1:
            for cb in combos:
                cb["x"] = _dot3(cb["x"], cb["x"])
    for cb in combos:
        cb["out"] = _bdot(cb["qk"], cb["sol"])
    for cb in combos:
        cb["state"] = jnp.dot(cb["kend_t"], cb["sol"].astype(BF16), preferred_element_type=F32)
    for cb in combos:
        d, hd = cb["d"], cb["hd"]
        qs_ref[0, 0, d, hd] = (jnp.exp(cb["gcol"]) * cb["q"] - cb["out"][:, C_DK:]).astype(qs_ref.dtype)
        oi_ref[0, 0, d, hd] = cb["out"][:, :C_DK]
        pp_ref[0, 0, d, hd] = cb["state"][:, C_DK:].astype(pp_ref.dtype)
        nn_ref[0, 0, d, hd] = cb["state"][:, :C_DK]
        al_ref[0, 0, d, hd:hd + 1, :] = jnp.exp(cb["glast"]) * ones_row


def _rec_intra(qc, kc, vc, qd, kd, vd, gate):
    bsz, lt, hw = qc.shape
    nc = lt // CHUNK
    nh = C_HEADS + D_HEADS
    row = lambda w: pl.BlockSpec((1, CHUNK, w), lambda b, n: (b, n, 0))
    lead = lambda *tail: pl.BlockSpec((1, 1, 2) + tail, lambda b, n: (b, n, 0) + (0,) * len(tail))
    return pl.pallas_call(
        _rec_intra_kernel,
        grid=(bsz, nc),
        in_specs=[row(hw)] * 6 + [row(LANES)],
        out_specs=[lead(nh, CHUNK, C_DK), lead(nh, CHUNK, C_DK), lead(C_HEADS, C_DK, C_DK), lead(nh, C_DK, C_DK),
                   lead(nh, LANES)],
        out_shape=[jax.ShapeDtypeStruct((bsz, nc, 2, nh, CHUNK, C_DK), BF16),
                   jax.ShapeDtypeStruct((bsz, nc, 2, nh, CHUNK, C_DK), F32),
                   jax.ShapeDtypeStruct((bsz, nc, 2, C_HEADS, C_DK, C_DK), BF16),
                   jax.ShapeDtypeStruct((bsz, nc, 2, nh, C_DK, C_DK), F32),
                   jax.ShapeDtypeStruct((bsz, nc, 2, nh, LANES), F32)],
        compiler_params=_cparams(("parallel", "parallel")),
        name="rec_intra",
    )(qc, kc, vc, qd, kd, vd, gate)


def _rec_scan_kernel(qs_ref, oi_ref, pp_ref, nn_ref, al_ref, o_ref, s_scr):
    @pl.when(pl.program_id(2) == 0)
    def _():
        s_scr[...] = jnp.zeros_like(s_scr)

    for hd in range(C_HEADS + D_HEADS):
        s = s_scr[hd]
        sb = s.astype(BF16)
        o_ref[0, 0, :, hd * C_DK:(hd + 1) * C_DK] = (
            jnp.dot(qs_ref[0, 0, 0, hd], sb, preferred_element_type=F32) + oi_ref[0, 0, 0, hd])
        new = al_ref[0, 0, 0, hd:hd + 1, :] * s + nn_ref[0, 0, 0, hd]
        if hd < C_HEADS:
            new = new - jnp.dot(pp_ref[0, 0, 0, hd], sb, preferred_element_type=F32)
        s_scr[hd] = new


def _rec_scan(qs, oi, pp, nn, al, L):
    bsz, nc = qs.shape[:2]
    nh = C_HEADS + D_HEADS
    n_lat = L // CHUNK
    n_ctx = nc - n_lat

    def chunk(d, s):
        fwd = jnp.where(s < n_ctx, n_lat + s, s - n_ctx)
        bwd = nc - 1 - s
        return jnp.where(d == 0, fwd, bwd)

    lead = lambda *tail: pl.BlockSpec((1, 1, 1) + tail, lambda b, d, s: (b, chunk(d, s), d) + (0,) * len(tail))
    return pl.pallas_call(
        _rec_scan_kernel,
        grid=(bsz, 2, nc),
        in_specs=[lead(nh, CHUNK, C_DK), lead(nh, CHUNK, C_DK), lead(C_HEADS, C_DK, C_DK), lead(nh, C_DK, C_DK),
                  lead(nh, LANES)],
        out_specs=pl.BlockSpec((1, 1, CHUNK, nh * C_DK), lambda b, d, s: (b, d, chunk(d, s), 0)),
        out_shape=jax.ShapeDtypeStruct((bsz, 2, nc * CHUNK, nh * C_DK), F32),
        scratch_shapes=[pltpu.VMEM((nh, C_DK, C_DK), F32)],
        compiler_params=_cparams(("parallel", "parallel", "arbitrary")),
        name="rec_scan",
    )(qs, oi, pp, nn, al)


def _rec_out_kernel(of_ref, ob_ref, z_ref, gd_ref, og_ref, gn_ref, w_ref, h_ref, mod_ref, out_ref):
    hw = C_HEADS * C_DK
    of, ob = of_ref[0, 0], ob_ref[0, 0]
    z, gd = z_ref[0], gd_ref[0]
    parts = []
    for hd in range(C_HEADS):
        sl = slice(hd * C_DK, (hd + 1) * C_DK)
        x = of[:, sl] + ob[:, sl]
        y = x * lax.rsqrt(jnp.mean(x * x, axis=-1, keepdims=True) + EPS) * og_ref[...]
        parts.append(y * _silu(z[:, sl]))
    for hd in range(D_HEADS):
        sl = slice(hd * D_DK, (hd + 1) * D_DK)
        y = 0.0
        for d, o in enumerate((of, ob)):
            x = o[:, hw + hd * D_DK: hw + (hd + 1) * D_DK]
            mu = jnp.mean(x, axis=-1, keepdims=True)
            xc = x - mu
            var = jnp.mean(xc * xc, axis=-1, keepdims=True)
            y = y + xc * lax.rsqrt(var + EPS) * gn_ref[:, sl] * _silu(gd[:, d * hw + hd * D_DK: d * hw + (hd + 1) * D_DK])
        parts.append(y)
    mix = jnp.concatenate(parts, axis=1)
    out_ref[0] = h_ref[0] + mod_ref[0, 0][2:3] * _bdot(mix, w_ref[...])


def _rec_out(o_scan, z, gd, out_g, gn_g, w_out, hh, mod, L):
    bsz, lt, d = hh.shape
    tm = ROW_TILE
    hw = C_HEADS * C_DK
    return pl.pallas_call(
        _rec_out_kernel,
        grid=(bsz, L // tm),
        in_specs=[pl.BlockSpec((1, 1, tm, 2 * hw), lambda b, i: (b, 0, i, 0)),
                  pl.BlockSpec((1, 1, tm, 2 * hw), lambda b, i: (b, 1, i, 0)),
                  pl.BlockSpec((1, tm, hw), lambda b, i: (b, i, 0)),
                  pl.BlockSpec((1, tm, 2 * hw), lambda b, i: (b, i, 0)),
                  pl.BlockSpec((1, C_DK), lambda b, i: (0, 0)),
                  pl.BlockSpec((1, hw), lambda b, i: (0, 0)),
                  pl.BlockSpec((2 * hw, d), lambda b, i: (0, 0)),
                  pl.BlockSpec((1, tm, d), lambda b, i: (b, i, 0)),
                  pl.BlockSpec((1, 1, 6, d), lambda b, i: (b, 0, 0, 0))],
        out_specs=pl.BlockSpec((1, tm, d), lambda b, i: (b, i, 0)),
        out_shape=jax.ShapeDtypeStruct(hh.shape, F32),
        input_output_aliases={7: 0},
        compiler_params=_cparams(("parallel", "parallel")),
        name="rec_out",
    )(o_scan, o_scan, z, gd, out_g.reshape(1, C_DK), gn_g.reshape(1, hw), w_out.astype(BF16), hh, mod)


def _att_layer(hh, mod, norm_g, w_in, q_g, k_g, sink, w_out, L):
    cos, sin = _rope_tables(L, hh.shape[1] - L, HEAD_DIM)
    qa, ka, va, qb, kb, vb = _att_project(hh, mod, norm_g, w_in, q_g, k_g, cos, sin, L)
    oa, ob = _attention(qa, ka, va, qb, kb, vb, sink, L)
    return _att_out(oa, ob, w_out, hh, mod, L)


def _rec_layer(hh, mod, norm_g, w_in, conv_w, a_log, dt_bias, out_g, gn_g, w_out, L):
    cos, sin = _rope_tables(L, hh.shape[1] - L, D_DK)
    qkv, z, qd, kd, vd, gd, gate = _rec_project(hh, mod, norm_g, w_in, a_log, dt_bias, cos, sin, L)
    qc, kc, vc = _rec_conv(qkv, conv_w, L)
    qs, oi, pp, nn, al = _rec_intra(qc, kc, vc, qd, kd, vd, gate)
    o_scan = _rec_scan(qs, oi, pp, nn, al, L)
    return _rec_out(o_scan, z, gd, out_g, gn_g, w_out, hh, mod, L)


def kernel(x, c, ctx, c_ctx, mod_w, mod_b, norm1_g, norm2_g, att_w_in, att_q_norm, att_k_norm, att_sink, att_w_out,
           rec_w_in, rec_conv_w, rec_a_log, rec_dt_bias, rec_out_norm, rec_gn_g, rec_w_out,
           peer_w_q, peer_sub_keys, peer_u, peer_v, final_norm_g):
    L = x.shape[1]
    depth = mod_w.shape[0]
    mods = _modulation(c, c_ctx, mod_w, mod_b)
    hh = jnp.concatenate([x, ctx], axis=1)
    for layer in range(depth):
        last = layer == depth - 1
        i = layer // 2
        if layer % 2 == 0:
            hh = _att_layer(hh, mods[layer], norm1_g[layer], att_w_in[i], att_q_norm[i], att_k_norm[i], att_sink[i],
                            att_w_out[i], L)
        else:
            hh = _rec_layer(hh, mods[layer], norm1_g[layer], rec_w_in[i], rec_conv_w[i], rec_a_log[i], rec_dt_bias[i],
                            rec_out_norm[i], rec_gn_g[i], rec_w_out[i], L)
        hh = _peer(hh, mods[layer], norm2_g[layer], peer_w_q[layer], peer_sub_keys[layer], peer_u[layer],
                   peer_v[layer], final_norm_g, L, with_ctx=not last, final=last)
    return hh
```

```python
import functools
import math

import numpy as np
import jax
import jax.numpy as jnp
from jax import lax
from jax.experimental import pallas as pl
from jax.experimental.pallas import tpu as pltpu

F32 = jnp.float32
BF16 = jnp.bfloat16

GRID_W = 64
EPS = 1e-6
HEAD_DIM = 64
A_HEADS = 8
A_KV = 2
B_HEADS = 8
B_KV = 2
WINDOW = 128
ROPE_THETA = 10000.0
C_HEADS = 4
C_DK = 128
CONV_W = 5
CHUNK = 64
D_HEADS = 4
D_DK = 128
RET_DECAY_BASE = 5.0
PEER_HEADS = 8
N_KEYS = 128
PEER_TOPK = 16

LANES = 128
BF16_ROWS = 16
VMEM_LIMIT = 56 * 1024 * 1024

ROW_TILE = 256
ATT_TQ = 128
PEER_TM = 256
PEER_EB = 2048
PEER_SUB = 256
PEER_TOK = 256

_NEG_INF = float("-inf")


def _cparams(sem, flags=None):
    return pltpu.CompilerParams(dimension_semantics=sem, vmem_limit_bytes=VMEM_LIMIT, flags=flags)


def _bdot(a, b):
    return jnp.dot(a.astype(BF16), b.astype(BF16), preferred_element_type=F32)


def _bdot_nt(a, b):
    return lax.dot_general(a.astype(BF16), b.astype(BF16), (((1,), (1,)), ((), ())), preferred_element_type=F32)


def _bdot_tn(a, b):
    return jnp.dot(a.T.astype(BF16), b.astype(BF16), preferred_element_type=F32)


def _split3(a):
    hi = a.astype(BF16)
    r1 = a - hi.astype(F32)
    mid = r1.astype(BF16)
    lo = (r1 - mid.astype(F32)).astype(BF16)
    return hi, mid, lo


def _dot3(a, b):
    a_hi, a_lo, _ = _split3(a)
    b_hi, b_lo, _ = _split3(b)
    return jnp.dot(jnp.concatenate([a_hi, a_lo, a_hi], axis=1), jnp.concatenate([b_hi, b_hi, b_lo], axis=0),
                   preferred_element_type=F32)


def _sigmoid(x):
    return 1.0 / (1.0 + jnp.exp(-x))


def _silu(x):
    return x * _sigmoid(x)


def _norm_mod(x, g, shift, scale):
    r = lax.rsqrt(jnp.mean(x * x, axis=-1, keepdims=True) + EPS)
    return (x * r * g) * (1.0 + scale) + shift


def _rope(x, cos, sin_signed, head_dim):
    quarter = head_dim // 4
    lane = lax.broadcasted_iota(jnp.int32, x.shape, 1)
    first = (lane % (2 * quarter)) < quarter
    partner = jnp.where(first, pltpu.roll(x, LANES - quarter, 1), pltpu.roll(x, quarter, 1))
    return x * cos + partner * sin_signed


def _mod_kernel(c_ref, w_ref, b_ref, o_ref):
    o_ref[0] = _bdot(_silu(c_ref[...]), w_ref[0]) + b_ref[0]


def _modulation(c, c_ctx, mod_w, mod_b):
    depth, d, n = mod_w.shape
    bsz = c.shape[0]
    rows = 16
    cc = jnp.zeros((rows, d), F32).at[:bsz].set(c).at[bsz].set(c_ctx)
    tn = 1536
    out = pl.pallas_call(
        _mod_kernel,
        grid=(depth, n // tn),
        in_specs=[pl.BlockSpec((rows, d), lambda l, j: (0, 0)),
                  pl.BlockSpec((1, d, tn), lambda l, j: (l, 0, j)),
                  pl.BlockSpec((1, 1, tn), lambda l, j: (l, 0, j))],
        out_specs=pl.BlockSpec((1, rows, tn), lambda l, j: (l, 0, j)),
        out_shape=jax.ShapeDtypeStruct((depth, rows, n), F32),
        compiler_params=_cparams(("arbitrary", "arbitrary")),
        name="modulation",
    )(cc, mod_w, mod_b.reshape(depth, 1, n))
    lat = out[:, :bsz].reshape(depth, bsz, 1, 6, d)
    ctx = jnp.broadcast_to(out[:, bsz].reshape(depth, 1, 1, 6, d), (depth, bsz, 1, 6, d))
    return jnp.concatenate([lat, ctx], axis=2)


def _rope_tables(L, LC, head_dim):
    quarter, half = head_dim // 4, head_dim // 2
    freqs = ROPE_THETA ** (-jnp.arange(quarter, dtype=F32) / quarter)
    lane = np.arange(LANES)
    within = lane % head_dim
    use_col = within >= half
    fidx = within % quarter
    sign = np.where((within % half) < quarter, -1.0, 1.0).astype(np.float32)
    t = jnp.arange(L, dtype=jnp.int32)
    row, col = (t // GRID_W).astype(F32), (t % GRID_W).astype(F32)
    pos = jnp.where(use_col[None, :], col[:, None], row[:, None])
    ang = pos * freqs[fidx][None, :]
    cos = jnp.concatenate([jnp.cos(ang), jnp.ones((LC, LANES), F32)], axis=0)
    sin = jnp.concatenate([jnp.sin(ang) * sign[None, :], jnp.zeros((LC, LANES), F32)], axis=0)
    return cos, sin


def _att_proj_kernel(h_ref, mod_ref, g_ref, w_ref, qg_ref, kg_ref, cos_ref, sin_ref, gm_ref,
                     qa_ref, ka_ref, va_ref, qb_ref, kb_ref, vb_ref):
    m = mod_ref[0, 0]
    a = _norm_mod(h_ref[0], g_ref[...], m[0:1], m[1:2])
    o = _bdot(a, w_ref[...])
    cos, sin = cos_ref[...], sin_ref[...]
    gm = gm_ref[...]

    def head_norm(x, gain):
        sq = x * x
        hi = sq.astype(BF16)
        lo = (sq - hi.astype(F32)).astype(BF16)
        ms = jnp.dot(hi, gm, preferred_element_type=F32) + jnp.dot(lo, gm, preferred_element_type=F32)
        return x * lax.rsqrt(ms + EPS) * gain

    def put(ref, first_head, x):
        ref[0, first_head] = x[:, :HEAD_DIM].astype(ref.dtype)
        ref[0, first_head + 1] = x[:, HEAD_DIM:].astype(ref.dtype)

    scale = HEAD_DIM ** -0.5
    qa_w = A_HEADS * HEAD_DIM
    kv_w = A_KV * HEAD_DIM
    off = 0
    for c in range(qa_w // LANES):
        x = o[:, off + c * LANES: off + (c + 1) * LANES]
        put(qa_ref, 2 * c, _rope(head_norm(x, qg_ref[...]), cos, sin, HEAD_DIM) * scale)
    off += qa_w
    put(ka_ref, 0, _rope(head_norm(o[:, off: off + kv_w], kg_ref[...]), cos, sin, HEAD_DIM))
    off += kv_w
    put(va_ref, 0, o[:, off: off + kv_w])
    off += kv_w
    for c in range(qa_w // LANES):
        x = o[:, off + c * LANES: off + (c + 1) * LANES]
        put(qb_ref, 2 * c, _rope(x, cos, sin, HEAD_DIM) * scale)
    off += qa_w
    put(kb_ref, 0, _rope(o[:, off: off + kv_w], cos, sin, HEAD_DIM))
    off += kv_w
    put(vb_ref, 0, o[:, off: off + kv_w])


def _att_project(hh, mod, norm_g, w_in, q_g, k_g, cos, sin, L):
    bsz, lt, d = hh.shape
    tm = ROW_TILE
    n = w_in.shape[1]
    gm = jnp.asarray(np.kron(np.eye(LANES // HEAD_DIM), np.full((HEAD_DIM, HEAD_DIM), 1.0 / HEAD_DIM)), BF16)
    tile2 = lambda v: jnp.tile(v.reshape(1, HEAD_DIM), (1, LANES // HEAD_DIM))
    qshape = jax.ShapeDtypeStruct((bsz, A_HEADS, lt, HEAD_DIM), BF16)
    kshape = jax.ShapeDtypeStruct((bsz, A_KV, lt, HEAD_DIM), BF16)
    qspec = pl.BlockSpec((1, A_HEADS, tm, HEAD_DIM), lambda b, i: (b, 0, i, 0))
    kspec = pl.BlockSpec((1, A_KV, tm, HEAD_DIM), lambda b, i: (b, 0, i, 0))
    const = lambda shape: pl.BlockSpec(shape, lambda b, i: (0,) * len(shape))
    return pl.pallas_call(
        _att_proj_kernel,
        grid=(bsz, lt // tm),
        in_specs=[pl.BlockSpec((1, tm, d), lambda b, i: (b, i, 0)),
                  pl.BlockSpec((1, 1, 6, d), lambda b, i: (b, i // (L // tm), 0, 0)),
                  const((1, d)), const((d, n)), const((1, LANES)), const((1, LANES)),
                  pl.BlockSpec((tm, LANES), lambda b, i: (i, 0)),
                  pl.BlockSpec((tm, LANES), lambda b, i: (i, 0)),
                  const((LANES, LANES))],
        out_specs=[qspec, kspec, kspec, qspec, kspec, kspec],
        out_shape=[qshape, kshape, kshape, qshape, kshape, kshape],
        compiler_params=_cparams(("parallel", "parallel")),
        name="att_project",
    )(hh, mod, norm_g.reshape(1, d), w_in.astype(BF16), tile2(q_g), tile2(k_g), cos, sin, gm)


def _softmax_pv(score_parts, value_parts, extra_logit=None):
    m = functools.reduce(jnp.maximum, [jnp.max(s, axis=-1, keepdims=True) for s in score_parts])
    if extra_logit is not None:
        m = jnp.maximum(m, extra_logit)
    l = 0.0
    acc = 0.0
    for s, v in zip(score_parts, value_parts):
        p = jnp.exp(s - m)
        l = l + jnp.sum(p, axis=-1, keepdims=True)
        acc = acc + jnp.dot(p.astype(BF16), v, preferred_element_type=F32)
    if extra_logit is not None:
        l = l + jnp.exp(extra_logit - m)
    return acc / l


def _att_kernel(sink_ref, qa_ref, ka_ref, va_ref, qb_ref, kb_ref, vb_ref, oa_ref, ob_ref, *, L, LC):
    i = pl.program_id(1)
    tq = ATT_TQ
    group = A_HEADS // A_KV
    band = tq + 2 * WINDOW

    def stacked_q(ref, kvh):
        return ref[0, kvh * group:(kvh + 1) * group].reshape(group * tq, HEAD_DIM)

    def put(ref, kvh, o):
        for g in range(group):
            hd = kvh * group + g
            ref[0, :, hd * HEAD_DIM:(hd + 1) * HEAD_DIM] = o[g * tq:(g + 1) * tq].astype(ref.dtype)

    def sink_col(kvh):
        return jnp.concatenate([jnp.full((tq, 1), sink_ref[kvh * group + g], F32) for g in range(group)], axis=0)

    @pl.when(i < L // tq)
    def _latent():
        for kvh in range(A_KV):
            q = stacked_q(qa_ref, kvh)
            put(oa_ref, kvh, _softmax_pv([_bdot_nt(q, ka_ref[0, kvh])], [va_ref[0, kvh]]))
            q = stacked_q(qb_ref, kvh)
            start = pl.multiple_of(jnp.clip((i - 1) * tq, 0, L - band), tq)
            s_band = _bdot_nt(q, kb_ref[0, kvh, pl.ds(start, band), :])
            qpos = i * tq + (lax.broadcasted_iota(jnp.int32, s_band.shape, 0) % tq)
            kpos = start + lax.broadcasted_iota(jnp.int32, s_band.shape, 1)
            s_band = jnp.where(jnp.abs(qpos - kpos) <= WINDOW, s_band, _NEG_INF)
            s_ctx = _bdot_nt(q, kb_ref[0, kvh, L:L + LC, :])
            put(ob_ref, kvh, _softmax_pv([s_band, s_ctx],
                                         [vb_ref[0, kvh, pl.ds(start, band), :], vb_ref[0, kvh, L:L + LC, :]],
                                         sink_col(kvh)))

    @pl.when(i >= L // tq)
    def _context():
        for kvh in range(A_KV):
            q = stacked_q(qa_ref, kvh)
            put(oa_ref, kvh, _softmax_pv([_bdot_nt(q, ka_ref[0, kvh, L:L + LC, :])], [va_ref[0, kvh, L:L + LC, :]]))
            q = stacked_q(qb_ref, kvh)
            put(ob_ref, kvh, _softmax_pv([_bdot_nt(q, kb_ref[0, kvh, L:L + LC, :])], [vb_ref[0, kvh, L:L + LC, :]],
                                         sink_col(kvh)))


def _attention(qa, ka, va, qb, kb, vb, sink, L):
    bsz, _, lt, _ = qa.shape
    tq = ATT_TQ
    qspec = pl.BlockSpec((1, A_HEADS, tq, HEAD_DIM), lambda b, i: (b, 0, i, 0))
    kspec = pl.BlockSpec((1, A_KV, lt, HEAD_DIM), lambda b, i: (b, 0, 0, 0))
    ospec = pl.BlockSpec((1, tq, A_HEADS * HEAD_DIM), lambda b, i: (b, i, 0))
    oshape = jax.ShapeDtypeStruct((bsz, lt, A_HEADS * HEAD_DIM), BF16)
    return pl.pallas_call(
        functools.partial(_att_kernel, L=L, LC=lt - L),
        grid=(bsz, lt // tq),
        in_specs=[pl.BlockSpec(memory_space=pltpu.SMEM), qspec, kspec, kspec, qspec, kspec, kspec],
        out_specs=[ospec, ospec],
        out_shape=[oshape, oshape],
        compiler_params=_cparams(("parallel", "parallel")),
        name="attention",
    )(sink, qa, ka, va, qb, kb, vb)


def _att_out_kernel(oa_ref, ob_ref, w_ref, h_ref, mod_ref, out_ref):
    half = oa_ref.shape[-1]
    y = (jnp.dot(oa_ref[0], w_ref[:half, :], preferred_element_type=F32)
         + jnp.dot(ob_ref[0], w_ref[half:, :], preferred_element_type=F32))
    out_ref[0] = h_ref[0] + mod_ref[0, 0][2:3] * y


def _att_out(oa, ob, w_out, hh, mod, L):
    bsz, lt, d = hh.shape
    tm = ROW_TILE
    half = oa.shape[-1]
    return pl.pallas_call(
        _att_out_kernel,
        grid=(bsz, lt // tm),
        in_specs=[pl.BlockSpec((1, tm, half), lambda b, i: (b, i, 0)),
                  pl.BlockSpec((1, tm, half), lambda b, i: (b, i, 0)),
                  pl.BlockSpec((2 * half, d), lambda b, i: (0, 0)),
                  pl.BlockSpec((1, tm, d), lambda b, i: (b, i, 0)),
                  pl.BlockSpec((1, 1, 6, d), lambda b, i: (b, i // (L // tm), 0, 0))],
        out_specs=pl.BlockSpec((1, tm, d), lambda b, i: (b, i, 0)),
        out_shape=jax.ShapeDtypeStruct(hh.shape, F32),
        input_output_aliases={3: 0},
        compiler_params=_cparams(("parallel", "parallel")),
        name="att_out",
    )(oa, ob, w_out.astype(BF16), hh, mod)


def _sort_network(n):
    pairs = []
    p = 1
    while p < n:
        k = p
        while k >= 1:
            for j in range(k % p, n - k, 2 * k):
                for i in range(min(k, n - j - k)):
                    if (i + j) // (2 * p) == (i + j + k) // (2 * p):
                        pairs.append((i + j, i + j + k))
            k //= 2
        p *= 2
    return pairs


def _pop_sorted(lists, n, singles=()):
    lists, singles = list(lists), list(singles)
    out = []
    for rnd in range(n):
        head = functools.reduce(jnp.maximum, [lists[0]] + singles)
        m = jnp.max(head, axis=0, keepdims=True)
        out.append(m)
        if rnd == n - 1:
            break
        hit = lists[0] == m
        depth = min(len(lists), n - rnd - 1)
        for d in range(depth):
            nxt = lists[d + 1] if d + 1 < len(lists) else _NEG_INF
            lists[d] = jnp.where(hit, nxt, lists[d])
        singles = [jnp.where(s == m, _NEG_INF, s) for s in singles]
    return out


def _top_values(x, n):
    slabs = [x[r * 8:(r + 1) * 8] for r in range(x.shape[0] // 8)]
    for i, j in _sort_network(len(slabs)):
        slabs[i], slabs[j] = jnp.maximum(slabs[i], slabs[j]), jnp.minimum(slabs[i], slabs[j])
    return _pop_sorted(slabs, n)


def _peer_route_kernel(h_ref, mod_ref, g_ref, wq_ref, keys_ref, at_ref, thr_ref, e0_ref, r1_ref, e1_ref, q_scr):
    m = mod_ref[0, 0]
    a = _norm_mod(h_ref[0], g_ref[...], m[3:4], m[4:5])
    at_ref[0] = a.T.astype(BF16)
    q_scr[...] = jnp.dot(a.astype(BF16), wq_ref[...], preferred_element_type=F32)
    k = PEER_TOPK
    n = k + 1

    def head(hd, carry):
        col = pl.multiple_of(hd * 2 * N_KEYS, 2 * N_KEYS)
        s0 = _bdot_nt(keys_ref[0], q_scr[:, pl.ds(col, N_KEYS)])
        s1 = _bdot_nt(keys_ref[1], q_scr[:, pl.ds(col + N_KEYS, N_KEYS)])
        tokens = s0.shape[1]
        top0 = _top_values(s0, n)
        top1 = _top_values(s1, n)
        first = jnp.concatenate(top0 + [jnp.full((24 - n, tokens), _NEG_INF, F32)], axis=0)
        best = _pop_sorted([first[0:8] + t for t in top1], n, [first[8:16] + top1[0], first[16:24] + top1[0]])
        z = functools.reduce(lambda x, y: x + y, [jnp.exp(b - best[0]) for b in best[:k]])
        tau = 0.5 * (best[k - 1] + best[k])
        r1 = jnp.zeros_like(s1)
        need = jnp.full_like(s0, n + 1.0)
        for b in range(n):
            r1 = jnp.where(s1 >= top1[n - 1 - b], b + 1.0, r1)
            need = jnp.where(s0 >= tau - top1[b], float(n - b), need)
        thr_ref[0, hd] = need
        e0_ref[0, hd] = jnp.exp(s0 - top0[0])
        r1_ref[0, hd] = pltpu.bitcast(r1.astype(BF16), jnp.uint32)
        e1_ref[0, hd] = pltpu.bitcast((jnp.exp(s1 - top1[0]) / z).astype(BF16), jnp.uint32)
        return carry

    lax.fori_loop(0, PEER_HEADS, head, 0, unroll=True)


def _peer_route(hh, mod, norm_g, w_q, sub_keys, L):
    bsz, lt, d = hh.shape
    tm = ROW_TILE
    nq = w_q.shape[1]
    tok = lambda: pl.BlockSpec((1, PEER_HEADS, N_KEYS, tm), lambda b, i: (b, 0, 0, i))
    tshape = jax.ShapeDtypeStruct((bsz, PEER_HEADS, N_KEYS, lt), F32)
    pshape = jax.ShapeDtypeStruct((bsz, PEER_HEADS, N_KEYS // 2, lt), jnp.uint32)
    ptok = pl.BlockSpec((1, PEER_HEADS, N_KEYS // 2, tm), lambda b, i: (b, 0, 0, i))
    return pl.pallas_call(
        _peer_route_kernel,
        grid=(bsz, lt // tm),
        in_specs=[pl.BlockSpec((1, tm, d), lambda b, i: (b, i, 0)),
                  pl.BlockSpec((1, 1, 6, d), lambda b, i: (b, i // (L // tm), 0, 0)),
                  pl.BlockSpec((1, d), lambda b, i: (0, 0)),
                  pl.BlockSpec((d, nq), lambda b, i: (0, 0)),
                  pl.BlockSpec((2, N_KEYS, N_KEYS), lambda b, i: (0, 0, 0))],
        out_specs=[pl.BlockSpec((1, d, tm), lambda b, i: (b, 0, i)), tok(), tok(), ptok, ptok],
        out_shape=[jax.ShapeDtypeStruct((bsz, d, lt), BF16), tshape, tshape, pshape, pshape],
        scratch_shapes=[pltpu.VMEM((tm, nq), F32)],
        compiler_params=_cparams(("parallel", "parallel")),
        name="peer_route",
    )(hh, mod, norm_g.reshape(1, d), w_q.astype(BF16), sub_keys.astype(BF16))


def _gelu(x):
    return 0.5 * x * (1.0 + lax.erf(x * (1.0 / math.sqrt(2.0))))


def _peer_expert_kernel(a_ref, an_ref, thr_ref, e0_ref, r1_ref, e1_ref, u_ref, vt_ref, h_ref, mod_ref, fg_ref,
                        out_ref, yt_scr, wa_scr, at0_scr, at1_scr, *, final):
    e = pl.program_id(1)
    t = pl.program_id(2)
    tm = a_ref.shape[2]
    n_sub = PEER_EB // PEER_SUB
    rows = [slice(q * PEER_SUB, (q + 1) * PEER_SUB) for q in range(n_sub)]

    def activations(dst, src, q):
        words = slice(rows[q].start // 2, rows[q].stop // 2)
        dst[rows[q], :] = jnp.dot(pltpu.bitcast(u_ref[words, :], BF16), src[0], preferred_element_type=F32)

    @pl.when(e == 0)
    def _():
        yt_scr[t] = jnp.zeros(yt_scr.shape[1:], F32)

    @pl.when(t == 0)
    def _():
        for q in range(n_sub):
            activations(at0_scr, a_ref, q)

    def step(act_ref, next_ref):
        y = None
        for q in range(n_sub):
            activations(next_ref, an_ref, q)
            for il in range(PEER_SUB // N_KEYS):
                key0 = q * (PEER_SUB // N_KEYS) + il
                for c in range(tm // LANES):
                    cs = slice(c * LANES, (c + 1) * LANES)
                    row = lambda ref, hd: jnp.broadcast_to(ref[0, hd, key0:key0 + 1, cs], (BF16_ROWS, LANES)).astype(BF16)
                    thr = [row(thr_ref, hd) for hd in range(PEER_HEADS)]
                    e0 = [row(e0_ref, hd) for hd in range(PEER_HEADS)]
                    zero = jnp.zeros((BF16_ROWS, LANES), BF16)
                    for r in range(N_KEYS // BF16_ROWS):
                        ws = slice(r * BF16_ROWS // 2, (r + 1) * BF16_ROWS // 2)
                        gate = None
                        for hd in range(PEER_HEADS):
                            rank = pltpu.bitcast(r1_ref[0, hd, ws, cs], BF16)
                            weight = pltpu.bitcast(e1_ref[0, hd, ws, cs], BF16)
                            part = jnp.where(rank >= thr[hd], weight * e0[hd], zero)
                            gate = part if gate is None else gate + part
                        ars = slice(rows[q].start + il * N_KEYS + r * BF16_ROWS,
                                    rows[q].start + il * N_KEYS + (r + 1) * BF16_ROWS)
                        wa_scr[ars, cs] = gate * _gelu(act_ref[ars, cs]).astype(BF16)
            part = jnp.dot(pltpu.bitcast(vt_ref[:, rows[q]], BF16), wa_scr[rows[q], :], preferred_element_type=F32)
            y = part if y is None else y + part
        yt_scr[t] += y

    @pl.when(t % 2 == 0)
    def _():
        step(at0_scr, at1_scr)

    @pl.when(t % 2 == 1)
    def _():
        step(at1_scr, at0_scr)

    @pl.when(e == pl.num_programs(1) - 1)
    def _():
        hn = h_ref[0] + mod_ref[0, 0][5:6] * yt_scr[t].T
        if final:
            hn = hn * lax.rsqrt(jnp.mean(hn * hn, axis=-1, keepdims=True) + EPS) * fg_ref[...]
        out_ref[0] = hn


def _peer_expert(hh, mod, a, thr, e0, r1, e1, u_bf, vt_bf, final_g, L, *, with_ctx, final):
    bsz, lt, d = hh.shape
    n_exp = vt_bf.shape[1]
    tm = PEER_TM
    n_tiles = (lt if with_ctx else L) // tm
    keys_per_step = PEER_EB // N_KEYS
    n_blocks = n_exp // PEER_EB
    tok = lambda r: pl.BlockSpec((1, PEER_HEADS, r, tm), (lambda b, e, t: (b, 0, e, t)) if r == keys_per_step
                                 else (lambda b, e, t: (b, 0, 0, t)))
    row_spec = pl.BlockSpec((1, tm, d), lambda b, e, t: (b, jnp.where(e == n_blocks - 1, t, 0), 0))
    if final:
        out_shape = jax.ShapeDtypeStruct((bsz, L, d), F32)
        aliases = {}
    else:
        out_shape = jax.ShapeDtypeStruct(hh.shape, F32)
        aliases = {8: 0}
    return pl.pallas_call(
        functools.partial(_peer_expert_kernel, final=final),
        grid=(bsz, n_blocks, n_tiles),
        in_specs=[pl.BlockSpec((1, d, tm), lambda b, e, t: (b, 0, t)),
                  pl.BlockSpec((1, d, tm), lambda b, e, t: (b, 0, jnp.minimum(t + 1, n_tiles - 1))),
                  tok(keys_per_step), tok(keys_per_step), tok(N_KEYS // 2), tok(N_KEYS // 2),
                  pl.BlockSpec((PEER_EB // 2, d), lambda b, e, t: (e, 0)),
                  pl.BlockSpec((d // 2, PEER_EB), lambda b, e, t: (0, e)),
                  row_spec,
                  pl.BlockSpec((1, 1, 6, d), lambda b, e, t: (b, t // (L // tm), 0, 0)),
                  pl.BlockSpec((1, d), lambda b, e, t: (0, 0))],
        out_specs=row_spec,
        out_shape=out_shape,
        input_output_aliases=aliases,
        scratch_shapes=[pltpu.VMEM((n_tiles, d, tm), F32), pltpu.VMEM((PEER_EB, tm), BF16),
                        pltpu.VMEM((PEER_EB, tm), F32), pltpu.VMEM((PEER_EB, tm), F32)],
        compiler_params=_cparams(("parallel", "arbitrary", "arbitrary")),
        name="peer_expert",
    )(a, a, thr, e0, r1, e1, u_bf, vt_bf, hh, mod, final_g.reshape(1, d))


def _pack_row_pairs(x):
    r, c = x.shape
    return lax.bitcast_convert_type(jnp.swapaxes(x.reshape(r // 2, 2, c), 1, 2), jnp.uint32)


def _peer(hh, mod, norm_g, w_q, sub_keys, u, v, final_g, L, *, with_ctx, final):
    a, thr, e0, r1, e1 = _peer_route(hh, mod, norm_g, w_q, sub_keys, L)
    return _peer_expert(hh, mod, a, thr, e0, r1, e1, _pack_row_pairs(u.astype(BF16)),
                        _pack_row_pairs(v.astype(BF16).T), final_g, L, with_ctx=with_ctx, final=final)


REC_QKV = C_HEADS * 3 * C_DK
REC_Z = C_HEADS * C_DK
REC_GATES = 4 * C_HEADS
REC_HD = D_HEADS * D_DK


def _rec_proj_kernel(h_ref, mod_ref, g_ref, w_ref, alog_ref, dtb_ref, cos_ref, sin_ref,
                     qkv_ref, z_ref, qd_ref, kd_ref, vd_ref, gd_ref, gate_ref):
    m = mod_ref[0, 0]
    a = _norm_mod(h_ref[0], g_ref[...], m[0:1], m[1:2])
    o = _bdot(a, w_ref[...])
    cos, sin = cos_ref[...], sin_ref[...]
    off = 0
    qkv_ref[0] = o[:, off:off + REC_QKV]
    off += REC_QKV
    z_ref[0] = o[:, off:off + REC_Z]
    off += REC_Z
    for hd in range(D_HEADS):
        qd_ref[0, :, hd * D_DK:(hd + 1) * D_DK] = _rope(o[:, off + hd * D_DK: off + (hd + 1) * D_DK], cos, sin, D_DK)
    off += REC_HD
    for hd in range(D_HEADS):
        kd_ref[0, :, hd * D_DK:(hd + 1) * D_DK] = (
            _rope(o[:, off + hd * D_DK: off + (hd + 1) * D_DK], cos, sin, D_DK) * (D_DK ** -0.5))
    off += REC_HD
    vd_ref[0] = o[:, off:off + REC_HD]
    off += REC_HD
    gd_ref[0] = o[:, off:off + 2 * REC_HD]
    off += 2 * REC_HD
    x = o[:, off:off + LANES]
    lane = lax.broadcasted_iota(jnp.int32, x.shape, 1)
    xb = x + dtb_ref[...]
    softplus = jnp.maximum(xb, 0.0) + jnp.log(1.0 + jnp.exp(-jnp.abs(xb)))
    gate_ref[0] = jnp.where(lane < 2 * C_HEADS, -jnp.exp(alog_ref[...]) * softplus, _sigmoid(x))


def _rec_project(hh, mod, norm_g, w_in, a_log, dt_bias, cos, sin, L):
    bsz, lt, d = hh.shape
    tm = ROW_TILE
    parts = np.cumsum([REC_QKV, REC_Z, REC_GATES, REC_HD, REC_HD, REC_HD])
    qkv_w, z_w, gates_w, qd_w, kd_w, vd_w, gd_w = jnp.split(w_in, [int(p) for p in parts], axis=1)
    w = jnp.concatenate([qkv_w, z_w, qd_w, kd_w, vd_w, gd_w, gates_w, jnp.zeros((d, LANES - REC_GATES), F32)],
                        axis=1).astype(BF16)
    n = w.shape[1]
    pad = lambda p: jnp.zeros((1, LANES), F32).at[0, :2 * C_HEADS].set(p.reshape(-1))
    widths = [REC_QKV, REC_Z, REC_HD, REC_HD, REC_HD, 2 * REC_HD, LANES]
    const = lambda shape: pl.BlockSpec(shape, lambda b, i: (0,) * len(shape))
    return pl.pallas_call(
        _rec_proj_kernel,
        grid=(bsz, lt // tm),
        in_specs=[pl.BlockSpec((1, tm, d), lambda b, i: (b, i, 0)),
                  pl.BlockSpec((1, 1, 6, d), lambda b, i: (b, i // (L // tm), 0, 0)),
                  const((1, d)), const((d, n)), const((1, LANES)), const((1, LANES)),
                  pl.BlockSpec((tm, LANES), lambda b, i: (i, 0)),
                  pl.BlockSpec((tm, LANES), lambda b, i: (i, 0))],
        out_specs=[pl.BlockSpec((1, tm, wd), lambda b, i: (b, i, 0)) for wd in widths],
        out_shape=[jax.ShapeDtypeStruct((bsz, lt, wd), F32) for wd in widths],
        compiler_params=_cparams(("parallel", "parallel")),
        name="rec_project",
    )(hh, mod, norm_g.reshape(1, d), w, pad(a_log), pad(dt_bias), cos, sin)


def _rec_conv_kernel(x_ref, prev_ref, next_ref, w_ref, q_ref, k_ref, v_ref, *, L):
    i = pl.program_id(1)
    tl = x_ref.shape[1]
    n_lat = L // tl
    at_start = (i == 0) | (i == n_lat)
    at_end = (i == n_lat - 1) | (i == pl.num_programs(1) - 1)
    prev = jnp.where(at_start, 0.0, prev_ref[0])
    nxt = jnp.where(at_end, 0.0, next_ref[0])
    xx = jnp.concatenate([prev, x_ref[0], nxt], axis=0)
    n = tl + 16
    acc = 0.0
    for tap in range(CONV_W):
        shift = (CONV_W // 2 - tap) % n
        shifted = xx if shift == 0 else pltpu.roll(xx, shift, 0)
        acc = acc + shifted[8:8 + tl] * w_ref[tap:tap + 1, :]
    y = _silu(acc)
    hw = C_HEADS * C_DK
    for hd in range(C_HEADS):
        def l2(x):
            return x * lax.rsqrt(jnp.sum(x * x, axis=-1, keepdims=True) + EPS)
        sl = slice(hd * C_DK, (hd + 1) * C_DK)
        q_ref[0, :, sl] = l2(y[:, hd * C_DK:(hd + 1) * C_DK]) * (C_DK ** -0.5)
        k_ref[0, :, sl] = l2(y[:, hw + hd * C_DK: hw + (hd + 1) * C_DK])
    v_ref[0] = y[:, 2 * hw:]


def _rec_conv(qkv, conv_w, L):
    bsz, lt, ch = qkv.shape
    tl = ROW_TILE
    hb = tl // 8
    last = lt // 8 - 1
    hw = C_HEADS * C_DK
    return pl.pallas_call(
        functools.partial(_rec_conv_kernel, L=L),
        grid=(bsz, lt // tl),
        in_specs=[pl.BlockSpec((1, tl, ch), lambda b, i: (b, i, 0)),
                  pl.BlockSpec((1, 8, ch), lambda b, i: (b, jnp.maximum(i * hb - 1, 0), 0)),
                  pl.BlockSpec((1, 8, ch), lambda b, i: (b, jnp.minimum((i + 1) * hb, last), 0)),
                  pl.BlockSpec((8, ch), lambda b, i: (0, 0))],
        out_specs=[pl.BlockSpec((1, tl, hw), lambda b, i: (b, i, 0))] * 3,
        out_shape=[jax.ShapeDtypeStruct((bsz, lt, hw), F32)] * 3,
        compiler_params=_cparams(("parallel", "parallel")),
        name="rec_conv",
    )(qkv, qkv, qkv, jnp.zeros((8, ch), F32).at[:CONV_W].set(conv_w))


def _ret_log_gamma(hd):
    return float(np.log1p(-np.exp2(-(RET_DECAY_BASE + hd))))


def _rec_intra_kernel(qc_ref, kc_ref, vc_ref, qd_ref, kd_ref, vd_ref, gate_ref,
                      qs_ref, oi_ref, pp_ref, nn_ref, al_ref):
    c = CHUNK
    ci = lax.broadcasted_iota(jnp.int32, (c, c), 0)
    si = lax.broadcasted_iota(jnp.int32, (c, c), 1)
    pos = lax.broadcasted_iota(jnp.int32, (c, 1), 0).astype(F32)
    gates = gate_ref[0]
    la_parts = _split3(gates)
    ones_row = jnp.ones((1, LANES), F32)
    combos = []
    for d in range(2):
        incl = (si <= ci) if d == 0 else (si >= ci)
        strict = (si < ci) if d == 0 else (si > ci)
        tri = incl.astype(BF16)
        g_all = jnp.dot(jnp.concatenate([tri] * 3, axis=1), jnp.concatenate(la_parts, axis=0),
                        preferred_element_type=F32)
        g_all_t = g_all.T
        last = c - 1 if d == 0 else 0
        for hd in range(C_HEADS):
            col = d * C_HEADS + hd
            sl = slice(hd * C_DK, (hd + 1) * C_DK)
            q, k, v = qc_ref[0, :, sl], kc_ref[0, :, sl], vc_ref[0, :, sl]
            gcol = g_all[:, col:col + 1]
            grow = g_all_t[col:col + 1, :]
            glast = g_all[last:last + 1, col:col + 1]
            beta = gates[:, 2 * C_HEADS + col: 2 * C_HEADS + col + 1]
            diff = gcol - grow
            dec_strict = jnp.exp(jnp.where(strict, diff, _NEG_INF))
            combos.append(dict(
                d=d, hd=hd, q=q, gcol=gcol, glast=glast,
                x=-(beta * _bdot_nt(k, k) * dec_strict),
                sol=jnp.concatenate([beta * v, (beta * jnp.exp(gcol)) * k], axis=1),
                qk=_bdot_nt(q, k) * jnp.exp(jnp.where(incl, diff, _NEG_INF)),
                kend=k * jnp.exp(glast - gcol)))
    for d in range(2):
        incl = (si <= ci) if d == 0 else (si >= ci)
        for hd in range(D_HEADS):
            lg = _ret_log_gamma(hd)
            sl = slice(hd * D_DK, (hd + 1) * D_DK)
            q, k, v = qd_ref[0, :, sl], kd_ref[0, :, sl], vd_ref[0, :, sl]
            steps = pos if d == 0 else (c - 1.0) - pos
            dist = (ci - si) if d == 0 else (si - ci)
            dmat = jnp.exp(jnp.where(incl, lg * dist.astype(F32), _NEG_INF))
            qk = _bdot_nt(q, k) * dmat
            qs_ref[0, 0, d, C_HEADS + hd] = (q * jnp.exp(lg * (steps + 1.0))).astype(qs_ref.dtype)
            oi_ref[0, 0, d, C_HEADS + hd] = _bdot(qk, v)
            nn_ref[0, 0, d, C_HEADS + hd] = _bdot_tn(k * jnp.exp(lg * ((c - 1.0) - steps)), v)
            al_ref[0, 0, d, C_HEADS + hd:C_HEADS + hd + 1, :] = math.exp(lg * c) * ones_row
    for cb in combos:
        cb["kend_t"] = cb["kend"].T.astype(BF16)
    levels = int(math.log2(c))
    for lvl in range(levels):
        for cb in combos:
            cb["sol"] = cb["sol"] + _dot3(cb["x"], cb["sol"])
        if lvl < levels - 1:
            for cb in combos:
                cb["x"] = _dot3(cb["x"], cb["x"])
    for cb in combos:
        cb["out"] = _bdot(cb["qk"], cb["sol"])
    for cb in combos:
        cb["state"] = jnp.dot(cb["kend_t"], cb["sol"].astype(BF16), preferred_element_type=F32)
    for cb in combos:
        d, hd = cb["d"], cb["hd"]
        qs_ref[0, 0, d, hd] = (jnp.exp(cb["gcol"]) * cb["q"] - cb["out"][:, C_DK:]).astype(qs_ref.dtype)
        oi_ref[0, 0, d, hd] = cb["out"][:, :C_DK]
        pp_ref[0, 0, d, hd] = cb["state"][:, C_DK:].astype(pp_ref.dtype)
        nn_ref[0, 0, d, hd] = cb["state"][:, :C_DK]
        al_ref[0, 0, d, hd:hd + 1, :] = jnp.exp(cb["glast"]) * ones_row


def _rec_intra(qc, kc, vc, qd, kd, vd, gate):
    bsz, lt, hw = qc.shape
    nc = lt // CHUNK
    nh = C_HEADS + D_HEADS
    row = lambda w: pl.BlockSpec((1, CHUNK, w), lambda b, n: (b, n, 0))
    lead = lambda *tail: pl.BlockSpec((1, 1, 2) + tail, lambda b, n: (b, n, 0) + (0,) * len(tail))
    return pl.pallas_call(
        _rec_intra_kernel,
        grid=(bsz, nc),
        in_specs=[row(hw)] * 6 + [row(LANES)],
        out_specs=[lead(nh, CHUNK, C_DK), lead(nh, CHUNK, C_DK), lead(C_HEADS, C_DK, C_DK), lead(nh, C_DK, C_DK),
                   lead(nh, LANES)],
        out_shape=[jax.ShapeDtypeStruct((bsz, nc, 2, nh, CHUNK, C_DK), BF16),
                   jax.ShapeDtypeStruct((bsz, nc, 2, nh, CHUNK, C_DK), F32),
                   jax.ShapeDtypeStruct((bsz, nc, 2, C_HEADS, C_DK, C_DK), BF16),
                   jax.ShapeDtypeStruct((bsz, nc, 2, nh, C_DK, C_DK), F32),
                   jax.ShapeDtypeStruct((bsz, nc, 2, nh, LANES), F32)],
        compiler_params=_cparams(("parallel", "parallel")),
        name="rec_intra",
    )(qc, kc, vc, qd, kd, vd, gate)


def _rec_scan_kernel(qs_ref, oi_ref, pp_ref, nn_ref, al_ref, o_ref, s_scr):
    @pl.when(pl.program_id(2) == 0)
    def _():
        s_scr[...] = jnp.zeros_like(s_scr)

    for hd in range(C_HEADS + D_HEADS):
        s = s_scr[hd]
        sb = s.astype(BF16)
        o_ref[0, 0, :, hd * C_DK:(hd + 1) * C_DK] = (
            jnp.dot(qs_ref[0, 0, 0, hd], sb, preferred_element_type=F32) + oi_ref[0, 0, 0, hd])
        new = al_ref[0, 0, 0, hd:hd + 1, :] * s + nn_ref[0, 0, 0, hd]
        if hd < C_HEADS:
            new = new - jnp.dot(pp_ref[0, 0, 0, hd], sb, preferred_element_type=F32)
        s_scr[hd] = new


def _rec_scan(qs, oi, pp, nn, al, L):
    bsz, nc = qs.shape[:2]
    nh = C_HEADS + D_HEADS
    n_lat = L // CHUNK
    n_ctx = nc - n_lat

    def chunk(d, s):
        fwd = jnp.where(s < n_ctx, n_lat + s, s - n_ctx)
        bwd = nc - 1 - s
        return jnp.where(d == 0, fwd, bwd)

    lead = lambda *tail: pl.BlockSpec((1, 1, 1) + tail, lambda b, d, s: (b, chunk(d, s), d) + (0,) * len(tail))
    return pl.pallas_call(
        _rec_scan_kernel,
        grid=(bsz, 2, nc),
        in_specs=[lead(nh, CHUNK, C_DK), lead(nh, CHUNK, C_DK), lead(C_HEADS, C_DK, C_DK), lead(nh, C_DK, C_DK),
                  lead(nh, LANES)],
        out_specs=pl.BlockSpec((1, 1, CHUNK, nh * C_DK), lambda b, d, s: (b, d, chunk(d, s), 0)),
        out_shape=jax.ShapeDtypeStruct((bsz, 2, nc * CHUNK, nh * C_DK), F32),
        scratch_shapes=[pltpu.VMEM((nh, C_DK, C_DK), F32)],
        compiler_params=_cparams(("parallel", "parallel", "arbitrary")),
        name="rec_scan",
    )(qs, oi, pp, nn, al)


def _rec_out_kernel(of_ref, ob_ref, z_ref, gd_ref, og_ref, gn_ref, w_ref, h_ref, mod_ref, out_ref):
    hw = C_HEADS * C_DK
    of, ob = of_ref[0, 0], ob_ref[0, 0]
    z, gd = z_ref[0], gd_ref[0]
    parts = []
    for hd in range(C_HEADS):
        sl = slice(hd * C_DK, (hd + 1) * C_DK)
        x = of[:, sl] + ob[:, sl]
        y = x * lax.rsqrt(jnp.mean(x * x, axis=-1, keepdims=True) + EPS) * og_ref[...]
        parts.append(y * _silu(z[:, sl]))
    for hd in range(D_HEADS):
        sl = slice(hd * D_DK, (hd + 1) * D_DK)
        y = 0.0
        for d, o in enumerate((of, ob)):
            x = o[:, hw + hd * D_DK: hw + (hd + 1) * D_DK]
            mu = jnp.mean(x, axis=-1, keepdims=True)
            xc = x - mu
            var = jnp.mean(xc * xc, axis=-1, keepdims=True)
            y = y + xc * lax.rsqrt(var + EPS) * gn_ref[:, sl] * _silu(gd[:, d * hw + hd * D_DK: d * hw + (hd + 1) * D_DK])
        parts.append(y)
    mix = jnp.concatenate(parts, axis=1)
    out_ref[0] = h_ref[0] + mod_ref[0, 0][2:3] * _bdot(mix, w_ref[...])


def _rec_out(o_scan, z, gd, out_g, gn_g, w_out, hh, mod, L):
    bsz, lt, d = hh.shape
    tm = ROW_TILE
    hw = C_HEADS * C_DK
    return pl.pallas_call(
        _rec_out_kernel,
        grid=(bsz, L // tm),
        in_specs=[pl.BlockSpec((1, 1, tm, 2 * hw), lambda b, i: (b, 0, i, 0)),
                  pl.BlockSpec((1, 1, tm, 2 * hw), lambda b, i: (b, 1, i, 0)),
                  pl.BlockSpec((1, tm, hw), lambda b, i: (b, i, 0)),
                  pl.BlockSpec((1, tm, 2 * hw), lambda b, i: (b, i, 0)),
                  pl.BlockSpec((1, C_DK), lambda b, i: (0, 0)),
                  pl.BlockSpec((1, hw), lambda b, i: (0, 0)),
                  pl.BlockSpec((2 * hw, d), lambda b, i: (0, 0)),
                  pl.BlockSpec((1, tm, d), lambda b, i: (b, i, 0)),
                  pl.BlockSpec((1, 1, 6, d), lambda b, i: (b, 0, 0, 0))],
        out_specs=pl.BlockSpec((1, tm, d), lambda b, i: (b, i, 0)),
        out_shape=jax.ShapeDtypeStruct(hh.shape, F32),
        input_output_aliases={7: 0},
        compiler_params=_cparams(("parallel", "parallel")),
        name="rec_out",
    )(o_scan, o_scan, z, gd, out_g.reshape(1, C_DK), gn_g.reshape(1, hw), w_out.astype(BF16), hh, mod)


def _att_layer(hh, mod, norm_g, w_in, q_g, k_g, sink, w_out, L):
    cos, sin = _rope_tables(L, hh.shape[1] - L, HEAD_DIM)
    qa, ka, va, qb, kb, vb = _att_project(hh, mod, norm_g, w_in, q_g, k_g, cos, sin, L)
    oa, ob = _attention(qa, ka, va, qb, kb, vb, sink, L)
    return _att_out(oa, ob, w_out, hh, mod, L)


def _rec_layer(hh, mod, norm_g, w_in, conv_w, a_log, dt_bias, out_g, gn_g, w_out, L):
    cos, sin = _rope_tables(L, hh.shape[1] - L, D_DK)
    qkv, z, qd, kd, vd, gd, gate = _rec_project(hh, mod, norm_g, w_in, a_log, dt_bias, cos, sin, L)
    qc, kc, vc = _rec_conv(qkv, conv_w, L)
    qs, oi, pp, nn, al = _rec_intra(qc, kc, vc, qd, kd, vd, gate)
    o_scan = _rec_scan(qs, oi, pp, nn, al, L)
    return _rec_out(o_scan, z, gd, out_g, gn_g, w_out, hh, mod, L)


def kernel(x, c, ctx, c_ctx, mod_w, mod_b, norm1_g, norm2_g, att_w_in, att_q_norm, att_k_norm, att_sink, att_w_out,
           rec_w_in, rec_conv_w, rec_a_log, rec_dt_bias, rec_out_norm, rec_gn_g, rec_w_out,
           peer_w_q, peer_sub_keys, peer_u, peer_v, final_norm_g):
    L = x.shape[1]
    depth = mod_w.shape[0]
    mods = _modulation(c, c_ctx, mod_w, mod_b)
    hh = jnp.concatenate([x, ctx], axis=1)
    for layer in range(depth):
        last = layer == depth - 1
        i = layer // 2
        if layer % 2 == 0:
            hh = _att_layer(hh, mods[layer], norm1_g[layer], att_w_in[i], att_q_norm[i], att_k_norm[i], att_sink[i],
                            att_w_out[i], L)
        else:
            hh = _rec_layer(hh, mods[layer], norm1_g[layer], rec_w_in[i], rec_conv_w[i], rec_a_log[i], rec_dt_bias[i],
                            rec_out_norm[i], rec_gn_g[i], rec_w_out[i], L)
        hh = _peer(hh, mods[layer], norm2_g[layer], peer_w_q[layer], peer_sub_keys[layer], peer_u[layer],
                   peer_v[layer], final_norm_g, L, with_ctx=not last, final=last)
    return hh
```

```python
import functools
import math

import numpy as np
import jax
import jax.numpy as jnp
from jax import lax
from jax.experimental import pallas as pl
from jax.experimental.pallas import tpu as pltpu

F32 = jnp.float32
BF16 = jnp.bfloat16

GRID_W = 64
EPS = 1e-6
HEAD_DIM = 64
A_HEADS = 8
A_KV = 2
B_HEADS = 8
B_KV = 2
WINDOW = 128
ROPE_THETA = 10000.0
C_HEADS = 4
C_DK = 128
CONV_W = 5
CHUNK = 64
D_HEADS = 4
D_DK = 128
RET_DECAY_BASE = 5.0
PEER_HEADS = 8
N_KEYS = 128
PEER_TOPK = 16

LANES = 128
BF16_ROWS = 16
VMEM_LIMIT = 56 * 1024 * 1024

ROW_TILE = 256
ATT_TQ = 128
PEER_TM = 256
PEER_EB = 2048
PEER_SUB = 256
PEER_TOK = 256

_NEG_INF = float("-inf")


def _cparams(sem, flags=None):
    return pltpu.CompilerParams(dimension_semantics=sem, vmem_limit_bytes=VMEM_LIMIT, flags=flags)


def _bdot(a, b):
    return jnp.dot(a.astype(BF16), b.astype(BF16), preferred_element_type=F32)


def _bdot_nt(a, b):
    return lax.dot_general(a.astype(BF16), b.astype(BF16), (((1,), (1,)), ((), ())), preferred_element_type=F32)


def _bdot_tn(a, b):
    return jnp.dot(a.T.astype(BF16), b.astype(BF16), preferred_element_type=F32)


def _split3(a):
    hi = a.astype(BF16)
    r1 = a - hi.astype(F32)
    mid = r1.astype(BF16)
    lo = (r1 - mid.astype(F32)).astype(BF16)
    return hi, mid, lo


def _dot3(a, b):
    a_hi, a_lo, _ = _split3(a)
    b_hi, b_lo, _ = _split3(b)
    return jnp.dot(jnp.concatenate([a_hi, a_lo, a_hi], axis=1), jnp.concatenate([b_hi, b_hi, b_lo], axis=0),
                   preferred_element_type=F32)


def _sigmoid(x):
    return 1.0 / (1.0 + jnp.exp(-x))


def _silu(x):
    return x * _sigmoid(x)


def _norm_mod(x, g, shift, scale):
    r = lax.rsqrt(jnp.mean(x * x, axis=-1, keepdims=True) + EPS)
    return (x * r * g) * (1.0 + scale) + shift


def _rope(x, cos, sin_signed, head_dim):
    quarter = head_dim // 4
    lane = lax.broadcasted_iota(jnp.int32, x.shape, 1)
    first = (lane % (2 * quarter)) < quarter
    partner = jnp.where(first, pltpu.roll(x, LANES - quarter, 1), pltpu.roll(x, quarter, 1))
    return x * cos + partner * sin_signed


def _mod_kernel(c_ref, w_ref, b_ref, o_ref):
    o_ref[0] = _bdot(_silu(c_ref[...]), w_ref[0]) + b_ref[0]


def _modulation(c, c_ctx, mod_w, mod_b):
    depth, d, n = mod_w.shape
    bsz = c.shape[0]
    rows = 16
    cc = jnp.zeros((rows, d), F32).at[:bsz].set(c).at[bsz].set(c_ctx)
    tn = 1536
    out = pl.pallas_call(
        _mod_kernel,
        grid=(depth, n // tn),
        in_specs=[pl.BlockSpec((rows, d), lambda l, j: (0, 0)),
                  pl.BlockSpec((1, d, tn), lambda l, j: (l, 0, j)),
                  pl.BlockSpec((1, 1, tn), lambda l, j: (l, 0, j))],
        out_specs=pl.BlockSpec((1, rows, tn), lambda l, j: (l, 0, j)),
        out_shape=jax.ShapeDtypeStruct((depth, rows, n), F32),
        compiler_params=_cparams(("arbitrary", "arbitrary")),
        name="modulation",
    )(cc, mod_w, mod_b.reshape(depth, 1, n))
    lat = out[:, :bsz].reshape(depth, bsz, 1, 6, d)
    ctx = jnp.broadcast_to(out[:, bsz].reshape(depth, 1, 1, 6, d), (depth, bsz, 1, 6, d))
    return jnp.concatenate([lat, ctx], axis=2)


def _rope_tables(L, LC, head_dim):
    quarter, half = head_dim // 4, head_dim // 2
    freqs = ROPE_THETA ** (-jnp.arange(quarter, dtype=F32) / quarter)
    lane = np.arange(LANES)
    within = lane % head_dim
    use_col = within >= half
    fidx = within % quarter
    sign = np.where((within % half) < quarter, -1.0, 1.0).astype(np.float32)
    t = jnp.arange(L, dtype=jnp.int32)
    row, col = (t // GRID_W).astype(F32), (t % GRID_W).astype(F32)
    pos = jnp.where(use_col[None, :], col[:, None], row[:, None])
    ang = pos * freqs[fidx][None, :]
    cos = jnp.concatenate([jnp.cos(ang), jnp.ones((LC, LANES), F32)], axis=0)
    sin = jnp.concatenate([jnp.sin(ang) * sign[None, :], jnp.zeros((LC, LANES), F32)], axis=0)
    return cos, sin


def _att_proj_kernel(h_ref, mod_ref, g_ref, w_ref, qg_ref, kg_ref, cos_ref, sin_ref, gm_ref,
                     qa_ref, ka_ref, va_ref, qb_ref, kb_ref, vb_ref):
    m = mod_ref[0, 0]
    a = _norm_mod(h_ref[0], g_ref[...], m[0:1], m[1:2])
    o = _bdot(a, w_ref[...])
    cos, sin = cos_ref[...], sin_ref[...]
    gm = gm_ref[...]

    def head_norm(x, gain):
        sq = x * x
        hi = sq.astype(BF16)
        lo = (sq - hi.astype(F32)).astype(BF16)
        ms = jnp.dot(hi, gm, preferred_element_type=F32) + jnp.dot(lo, gm, preferred_element_type=F32)
        return x * lax.rsqrt(ms + EPS) * gain

    def put(ref, first_head, x):
        ref[0, first_head] = x[:, :HEAD_DIM].astype(ref.dtype)
        ref[0, first_head + 1] = x[:, HEAD_DIM:].astype(ref.dtype)

    scale = HEAD_DIM ** -0.5
    qa_w = A_HEADS * HEAD_DIM
    kv_w = A_KV * HEAD_DIM
    off = 0
    for c in range(qa_w // LANES):
        x = o[:, off + c * LANES: off + (c + 1) * LANES]
        put(qa_ref, 2 * c, _rope(head_norm(x, qg_ref[...]), cos, sin, HEAD_DIM) * scale)
    off += qa_w
    put(ka_ref, 0, _rope(head_norm(o[:, off: off + kv_w], kg_ref[...]), cos, sin, HEAD_DIM))
    off += kv_w
    put(va_ref, 0, o[:, off: off + kv_w])
    off += kv_w
    for c in range(qa_w // LANES):
        x = o[:, off + c * LANES: off + (c + 1) * LANES]
        put(qb_ref, 2 * c, _rope(x, cos, sin, HEAD_DIM) * scale)
    off += qa_w
    put(kb_ref, 0, _rope(o[:, off: off + kv_w], cos, sin, HEAD_DIM))
    off += kv_w
    put(vb_ref, 0, o[:, off: off + kv_w])


def _att_project(hh, mod, norm_g, w_in, q_g, k_g, cos, sin, L):
    bsz, lt, d = hh.shape
    tm = ROW_TILE
    n = w_in.shape[1]
    gm = jnp.asarray(np.kron(np.eye(LANES // HEAD_DIM), np.full((HEAD_DIM, HEAD_DIM), 1.0 / HEAD_DIM)), BF16)
    tile2 = lambda v: jnp.tile(v.reshape(1, HEAD_DIM), (1, LANES // HEAD_DIM))
    qshape = jax.ShapeDtypeStruct((bsz, A_HEADS, lt, HEAD_DIM), BF16)
    kshape = jax.ShapeDtypeStruct((bsz, A_KV, lt, HEAD_DIM), BF16)
    qspec = pl.BlockSpec((1, A_HEADS, tm, HEAD_DIM), lambda b, i: (b, 0, i, 0))
    kspec = pl.BlockSpec((1, A_KV, tm, HEAD_DIM), lambda b, i: (b, 0, i, 0))
    const = lambda shape: pl.BlockSpec(shape, lambda b, i: (0,) * len(shape))
    return pl.pallas_call(
        _att_proj_kernel,
        grid=(bsz, lt // tm),
        in_specs=[pl.BlockSpec((1, tm, d), lambda b, i: (b, i, 0)),
                  pl.BlockSpec((1, 1, 6, d), lambda b, i: (b, i // (L // tm), 0, 0)),
                  const((1, d)), const((d, n)), const((1, LANES)), const((1, LANES)),
                  pl.BlockSpec((tm, LANES), lambda b, i: (i, 0)),
                  pl.BlockSpec((tm, LANES), lambda b, i: (i, 0)),
                  const((LANES, LANES))],
        out_specs=[qspec, kspec, kspec, qspec, kspec, kspec],
        out_shape=[qshape, kshape, kshape, qshape, kshape, kshape],
        compiler_params=_cparams(("parallel", "parallel")),
        name="att_project",
    )(hh, mod, norm_g.reshape(1, d), w_in.astype(BF16), tile2(q_g), tile2(k_g), cos, sin, gm)


def _softmax_pv(score_parts, value_parts, extra_logit=None):
    m = functools.reduce(jnp.maximum, [jnp.max(s, axis=-1, keepdims=True) for s in score_parts])
    if extra_logit is not None:
        m = jnp.maximum(m, extra_logit)
    l = 0.0
    acc = 0.0
    for s, v in zip(score_parts, value_parts):
        p = jnp.exp(s - m)
        l = l + jnp.sum(p, axis=-1, keepdims=True)
        acc = acc + jnp.dot(p.astype(BF16), v, preferred_element_type=F32)
    if extra_logit is not None:
        l = l + jnp.exp(extra_logit - m)
    return acc / l


def _att_kernel(sink_ref, qa_ref, ka_ref, va_ref, qb_ref, kb_ref, vb_ref, oa_ref, ob_ref, *, L, LC):
    i = pl.program_id(1)
    tq = ATT_TQ
    group = A_HEADS // A_KV
    band = tq + 2 * WINDOW

    def stacked_q(ref, kvh):
        return ref[0, kvh * group:(kvh + 1) * group].reshape(group * tq, HEAD_DIM)

    def put(ref, kvh, o):
        for g in range(group):
            hd = kvh * group + g
            ref[0, :, hd * HEAD_DIM:(hd + 1) * HEAD_DIM] = o[g * tq:(g + 1) * tq].astype(ref.dtype)

    def sink_col(kvh):
        return jnp.concatenate([jnp.full((tq, 1), sink_ref[kvh * group + g], F32) for g in range(group)], axis=0)

    @pl.when(i < L // tq)
    def _latent():
        for kvh in range(A_KV):
            q = stacked_q(qa_ref, kvh)
            put(oa_ref, kvh, _softmax_pv([_bdot_nt(q, ka_ref[0, kvh])], [va_ref[0, kvh]]))
            q = stacked_q(qb_ref, kvh)
            start = pl.multiple_of(jnp.clip((i - 1) * tq, 0, L - band), tq)
            s_band = _bdot_nt(q, kb_ref[0, kvh, pl.ds(start, band), :])
            qpos = i * tq + (lax.broadcasted_iota(jnp.int32, s_band.shape, 0) % tq)
            kpos = start + lax.broadcasted_iota(jnp.int32, s_band.shape, 1)
            s_band = jnp.where(jnp.abs(qpos - kpos) <= WINDOW, s_band, _NEG_INF)
            s_ctx = _bdot_nt(q, kb_ref[0, kvh, L:L + LC, :])
            put(ob_ref, kvh, _softmax_pv([s_band, s_ctx],
                                         [vb_ref[0, kvh, pl.ds(start, band), :], vb_ref[0, kvh, L:L + LC, :]],
                                         sink_col(kvh)))

    @pl.when(i >= L // tq)
    def _context():
        for kvh in range(A_KV):
            q = stacked_q(qa_ref, kvh)
            put(oa_ref, kvh, _softmax_pv([_bdot_nt(q, ka_ref[0, kvh, L:L + LC, :])], [va_ref[0, kvh, L:L + LC, :]]))
            q = stacked_q(qb_ref, kvh)
            put(ob_ref, kvh, _softmax_pv([_bdot_nt(q, kb_ref[0, kvh, L:L + LC, :])], [vb_ref[0, kvh, L:L + LC, :]],
                                         sink_col(kvh)))


def _attention(qa, ka, va, qb, kb, vb, sink, L):
    bsz, _, lt, _ = qa.shape
    tq = ATT_TQ
    qspec = pl.BlockSpec((1, A_HEADS, tq, HEAD_DIM), lambda b, i: (b, 0, i, 0))
    kspec = pl.BlockSpec((1, A_KV, lt, HEAD_DIM), lambda b, i: (b, 0, 0, 0))
    ospec = pl.BlockSpec((1, tq, A_HEADS * HEAD_DIM), lambda b, i: (b, i, 0))
    oshape = jax.ShapeDtypeStruct((bsz, lt, A_HEADS * HEAD_DIM), BF16)
    return pl.pallas_call(
        functools.partial(_att_kernel, L=L, LC=lt - L),
        grid=(bsz, lt // tq),
        in_specs=[pl.BlockSpec(memory_space=pltpu.SMEM), qspec, kspec, kspec, qspec, kspec, kspec],
        out_specs=[ospec, ospec],
        out_shape=[oshape, oshape],
        compiler_params=_cparams(("parallel", "parallel")),
        name="attention",
    )(sink, qa, ka, va, qb, kb, vb)


def _att_out_kernel(oa_ref, ob_ref, w_ref, h_ref, mod_ref, out_ref):
    half = oa_ref.shape[-1]
    y = (jnp.dot(oa_ref[0], w_ref[:half, :], preferred_element_type=F32)
         + jnp.dot(ob_ref[0], w_ref[half:, :], preferred_element_type=F32))
    out_ref[0] = h_ref[0] + mod_ref[0, 0][2:3] * y


def _att_out(oa, ob, w_out, hh, mod, L):
    bsz, lt, d = hh.shape
    tm = ROW_TILE
    half = oa.shape[-1]
    return pl.pallas_call(
        _att_out_kernel,
        grid=(bsz, lt // tm),
        in_specs=[pl.BlockSpec((1, tm, half), lambda b, i: (b, i, 0)),
                  pl.BlockSpec((1, tm, half), lambda b, i: (b, i, 0)),
                  pl.BlockSpec((2 * half, d), lambda b, i: (0, 0)),
                  pl.BlockSpec((1, tm, d), lambda b, i: (b, i, 0)),
                  pl.BlockSpec((1, 1, 6, d), lambda b, i: (b, i // (L // tm), 0, 0))],
        out_specs=pl.BlockSpec((1, tm, d), lambda b, i: (b, i, 0)),
        out_shape=jax.ShapeDtypeStruct(hh.shape, F32),
        input_output_aliases={3: 0},
        compiler_params=_cparams(("parallel", "parallel")),
        name="att_out",
    )(oa, ob, w_out.astype(BF16), hh, mod)


def _sort_network(n):
    pairs = []
    p = 1
    while p < n:
        k = p
        while k >= 1:
            for j in range(k % p, n - k, 2 * k):
                for i in range(min(k, n - j - k)):
                    if (i + j) // (2 * p) == (i + j + k) // (2 * p):
                        pairs.append((i + j, i + j + k))
            k //= 2
        p *= 2
    return pairs


def _pop_sorted(lists, n, singles=()):
    lists, singles = list(lists), list(singles)
    out = []
    for rnd in range(n):
        head = functools.reduce(jnp.maximum, [lists[0]] + singles)
        m = jnp.max(head, axis=0, keepdims=True)
        out.append(m)
        if rnd == n - 1:
            break
        hit = lists[0] == m
        depth = min(len(lists), n - rnd - 1)
        for d in range(depth):
            nxt = lists[d + 1] if d + 1 < len(lists) else _NEG_INF
            lists[d] = jnp.where(hit, nxt, lists[d])
        singles = [jnp.where(s == m, _NEG_INF, s) for s in singles]
    return out


def _top_values(x, n):
    slabs = [x[r * 8:(r + 1) * 8] for r in range(x.shape[0] // 8)]
    for i, j in _sort_network(len(slabs)):
        slabs[i], slabs[j] = jnp.maximum(slabs[i], slabs[j]), jnp.minimum(slabs[i], slabs[j])
    return _pop_sorted(slabs, n)


def _peer_route_kernel(h_ref, mod_ref, g_ref, wq_ref, keys_ref, at_ref, thr_ref, e0_ref, r1_ref, e1_ref, q_scr):
    m = mod_ref[0, 0]
    a = _norm_mod(h_ref[0], g_ref[...], m[3:4], m[4:5])
    at_ref[0] = a.T.astype(BF16)
    q_scr[...] = jnp.dot(a.astype(BF16), wq_ref[...], preferred_element_type=F32)
    k = PEER_TOPK
    n = k + 1

    def head(hd, carry):
        col = pl.multiple_of(hd * 2 * N_KEYS, 2 * N_KEYS)
        s0 = _bdot_nt(keys_ref[0], q_scr[:, pl.ds(col, N_KEYS)])
        s1 = _bdot_nt(keys_ref[1], q_scr[:, pl.ds(col + N_KEYS, N_KEYS)])
        tokens = s0.shape[1]
        top0 = _top_values(s0, n)
        top1 = _top_values(s1, n)
        first = jnp.concatenate(top0 + [jnp.full((24 - n, tokens), _NEG_INF, F32)], axis=0)
        best = _pop_sorted([first[0:8] + t for t in top1], n, [first[8:16] + top1[0], first[16:24] + top1[0]])
        z = functools.reduce(lambda x, y: x + y, [jnp.exp(b - best[0]) for b in best[:k]])
        tau = 0.5 * (best[k - 1] + best[k])
        r1 = jnp.zeros_like(s1)
        need = jnp.full_like(s0, n + 1.0)
        for b in range(n):
            r1 = jnp.where(s1 >= top1[n - 1 - b], b + 1.0, r1)
            need = jnp.where(s0 >= tau - top1[b], float(n - b), need)
        thr_ref[0, hd] = need
        e0_ref[0, hd] = jnp.exp(s0 - top0[0])
        r1_ref[0, hd] = pltpu.bitcast(r1.astype(BF16), jnp.uint32)
        e1_ref[0, hd] = pltpu.bitcast((jnp.exp(s1 - top1[0]) / z).astype(BF16), jnp.uint32)
        return carry

    lax.fori_loop(0, PEER_HEADS, head, 0, unroll=True)


def _peer_route(hh, mod, norm_g, w_q, sub_keys, L):
    bsz, lt, d = hh.shape
    tm = ROW_TILE
    nq = w_q.shape[1]
    tok = lambda: pl.BlockSpec((1, PEER_HEADS, N_KEYS, tm), lambda b, i: (b, 0, 0, i))
    tshape = jax.ShapeDtypeStruct((bsz, PEER_HEADS, N_KEYS, lt), F32)
    pshape = jax.ShapeDtypeStruct((bsz, PEER_HEADS, N_KEYS // 2, lt), jnp.uint32)
    ptok = pl.BlockSpec((1, PEER_HEADS, N_KEYS // 2, tm), lambda b, i: (b, 0, 0, i))
    return pl.pallas_call(
        _peer_route_kernel,
        grid=(bsz, lt // tm),
        in_specs=[pl.BlockSpec((1, tm, d), lambda b, i: (b, i, 0)),
                  pl.BlockSpec((1, 1, 6, d), lambda b, i: (b, i // (L // tm), 0, 0)),
                  pl.BlockSpec((1, d), lambda b, i: (0, 0)),
                  pl.BlockSpec((d, nq), lambda b, i: (0, 0)),
                  pl.BlockSpec((2, N_KEYS, N_KEYS), lambda b, i: (0, 0, 0))],
        out_specs=[pl.BlockSpec((1, d, tm), lambda b, i: (b, 0, i)), tok(), tok(), ptok, ptok],
        out_shape=[jax.ShapeDtypeStruct((bsz, d, lt), BF16), tshape, tshape, pshape, pshape],
        scratch_shapes=[pltpu.VMEM((tm, nq), F32)],
        compiler_params=_cparams(("parallel", "parallel")),
        name="peer_route",
    )(hh, mod, norm_g.reshape(1, d), w_q.astype(BF16), sub_keys.astype(BF16))


def _gelu(x):
    return 0.5 * x * (1.0 + lax.erf(x * (1.0 / math.sqrt(2.0))))


def _peer_expert_kernel(a_ref, an_ref, thr_ref, e0_ref, r1_ref, e1_ref, u_ref, vt_ref, h_ref, mod_ref, fg_ref,
                        out_ref, yt_scr, wa_scr, at0_scr, at1_scr, *, final):
    e = pl.program_id(1)
    t = pl.program_id(2)
    tm = a_ref.shape[2]
    n_sub = PEER_EB // PEER_SUB
    rows = [slice(q * PEER_SUB, (q + 1) * PEER_SUB) for q in range(n_sub)]

    def activations(dst, src, q):
        words = slice(rows[q].start // 2, rows[q].stop // 2)
        dst[rows[q], :] = jnp.dot(pltpu.bitcast(u_ref[words, :], BF16), src[0], preferred_element_type=F32)

    @pl.when(e == 0)
    def _():
        yt_scr[t] = jnp.zeros(yt_scr.shape[1:], F32)

    @pl.when(t == 0)
    def _():
        for q in range(n_sub):
            activations(at0_scr, a_ref, q)

    def step(act_ref, next_ref):
        y = None
        for q in range(n_sub):
            activations(next_ref, an_ref, q)
            for il in range(PEER_SUB // N_KEYS):
                key0 = q * (PEER_SUB // N_KEYS) + il
                for c in range(tm // LANES):
                    cs = slice(c * LANES, (c + 1) * LANES)
                    row = lambda ref, hd: jnp.broadcast_to(ref[0, hd, key0:key0 + 1, cs], (BF16_ROWS, LANES)).astype(BF16)
                    thr = [row(thr_ref, hd) for hd in range(PEER_HEADS)]
                    e0 = [row(e0_ref, hd) for hd in range(PEER_HEADS)]
                    zero = jnp.zeros((BF16_ROWS, LANES), BF16)
                    for r in range(N_KEYS // BF16_ROWS):
                        ws = slice(r * BF16_ROWS // 2, (r + 1) * BF16_ROWS // 2)
                        gate = None
                        for hd in range(PEER_HEADS):
                            rank = pltpu.bitcast(r1_ref[0, hd, ws, cs], BF16)
                            weight = pltpu.bitcast(e1_ref[0, hd, ws, cs], BF16)
                            part = jnp.where(rank >= thr[hd], weight * e0[hd], zero)
                            gate = part if gate is None else gate + part
                        ars = slice(rows[q].start + il * N_KEYS + r * BF16_ROWS,
                                    rows[q].start + il * N_KEYS + (r + 1) * BF16_ROWS)
                        wa_scr[ars, cs] = gate * _gelu(act_ref[ars, cs]).astype(BF16)
            part = jnp.dot(pltpu.bitcast(vt_ref[:, rows[q]], BF16), wa_scr[rows[q], :], preferred_element_type=F32)
            y = part if y is None else y + part
        yt_scr[t] += y

    @pl.when(t % 2 == 0)
    def _():
        step(at0_scr, at1_scr)

    @pl.when(t % 2 == 1)
    def _():
        step(at1_scr, at0_scr)

    @pl.when(e == pl.num_programs(1) - 1)
    def _():
        hn = h_ref[0] + mod_ref[0, 0][5:6] * yt_scr[t].T
        if final:
            hn = hn * lax.rsqrt(jnp.mean(hn * hn, axis=-1, keepdims=True) + EPS) * fg_ref[...]
        out_ref[0] = hn


def _peer_expert(hh, mod, a, thr, e0, r1, e1, u_bf, vt_bf, final_g, L, *, with_ctx, final):
    bsz, lt, d = hh.shape
    n_exp = vt_bf.shape[1]
    tm = PEER_TM
    n_tiles = (lt if with_ctx else L) // tm
    keys_per_step = PEER_EB // N_KEYS
    n_blocks = n_exp // PEER_EB
    tok = lambda r: pl.BlockSpec((1, PEER_HEADS, r, tm), (lambda b, e, t: (b, 0, e, t)) if r == keys_per_step
                                 else (lambda b, e, t: (b, 0, 0, t)))
    row_spec = pl.BlockSpec((1, tm, d), lambda b, e, t: (b, jnp.where(e == n_blocks - 1, t, 0), 0))
    if final:
        out_shape = jax.ShapeDtypeStruct((bsz, L, d), F32)
        aliases = {}
    else:
        out_shape = jax.ShapeDtypeStruct(hh.shape, F32)
        aliases = {8: 0}
    return pl.pallas_call(
        functools.partial(_peer_expert_kernel, final=final),
        grid=(bsz, n_blocks, n_tiles),
        in_specs=[pl.BlockSpec((1, d, tm), lambda b, e, t: (b, 0, t)),
                  pl.BlockSpec((1, d, tm), lambda b, e, t: (b, 0, jnp.minimum(t + 1, n_tiles - 1))),
                  tok(keys_per_step), tok(keys_per_step), tok(N_KEYS // 2), tok(N_KEYS // 2),
                  pl.BlockSpec((PEER_EB // 2, d), lambda b, e, t: (e, 0)),
                  pl.BlockSpec((d // 2, PEER_EB), lambda b, e, t: (0, e)),
                  row_spec,
                  pl.BlockSpec((1, 1, 6, d), lambda b, e, t: (b, t // (L // tm), 0, 0)),
                  pl.BlockSpec((1, d), lambda b, e, t: (0, 0))],
        out_specs=row_spec,
        out_shape=out_shape,
        input_output_aliases=aliases,
        scratch_shapes=[pltpu.VMEM((n_tiles, d, tm), F32), pltpu.VMEM((PEER_EB, tm), BF16),
                        pltpu.VMEM((PEER_EB, tm), F32), pltpu.VMEM((PEER_EB, tm), F32)],
        compiler_params=_cparams(("parallel", "arbitrary", "arbitrary")),
        name="peer_expert",
    )(a, a, thr, e0, r1, e1, u_bf, vt_bf, hh, mod, final_g.reshape(1, d))


def _pack_kernel(x_ref, o_ref, *, transpose):
    x = x_ref[...]
    if transpose:
        x = x.T
    o_ref[...] = pltpu.bitcast(x.astype(BF16), jnp.uint32)


def _pack_row_pairs(x, *, transpose):
    r, c = x.shape
    blk = 512
    if transpose:
        out_shape, out_spec = (c // 2, r), pl.BlockSpec((c // 2, blk), lambda i: (0, i))
    else:
        out_shape, out_spec = (r // 2, c), pl.BlockSpec((blk // 2, c), lambda i: (i, 0))
    return pl.pallas_call(
        functools.partial(_pack_kernel, transpose=transpose),
        grid=(r // blk,),
        in_specs=[pl.BlockSpec((blk, c), lambda i: (i, 0))],
        out_specs=out_spec,
        out_shape=jax.ShapeDtypeStruct(out_shape, jnp.uint32),
        compiler_params=_cparams(("parallel",)),
        name="pack_pairs_t" if transpose else "pack_pairs",
    )(x)


def _peer(hh, mod, norm_g, w_q, sub_keys, u, v, final_g, L, *, with_ctx, final):
    a, thr, e0, r1, e1 = _peer_route(hh, mod, norm_g, w_q, sub_keys, L)
    return _peer_expert(hh, mod, a, thr, e0, r1, e1, _pack_row_pairs(u, transpose=False),
                        _pack_row_pairs(v, transpose=True), final_g, L, with_ctx=with_ctx, final=final)


REC_CHUNKS_PER_STEP = 2
REC_QKV = C_HEADS * 3 * C_DK
REC_Z = C_HEADS * C_DK
REC_GATES = 4 * C_HEADS
REC_HD = D_HEADS * D_DK


def _rec_proj_kernel(h_ref, mod_ref, g_ref, w_ref, alog_ref, dtb_ref, cos_ref, sin_ref,
                     qkv_ref, z_ref, qd_ref, kd_ref, vd_ref, gd_ref, gate_ref):
    m = mod_ref[0, 0]
    a = _norm_mod(h_ref[0], g_ref[...], m[0:1], m[1:2])
    o = _bdot(a, w_ref[...])
    cos, sin = cos_ref[...], sin_ref[...]
    off = 0
    qkv_ref[0] = o[:, off:off + REC_QKV]
    off += REC_QKV
    z_ref[0] = o[:, off:off + REC_Z]
    off += REC_Z
    for hd in range(D_HEADS):
        qd_ref[0, :, hd * D_DK:(hd + 1) * D_DK] = _rope(o[:, off + hd * D_DK: off + (hd + 1) * D_DK], cos, sin, D_DK)
    off += REC_HD
    for hd in range(D_HEADS):
        kd_ref[0, :, hd * D_DK:(hd + 1) * D_DK] = (
            _rope(o[:, off + hd * D_DK: off + (hd + 1) * D_DK], cos, sin, D_DK) * (D_DK ** -0.5))
    off += REC_HD
    vd_ref[0] = o[:, off:off + REC_HD]
    off += REC_HD
    gd_ref[0] = o[:, off:off + 2 * REC_HD]
    off += 2 * REC_HD
    x = o[:, off:off + LANES]
    lane = lax.broadcasted_iota(jnp.int32, x.shape, 1)
    xb = x + dtb_ref[...]
    softplus = jnp.maximum(xb, 0.0) + jnp.log(1.0 + jnp.exp(-jnp.abs(xb)))
    gate_ref[0] = jnp.where(lane < 2 * C_HEADS, -jnp.exp(alog_ref[...]) * softplus, _sigmoid(x))


def _rec_project(hh, mod, norm_g, w_in, a_log, dt_bias, cos, sin, L):
    bsz, lt, d = hh.shape
    tm = ROW_TILE
    parts = np.cumsum([REC_QKV, REC_Z, REC_GATES, REC_HD, REC_HD, REC_HD])
    qkv_w, z_w, gates_w, qd_w, kd_w, vd_w, gd_w = jnp.split(w_in, [int(p) for p in parts], axis=1)
    w = jnp.concatenate([qkv_w, z_w, qd_w, kd_w, vd_w, gd_w, gates_w, jnp.zeros((d, LANES - REC_GATES), F32)],
                        axis=1).astype(BF16)
    n = w.shape[1]
    pad = lambda p: jnp.zeros((1, LANES), F32).at[0, :2 * C_HEADS].set(p.reshape(-1))
    widths = [REC_QKV, REC_Z, REC_HD, REC_HD, REC_HD, 2 * REC_HD, LANES]
    const = lambda shape: pl.BlockSpec(shape, lambda b, i: (0,) * len(shape))
    return pl.pallas_call(
        _rec_proj_kernel,
        grid=(bsz, lt // tm),
        in_specs=[pl.BlockSpec((1, tm, d), lambda b, i: (b, i, 0)),
                  pl.BlockSpec((1, 1, 6, d), lambda b, i: (b, i // (L // tm), 0, 0)),
                  const((1, d)), const((d, n)), const((1, LANES)), const((1, LANES)),
                  pl.BlockSpec((tm, LANES), lambda b, i: (i, 0)),
                  pl.BlockSpec((tm, LANES), lambda b, i: (i, 0))],
        out_specs=[pl.BlockSpec((1, tm, wd), lambda b, i: (b, i, 0)) for wd in widths],
        out_shape=[jax.ShapeDtypeStruct((bsz, lt, wd), F32) for wd in widths],
        compiler_params=_cparams(("parallel", "parallel")),
        name="rec_project",
    )(hh, mod, norm_g.reshape(1, d), w, pad(a_log), pad(dt_bias), cos, sin)


def _rec_conv_kernel(x_ref, prev_ref, next_ref, w_ref, q_ref, k_ref, v_ref, *, L):
    i = pl.program_id(1)
    tl = x_ref.shape[1]
    n_lat = L // tl
    at_start = (i == 0) | (i == n_lat)
    at_end = (i == n_lat - 1) | (i == pl.num_programs(1) - 1)
    prev = jnp.where(at_start, 0.0, prev_ref[0])
    nxt = jnp.where(at_end, 0.0, next_ref[0])
    xx = jnp.concatenate([prev, x_ref[0], nxt], axis=0)
    n = tl + 16
    acc = 0.0
    for tap in range(CONV_W):
        shift = (CONV_W // 2 - tap) % n
        shifted = xx if shift == 0 else pltpu.roll(xx, shift, 0)
        acc = acc + shifted[8:8 + tl] * w_ref[tap:tap + 1, :]
    y = _silu(acc)
    hw = C_HEADS * C_DK
    for hd in range(C_HEADS):
        def l2(x):
            return x * lax.rsqrt(jnp.sum(x * x, axis=-1, keepdims=True) + EPS)
        sl = slice(hd * C_DK, (hd + 1) * C_DK)
        q_ref[0, :, sl] = l2(y[:, hd * C_DK:(hd + 1) * C_DK]) * (C_DK ** -0.5)
        k_ref[0, :, sl] = l2(y[:, hw + hd * C_DK: hw + (hd + 1) * C_DK])
    v_ref[0] = y[:, 2 * hw:]


def _rec_conv(qkv, conv_w, L):
    bsz, lt, ch = qkv.shape
    tl = ROW_TILE
    hb = tl // 8
    last = lt // 8 - 1
    hw = C_HEADS * C_DK
    return pl.pallas_call(
        functools.partial(_rec_conv_kernel, L=L),
        grid=(bsz, lt // tl),
        in_specs=[pl.BlockSpec((1, tl, ch), lambda b, i: (b, i, 0)),
                  pl.BlockSpec((1, 8, ch), lambda b, i: (b, jnp.maximum(i * hb - 1, 0), 0)),
                  pl.BlockSpec((1, 8, ch), lambda b, i: (b, jnp.minimum((i + 1) * hb, last), 0)),
                  pl.BlockSpec((8, ch), lambda b, i: (0, 0))],
        out_specs=[pl.BlockSpec((1, tl, hw), lambda b, i: (b, i, 0))] * 3,
        out_shape=[jax.ShapeDtypeStruct((bsz, lt, hw), F32)] * 3,
        compiler_params=_cparams(("parallel", "parallel")),
        name="rec_conv",
    )(qkv, qkv, qkv, jnp.zeros((8, ch), F32).at[:CONV_W].set(conv_w))


def _ret_log_gamma(hd):
    return float(np.log1p(-np.exp2(-(RET_DECAY_BASE + hd))))


def _rec_intra_kernel(qc_ref, kc_ref, vc_ref, qd_ref, kd_ref, vd_ref, gate_ref,
                      qs_ref, oi_ref, pp_ref, nn_ref, al_ref):
    c = CHUNK
    ci = lax.broadcasted_iota(jnp.int32, (c, c), 0)
    si = lax.broadcasted_iota(jnp.int32, (c, c), 1)
    pos = lax.broadcasted_iota(jnp.int32, (c, 1), 0).astype(F32)
    ones_row = jnp.ones((1, LANES), F32)
    combos = []
    for cc in range(REC_CHUNKS_PER_STEP):
        _rec_intra_setup(cc, combos, ci, si, pos, ones_row, qc_ref, kc_ref, vc_ref, qd_ref, kd_ref, vd_ref, gate_ref,
                         qs_ref, oi_ref, nn_ref, al_ref)
    _rec_intra_solve(combos, ones_row, qs_ref, oi_ref, pp_ref, nn_ref, al_ref)


def _rec_intra_setup(cc, combos, ci, si, pos, ones_row, qc_ref, kc_ref, vc_ref, qd_ref, kd_ref, vd_ref, gate_ref,
                     qs_ref, oi_ref, nn_ref, al_ref):
    c = CHUNK
    rs = slice(cc * c, (cc + 1) * c)
    gates = gate_ref[0, rs]
    la_parts = _split3(gates)
    for d in range(2):
        incl = (si <= ci) if d == 0 else (si >= ci)
        strict = (si < ci) if d == 0 else (si > ci)
        tri = incl.astype(BF16)
        g_all = jnp.dot(jnp.concatenate([tri] * 3, axis=1), jnp.concatenate(la_parts, axis=0),
                        preferred_element_type=F32)
        g_all_t = g_all.T
        last = c - 1 if d == 0 else 0
        for hd in range(C_HEADS):
            col = d * C_HEADS + hd
            sl = slice(hd * C_DK, (hd + 1) * C_DK)
            q, k, v = qc_ref[0, rs, sl], kc_ref[0, rs, sl], vc_ref[0, rs, sl]
            g_rows = jnp.broadcast_to(g_all[:, col:col + 1], (c, C_DK))
            beta = jnp.broadcast_to(gates[:, 2 * C_HEADS + col: 2 * C_HEADS + col + 1], (c, C_DK))
            grow = g_all_t[col:col + 1, :]
            glast = g_all[last:last + 1, col:col + 1]
            exp_g = jnp.exp(g_rows)
            diff = g_rows[:, :c] - grow
            dec_strict = jnp.exp(jnp.where(strict, diff, _NEG_INF))
            combos.append(dict(
                cc=cc, d=d, hd=hd, q=q, exp_g=exp_g, glast=glast,
                x=-(beta[:, :c] * _bdot_nt(k, k) * dec_strict),
                sol=jnp.concatenate([beta * v, (beta * exp_g) * k], axis=1),
                qk=_bdot_nt(q, k) * jnp.exp(jnp.where(incl, diff, _NEG_INF)),
                kend=k * jnp.exp(glast - g_rows)))
    for d in range(2):
        incl = (si <= ci) if d == 0 else (si >= ci)
        for hd in range(D_HEADS):
            lg = _ret_log_gamma(hd)
            sl = slice(hd * D_DK, (hd + 1) * D_DK)
            q, k, v = qd_ref[0, rs, sl], kd_ref[0, rs, sl], vd_ref[0, rs, sl]
            steps = pos if d == 0 else (c - 1.0) - pos
            dist = (ci - si) if d == 0 else (si - ci)
            dmat = jnp.exp(jnp.where(incl, lg * dist.astype(F32), _NEG_INF))
            qk = _bdot_nt(q, k) * dmat
            qs_ref[0, cc, d, C_HEADS + hd] = (q * jnp.exp(lg * (steps + 1.0))).astype(qs_ref.dtype)
            oi_ref[0, cc, d, C_HEADS + hd] = _bdot(qk, v)
            nn_ref[0, cc, d, C_HEADS + hd] = _bdot_tn(k * jnp.exp(lg * ((c - 1.0) - steps)), v)
            al_ref[0, cc, d, C_HEADS + hd:C_HEADS + hd + 1, :] = math.exp(lg * c) * ones_row


def _rec_intra_solve(combos, ones_row, qs_ref, oi_ref, pp_ref, nn_ref, al_ref):
    c = CHUNK
    for cb in combos:
        cb["kend_t"] = cb["kend"].T.astype(BF16)
    levels = int(math.log2(c))
    for lvl in range(levels):
        for cb in combos:
            cb["sol"] = cb["sol"] + _dot3(cb["x"], cb["sol"])
        if lvl < levels - 1:
            for cb in combos:
                cb["x"] = _dot3(cb["x"], cb["x"])
    for cb in combos:
        cb["out"] = _bdot(cb["qk"], cb["sol"])
    for cb in combos:
        cb["state"] = jnp.dot(cb["kend_t"], cb["sol"].astype(BF16), preferred_element_type=F32)
    for cb in combos:
        cc, d, hd = cb["cc"], cb["d"], cb["hd"]
        qs_ref[0, cc, d, hd] = (cb["exp_g"] * cb["q"] - cb["out"][:, C_DK:]).astype(qs_ref.dtype)
        oi_ref[0, cc, d, hd] = cb["out"][:, :C_DK]
        pp_ref[0, cc, d, hd] = cb["state"][:, C_DK:].astype(pp_ref.dtype)
        nn_ref[0, cc, d, hd] = cb["state"][:, :C_DK]
        al_ref[0, cc, d, hd:hd + 1, :] = jnp.exp(cb["glast"]) * ones_row


def _rec_intra(qc, kc, vc, qd, kd, vd, gate):
    bsz, lt, hw = qc.shape
    nc = lt // CHUNK
    nh = C_HEADS + D_HEADS
    per = REC_CHUNKS_PER_STEP
    row = lambda w: pl.BlockSpec((1, per * CHUNK, w), lambda b, n: (b, n, 0))
    lead = lambda *tail: pl.BlockSpec((1, per, 2) + tail, lambda b, n: (b, n, 0) + (0,) * len(tail))
    return pl.pallas_call(
        _rec_intra_kernel,
        grid=(bsz, nc // per),
        in_specs=[row(hw)] * 6 + [row(LANES)],
        out_specs=[lead(nh, CHUNK, C_DK), lead(nh, CHUNK, C_DK), lead(C_HEADS, C_DK, C_DK), lead(nh, C_DK, C_DK),
                   lead(nh, LANES)],
        out_shape=[jax.ShapeDtypeStruct((bsz, nc, 2, nh, CHUNK, C_DK), BF16),
                   jax.ShapeDtypeStruct((bsz, nc, 2, nh, CHUNK, C_DK), F32),
                   jax.ShapeDtypeStruct((bsz, nc, 2, C_HEADS, C_DK, C_DK), BF16),
                   jax.ShapeDtypeStruct((bsz, nc, 2, nh, C_DK, C_DK), F32),
                   jax.ShapeDtypeStruct((bsz, nc, 2, nh, LANES), F32)],
        compiler_params=_cparams(("parallel", "parallel")),
        name="rec_intra",
    )(qc, kc, vc, qd, kd, vd, gate)


def _rec_scan_kernel(qs_ref, oi_ref, pp_ref, nn_ref, al_ref, o_ref, s_scr):
    @pl.when(pl.program_id(2) == 0)
    def _():
        s_scr[...] = jnp.zeros_like(s_scr)

    for hd in range(C_HEADS + D_HEADS):
        s = s_scr[hd]
        sb = s.astype(BF16)
        o_ref[0, 0, :, hd * C_DK:(hd + 1) * C_DK] = (
            jnp.dot(qs_ref[0, 0, 0, hd], sb, preferred_element_type=F32) + oi_ref[0, 0, 0, hd])
        new = al_ref[0, 0, 0, hd:hd + 1, :] * s + nn_ref[0, 0, 0, hd]
        if hd < C_HEADS:
            new = new - jnp.dot(pp_ref[0, 0, 0, hd], sb, preferred_element_type=F32)
        s_scr[hd] = new


def _rec_scan(qs, oi, pp, nn, al, L):
    bsz, nc = qs.shape[:2]
    nh = C_HEADS + D_HEADS
    n_lat = L // CHUNK
    n_ctx = nc - n_lat

    def chunk(d, s):
        fwd = jnp.where(s < n_ctx, n_lat + s, s - n_ctx)
        bwd = nc - 1 - s
        return jnp.where(d == 0, fwd, bwd)

    lead = lambda *tail: pl.BlockSpec((1, 1, 1) + tail, lambda b, d, s: (b, chunk(d, s), d) + (0,) * len(tail))
    return pl.pallas_call(
        _rec_scan_kernel,
        grid=(bsz, 2, nc),
        in_specs=[lead(nh, CHUNK, C_DK), lead(nh, CHUNK, C_DK), lead(C_HEADS, C_DK, C_DK), lead(nh, C_DK, C_DK),
                  lead(nh, LANES)],
        out_specs=pl.BlockSpec((1, 1, CHUNK, nh * C_DK), lambda b, d, s: (b, d, chunk(d, s), 0)),
        out_shape=jax.ShapeDtypeStruct((bsz, 2, nc * CHUNK, nh * C_DK), F32),
        scratch_shapes=[pltpu.VMEM((nh, C_DK, C_DK), F32)],
        compiler_params=_cparams(("parallel", "parallel", "arbitrary")),
        name="rec_scan",
    )(qs, oi, pp, nn, al)


def _rec_out_kernel(of_ref, ob_ref, z_ref, gd_ref, og_ref, gn_ref, w_ref, h_ref, mod_ref, out_ref):
    hw = C_HEADS * C_DK
    of, ob = of_ref[0, 0], ob_ref[0, 0]
    z, gd = z_ref[0], gd_ref[0]
    parts = []
    for hd in range(C_HEADS):
        sl = slice(hd * C_DK, (hd + 1) * C_DK)
        x = of[:, sl] + ob[:, sl]
        y = x * lax.rsqrt(jnp.mean(x * x, axis=-1, keepdims=True) + EPS) * og_ref[...]
        parts.append(y * _silu(z[:, sl]))
    for hd in range(D_HEADS):
        sl = slice(hd * D_DK, (hd + 1) * D_DK)
        y = 0.0
        for d, o in enumerate((of, ob)):
            x = o[:, hw + hd * D_DK: hw + (hd + 1) * D_DK]
            mu = jnp.mean(x, axis=-1, keepdims=True)
            xc = x - mu
            var = jnp.mean(xc * xc, axis=-1, keepdims=True)
            y = y + xc * lax.rsqrt(var + EPS) * gn_ref[:, sl] * _silu(gd[:, d * hw + hd * D_DK: d * hw + (hd + 1) * D_DK])
        parts.append(y)
    mix = jnp.concatenate(parts, axis=1)
    out_ref[0] = h_ref[0] + mod_ref[0, 0][2:3] * _bdot(mix, w_ref[...])


def _rec_out(o_scan, z, gd, out_g, gn_g, w_out, hh, mod, L):
    bsz, lt, d = hh.shape
    tm = ROW_TILE
    hw = C_HEADS * C_DK
    return pl.pallas_call(
        _rec_out_kernel,
        grid=(bsz, L // tm),
        in_specs=[pl.BlockSpec((1, 1, tm, 2 * hw), lambda b, i: (b, 0, i, 0)),
                  pl.BlockSpec((1, 1, tm, 2 * hw), lambda b, i: (b, 1, i, 0)),
                  pl.BlockSpec((1, tm, hw), lambda b, i: (b, i, 0)),
                  pl.BlockSpec((1, tm, 2 * hw), lambda b, i: (b, i, 0)),
                  pl.BlockSpec((1, C_DK), lambda b, i: (0, 0)),
                  pl.BlockSpec((1, hw), lambda b, i: (0, 0)),
                  pl.BlockSpec((2 * hw, d), lambda b, i: (0, 0)),
                  pl.BlockSpec((1, tm, d), lambda b, i: (b, i, 0)),
                  pl.BlockSpec((1, 1, 6, d), lambda b, i: (b, 0, 0, 0))],
        out_specs=pl.BlockSpec((1, tm, d), lambda b, i: (b, i, 0)),
        out_shape=jax.ShapeDtypeStruct(hh.shape, F32),
        input_output_aliases={7: 0},
        compiler_params=_cparams(("parallel", "parallel")),
        name="rec_out",
    )(o_scan, o_scan, z, gd, out_g.reshape(1, C_DK), gn_g.reshape(1, hw), w_out.astype(BF16), hh, mod)


def _att_layer(hh, mod, norm_g, w_in, q_g, k_g, sink, w_out, L):
    cos, sin = _rope_tables(L, hh.shape[1] - L, HEAD_DIM)
    qa, ka, va, qb, kb, vb = _att_project(hh, mod, norm_g, w_in, q_g, k_g, cos, sin, L)
    oa, ob = _attention(qa, ka, va, qb, kb, vb, sink, L)
    return _att_out(oa, ob, w_out, hh, mod, L)


def _rec_layer(hh, mod, norm_g, w_in, conv_w, a_log, dt_bias, out_g, gn_g, w_out, L):
    cos, sin = _rope_tables(L, hh.shape[1] - L, D_DK)
    qkv, z, qd, kd, vd, gd, gate = _rec_project(hh, mod, norm_g, w_in, a_log, dt_bias, cos, sin, L)
    qc, kc, vc = _rec_conv(qkv, conv_w, L)
    qs, oi, pp, nn, al = _rec_intra(qc, kc, vc, qd, kd, vd, gate)
    o_scan = _rec_scan(qs, oi, pp, nn, al, L)
    return _rec_out(o_scan, z, gd, out_g, gn_g, w_out, hh, mod, L)


def kernel(x, c, ctx, c_ctx, mod_w, mod_b, norm1_g, norm2_g, att_w_in, att_q_norm, att_k_norm, att_sink, att_w_out,
           rec_w_in, rec_conv_w, rec_a_log, rec_dt_bias, rec_out_norm, rec_gn_g, rec_w_out,
           peer_w_q, peer_sub_keys, peer_u, peer_v, final_norm_g):
    L = x.shape[1]
    depth = mod_w.shape[0]
    mods = _modulation(c, c_ctx, mod_w, mod_b)
    hh = jnp.concatenate([x, ctx], axis=1)
    for layer in range(depth):
        last = layer == depth - 1
        i = layer // 2
        if layer % 2 == 0:
            hh = _att_layer(hh, mods[layer], norm1_g[layer], att_w_in[i], att_q_norm[i], att_k_norm[i], att_sink[i],
                            att_w_out[i], L)
        else:
            hh = _rec_layer(hh, mods[layer], norm1_g[layer], rec_w_in[i], rec_conv_w[i], rec_a_log[i], rec_dt_bias[i],
                            rec_out_norm[i], rec_gn_g[i], rec_w_out[i], L)
        hh = _peer(hh, mods[layer], norm2_g[layer], peer_w_q[layer], peer_sub_keys[layer], peer_u[layer],
                   peer_v[layer], final_norm_g, L, with_ctx=not last, final=last)
    return hh
```

```python
import functools
import math

import numpy as np
import jax
import jax.numpy as jnp
from jax import lax
from jax.experimental import pallas as pl
from jax.experimental.pallas import tpu as pltpu

F32 = jnp.float32
BF16 = jnp.bfloat16

GRID_W = 64
EPS = 1e-6
HEAD_DIM = 64
A_HEADS = 8
A_KV = 2
B_HEADS = 8
B_KV = 2
WINDOW = 128
ROPE_THETA = 10000.0
C_HEADS = 4
C_DK = 128
CONV_W = 5
CHUNK = 64
D_HEADS = 4
D_DK = 128
RET_DECAY_BASE = 5.0
PEER_HEADS = 8
N_KEYS = 128
PEER_TOPK = 16

LANES = 128
BF16_ROWS = 16
VMEM_LIMIT = 56 * 1024 * 1024

ROW_TILE = 256
ATT_TQ = 128
PEER_TM = 256
PEER_EB = 2048
PEER_SUB = 256
PEER_TOK = 256

_NEG_INF = float("-inf")


def _cparams(sem, flags=None):
    return pltpu.CompilerParams(dimension_semantics=sem, vmem_limit_bytes=VMEM_LIMIT, flags=flags)


def _bdot(a, b):
    return jnp.dot(a.astype(BF16), b.astype(BF16), preferred_element_type=F32)


def _bdot_nt(a, b):
    return lax.dot_general(a.astype(BF16), b.astype(BF16), (((1,), (1,)), ((), ())), preferred_element_type=F32)


def _bdot_tn(a, b):
    return jnp.dot(a.T.astype(BF16), b.astype(BF16), preferred_element_type=F32)


def _split3(a):
    hi = a.astype(BF16)
    r1 = a - hi.astype(F32)
    mid = r1.astype(BF16)
    lo = (r1 - mid.astype(F32)).astype(BF16)
    return hi, mid, lo


def _dot3(a, b):
    a_hi, a_lo, _ = _split3(a)
    b_hi, b_lo, _ = _split3(b)
    return jnp.dot(jnp.concatenate([a_hi, a_lo, a_hi], axis=1), jnp.concatenate([b_hi, b_hi, b_lo], axis=0),
                   preferred_element_type=F32)


def _sigmoid(x):
    return 1.0 / (1.0 + jnp.exp(-x))


def _silu(x):
    return x * _sigmoid(x)


def _norm_mod(x, g, shift, scale):
    r = lax.rsqrt(jnp.mean(x * x, axis=-1, keepdims=True) + EPS)
    return (x * r * g) * (1.0 + scale) + shift


def _rope(x, cos, sin_signed, head_dim):
    quarter = head_dim // 4
    lane = lax.broadcasted_iota(jnp.int32, x.shape, 1)
    first = (lane % (2 * quarter)) < quarter
    partner = jnp.where(first, pltpu.roll(x, LANES - quarter, 1), pltpu.roll(x, quarter, 1))
    return x * cos + partner * sin_signed


def _mod_kernel(c_ref, w_ref, b_ref, o_ref):
    o_ref[0] = _bdot(_silu(c_ref[...]), w_ref[0]) + b_ref[0]


def _modulation(c, c_ctx, mod_w, mod_b):
    depth, d, n = mod_w.shape
    bsz = c.shape[0]
    rows = 16
    cc = jnp.zeros((rows, d), F32).at[:bsz].set(c).at[bsz].set(c_ctx)
    tn = 1536
    out = pl.pallas_call(
        _mod_kernel,
        grid=(depth, n // tn),
        in_specs=[pl.BlockSpec((rows, d), lambda l, j: (0, 0)),
                  pl.BlockSpec((1, d, tn), lambda l, j: (l, 0, j)),
                  pl.BlockSpec((1, 1, tn), lambda l, j: (l, 0, j))],
        out_specs=pl.BlockSpec((1, rows, tn), lambda l, j: (l, 0, j)),
        out_shape=jax.ShapeDtypeStruct((depth, rows, n), F32),
        compiler_params=_cparams(("arbitrary", "arbitrary")),
        name="modulation",
    )(cc, mod_w, mod_b.reshape(depth, 1, n))
    lat = out[:, :bsz].reshape(depth, bsz, 1, 6, d)
    ctx = jnp.broadcast_to(out[:, bsz].reshape(depth, 1, 1, 6, d), (depth, bsz, 1, 6, d))
    return jnp.concatenate([lat, ctx], axis=2)


def _rope_tables(L, LC, head_dim):
    quarter, half = head_dim // 4, head_dim // 2
    freqs = ROPE_THETA ** (-jnp.arange(quarter, dtype=F32) / quarter)
    lane = np.arange(LANES)
    within = lane % head_dim
    use_col = within >= half
    fidx = within % quarter
    sign = np.where((within % half) < quarter, -1.0, 1.0).astype(np.float32)
    t = jnp.arange(L, dtype=jnp.int32)
    row, col = (t // GRID_W).astype(F32), (t % GRID_W).astype(F32)
    pos = jnp.where(use_col[None, :], col[:, None], row[:, None])
    ang = pos * freqs[fidx][None, :]
    cos = jnp.concatenate([jnp.cos(ang), jnp.ones((LC, LANES), F32)], axis=0)
    sin = jnp.concatenate([jnp.sin(ang) * sign[None, :], jnp.zeros((LC, LANES), F32)], axis=0)
    return cos, sin


def _att_proj_kernel(h_ref, mod_ref, g_ref, w_ref, qg_ref, kg_ref, cos_ref, sin_ref, gm_ref,
                     qa_ref, ka_ref, va_ref, qb_ref, kb_ref, vb_ref):
    m = mod_ref[0, 0]
    a = _norm_mod(h_ref[0], g_ref[...], m[0:1], m[1:2])
    o = _bdot(a, w_ref[...])
    cos, sin = cos_ref[...], sin_ref[...]
    gm = gm_ref[...]

    def head_norm(x, gain):
        sq = x * x
        hi = sq.astype(BF16)
        lo = (sq - hi.astype(F32)).astype(BF16)
        ms = jnp.dot(hi, gm, preferred_element_type=F32) + jnp.dot(lo, gm, preferred_element_type=F32)
        return x * lax.rsqrt(ms + EPS) * gain

    def put(ref, first_head, x):
        ref[0, first_head] = x[:, :HEAD_DIM].astype(ref.dtype)
        ref[0, first_head + 1] = x[:, HEAD_DIM:].astype(ref.dtype)

    scale = HEAD_DIM ** -0.5
    qa_w = A_HEADS * HEAD_DIM
    kv_w = A_KV * HEAD_DIM
    off = 0
    for c in range(qa_w // LANES):
        x = o[:, off + c * LANES: off + (c + 1) * LANES]
        put(qa_ref, 2 * c, _rope(head_norm(x, qg_ref[...]), cos, sin, HEAD_DIM) * scale)
    off += qa_w
    put(ka_ref, 0, _rope(head_norm(o[:, off: off + kv_w], kg_ref[...]), cos, sin, HEAD_DIM))
    off += kv_w
    put(va_ref, 0, o[:, off: off + kv_w])
    off += kv_w
    for c in range(qa_w // LANES):
        x = o[:, off + c * LANES: off + (c + 1) * LANES]
        put(qb_ref, 2 * c, _rope(x, cos, sin, HEAD_DIM) * scale)
    off += qa_w
    put(kb_ref, 0, _rope(o[:, off: off + kv_w], cos, sin, HEAD_DIM))
    off += kv_w
    put(vb_ref, 0, o[:, off: off + kv_w])


def _att_project(hh, mod, norm_g, w_in, q_g, k_g, cos, sin, L):
    bsz, lt, d = hh.shape
    tm = ROW_TILE
    n = w_in.shape[1]
    gm = jnp.asarray(np.kron(np.eye(LANES // HEAD_DIM), np.full((HEAD_DIM, HEAD_DIM), 1.0 / HEAD_DIM)), BF16)
    tile2 = lambda v: jnp.tile(v.reshape(1, HEAD_DIM), (1, LANES // HEAD_DIM))
    qshape = jax.ShapeDtypeStruct((bsz, A_HEADS, lt, HEAD_DIM), BF16)
    kshape = jax.ShapeDtypeStruct((bsz, A_KV, lt, HEAD_DIM), BF16)
    qspec = pl.BlockSpec((1, A_HEADS, tm, HEAD_DIM), lambda b, i: (b, 0, i, 0))
    kspec = pl.BlockSpec((1, A_KV, tm, HEAD_DIM), lambda b, i: (b, 0, i, 0))
    const = lambda shape: pl.BlockSpec(shape, lambda b, i: (0,) * len(shape))
    return pl.pallas_call(
        _att_proj_kernel,
        grid=(bsz, lt // tm),
        in_specs=[pl.BlockSpec((1, tm, d), lambda b, i: (b, i, 0)),
                  pl.BlockSpec((1, 1, 6, d), lambda b, i: (b, i // (L // tm), 0, 0)),
                  const((1, d)), const((d, n)), const((1, LANES)), const((1, LANES)),
                  pl.BlockSpec((tm, LANES), lambda b, i: (i, 0)),
                  pl.BlockSpec((tm, LANES), lambda b, i: (i, 0)),
                  const((LANES, LANES))],
        out_specs=[qspec, kspec, kspec, qspec, kspec, kspec],
        out_shape=[qshape, kshape, kshape, qshape, kshape, kshape],
        compiler_params=_cparams(("parallel", "parallel")),
        name="att_project",
    )(hh, mod, norm_g.reshape(1, d), w_in.astype(BF16), tile2(q_g), tile2(k_g), cos, sin, gm)


def _softmax_pv(score_parts, value_parts, extra_logit=None):
    m = functools.reduce(jnp.maximum, [jnp.max(s, axis=-1, keepdims=True) for s in score_parts])
    if extra_logit is not None:
        m = jnp.maximum(m, extra_logit)
    l = 0.0
    acc = 0.0
    for s, v in zip(score_parts, value_parts):
        p = jnp.exp(s - m)
        l = l + jnp.sum(p, axis=-1, keepdims=True)
        acc = acc + jnp.dot(p.astype(BF16), v, preferred_element_type=F32)
    if extra_logit is not None:
        l = l + jnp.exp(extra_logit - m)
    return acc / l


def _att_kernel(sink_ref, qa_ref, ka_ref, va_ref, qb_ref, kb_ref, vb_ref, oa_ref, ob_ref, *, L, LC):
    i = pl.program_id(1)
    tq = ATT_TQ
    group = A_HEADS // A_KV
    band = tq + 2 * WINDOW

    def stacked_q(ref, kvh):
        return ref[0, kvh * group:(kvh + 1) * group].reshape(group * tq, HEAD_DIM)

    def put(ref, kvh, o):
        for g in range(group):
            hd = kvh * group + g
            ref[0, :, hd * HEAD_DIM:(hd + 1) * HEAD_DIM] = o[g * tq:(g + 1) * tq].astype(ref.dtype)

    def sink_col(kvh):
        return jnp.concatenate([jnp.full((tq, 1), sink_ref[kvh * group + g], F32) for g in range(group)], axis=0)

    @pl.when(i < L // tq)
    def _latent():
        for kvh in range(A_KV):
            q = stacked_q(qa_ref, kvh)
            put(oa_ref, kvh, _softmax_pv([_bdot_nt(q, ka_ref[0, kvh])], [va_ref[0, kvh]]))
            q = stacked_q(qb_ref, kvh)
            start = pl.multiple_of(jnp.clip((i - 1) * tq, 0, L - band), tq)
            s_band = _bdot_nt(q, kb_ref[0, kvh, pl.ds(start, band), :])
            qpos = i * tq + (lax.broadcasted_iota(jnp.int32, s_band.shape, 0) % tq)
            kpos = start + lax.broadcasted_iota(jnp.int32, s_band.shape, 1)
            s_band = jnp.where(jnp.abs(qpos - kpos) <= WINDOW, s_band, _NEG_INF)
            s_ctx = _bdot_nt(q, kb_ref[0, kvh, L:L + LC, :])
            put(ob_ref, kvh, _softmax_pv([s_band, s_ctx],
                                         [vb_ref[0, kvh, pl.ds(start, band), :], vb_ref[0, kvh, L:L + LC, :]],
                                         sink_col(kvh)))

    @pl.when(i >= L // tq)
    def _context():
        for kvh in range(A_KV):
            q = stacked_q(qa_ref, kvh)
            put(oa_ref, kvh, _softmax_pv([_bdot_nt(q, ka_ref[0, kvh, L:L + LC, :])], [va_ref[0, kvh, L:L + LC, :]]))
            q = stacked_q(qb_ref, kvh)
            put(ob_ref, kvh, _softmax_pv([_bdot_nt(q, kb_ref[0, kvh, L:L + LC, :])], [vb_ref[0, kvh, L:L + LC, :]],
                                         sink_col(kvh)))


def _attention(qa, ka, va, qb, kb, vb, sink, L):
    bsz, _, lt, _ = qa.shape
    tq = ATT_TQ
    qspec = pl.BlockSpec((1, A_HEADS, tq, HEAD_DIM), lambda b, i: (b, 0, i, 0))
    kspec = pl.BlockSpec((1, A_KV, lt, HEAD_DIM), lambda b, i: (b, 0, 0, 0))
    ospec = pl.BlockSpec((1, tq, A_HEADS * HEAD_DIM), lambda b, i: (b, i, 0))
    oshape = jax.ShapeDtypeStruct((bsz, lt, A_HEADS * HEAD_DIM), BF16)
    return pl.pallas_call(
        functools.partial(_att_kernel, L=L, LC=lt - L),
        grid=(bsz, lt // tq),
        in_specs=[pl.BlockSpec(memory_space=pltpu.SMEM), qspec, kspec, kspec, qspec, kspec, kspec],
        out_specs=[ospec, ospec],
        out_shape=[oshape, oshape],
        compiler_params=_cparams(("parallel", "parallel")),
        name="attention",
    )(sink, qa, ka, va, qb, kb, vb)


def _att_out_kernel(oa_ref, ob_ref, w_ref, h_ref, mod_ref, out_ref):
    half = oa_ref.shape[-1]
    y = (jnp.dot(oa_ref[0], w_ref[:half, :], preferred_element_type=F32)
         + jnp.dot(ob_ref[0], w_ref[half:, :], preferred_element_type=F32))
    out_ref[0] = h_ref[0] + mod_ref[0, 0][2:3] * y


def _att_out(oa, ob, w_out, hh, mod, L):
    bsz, lt, d = hh.shape
    tm = ROW_TILE
    half = oa.shape[-1]
    return pl.pallas_call(
        _att_out_kernel,
        grid=(bsz, lt // tm),
        in_specs=[pl.BlockSpec((1, tm, half), lambda b, i: (b, i, 0)),
                  pl.BlockSpec((1, tm, half), lambda b, i: (b, i, 0)),
                  pl.BlockSpec((2 * half, d), lambda b, i: (0, 0)),
                  pl.BlockSpec((1, tm, d), lambda b, i: (b, i, 0)),
                  pl.BlockSpec((1, 1, 6, d), lambda b, i: (b, i // (L // tm), 0, 0))],
        out_specs=pl.BlockSpec((1, tm, d), lambda b, i: (b, i, 0)),
        out_shape=jax.ShapeDtypeStruct(hh.shape, F32),
        input_output_aliases={3: 0},
        compiler_params=_cparams(("parallel", "parallel")),
        name="att_out",
    )(oa, ob, w_out.astype(BF16), hh, mod)


def _sort_network(n):
    pairs = []
    p = 1
    while p < n:
        k = p
        while k >= 1:
            for j in range(k % p, n - k, 2 * k):
                for i in range(min(k, n - j - k)):
                    if (i + j) // (2 * p) == (i + j + k) // (2 * p):
                        pairs.append((i + j, i + j + k))
            k //= 2
        p *= 2
    return pairs


def _pop_sorted(lists, n, singles=()):
    lists, singles = list(lists), list(singles)
    out = []
    for rnd in range(n):
        head = functools.reduce(jnp.maximum, [lists[0]] + singles)
        m = jnp.max(head, axis=0, keepdims=True)
        out.append(m)
        if rnd == n - 1:
            break
        hit = lists[0] == m
        depth = min(len(lists), n - rnd - 1)
        for d in range(depth):
            nxt = lists[d + 1] if d + 1 < len(lists) else _NEG_INF
            lists[d] = jnp.where(hit, nxt, lists[d])
        singles = [jnp.where(s == m, _NEG_INF, s) for s in singles]
    return out


def _top_values(x, n):
    slabs = [x[r * 8:(r + 1) * 8] for r in range(x.shape[0] // 8)]
    for i, j in _sort_network(len(slabs)):
        slabs[i], slabs[j] = jnp.maximum(slabs[i], slabs[j]), jnp.minimum(slabs[i], slabs[j])
    return _pop_sorted(slabs, n)


def _peer_route_kernel(h_ref, mod_ref, g_ref, wq_ref, keys_ref, at_ref, thr_ref, e0_ref, r1_ref, e1_ref, q_scr):
    m = mod_ref[0, 0]
    a = _norm_mod(h_ref[0], g_ref[...], m[3:4], m[4:5])
    at_ref[0] = a.T.astype(BF16)
    q_scr[...] = jnp.dot(a.astype(BF16), wq_ref[...], preferred_element_type=F32)
    k = PEER_TOPK
    n = k + 1

    def head(hd, carry):
        col = pl.multiple_of(hd * 2 * N_KEYS, 2 * N_KEYS)
        s0 = _bdot_nt(keys_ref[0], q_scr[:, pl.ds(col, N_KEYS)])
        s1 = _bdot_nt(keys_ref[1], q_scr[:, pl.ds(col + N_KEYS, N_KEYS)])
        tokens = s0.shape[1]
        top0 = _top_values(s0, n)
        top1 = _top_values(s1, n)
        first = jnp.concatenate(top0 + [jnp.full((24 - n, tokens), _NEG_INF, F32)], axis=0)
        best = _pop_sorted([first[0:8] + t for t in top1], n, [first[8:16] + top1[0], first[16:24] + top1[0]])
        z = functools.reduce(lambda x, y: x + y, [jnp.exp(b - best[0]) for b in best[:k]])
        tau = 0.5 * (best[k - 1] + best[k])
        r1 = jnp.zeros_like(s1)
        need = jnp.full_like(s0, n + 1.0)
        for b in range(k):
            r1 = jnp.where(s1 >= top1[k - 1 - b], b + 2.0, r1)
            need = jnp.where(s0 >= tau - top1[b], float(n - b), need)
        thr_ref[0, hd] = need
        e0_ref[0, hd] = jnp.exp(s0 - top0[0])
        r1_ref[0, hd] = pltpu.bitcast(r1.astype(BF16), jnp.uint32)
        e1_ref[0, hd] = pltpu.bitcast((jnp.exp(s1 - top1[0]) / z).astype(BF16), jnp.uint32)
        return carry

    lax.fori_loop(0, PEER_HEADS, head, 0, unroll=True)


def _peer_route(hh, mod, norm_g, w_q, sub_keys, L):
    bsz, lt, d = hh.shape
    tm = ROW_TILE
    nq = w_q.shape[1]
    tok = lambda: pl.BlockSpec((1, PEER_HEADS, N_KEYS, tm), lambda b, i: (b, 0, 0, i))
    tshape = jax.ShapeDtypeStruct((bsz, PEER_HEADS, N_KEYS, lt), F32)
    pshape = jax.ShapeDtypeStruct((bsz, PEER_HEADS, N_KEYS // 2, lt), jnp.uint32)
    ptok = pl.BlockSpec((1, PEER_HEADS, N_KEYS // 2, tm), lambda b, i: (b, 0, 0, i))
    return pl.pallas_call(
        _peer_route_kernel,
        grid=(bsz, lt // tm),
        in_specs=[pl.BlockSpec((1, tm, d), lambda b, i: (b, i, 0)),
                  pl.BlockSpec((1, 1, 6, d), lambda b, i: (b, i // (L // tm), 0, 0)),
                  pl.BlockSpec((1, d), lambda b, i: (0, 0)),
                  pl.BlockSpec((d, nq), lambda b, i: (0, 0)),
                  pl.BlockSpec((2, N_KEYS, N_KEYS), lambda b, i: (0, 0, 0))],
        out_specs=[pl.BlockSpec((1, d, tm), lambda b, i: (b, 0, i)), tok(), tok(), ptok, ptok],
        out_shape=[jax.ShapeDtypeStruct((bsz, d, lt), BF16), tshape, tshape, pshape, pshape],
        scratch_shapes=[pltpu.VMEM((tm, nq), F32)],
        compiler_params=_cparams(("parallel", "parallel")),
        name="peer_route",
    )(hh, mod, norm_g.reshape(1, d), w_q.astype(BF16), sub_keys.astype(BF16))


def _gelu(x):
    return 0.5 * x * (1.0 + lax.erf(x * (1.0 / math.sqrt(2.0))))


def _peer_expert_kernel(a_ref, an_ref, thr_ref, e0_ref, r1_ref, e1_ref, u_ref, vt_ref, h_ref, mod_ref, fg_ref,
                        out_ref, yt_scr, wa_scr, at0_scr, at1_scr, *, final):
    e = pl.program_id(1)
    t = pl.program_id(2)
    tm = a_ref.shape[2]
    n_sub = PEER_EB // PEER_SUB
    rows = [slice(q * PEER_SUB, (q + 1) * PEER_SUB) for q in range(n_sub)]

    def activations(dst, src, q):
        words = slice(rows[q].start // 2, rows[q].stop // 2)
        dst[rows[q], :] = jnp.dot(pltpu.bitcast(u_ref[words, :], BF16), src[0], preferred_element_type=F32)

    @pl.when(e == 0)
    def _():
        yt_scr[t] = jnp.zeros(yt_scr.shape[1:], F32)

    @pl.when(t == 0)
    def _():
        for q in range(n_sub):
            activations(at0_scr, a_ref, q)

    def step(act_ref, next_ref):
        y = None
        for q in range(n_sub):
            activations(next_ref, an_ref, q)
            for il in range(PEER_SUB // N_KEYS):
                key0 = q * (PEER_SUB // N_KEYS) + il
                for c in range(tm // LANES):
                    cs = slice(c * LANES, (c + 1) * LANES)
                    row = lambda ref, hd: jnp.broadcast_to(ref[0, hd, key0:key0 + 1, cs], (BF16_ROWS, LANES)).astype(BF16)
                    thr = [row(thr_ref, hd) for hd in range(PEER_HEADS)]
                    e0 = [row(e0_ref, hd) for hd in range(PEER_HEADS)]
                    zero = jnp.zeros((BF16_ROWS, LANES), BF16)
                    for r in range(N_KEYS // BF16_ROWS):
                        ws = slice(r * BF16_ROWS // 2, (r + 1) * BF16_ROWS // 2)
                        gate = None
                        for hd in range(PEER_HEADS):
                            rank = pltpu.bitcast(r1_ref[0, hd, ws, cs], BF16)
                            weight = pltpu.bitcast(e1_ref[0, hd, ws, cs], BF16)
                            part = jnp.where(rank >= thr[hd], weight * e0[hd], zero)
                            gate = part if gate is None else gate + part
                        ars = slice(rows[q].start + il * N_KEYS + r * BF16_ROWS,
                                    rows[q].start + il * N_KEYS + (r + 1) * BF16_ROWS)
                        wa_scr[ars, cs] = gate * _gelu(act_ref[ars, cs]).astype(BF16)
            part = jnp.dot(pltpu.bitcast(vt_ref[:, rows[q]], BF16), wa_scr[rows[q], :], preferred_element_type=F32)
            y = part if y is None else y + part
        yt_scr[t] += y

    @pl.when(t % 2 == 0)
    def _():
        step(at0_scr, at1_scr)

    @pl.when(t % 2 == 1)
    def _():
        step(at1_scr, at0_scr)

    @pl.when(e == pl.num_programs(1) - 1)
    def _():
        hn = h_ref[0] + mod_ref[0, 0][5:6] * yt_scr[t].T
        if final:
            hn = hn * lax.rsqrt(jnp.mean(hn * hn, axis=-1, keepdims=True) + EPS) * fg_ref[...]
        out_ref[0] = hn


def _peer_expert(hh, mod, a, thr, e0, r1, e1, u_bf, vt_bf, final_g, L, *, with_ctx, final):
    bsz, lt, d = hh.shape
    n_exp = vt_bf.shape[1]
    tm = PEER_TM
    n_tiles = (lt if with_ctx else L) // tm
    keys_per_step = PEER_EB // N_KEYS
    n_blocks = n_exp // PEER_EB
    tok = lambda r: pl.BlockSpec((1, PEER_HEADS, r, tm), (lambda b, e, t: (b, 0, e, t)) if r == keys_per_step
                                 else (lambda b, e, t: (b, 0, 0, t)))
    row_spec = pl.BlockSpec((1, tm, d), lambda b, e, t: (b, jnp.where(e == n_blocks - 1, t, 0), 0))
    if final:
        out_shape = jax.ShapeDtypeStruct((bsz, L, d), F32)
        aliases = {}
    else:
        out_shape = jax.ShapeDtypeStruct(hh.shape, F32)
        aliases = {8: 0}
    return pl.pallas_call(
        functools.partial(_peer_expert_kernel, final=final),
        grid=(bsz, n_blocks, n_tiles),
        in_specs=[pl.BlockSpec((1, d, tm), lambda b, e, t: (b, 0, t)),
                  pl.BlockSpec((1, d, tm), lambda b, e, t: (b, 0, jnp.minimum(t + 1, n_tiles - 1))),
                  tok(keys_per_step), tok(keys_per_step), tok(N_KEYS // 2), tok(N_KEYS // 2),
                  pl.BlockSpec((PEER_EB // 2, d), lambda b, e, t: (e, 0)),
                  pl.BlockSpec((d // 2, PEER_EB), lambda b, e, t: (0, e)),
                  row_spec,
                  pl.BlockSpec((1, 1, 6, d), lambda b, e, t: (b, t // (L // tm), 0, 0)),
                  pl.BlockSpec((1, d), lambda b, e, t: (0, 0))],
        out_specs=row_spec,
        out_shape=out_shape,
        input_output_aliases=aliases,
        scratch_shapes=[pltpu.VMEM((n_tiles, d, tm), F32), pltpu.VMEM((PEER_EB, tm), BF16),
                        pltpu.VMEM((PEER_EB, tm), F32), pltpu.VMEM((PEER_EB, tm), F32)],
        compiler_params=_cparams(("parallel", "arbitrary", "arbitrary")),
        name="peer_expert",
    )(a, a, thr, e0, r1, e1, u_bf, vt_bf, hh, mod, final_g.reshape(1, d))


def _pack_kernel(x_ref, o_ref, *, transpose):
    x = x_ref[0]
    if transpose:
        x = x.T
    o_ref[...] = pltpu.bitcast(x.astype(BF16), jnp.uint32)


def _pack_row_pairs(x, layer, *, transpose):
    _, r, c = x.shape
    blk = 512
    if transpose:
        out_shape, out_spec = (c // 2, r), pl.BlockSpec((c // 2, blk), lambda i: (0, i))
    else:
        out_shape, out_spec = (r // 2, c), pl.BlockSpec((blk // 2, c), lambda i: (i, 0))
    return pl.pallas_call(
        functools.partial(_pack_kernel, transpose=transpose),
        grid=(r // blk,),
        in_specs=[pl.BlockSpec((1, blk, c), lambda i: (layer, i, 0))],
        out_specs=out_spec,
        out_shape=jax.ShapeDtypeStruct(out_shape, jnp.uint32),
        compiler_params=_cparams(("parallel",)),
        name="pack_pairs_t" if transpose else "pack_pairs",
    )(x)


def _peer(hh, mod, norm_g, w_q, sub_keys, u_all, v_all, layer, final_g, L, *, with_ctx, final):
    a, thr, e0, r1, e1 = _peer_route(hh, mod, norm_g, w_q, sub_keys, L)
    return _peer_expert(hh, mod, a, thr, e0, r1, e1, _pack_row_pairs(u_all, layer, transpose=False),
                        _pack_row_pairs(v_all, layer, transpose=True), final_g, L, with_ctx=with_ctx, final=final)


REC_CHUNKS_PER_STEP = 2
REC_QKV = C_HEADS * 3 * C_DK
REC_Z = C_HEADS * C_DK
REC_GATES = 4 * C_HEADS
REC_HD = D_HEADS * D_DK


def _rec_proj_kernel(h_ref, mod_ref, g_ref, w_ref, alog_ref, dtb_ref, cos_ref, sin_ref,
                     qkv_ref, z_ref, qd_ref, kd_ref, vd_ref, gd_ref, gate_ref):
    m = mod_ref[0, 0]
    a = _norm_mod(h_ref[0], g_ref[...], m[0:1], m[1:2])
    o = _bdot(a, w_ref[...])
    cos, sin = cos_ref[...], sin_ref[...]
    off = 0
    qkv_ref[0] = o[:, off:off + REC_QKV]
    off += REC_QKV
    z_ref[0] = o[:, off:off + REC_Z]
    off += REC_Z
    for hd in range(D_HEADS):
        qd_ref[0, :, hd * D_DK:(hd + 1) * D_DK] = _rope(o[:, off + hd * D_DK: off + (hd + 1) * D_DK], cos, sin, D_DK)
    off += REC_HD
    for hd in range(D_HEADS):
        kd_ref[0, :, hd * D_DK:(hd + 1) * D_DK] = (
            _rope(o[:, off + hd * D_DK: off + (hd + 1) * D_DK], cos, sin, D_DK) * (D_DK ** -0.5))
    off += REC_HD
    vd_ref[0] = o[:, off:off + REC_HD]
    off += REC_HD
    gd_ref[0] = o[:, off:off + 2 * REC_HD]
    off += 2 * REC_HD
    x = o[:, off:off + LANES]
    lane = lax.broadcasted_iota(jnp.int32, x.shape, 1)
    xb = x + dtb_ref[...]
    softplus = jnp.maximum(xb, 0.0) + jnp.log(1.0 + jnp.exp(-jnp.abs(xb)))
    gate_ref[0] = jnp.where(lane < 2 * C_HEADS, -jnp.exp(alog_ref[...]) * softplus, _sigmoid(x))


def _rec_project(hh, mod, norm_g, w_in, a_log, dt_bias, cos, sin, L):
    bsz, lt, d = hh.shape
    tm = ROW_TILE
    parts = np.cumsum([REC_QKV, REC_Z, REC_GATES, REC_HD, REC_HD, REC_HD])
    qkv_w, z_w, gates_w, qd_w, kd_w, vd_w, gd_w = jnp.split(w_in, [int(p) for p in parts], axis=1)
    w = jnp.concatenate([qkv_w, z_w, qd_w, kd_w, vd_w, gd_w, gates_w, jnp.zeros((d, LANES - REC_GATES), F32)],
                        axis=1).astype(BF16)
    n = w.shape[1]
    pad = lambda p: jnp.zeros((1, LANES), F32).at[0, :2 * C_HEADS].set(p.reshape(-1))
    widths = [REC_QKV, REC_Z, REC_HD, REC_HD, REC_HD, 2 * REC_HD, LANES]
    const = lambda shape: pl.BlockSpec(shape, lambda b, i: (0,) * len(shape))
    return pl.pallas_call(
        _rec_proj_kernel,
        grid=(bsz, lt // tm),
        in_specs=[pl.BlockSpec((1, tm, d), lambda b, i: (b, i, 0)),
                  pl.BlockSpec((1, 1, 6, d), lambda b, i: (b, i // (L // tm), 0, 0)),
                  const((1, d)), const((d, n)), const((1, LANES)), const((1, LANES)),
                  pl.BlockSpec((tm, LANES), lambda b, i: (i, 0)),
                  pl.BlockSpec((tm, LANES), lambda b, i: (i, 0))],
        out_specs=[pl.BlockSpec((1, tm, wd), lambda b, i: (b, i, 0)) for wd in widths],
        out_shape=[jax.ShapeDtypeStruct((bsz, lt, wd), F32) for wd in widths],
        compiler_params=_cparams(("parallel", "parallel")),
        name="rec_project",
    )(hh, mod, norm_g.reshape(1, d), w, pad(a_log), pad(dt_bias), cos, sin)


def _rec_conv_kernel(x_ref, prev_ref, next_ref, w_ref, q_ref, k_ref, v_ref, *, L):
    i = pl.program_id(1)
    tl = x_ref.shape[1]
    n_lat = L // tl
    at_start = (i == 0) | (i == n_lat)
    at_end = (i == n_lat - 1) | (i == pl.num_programs(1) - 1)
    prev = jnp.where(at_start, 0.0, prev_ref[0])
    nxt = jnp.where(at_end, 0.0, next_ref[0])
    xx = jnp.concatenate([prev, x_ref[0], nxt], axis=0)
    n = tl + 16
    acc = 0.0
    for tap in range(CONV_W):
        shift = (CONV_W // 2 - tap) % n
        shifted = xx if shift == 0 else pltpu.roll(xx, shift, 0)
        acc = acc + shifted[8:8 + tl] * w_ref[tap:tap + 1, :]
    y = _silu(acc)
    hw = C_HEADS * C_DK
    for hd in range(C_HEADS):
        def l2(x):
            return x * lax.rsqrt(jnp.sum(x * x, axis=-1, keepdims=True) + EPS)
        sl = slice(hd * C_DK, (hd + 1) * C_DK)
        q_ref[0, :, sl] = l2(y[:, hd * C_DK:(hd + 1) * C_DK]) * (C_DK ** -0.5)
        k_ref[0, :, sl] = l2(y[:, hw + hd * C_DK: hw + (hd + 1) * C_DK])
    v_ref[0] = y[:, 2 * hw:]


def _rec_conv(qkv, conv_w, L):
    bsz, lt, ch = qkv.shape
    tl = ROW_TILE
    hb = tl // 8
    last = lt // 8 - 1
    hw = C_HEADS * C_DK
    return pl.pallas_call(
        functools.partial(_rec_conv_kernel, L=L),
        grid=(bsz, lt // tl),
        in_specs=[pl.BlockSpec((1, tl, ch), lambda b, i: (b, i, 0)),
                  pl.BlockSpec((1, 8, ch), lambda b, i: (b, jnp.maximum(i * hb - 1, 0), 0)),
                  pl.BlockSpec((1, 8, ch), lambda b, i: (b, jnp.minimum((i + 1) * hb, last), 0)),
                  pl.BlockSpec((8, ch), lambda b, i: (0, 0))],
        out_specs=[pl.BlockSpec((1, tl, hw), lambda b, i: (b, i, 0))] * 3,
        out_shape=[jax.ShapeDtypeStruct((bsz, lt, hw), F32)] * 3,
        compiler_params=_cparams(("parallel", "parallel")),
        name="rec_conv",
    )(qkv, qkv, qkv, jnp.zeros((8, ch), F32).at[:CONV_W].set(conv_w))


def _ret_log_gamma(hd):
    return float(np.log1p(-np.exp2(-(RET_DECAY_BASE + hd))))


def _rec_intra_kernel(qc_ref, kc_ref, vc_ref, qd_ref, kd_ref, vd_ref, gate_ref,
                      eq_ref, qk_ref, kt_ref, sol_ref, vr_ref, al_ref):
    c = CHUNK
    ci = lax.broadcasted_iota(jnp.int32, (c, c), 0)
    si = lax.broadcasted_iota(jnp.int32, (c, c), 1)
    pos = lax.broadcasted_iota(jnp.int32, (c, 1), 0).astype(F32)
    ones_row = jnp.ones((1, LANES), F32)
    combos = []
    for cc in range(REC_CHUNKS_PER_STEP):
        _rec_intra_setup(cc, combos, ci, si, pos, ones_row, qc_ref, kc_ref, vc_ref, qd_ref, kd_ref, vd_ref, gate_ref,
                         eq_ref, qk_ref, kt_ref, vr_ref, al_ref)
    _rec_intra_solve(combos, ones_row, eq_ref, qk_ref, kt_ref, sol_ref, al_ref)


def _rec_intra_setup(cc, combos, ci, si, pos, ones_row, qc_ref, kc_ref, vc_ref, qd_ref, kd_ref, vd_ref, gate_ref,
                     eq_ref, qk_ref, kt_ref, vr_ref, al_ref):
    c = CHUNK
    rs = slice(cc * c, (cc + 1) * c)
    gates = gate_ref[0, rs]
    la_parts = _split3(gates)
    for d in range(2):
        incl = (si <= ci) if d == 0 else (si >= ci)
        strict = (si < ci) if d == 0 else (si > ci)
        tri = incl.astype(BF16)
        g_all = jnp.dot(jnp.concatenate([tri] * 3, axis=1), jnp.concatenate(la_parts, axis=0),
                        preferred_element_type=F32)
        g_all_t = g_all.T
        last = c - 1 if d == 0 else 0
        for hd in range(C_HEADS):
            col = d * C_HEADS + hd
            sl = slice(hd * C_DK, (hd + 1) * C_DK)
            q, k, v = qc_ref[0, rs, sl], kc_ref[0, rs, sl], vc_ref[0, rs, sl]
            g_rows = jnp.broadcast_to(g_all[:, col:col + 1], (c, C_DK))
            beta = jnp.broadcast_to(gates[:, 2 * C_HEADS + col: 2 * C_HEADS + col + 1], (c, C_DK))
            grow = g_all_t[col:col + 1, :]
            glast = g_all[last:last + 1, col:col + 1]
            exp_g = jnp.exp(g_rows)
            diff = g_rows[:, :c] - grow
            dec_strict = jnp.exp(jnp.where(strict, diff, _NEG_INF))
            combos.append(dict(
                cc=cc, d=d, hd=hd, q=q, exp_g=exp_g, glast=glast,
                x=-(beta[:, :c] * _bdot_nt(k, k) * dec_strict),
                sol=jnp.concatenate([beta * v, (beta * exp_g) * k], axis=1),
                qk=_bdot_nt(q, k) * jnp.exp(jnp.where(incl, diff, _NEG_INF)),
                kend=k * jnp.exp(glast - g_rows)))
    for d in range(2):
        incl = (si <= ci) if d == 0 else (si >= ci)
        for hd in range(D_HEADS):
            lg = _ret_log_gamma(hd)
            sl = slice(hd * D_DK, (hd + 1) * D_DK)
            q, k, v = qd_ref[0, rs, sl], kd_ref[0, rs, sl], vd_ref[0, rs, sl]
            steps = pos if d == 0 else (c - 1.0) - pos
            dist = (ci - si) if d == 0 else (si - ci)
            dmat = jnp.exp(jnp.where(incl, lg * dist.astype(F32), _NEG_INF))
            qk = _bdot_nt(q, k) * dmat
            eq_ref[0, cc, d, C_HEADS + hd] = (q * jnp.exp(lg * (steps + 1.0))).astype(BF16)
            qk_ref[0, cc, d, C_HEADS + hd] = qk.astype(BF16)
            kt_ref[0, cc, d, C_HEADS + hd] = (k * jnp.exp(lg * ((c - 1.0) - steps))).T.astype(BF16)
            vr_ref[0, cc, d, hd] = v.astype(BF16)
            al_ref[0, cc, d, C_HEADS + hd:C_HEADS + hd + 1, :] = math.exp(lg * c) * ones_row


def _rec_intra_solve(combos, ones_row, eq_ref, qk_ref, kt_ref, sol_ref, al_ref):
    c = CHUNK
    for cb in combos:
        cc, d, hd = cb["cc"], cb["d"], cb["hd"]
        eq_ref[0, cc, d, hd] = (cb["exp_g"] * cb["q"]).astype(BF16)
        qk_ref[0, cc, d, hd] = cb["qk"].astype(BF16)
        kt_ref[0, cc, d, hd] = cb["kend"].T.astype(BF16)
        al_ref[0, cc, d, hd:hd + 1, :] = jnp.exp(cb["glast"]) * ones_row
    levels = int(math.log2(c))
    for lvl in range(levels):
        for cb in combos:
            cb["sol"] = cb["sol"] + _dot3(cb["x"], cb["sol"])
        if lvl < levels - 1:
            for cb in combos:
                cb["x"] = _dot3(cb["x"], cb["x"])
    for cb in combos:
        sol_ref[0, cb["cc"], cb["d"], cb["hd"]] = cb["sol"].astype(BF16)


def _rec_intra(qc, kc, vc, qd, kd, vd, gate):
    bsz, lt, hw = qc.shape
    nc = lt // CHUNK
    nh = C_HEADS + D_HEADS
    per = REC_CHUNKS_PER_STEP
    row = lambda w: pl.BlockSpec((1, per * CHUNK, w), lambda b, n: (b, n, 0))
    lead = lambda *tail: pl.BlockSpec((1, per, 2) + tail, lambda b, n: (b, n, 0) + (0,) * len(tail))
    return pl.pallas_call(
        _rec_intra_kernel,
        grid=(bsz, nc // per),
        in_specs=[row(hw)] * 6 + [row(LANES)],
        out_specs=[lead(nh, CHUNK, C_DK), lead(nh, CHUNK, CHUNK), lead(nh, C_DK, CHUNK), lead(C_HEADS, CHUNK, 2 * C_DK),
                   lead(D_HEADS, CHUNK, C_DK), lead(nh, LANES)],
        out_shape=[jax.ShapeDtypeStruct((bsz, nc, 2, nh, CHUNK, C_DK), BF16),
                   jax.ShapeDtypeStruct((bsz, nc, 2, nh, CHUNK, CHUNK), BF16),
                   jax.ShapeDtypeStruct((bsz, nc, 2, nh, C_DK, CHUNK), BF16),
                   jax.ShapeDtypeStruct((bsz, nc, 2, C_HEADS, CHUNK, 2 * C_DK), BF16),
                   jax.ShapeDtypeStruct((bsz, nc, 2, D_HEADS, CHUNK, C_DK), BF16),
                   jax.ShapeDtypeStruct((bsz, nc, 2, nh, LANES), F32)],
        compiler_params=_cparams(("parallel", "parallel")),
        name="rec_intra",
    )(qc, kc, vc, qd, kd, vd, gate)


def _rec_scan_kernel(eq_ref, qk_ref, kt_ref, sol_ref, vr_ref, al_ref, o_ref, s_scr):
    @pl.when(pl.program_id(2) == 0)
    def _():
        s_scr[...] = jnp.zeros_like(s_scr)

    dot = functools.partial(jnp.dot, preferred_element_type=F32)
    heads = range(C_HEADS + D_HEADS)
    s = [s_scr[hd] for hd in heads]
    sb = [x.astype(BF16) for x in s]
    wks = [dot(sol_ref[0, 0, 0, hd, :, C_DK:], sb[hd]) for hd in range(C_HEADS)]
    cross = [dot(eq_ref[0, 0, 0, hd], sb[hd]) for hd in heads]
    w = [(sol_ref[0, 0, 0, hd, :, :C_DK].astype(F32) - wks[hd]).astype(BF16) for hd in range(C_HEADS)]
    w += [vr_ref[0, 0, 0, hd] for hd in range(D_HEADS)]
    inner = [dot(qk_ref[0, 0, 0, hd], w[hd]) for hd in heads]
    update = [dot(kt_ref[0, 0, 0, hd], w[hd]) for hd in heads]
    for hd in heads:
        o_ref[0, 0, :, hd * C_DK:(hd + 1) * C_DK] = cross[hd] + inner[hd]
        s_scr[hd] = al_ref[0, 0, 0, hd:hd + 1, :] * s[hd] + update[hd]


def _rec_scan(eq, qk, kt, sol, vr, al, L):
    bsz, nc = eq.shape[:2]
    nh = C_HEADS + D_HEADS
    n_lat = L // CHUNK
    n_ctx = nc - n_lat

    def chunk(d, s):
        fwd = jnp.where(s < n_ctx, n_lat + s, s - n_ctx)
        bwd = nc - 1 - s
        return jnp.where(d == 0, fwd, bwd)

    lead = lambda *tail: pl.BlockSpec((1, 1, 1) + tail, lambda b, d, s: (b, chunk(d, s), d) + (0,) * len(tail))
    return pl.pallas_call(
        _rec_scan_kernel,
        grid=(bsz, 2, nc),
        in_specs=[lead(nh, CHUNK, C_DK), lead(nh, CHUNK, CHUNK), lead(nh, C_DK, CHUNK), lead(C_HEADS, CHUNK, 2 * C_DK),
                  lead(D_HEADS, CHUNK, C_DK), lead(nh, LANES)],
        out_specs=pl.BlockSpec((1, 1, CHUNK, nh * C_DK), lambda b, d, s: (b, d, chunk(d, s), 0)),
        out_shape=jax.ShapeDtypeStruct((bsz, 2, nc * CHUNK, nh * C_DK), F32),
        scratch_shapes=[pltpu.VMEM((nh, C_DK, C_DK), F32)],
        compiler_params=_cparams(("parallel", "parallel", "arbitrary")),
        name="rec_scan",
    )(eq, qk, kt, sol, vr, al)


def _rec_out_kernel(of_ref, ob_ref, z_ref, gd_ref, og_ref, gn_ref, w_ref, h_ref, mod_ref, out_ref):
    hw = C_HEADS * C_DK
    of, ob = of_ref[0, 0], ob_ref[0, 0]
    z, gd = z_ref[0], gd_ref[0]
    parts = []
    for hd in range(C_HEADS):
        sl = slice(hd * C_DK, (hd + 1) * C_DK)
        x = of[:, sl] + ob[:, sl]
        y = x * lax.rsqrt(jnp.mean(x * x, axis=-1, keepdims=True) + EPS) * og_ref[...]
        parts.append(y * _silu(z[:, sl]))
    for hd in range(D_HEADS):
        sl = slice(hd * D_DK, (hd + 1) * D_DK)
        y = 0.0
        for d, o in enumerate((of, ob)):
            x = o[:, hw + hd * D_DK: hw + (hd + 1) * D_DK]
            mu = jnp.mean(x, axis=-1, keepdims=True)
            xc = x - mu
            var = jnp.mean(xc * xc, axis=-1, keepdims=True)
            y = y + xc * lax.rsqrt(var + EPS) * gn_ref[:, sl] * _silu(gd[:, d * hw + hd * D_DK: d * hw + (hd + 1) * D_DK])
        parts.append(y)
    mix = jnp.concatenate(parts, axis=1)
    out_ref[0] = h_ref[0] + mod_ref[0, 0][2:3] * _bdot(mix, w_ref[...])


def _rec_out(o_scan, z, gd, out_g, gn_g, w_out, hh, mod, L):
    bsz, lt, d = hh.shape
    tm = ROW_TILE
    hw = C_HEADS * C_DK
    return pl.pallas_call(
        _rec_out_kernel,
        grid=(bsz, L // tm),
        in_specs=[pl.BlockSpec((1, 1, tm, 2 * hw), lambda b, i: (b, 0, i, 0)),
                  pl.BlockSpec((1, 1, tm, 2 * hw), lambda b, i: (b, 1, i, 0)),
                  pl.BlockSpec((1, tm, hw), lambda b, i: (b, i, 0)),
                  pl.BlockSpec((1, tm, 2 * hw), lambda b, i: (b, i, 0)),
                  pl.BlockSpec((1, C_DK), lambda b, i: (0, 0)),
                  pl.BlockSpec((1, hw), lambda b, i: (0, 0)),
                  pl.BlockSpec((2 * hw, d), lambda b, i: (0, 0)),
                  pl.BlockSpec((1, tm, d), lambda b, i: (b, i, 0)),
                  pl.BlockSpec((1, 1, 6, d), lambda b, i: (b, 0, 0, 0))],
        out_specs=pl.BlockSpec((1, tm, d), lambda b, i: (b, i, 0)),
        out_shape=jax.ShapeDtypeStruct(hh.shape, F32),
        input_output_aliases={7: 0},
        compiler_params=_cparams(("parallel", "parallel")),
        name="rec_out",
    )(o_scan, o_scan, z, gd, out_g.reshape(1, C_DK), gn_g.reshape(1, hw), w_out.astype(BF16), hh, mod)


def _att_layer(hh, mod, norm_g, w_in, q_g, k_g, sink, w_out, L):
    cos, sin = _rope_tables(L, hh.shape[1] - L, HEAD_DIM)
    qa, ka, va, qb, kb, vb = _att_project(hh, mod, norm_g, w_in, q_g, k_g, cos, sin, L)
    oa, ob = _attention(qa, ka, va, qb, kb, vb, sink, L)
    return _att_out(oa, ob, w_out, hh, mod, L)


def _rec_layer(hh, mod, norm_g, w_in, conv_w, a_log, dt_bias, out_g, gn_g, w_out, L):
    cos, sin = _rope_tables(L, hh.shape[1] - L, D_DK)
    qkv, z, qd, kd, vd, gd, gate = _rec_project(hh, mod, norm_g, w_in, a_log, dt_bias, cos, sin, L)
    qc, kc, vc = _rec_conv(qkv, conv_w, L)
    o_scan = _rec_scan(*_rec_intra(qc, kc, vc, qd, kd, vd, gate), L)
    return _rec_out(o_scan, z, gd, out_g, gn_g, w_out, hh, mod, L)


def kernel(x, c, ctx, c_ctx, mod_w, mod_b, norm1_g, norm2_g, att_w_in, att_q_norm, att_k_norm, att_sink, att_w_out,
           rec_w_in, rec_conv_w, rec_a_log, rec_dt_bias, rec_out_norm, rec_gn_g, rec_w_out,
           peer_w_q, peer_sub_keys, peer_u, peer_v, final_norm_g):
    L = x.shape[1]
    depth = mod_w.shape[0]
    mods = _modulation(c, c_ctx, mod_w, mod_b)
    hh = jnp.concatenate([x, ctx], axis=1)
    for layer in range(depth):
        last = layer == depth - 1
        i = layer // 2
        if layer % 2 == 0:
            hh = _att_layer(hh, mods[layer], norm1_g[layer], att_w_in[i], att_q_norm[i], att_k_norm[i], att_sink[i],
                            att_w_out[i], L)
        else:
            hh = _rec_layer(hh, mods[layer], norm1_g[layer], rec_w_in[i], rec_conv_w[i], rec_a_log[i], rec_dt_bias[i],
                            rec_out_norm[i], rec_gn_g[i], rec_w_out[i], L)
        hh = _peer(hh, mods[layer], norm2_g[layer], peer_w_q[layer], peer_sub_keys[layer], peer_u, peer_v, layer,
                   final_norm_g, L, with_ctx=not last, final=last)
    return hh
```

```python
import functools
import math

import numpy as np
import jax
import jax.numpy as jnp
from jax import lax
from jax.experimental import pallas as pl
from jax.experimental.pallas import tpu as pltpu

F32 = jnp.float32
BF16 = jnp.bfloat16

GRID_W = 64
EPS = 1e-6
HEAD_DIM = 64
A_HEADS = 8
A_KV = 2
B_HEADS = 8
B_KV = 2
WINDOW = 128
ROPE_THETA = 10000.0
C_HEADS = 4
C_DK = 128
CONV_W = 5
CHUNK = 64
D_HEADS = 4
D_DK = 128
RET_DECAY_BASE = 5.0
PEER_HEADS = 8
N_KEYS = 128
PEER_TOPK = 16

LANES = 128
BF16_ROWS = 16
VMEM_LIMIT = 56 * 1024 * 1024

ROW_TILE = 256
ATT_TQ = 128
PEER_TM = 256
PEER_EB = 2048
PEER_SUB = 256
PEER_TOK = 256

_NEG_INF = float("-inf")


def _cparams(sem, flags=None):
    return pltpu.CompilerParams(dimension_semantics=sem, vmem_limit_bytes=VMEM_LIMIT, flags=flags)


def _bdot(a, b):
    return jnp.dot(a.astype(BF16), b.astype(BF16), preferred_element_type=F32)


def _bdot_nt(a, b):
    return lax.dot_general(a.astype(BF16), b.astype(BF16), (((1,), (1,)), ((), ())), preferred_element_type=F32)


def _bdot_tn(a, b):
    return jnp.dot(a.T.astype(BF16), b.astype(BF16), preferred_element_type=F32)


def _split3(a):
    hi = a.astype(BF16)
    r1 = a - hi.astype(F32)
    mid = r1.astype(BF16)
    lo = (r1 - mid.astype(F32)).astype(BF16)
    return hi, mid, lo


def _dot3(a, b):
    a_hi, a_lo, _ = _split3(a)
    b_hi, b_lo, _ = _split3(b)
    return jnp.dot(jnp.concatenate([a_hi, a_lo, a_hi], axis=1), jnp.concatenate([b_hi, b_hi, b_lo], axis=0),
                   preferred_element_type=F32)


def _sigmoid(x):
    return 1.0 / (1.0 + jnp.exp(-x))


def _silu(x):
    return x * _sigmoid(x)


def _norm_mod(x, g, shift, scale):
    r = lax.rsqrt(jnp.mean(x * x, axis=-1, keepdims=True) + EPS)
    return (x * r * g) * (1.0 + scale) + shift


def _rope(x, cos, sin_signed, head_dim):
    quarter = head_dim // 4
    lane = lax.broadcasted_iota(jnp.int32, x.shape, 1)
    first = (lane % (2 * quarter)) < quarter
    partner = jnp.where(first, pltpu.roll(x, LANES - quarter, 1), pltpu.roll(x, quarter, 1))
    return x * cos + partner * sin_signed


def _mod_kernel(c_ref, w_ref, b_ref, o_ref):
    o_ref[0] = _bdot(_silu(c_ref[...]), w_ref[0]) + b_ref[0]


def _modulation(c, c_ctx, mod_w, mod_b):
    depth, d, n = mod_w.shape
    bsz = c.shape[0]
    rows = 16
    cc = jnp.zeros((rows, d), F32).at[:bsz].set(c).at[bsz].set(c_ctx)
    tn = 1536
    out = pl.pallas_call(
        _mod_kernel,
        grid=(depth, n // tn),
        in_specs=[pl.BlockSpec((rows, d), lambda l, j: (0, 0)),
                  pl.BlockSpec((1, d, tn), lambda l, j: (l, 0, j)),
                  pl.BlockSpec((1, 1, tn), lambda l, j: (l, 0, j))],
        out_specs=pl.BlockSpec((1, rows, tn), lambda l, j: (l, 0, j)),
        out_shape=jax.ShapeDtypeStruct((depth, rows, n), F32),
        compiler_params=_cparams(("arbitrary", "arbitrary")),
        name="modulation",
    )(cc, mod_w, mod_b.reshape(depth, 1, n))
    lat = out[:, :bsz].reshape(depth, bsz, 1, 6, d)
    ctx = jnp.broadcast_to(out[:, bsz].reshape(depth, 1, 1, 6, d), (depth, bsz, 1, 6, d))
    return jnp.concatenate([lat, ctx], axis=2)


def _rope_tables(L, LC, head_dim):
    quarter, half = head_dim // 4, head_dim // 2
    freqs = ROPE_THETA ** (-jnp.arange(quarter, dtype=F32) / quarter)
    lane = np.arange(LANES)
    within = lane % head_dim
    use_col = within >= half
    fidx = within % quarter
    sign = np.where((within % half) < quarter, -1.0, 1.0).astype(np.float32)
    t = jnp.arange(L, dtype=jnp.int32)
    row, col = (t // GRID_W).astype(F32), (t % GRID_W).astype(F32)
    pos = jnp.where(use_col[None, :], col[:, None], row[:, None])
    ang = pos * freqs[fidx][None, :]
    cos = jnp.concatenate([jnp.cos(ang), jnp.ones((LC, LANES), F32)], axis=0)
    sin = jnp.concatenate([jnp.sin(ang) * sign[None, :], jnp.zeros((LC, LANES), F32)], axis=0)
    return cos, sin


def _att_proj_kernel(h_ref, mod_ref, g_ref, w_ref, qg_ref, kg_ref, cos_ref, sin_ref, gm_ref,
                     qa_ref, ka_ref, va_ref, qb_ref, kb_ref, vb_ref):
    m = mod_ref[0, 0]
    a = _norm_mod(h_ref[0], g_ref[...], m[0:1], m[1:2])
    o = _bdot(a, w_ref[...])
    cos, sin = cos_ref[...], sin_ref[...]
    gm = gm_ref[...]

    def head_norm(x, gain):
        sq = x * x
        hi = sq.astype(BF16)
        lo = (sq - hi.astype(F32)).astype(BF16)
        ms = jnp.dot(hi, gm, preferred_element_type=F32) + jnp.dot(lo, gm, preferred_element_type=F32)
        return x * lax.rsqrt(ms + EPS) * gain

    def put(ref, first_head, x):
        ref[0, first_head] = x[:, :HEAD_DIM].astype(ref.dtype)
        ref[0, first_head + 1] = x[:, HEAD_DIM:].astype(ref.dtype)

    scale = HEAD_DIM ** -0.5
    qa_w = A_HEADS * HEAD_DIM
    kv_w = A_KV * HEAD_DIM
    off = 0
    for c in range(qa_w // LANES):
        x = o[:, off + c * LANES: off + (c + 1) * LANES]
        put(qa_ref, 2 * c, _rope(head_norm(x, qg_ref[...]), cos, sin, HEAD_DIM) * scale)
    off += qa_w
    put(ka_ref, 0, _rope(head_norm(o[:, off: off + kv_w], kg_ref[...]), cos, sin, HEAD_DIM))
    off += kv_w
    put(va_ref, 0, o[:, off: off + kv_w])
    off += kv_w
    for c in range(qa_w // LANES):
        x = o[:, off + c * LANES: off + (c + 1) * LANES]
        put(qb_ref, 2 * c, _rope(x, cos, sin, HEAD_DIM) * scale)
    off += qa_w
    put(kb_ref, 0, _rope(o[:, off: off + kv_w], cos, sin, HEAD_DIM))
    off += kv_w
    put(vb_ref, 0, o[:, off: off + kv_w])


def _att_project(hh, mod, norm_g, w_in, q_g, k_g, cos, sin, L):
    bsz, lt, d = hh.shape
    tm = ROW_TILE
    n = w_in.shape[1]
    gm = jnp.asarray(np.kron(np.eye(LANES // HEAD_DIM), np.full((HEAD_DIM, HEAD_DIM), 1.0 / HEAD_DIM)), BF16)
    tile2 = lambda v: jnp.tile(v.reshape(1, HEAD_DIM), (1, LANES // HEAD_DIM))
    qshape = jax.ShapeDtypeStruct((bsz, A_HEADS, lt, HEAD_DIM), BF16)
    kshape = jax.ShapeDtypeStruct((bsz, A_KV, lt, HEAD_DIM), BF16)
    qspec = pl.BlockSpec((1, A_HEADS, tm, HEAD_DIM), lambda b, i: (b, 0, i, 0))
    kspec = pl.BlockSpec((1, A_KV, tm, HEAD_DIM), lambda b, i: (b, 0, i, 0))
    const = lambda shape: pl.BlockSpec(shape, lambda b, i: (0,) * len(shape))
    return pl.pallas_call(
        _att_proj_kernel,
        grid=(bsz, lt // tm),
        in_specs=[pl.BlockSpec((1, tm, d), lambda b, i: (b, i, 0)),
                  pl.BlockSpec((1, 1, 6, d), lambda b, i: (b, i // (L // tm), 0, 0)),
                  const((1, d)), const((d, n)), const((1, LANES)), const((1, LANES)),
                  pl.BlockSpec((tm, LANES), lambda b, i: (i, 0)),
                  pl.BlockSpec((tm, LANES), lambda b, i: (i, 0)),
                  const((LANES, LANES))],
        out_specs=[qspec, kspec, kspec, qspec, kspec, kspec],
        out_shape=[qshape, kshape, kshape, qshape, kshape, kshape],
        compiler_params=_cparams(("parallel", "parallel")),
        name="att_project",
    )(hh, mod, norm_g.reshape(1, d), w_in.astype(BF16), tile2(q_g), tile2(k_g), cos, sin, gm)


def _softmax_pv(score_parts, value_parts, extra_logit=None):
    m = functools.reduce(jnp.maximum, [jnp.max(s, axis=-1, keepdims=True) for s in score_parts])
    if extra_logit is not None:
        m = jnp.maximum(m, extra_logit)
    l = 0.0
    acc = 0.0
    for s, v in zip(score_parts, value_parts):
        p = jnp.exp(s - m)
        l = l + jnp.sum(p, axis=-1, keepdims=True)
        acc = acc + jnp.dot(p.astype(BF16), v, preferred_element_type=F32)
    if extra_logit is not None:
        l = l + jnp.exp(extra_logit - m)
    return acc / l


def _att_kernel(sink_ref, qa_ref, ka_ref, va_ref, qb_ref, kb_ref, vb_ref, oa_ref, ob_ref, *, L, LC):
    i = pl.program_id(1)
    tq = ATT_TQ
    group = A_HEADS // A_KV
    band = tq + 2 * WINDOW

    def stacked_q(ref, kvh):
        return ref[0, kvh * group:(kvh + 1) * group].reshape(group * tq, HEAD_DIM)

    def put(ref, kvh, o):
        for g in range(group):
            hd = kvh * group + g
            ref[0, :, hd * HEAD_DIM:(hd + 1) * HEAD_DIM] = o[g * tq:(g + 1) * tq].astype(ref.dtype)

    def sink_col(kvh):
        return jnp.concatenate([jnp.full((tq, 1), sink_ref[kvh * group + g], F32) for g in range(group)], axis=0)

    @pl.when(i < L // tq)
    def _latent():
        for kvh in range(A_KV):
            q = stacked_q(qa_ref, kvh)
            put(oa_ref, kvh, _softmax_pv([_bdot_nt(q, ka_ref[0, kvh])], [va_ref[0, kvh]]))
            q = stacked_q(qb_ref, kvh)
            start = pl.multiple_of(jnp.clip((i - 1) * tq, 0, L - band), tq)
            s_band = _bdot_nt(q, kb_ref[0, kvh, pl.ds(start, band), :])
            qpos = i * tq + (lax.broadcasted_iota(jnp.int32, s_band.shape, 0) % tq)
            kpos = start + lax.broadcasted_iota(jnp.int32, s_band.shape, 1)
            s_band = jnp.where(jnp.abs(qpos - kpos) <= WINDOW, s_band, _NEG_INF)
            s_ctx = _bdot_nt(q, kb_ref[0, kvh, L:L + LC, :])
            put(ob_ref, kvh, _softmax_pv([s_band, s_ctx],
                                         [vb_ref[0, kvh, pl.ds(start, band), :], vb_ref[0, kvh, L:L + LC, :]],
                                         sink_col(kvh)))

    @pl.when(i >= L // tq)
    def _context():
        for kvh in range(A_KV):
            q = stacked_q(qa_ref, kvh)
            put(oa_ref, kvh, _softmax_pv([_bdot_nt(q, ka_ref[0, kvh, L:L + LC, :])], [va_ref[0, kvh, L:L + LC, :]]))
            q = stacked_q(qb_ref, kvh)
            put(ob_ref, kvh, _softmax_pv([_bdot_nt(q, kb_ref[0, kvh, L:L + LC, :])], [vb_ref[0, kvh, L:L + LC, :]],
                                         sink_col(kvh)))


def _attention(qa, ka, va, qb, kb, vb, sink, L):
    bsz, _, lt, _ = qa.shape
    tq = ATT_TQ
    qspec = pl.BlockSpec((1, A_HEADS, tq, HEAD_DIM), lambda b, i: (b, 0, i, 0))
    kspec = pl.BlockSpec((1, A_KV, lt, HEAD_DIM), lambda b, i: (b, 0, 0, 0))
    ospec = pl.BlockSpec((1, tq, A_HEADS * HEAD_DIM), lambda b, i: (b, i, 0))
    oshape = jax.ShapeDtypeStruct((bsz, lt, A_HEADS * HEAD_DIM), BF16)
    return pl.pallas_call(
        functools.partial(_att_kernel, L=L, LC=lt - L),
        grid=(bsz, lt // tq),
        in_specs=[pl.BlockSpec(memory_space=pltpu.SMEM), qspec, kspec, kspec, qspec, kspec, kspec],
        out_specs=[ospec, ospec],
        out_shape=[oshape, oshape],
        compiler_params=_cparams(("parallel", "parallel")),
        name="attention",
    )(sink, qa, ka, va, qb, kb, vb)


def _att_out_kernel(oa_ref, ob_ref, w_ref, h_ref, mod_ref, out_ref):
    half = oa_ref.shape[-1]
    y = (jnp.dot(oa_ref[0], w_ref[:half, :], preferred_element_type=F32)
         + jnp.dot(ob_ref[0], w_ref[half:, :], preferred_element_type=F32))
    out_ref[0] = h_ref[0] + mod_ref[0, 0][2:3] * y


def _att_out(oa, ob, w_out, hh, mod, L):
    bsz, lt, d = hh.shape
    tm = ROW_TILE
    half = oa.shape[-1]
    return pl.pallas_call(
        _att_out_kernel,
        grid=(bsz, lt // tm),
        in_specs=[pl.BlockSpec((1, tm, half), lambda b, i: (b, i, 0)),
                  pl.BlockSpec((1, tm, half), lambda b, i: (b, i, 0)),
                  pl.BlockSpec((2 * half, d), lambda b, i: (0, 0)),
                  pl.BlockSpec((1, tm, d), lambda b, i: (b, i, 0)),
                  pl.BlockSpec((1, 1, 6, d), lambda b, i: (b, i // (L // tm), 0, 0))],
        out_specs=pl.BlockSpec((1, tm, d), lambda b, i: (b, i, 0)),
        out_shape=jax.ShapeDtypeStruct(hh.shape, F32),
        input_output_aliases={3: 0},
        compiler_params=_cparams(("parallel", "parallel")),
        name="att_out",
    )(oa, ob, w_out.astype(BF16), hh, mod)


def _sort_network(n):
    pairs = []
    p = 1
    while p < n:
        k = p
        while k >= 1:
            for j in range(k % p, n - k, 2 * k):
                for i in range(min(k, n - j - k)):
                    if (i + j) // (2 * p) == (i + j + k) // (2 * p):
                        pairs.append((i + j, i + j + k))
            k //= 2
        p *= 2
    return pairs


def _pop_sorted(lists, n, singles=()):
    lists, singles = list(lists), list(singles)
    out = []
    for rnd in range(n):
        head = functools.reduce(jnp.maximum, [lists[0]] + singles)
        m = jnp.max(head, axis=0, keepdims=True)
        out.append(m)
        if rnd == n - 1:
            break
        hit = lists[0] == m
        depth = min(len(lists), n - rnd - 1)
        for d in range(depth):
            nxt = lists[d + 1] if d + 1 < len(lists) else _NEG_INF
            lists[d] = jnp.where(hit, nxt, lists[d])
        singles = [jnp.where(s == m, _NEG_INF, s) for s in singles]
    return out


def _top_values(x, n):
    slabs = [x[r * 8:(r + 1) * 8] for r in range(x.shape[0] // 8)]
    for i, j in _sort_network(len(slabs)):
        slabs[i], slabs[j] = jnp.maximum(slabs[i], slabs[j]), jnp.minimum(slabs[i], slabs[j])
    return _pop_sorted(slabs, n)


def _peer_route_kernel(h_ref, mod_ref, g_ref, wq_ref, keys_ref, at_ref, thr_ref, e0_ref, r1_ref, e1_ref, q_scr):
    m = mod_ref[0, 0]
    a = _norm_mod(h_ref[0], g_ref[...], m[3:4], m[4:5])
    at_ref[0] = a.T.astype(BF16)
    q_scr[...] = jnp.dot(a.astype(BF16), wq_ref[...], preferred_element_type=F32)
    k = PEER_TOPK
    n = k + 1

    def head(hd, carry):
        col = pl.multiple_of(hd * 2 * N_KEYS, 2 * N_KEYS)
        s0 = _bdot_nt(keys_ref[0], q_scr[:, pl.ds(col, N_KEYS)])
        s1 = _bdot_nt(keys_ref[1], q_scr[:, pl.ds(col + N_KEYS, N_KEYS)])
        tokens = s0.shape[1]
        top0 = _top_values(s0, n)
        top1 = _top_values(s1, n)
        first = jnp.concatenate(top0 + [jnp.full((24 - n, tokens), _NEG_INF, F32)], axis=0)
        best = _pop_sorted([first[0:8] + t for t in top1], n, [first[8:16] + top1[0], first[16:24] + top1[0]])
        z = functools.reduce(lambda x, y: x + y, [jnp.exp(b - best[0]) for b in best[:k]])
        tau = 0.5 * (best[k - 1] + best[k])
        r1 = jnp.zeros_like(s1)
        need = jnp.full_like(s0, n + 1.0)
        for b in range(k):
            r1 = jnp.where(s1 >= top1[k - 1 - b], b + 2.0, r1)
            need = jnp.where(s0 >= tau - top1[b], float(n - b), need)
        thr_ref[0, hd] = need
        e0_ref[0, hd] = jnp.exp(s0 - top0[0])
        r1_ref[0, hd] = pltpu.bitcast(r1.astype(BF16), jnp.uint32)
        e1_ref[0, hd] = pltpu.bitcast((jnp.exp(s1 - top1[0]) / z).astype(BF16), jnp.uint32)
        return carry

    lax.fori_loop(0, PEER_HEADS, head, 0, unroll=True)


def _peer_route(hh, mod, norm_g, w_q, sub_keys, L):
    bsz, lt, d = hh.shape
    tm = ROW_TILE
    nq = w_q.shape[1]
    tok = lambda: pl.BlockSpec((1, PEER_HEADS, N_KEYS, tm), lambda b, i: (b, 0, 0, i))
    tshape = jax.ShapeDtypeStruct((bsz, PEER_HEADS, N_KEYS, lt), F32)
    pshape = jax.ShapeDtypeStruct((bsz, PEER_HEADS, N_KEYS // 2, lt), jnp.uint32)
    ptok = pl.BlockSpec((1, PEER_HEADS, N_KEYS // 2, tm), lambda b, i: (b, 0, 0, i))
    return pl.pallas_call(
        _peer_route_kernel,
        grid=(bsz, lt // tm),
        in_specs=[pl.BlockSpec((1, tm, d), lambda b, i: (b, i, 0)),
                  pl.BlockSpec((1, 1, 6, d), lambda b, i: (b, i // (L // tm), 0, 0)),
                  pl.BlockSpec((1, d), lambda b, i: (0, 0)),
                  pl.BlockSpec((d, nq), lambda b, i: (0, 0)),
                  pl.BlockSpec((2, N_KEYS, N_KEYS), lambda b, i: (0, 0, 0))],
        out_specs=[pl.BlockSpec((1, d, tm), lambda b, i: (b, 0, i)), tok(), tok(), ptok, ptok],
        out_shape=[jax.ShapeDtypeStruct((bsz, d, lt), BF16), tshape, tshape, pshape, pshape],
        scratch_shapes=[pltpu.VMEM((tm, nq), F32)],
        compiler_params=_cparams(("parallel", "parallel")),
        name="peer_route",
    )(hh, mod, norm_g.reshape(1, d), w_q.astype(BF16), sub_keys.astype(BF16))


def _gelu(x):
    return 0.5 * x * (1.0 + lax.erf(x * (1.0 / math.sqrt(2.0))))


def _peer_expert_kernel(a_ref, an_ref, thr_ref, e0_ref, r1_ref, e1_ref, u_ref, vt_ref, h_ref, mod_ref, fg_ref,
                        out_ref, yt_scr, wa_scr, at0_scr, at1_scr, *, final):
    e = pl.program_id(1)
    t = pl.program_id(2)
    tm = a_ref.shape[2]
    n_sub = PEER_EB // PEER_SUB
    rows = [slice(q * PEER_SUB, (q + 1) * PEER_SUB) for q in range(n_sub)]

    def activations(dst, src, q):
        words = slice(rows[q].start // 2, rows[q].stop // 2)
        dst[rows[q], :] = jnp.dot(pltpu.bitcast(u_ref[words, :], BF16), src[0], preferred_element_type=F32)

    @pl.when(e == 0)
    def _():
        yt_scr[t] = jnp.zeros(yt_scr.shape[1:], F32)

    @pl.when(t == 0)
    def _():
        for q in range(n_sub):
            activations(at0_scr, a_ref, q)

    def step(act_ref, next_ref):
        y = None
        for q in range(n_sub):
            activations(next_ref, an_ref, q)
            for il in range(PEER_SUB // N_KEYS):
                key0 = q * (PEER_SUB // N_KEYS) + il
                for c in range(tm // LANES):
                    cs = slice(c * LANES, (c + 1) * LANES)
                    row = lambda ref, hd: jnp.broadcast_to(ref[0, hd, key0:key0 + 1, cs], (BF16_ROWS, LANES)).astype(BF16)
                    thr = [row(thr_ref, hd) for hd in range(PEER_HEADS)]
                    e0 = [row(e0_ref, hd) for hd in range(PEER_HEADS)]
                    zero = jnp.zeros((BF16_ROWS, LANES), BF16)
                    for r in range(N_KEYS // BF16_ROWS):
                        ws = slice(r * BF16_ROWS // 2, (r + 1) * BF16_ROWS // 2)
                        gate = None
                        for hd in range(PEER_HEADS):
                            rank = pltpu.bitcast(r1_ref[0, hd, ws, cs], BF16)
                            weight = pltpu.bitcast(e1_ref[0, hd, ws, cs], BF16)
                            part = jnp.where(rank >= thr[hd], weight * e0[hd], zero)
                            gate = part if gate is None else gate + part
                        ars = slice(rows[q].start + il * N_KEYS + r * BF16_ROWS,
                                    rows[q].start + il * N_KEYS + (r + 1) * BF16_ROWS)
                        wa_scr[ars, cs] = gate * _gelu(act_ref[ars, cs]).astype(BF16)
            part = jnp.dot(pltpu.bitcast(vt_ref[:, rows[q]], BF16), wa_scr[rows[q], :], preferred_element_type=F32)
            y = part if y is None else y + part
        yt_scr[t] += y

    @pl.when(t % 2 == 0)
    def _():
        step(at0_scr, at1_scr)

    @pl.when(t % 2 == 1)
    def _():
        step(at1_scr, at0_scr)

    @pl.when(e == pl.num_programs(1) - 1)
    def _():
        hn = h_ref[0] + mod_ref[0, 0][5:6] * yt_scr[t].T
        if final:
            hn = hn * lax.rsqrt(jnp.mean(hn * hn, axis=-1, keepdims=True) + EPS) * fg_ref[...]
        out_ref[0] = hn


def _peer_expert(hh, mod, a, thr, e0, r1, e1, u_bf, vt_bf, final_g, L, *, with_ctx, final):
    bsz, lt, d = hh.shape
    n_exp = vt_bf.shape[1]
    tm = PEER_TM
    n_tiles = (lt if with_ctx else L) // tm
    keys_per_step = PEER_EB // N_KEYS
    n_blocks = n_exp // PEER_EB
    tok = lambda r: pl.BlockSpec((1, PEER_HEADS, r, tm), (lambda b, e, t: (b, 0, e, t)) if r == keys_per_step
                                 else (lambda b, e, t: (b, 0, 0, t)))
    row_spec = pl.BlockSpec((1, tm, d), lambda b, e, t: (b, jnp.where(e == n_blocks - 1, t, 0), 0))
    if final:
        out_shape = jax.ShapeDtypeStruct((bsz, L, d), F32)
        aliases = {}
    else:
        out_shape = jax.ShapeDtypeStruct(hh.shape, F32)
        aliases = {8: 0}
    return pl.pallas_call(
        functools.partial(_peer_expert_kernel, final=final),
        grid=(bsz, n_blocks, n_tiles),
        in_specs=[pl.BlockSpec((1, d, tm), lambda b, e, t: (b, 0, t)),
                  pl.BlockSpec((1, d, tm), lambda b, e, t: (b, 0, jnp.minimum(t + 1, n_tiles - 1))),
                  tok(keys_per_step), tok(keys_per_step), tok(N_KEYS // 2), tok(N_KEYS // 2),
                  pl.BlockSpec((PEER_EB // 2, d), lambda b, e, t: (e, 0)),
                  pl.BlockSpec((d // 2, PEER_EB), lambda b, e, t: (0, e)),
                  row_spec,
                  pl.BlockSpec((1, 1, 6, d), lambda b, e, t: (b, t // (L // tm), 0, 0)),
                  pl.BlockSpec((1, d), lambda b, e, t: (0, 0))],
        out_specs=row_spec,
        out_shape=out_shape,
        input_output_aliases=aliases,
        scratch_shapes=[pltpu.VMEM((n_tiles, d, tm), F32), pltpu.VMEM((PEER_EB, tm), BF16),
                        pltpu.VMEM((PEER_EB, tm), F32), pltpu.VMEM((PEER_EB, tm), F32)],
        compiler_params=_cparams(("parallel", "arbitrary", "arbitrary")),
        name="peer_expert",
    )(a, a, thr, e0, r1, e1, u_bf, vt_bf, hh, mod, final_g.reshape(1, d))


def _pack_kernel(x_ref, o_ref, *, transpose):
    x = x_ref[0]
    if transpose:
        x = x.T
    o_ref[...] = pltpu.bitcast(x.astype(BF16), jnp.uint32)


def _pack_row_pairs(x, layer, *, transpose):
    _, r, c = x.shape
    blk = 512
    if transpose:
        out_shape, out_spec = (c // 2, r), pl.BlockSpec((c // 2, blk), lambda i: (0, i))
    else:
        out_shape, out_spec = (r // 2, c), pl.BlockSpec((blk // 2, c), lambda i: (i, 0))
    return pl.pallas_call(
        functools.partial(_pack_kernel, transpose=transpose),
        grid=(r // blk,),
        in_specs=[pl.BlockSpec((1, blk, c), lambda i: (layer, i, 0))],
        out_specs=out_spec,
        out_shape=jax.ShapeDtypeStruct(out_shape, jnp.uint32),
        compiler_params=_cparams(("parallel",)),
        name="pack_pairs_t" if transpose else "pack_pairs",
    )(x)


def _peer(hh, mod, norm_g, w_q, sub_keys, u_all, v_all, layer, final_g, L, *, with_ctx, final):
    a, thr, e0, r1, e1 = _peer_route(hh, mod, norm_g, w_q, sub_keys, L)
    return _peer_expert(hh, mod, a, thr, e0, r1, e1, _pack_row_pairs(u_all, layer, transpose=False),
                        _pack_row_pairs(v_all, layer, transpose=True), final_g, L, with_ctx=with_ctx, final=final)


REC_SCAN_CHUNKS = 4
REC_CHUNKS_PER_STEP = 2
REC_QKV = C_HEADS * 3 * C_DK
REC_Z = C_HEADS * C_DK
REC_GATES = 4 * C_HEADS
REC_HD = D_HEADS * D_DK


def _rec_proj_kernel(h_ref, mod_ref, g_ref, w_ref, alog_ref, dtb_ref, cos_ref, sin_ref,
                     qkv_ref, z_ref, qd_ref, kd_ref, vd_ref, gd_ref, gate_ref):
    m = mod_ref[0, 0]
    a = _norm_mod(h_ref[0], g_ref[...], m[0:1], m[1:2])
    o = _bdot(a, w_ref[...])
    cos, sin = cos_ref[...], sin_ref[...]
    off = 0
    qkv_ref[0] = o[:, off:off + REC_QKV]
    off += REC_QKV
    z_ref[0] = o[:, off:off + REC_Z]
    off += REC_Z
    for hd in range(D_HEADS):
        qd_ref[0, :, hd * D_DK:(hd + 1) * D_DK] = _rope(o[:, off + hd * D_DK: off + (hd + 1) * D_DK], cos, sin, D_DK)
    off += REC_HD
    for hd in range(D_HEADS):
        kd_ref[0, :, hd * D_DK:(hd + 1) * D_DK] = (
            _rope(o[:, off + hd * D_DK: off + (hd + 1) * D_DK], cos, sin, D_DK) * (D_DK ** -0.5))
    off += REC_HD
    vd_ref[0] = o[:, off:off + REC_HD]
    off += REC_HD
    gd_ref[0] = o[:, off:off + 2 * REC_HD]
    off += 2 * REC_HD
    x = o[:, off:off + LANES]
    lane = lax.broadcasted_iota(jnp.int32, x.shape, 1)
    xb = x + dtb_ref[...]
    softplus = jnp.maximum(xb, 0.0) + jnp.log(1.0 + jnp.exp(-jnp.abs(xb)))
    gate_ref[0] = jnp.where(lane < 2 * C_HEADS, -jnp.exp(alog_ref[...]) * softplus, _sigmoid(x))


def _rec_project(hh, mod, norm_g, w_in, a_log, dt_bias, cos, sin, L):
    bsz, lt, d = hh.shape
    tm = ROW_TILE
    parts = np.cumsum([REC_QKV, REC_Z, REC_GATES, REC_HD, REC_HD, REC_HD])
    qkv_w, z_w, gates_w, qd_w, kd_w, vd_w, gd_w = jnp.split(w_in, [int(p) for p in parts], axis=1)
    w = jnp.concatenate([qkv_w, z_w, qd_w, kd_w, vd_w, gd_w, gates_w, jnp.zeros((d, LANES - REC_GATES), F32)],
                        axis=1).astype(BF16)
    n = w.shape[1]
    pad = lambda p: jnp.zeros((1, LANES), F32).at[0, :2 * C_HEADS].set(p.reshape(-1))
    widths = [REC_QKV, REC_Z, REC_HD, REC_HD, REC_HD, 2 * REC_HD, LANES]
    const = lambda shape: pl.BlockSpec(shape, lambda b, i: (0,) * len(shape))
    return pl.pallas_call(
        _rec_proj_kernel,
        grid=(bsz, lt // tm),
        in_specs=[pl.BlockSpec((1, tm, d), lambda b, i: (b, i, 0)),
                  pl.BlockSpec((1, 1, 6, d), lambda b, i: (b, i // (L // tm), 0, 0)),
                  const((1, d)), const((d, n)), const((1, LANES)), const((1, LANES)),
                  pl.BlockSpec((tm, LANES), lambda b, i: (i, 0)),
                  pl.BlockSpec((tm, LANES), lambda b, i: (i, 0))],
        out_specs=[pl.BlockSpec((1, tm, wd), lambda b, i: (b, i, 0)) for wd in widths],
        out_shape=[jax.ShapeDtypeStruct((bsz, lt, wd), F32) for wd in widths],
        compiler_params=_cparams(("parallel", "parallel")),
        name="rec_project",
    )(hh, mod, norm_g.reshape(1, d), w, pad(a_log), pad(dt_bias), cos, sin)


def _rec_conv_kernel(x_ref, prev_ref, next_ref, w_ref, q_ref, k_ref, v_ref, *, L):
    i = pl.program_id(1)
    tl = x_ref.shape[1]
    n_lat = L // tl
    at_start = (i == 0) | (i == n_lat)
    at_end = (i == n_lat - 1) | (i == pl.num_programs(1) - 1)
    prev = jnp.where(at_start, 0.0, prev_ref[0])
    nxt = jnp.where(at_end, 0.0, next_ref[0])
    xx = jnp.concatenate([prev, x_ref[0], nxt], axis=0)
    n = tl + 16
    acc = 0.0
    for tap in range(CONV_W):
        shift = (CONV_W // 2 - tap) % n
        shifted = xx if shift == 0 else pltpu.roll(xx, shift, 0)
        acc = acc + shifted[8:8 + tl] * w_ref[tap:tap + 1, :]
    y = _silu(acc)
    hw = C_HEADS * C_DK
    for hd in range(C_HEADS):
        def l2(x):
            return x * lax.rsqrt(jnp.sum(x * x, axis=-1, keepdims=True) + EPS)
        sl = slice(hd * C_DK, (hd + 1) * C_DK)
        q_ref[0, :, sl] = l2(y[:, hd * C_DK:(hd + 1) * C_DK]) * (C_DK ** -0.5)
        k_ref[0, :, sl] = l2(y[:, hw + hd * C_DK: hw + (hd + 1) * C_DK])
    v_ref[0] = y[:, 2 * hw:]


def _rec_conv(qkv, conv_w, L):
    bsz, lt, ch = qkv.shape
    tl = ROW_TILE
    hb = tl // 8
    last = lt // 8 - 1
    hw = C_HEADS * C_DK
    return pl.pallas_call(
        functools.partial(_rec_conv_kernel, L=L),
        grid=(bsz, lt // tl),
        in_specs=[pl.BlockSpec((1, tl, ch), lambda b, i: (b, i, 0)),
                  pl.BlockSpec((1, 8, ch), lambda b, i: (b, jnp.maximum(i * hb - 1, 0), 0)),
                  pl.BlockSpec((1, 8, ch), lambda b, i: (b, jnp.minimum((i + 1) * hb, last), 0)),
                  pl.BlockSpec((8, ch), lambda b, i: (0, 0))],
        out_specs=[pl.BlockSpec((1, tl, hw), lambda b, i: (b, i, 0))] * 3,
        out_shape=[jax.ShapeDtypeStruct((bsz, lt, hw), F32)] * 3,
        compiler_params=_cparams(("parallel", "parallel")),
        name="rec_conv",
    )(qkv, qkv, qkv, jnp.zeros((8, ch), F32).at[:CONV_W].set(conv_w))


def _ret_log_gamma(hd):
    return float(np.log1p(-np.exp2(-(RET_DECAY_BASE + hd))))


def _rec_intra_kernel(qc_ref, kc_ref, vc_ref, qd_ref, kd_ref, vd_ref, gate_ref,
                      eq_ref, qk_ref, kt_ref, sol_ref, vr_ref, al_ref):
    c = CHUNK
    ci = lax.broadcasted_iota(jnp.int32, (c, c), 0)
    si = lax.broadcasted_iota(jnp.int32, (c, c), 1)
    pos = lax.broadcasted_iota(jnp.int32, (c, 1), 0).astype(F32)
    ones_row = jnp.ones((1, LANES), F32)
    combos = []
    for cc in range(REC_CHUNKS_PER_STEP):
        _rec_intra_setup(cc, combos, ci, si, pos, ones_row, qc_ref, kc_ref, vc_ref, qd_ref, kd_ref, vd_ref, gate_ref,
                         eq_ref, qk_ref, kt_ref, vr_ref, al_ref)
    _rec_intra_solve(combos, ones_row, eq_ref, qk_ref, kt_ref, sol_ref, al_ref)


def _rec_intra_setup(cc, combos, ci, si, pos, ones_row, qc_ref, kc_ref, vc_ref, qd_ref, kd_ref, vd_ref, gate_ref,
                     eq_ref, qk_ref, kt_ref, vr_ref, al_ref):
    c = CHUNK
    rs = slice(cc * c, (cc + 1) * c)
    gates = gate_ref[0, rs]
    la_parts = _split3(gates)
    for d in range(2):
        incl = (si <= ci) if d == 0 else (si >= ci)
        strict = (si < ci) if d == 0 else (si > ci)
        tri = incl.astype(BF16)
        g_all = jnp.dot(jnp.concatenate([tri] * 3, axis=1), jnp.concatenate(la_parts, axis=0),
                        preferred_element_type=F32)
        g_all_t = g_all.T
        last = c - 1 if d == 0 else 0
        for hd in range(C_HEADS):
            col = d * C_HEADS + hd
            sl = slice(hd * C_DK, (hd + 1) * C_DK)
            q, k, v = qc_ref[0, rs, sl], kc_ref[0, rs, sl], vc_ref[0, rs, sl]
            g_rows = jnp.broadcast_to(g_all[:, col:col + 1], (c, C_DK))
            beta = jnp.broadcast_to(gates[:, 2 * C_HEADS + col: 2 * C_HEADS + col + 1], (c, C_DK))
            grow = g_all_t[col:col + 1, :]
            glast = g_all[last:last + 1, col:col + 1]
            exp_g = jnp.exp(g_rows)
            diff = g_rows[:, :c] - grow
            dec_strict = jnp.exp(jnp.where(strict, diff, _NEG_INF))
            combos.append(dict(
                cc=cc, d=d, hd=hd, q=q, exp_g=exp_g, glast=glast,
                x=-(beta[:, :c] * _bdot_nt(k, k) * dec_strict),
                sol=jnp.concatenate([beta * v, (beta * exp_g) * k], axis=1),
                qk=_bdot_nt(q, k) * jnp.exp(jnp.where(incl, diff, _NEG_INF)),
                kend=k * jnp.exp(glast - g_rows)))
    for d in range(2):
        incl = (si <= ci) if d == 0 else (si >= ci)
        for hd in range(D_HEADS):
            lg = _ret_log_gamma(hd)
            sl = slice(hd * D_DK, (hd + 1) * D_DK)
            q, k, v = qd_ref[0, rs, sl], kd_ref[0, rs, sl], vd_ref[0, rs, sl]
            steps = pos if d == 0 else (c - 1.0) - pos
            dist = (ci - si) if d == 0 else (si - ci)
            dmat = jnp.exp(jnp.where(incl, lg * dist.astype(F32), _NEG_INF))
            qk = _bdot_nt(q, k) * dmat
            eq_ref[0, cc, d, C_HEADS + hd] = (q * jnp.exp(lg * (steps + 1.0))).astype(BF16)
            qk_ref[0, cc, d, C_HEADS + hd] = qk.astype(BF16)
            kt_ref[0, cc, d, C_HEADS + hd] = (k * jnp.exp(lg * ((c - 1.0) - steps))).T.astype(BF16)
            vr_ref[0, cc, d, hd] = v.astype(BF16)
            al_ref[0, cc, d, C_HEADS + hd:C_HEADS + hd + 1, :] = math.exp(lg * c) * ones_row


def _rec_intra_solve(combos, ones_row, eq_ref, qk_ref, kt_ref, sol_ref, al_ref):
    c = CHUNK
    for cb in combos:
        cc, d, hd = cb["cc"], cb["d"], cb["hd"]
        eq_ref[0, cc, d, hd] = (cb["exp_g"] * cb["q"]).astype(BF16)
        qk_ref[0, cc, d, hd] = cb["qk"].astype(BF16)
        kt_ref[0, cc, d, hd] = cb["kend"].T.astype(BF16)
        al_ref[0, cc, d, hd:hd + 1, :] = jnp.exp(cb["glast"]) * ones_row
    levels = int(math.log2(c))
    for lvl in range(levels):
        for cb in combos:
            cb["sol"] = cb["sol"] + _dot3(cb["x"], cb["sol"])
        if lvl < levels - 1:
            for cb in combos:
                cb["x"] = _dot3(cb["x"], cb["x"])
    for cb in combos:
        sol_ref[0, cb["cc"], cb["d"], cb["hd"]] = cb["sol"].astype(BF16)


def _rec_intra(qc, kc, vc, qd, kd, vd, gate):
    bsz, lt, hw = qc.shape
    nc = lt // CHUNK
    nh = C_HEADS + D_HEADS
    per = REC_CHUNKS_PER_STEP
    row = lambda w: pl.BlockSpec((1, per * CHUNK, w), lambda b, n: (b, n, 0))
    lead = lambda *tail: pl.BlockSpec((1, per, 2) + tail, lambda b, n: (b, n, 0) + (0,) * len(tail))
    return pl.pallas_call(
        _rec_intra_kernel,
        grid=(bsz, nc // per),
        in_specs=[row(hw)] * 6 + [row(LANES)],
        out_specs=[lead(nh, CHUNK, C_DK), lead(nh, CHUNK, CHUNK), lead(nh, C_DK, CHUNK), lead(C_HEADS, CHUNK, 2 * C_DK),
                   lead(D_HEADS, CHUNK, C_DK), lead(nh, LANES)],
        out_shape=[jax.ShapeDtypeStruct((bsz, nc, 2, nh, CHUNK, C_DK), BF16),
                   jax.ShapeDtypeStruct((bsz, nc, 2, nh, CHUNK, CHUNK), BF16),
                   jax.ShapeDtypeStruct((bsz, nc, 2, nh, C_DK, CHUNK), BF16),
                   jax.ShapeDtypeStruct((bsz, nc, 2, C_HEADS, CHUNK, 2 * C_DK), BF16),
                   jax.ShapeDtypeStruct((bsz, nc, 2, D_HEADS, CHUNK, C_DK), BF16),
                   jax.ShapeDtypeStruct((bsz, nc, 2, nh, LANES), F32)],
        compiler_params=_cparams(("parallel", "parallel")),
        name="rec_intra",
    )(qc, kc, vc, qd, kd, vd, gate)


def _rec_scan_kernel(eq_ref, qk_ref, kt_ref, sol_ref, vr_ref, al_ref, o_ref, s_scr):
    @pl.when(pl.program_id(2) == 0)
    def _():
        s_scr[...] = jnp.zeros_like(s_scr)

    dot = functools.partial(jnp.dot, preferred_element_type=F32)
    heads = range(C_HEADS + D_HEADS)

    def run(order):
        s = [s_scr[hd] for hd in heads]
        for cc in order:
            sb = [x.astype(BF16) for x in s]
            wks = [dot(sol_ref[0, cc, 0, hd, :, C_DK:], sb[hd]) for hd in range(C_HEADS)]
            cross = [dot(eq_ref[0, cc, 0, hd], sb[hd]) for hd in heads]
            w = [(sol_ref[0, cc, 0, hd, :, :C_DK].astype(F32) - wks[hd]).astype(BF16) for hd in range(C_HEADS)]
            w += [vr_ref[0, cc, 0, hd] for hd in range(D_HEADS)]
            inner = [dot(qk_ref[0, cc, 0, hd], w[hd]) for hd in heads]
            update = [dot(kt_ref[0, cc, 0, hd], w[hd]) for hd in heads]
            for hd in heads:
                o_ref[0, 0, cc * CHUNK:(cc + 1) * CHUNK, hd * C_DK:(hd + 1) * C_DK] = cross[hd] + inner[hd]
            s = [al_ref[0, cc, 0, hd:hd + 1, :] * s[hd] + update[hd] for hd in heads]
        for hd in heads:
            s_scr[hd] = s[hd]

    per = eq_ref.shape[1]

    @pl.when(pl.program_id(1) == 0)
    def _():
        run(range(per))

    @pl.when(pl.program_id(1) == 1)
    def _():
        run(range(per - 1, -1, -1))


def _rec_scan(eq, qk, kt, sol, vr, al, L):
    bsz, nc = eq.shape[:2]
    nh = C_HEADS + D_HEADS
    n_lat = L // CHUNK
    n_ctx = nc - n_lat

    per = REC_SCAN_CHUNKS
    n_lat, n_ctx, nb = n_lat // per, n_ctx // per, nc // per

    def chunk(d, s):
        fwd = jnp.where(s < n_ctx, n_lat + s, s - n_ctx)
        bwd = nb - 1 - s
        return jnp.where(d == 0, fwd, bwd)

    lead = lambda *tail: pl.BlockSpec((1, per, 1) + tail, lambda b, d, s: (b, chunk(d, s), d) + (0,) * len(tail))
    return pl.pallas_call(
        _rec_scan_kernel,
        grid=(bsz, 2, nb),
        in_specs=[lead(nh, CHUNK, C_DK), lead(nh, CHUNK, CHUNK), lead(nh, C_DK, CHUNK), lead(C_HEADS, CHUNK, 2 * C_DK),
                  lead(D_HEADS, CHUNK, C_DK), lead(nh, LANES)],
        out_specs=pl.BlockSpec((1, 1, per * CHUNK, nh * C_DK), lambda b, d, s: (b, d, chunk(d, s), 0)),
        out_shape=jax.ShapeDtypeStruct((bsz, 2, nc * CHUNK, nh * C_DK), F32),
        scratch_shapes=[pltpu.VMEM((nh, C_DK, C_DK), F32)],
        compiler_params=_cparams(("parallel", "parallel", "arbitrary")),
        name="rec_scan",
    )(eq, qk, kt, sol, vr, al)


def _rec_out_kernel(of_ref, ob_ref, z_ref, gd_ref, og_ref, gn_ref, w_ref, h_ref, mod_ref, out_ref):
    hw = C_HEADS * C_DK
    of, ob = of_ref[0, 0], ob_ref[0, 0]
    z, gd = z_ref[0], gd_ref[0]
    parts = []
    for hd in range(C_HEADS):
        sl = slice(hd * C_DK, (hd + 1) * C_DK)
        x = of[:, sl] + ob[:, sl]
        y = x * lax.rsqrt(jnp.mean(x * x, axis=-1, keepdims=True) + EPS) * og_ref[...]
        parts.append(y * _silu(z[:, sl]))
    for hd in range(D_HEADS):
        sl = slice(hd * D_DK, (hd + 1) * D_DK)
        y = 0.0
        for d, o in enumerate((of, ob)):
            x = o[:, hw + hd * D_DK: hw + (hd + 1) * D_DK]
            mu = jnp.mean(x, axis=-1, keepdims=True)
            xc = x - mu
            var = jnp.mean(xc * xc, axis=-1, keepdims=True)
            y = y + xc * lax.rsqrt(var + EPS) * gn_ref[:, sl] * _silu(gd[:, d * hw + hd * D_DK: d * hw + (hd + 1) * D_DK])
        parts.append(y)
    mix = jnp.concatenate(parts, axis=1)
    out_ref[0] = h_ref[0] + mod_ref[0, 0][2:3] * _bdot(mix, w_ref[...])


def _rec_out(o_scan, z, gd, out_g, gn_g, w_out, hh, mod, L):
    bsz, lt, d = hh.shape
    tm = ROW_TILE
    hw = C_HEADS * C_DK
    return pl.pallas_call(
        _rec_out_kernel,
        grid=(bsz, L // tm),
        in_specs=[pl.BlockSpec((1, 1, tm, 2 * hw), lambda b, i: (b, 0, i, 0)),
                  pl.BlockSpec((1, 1, tm, 2 * hw), lambda b, i: (b, 1, i, 0)),
                  pl.BlockSpec((1, tm, hw), lambda b, i: (b, i, 0)),
                  pl.BlockSpec((1, tm, 2 * hw), lambda b, i: (b, i, 0)),
                  pl.BlockSpec((1, C_DK), lambda b, i: (0, 0)),
                  pl.BlockSpec((1, hw), lambda b, i: (0, 0)),
                  pl.BlockSpec((2 * hw, d), lambda b, i: (0, 0)),
                  pl.BlockSpec((1, tm, d), lambda b, i: (b, i, 0)),
                  pl.BlockSpec((1, 1, 6, d), lambda b, i: (b, 0, 0, 0))],
        out_specs=pl.BlockSpec((1, tm, d), lambda b, i: (b, i, 0)),
        out_shape=jax.ShapeDtypeStruct(hh.shape, F32),
        input_output_aliases={7: 0},
        compiler_params=_cparams(("parallel", "parallel")),
        name="rec_out",
    )(o_scan, o_scan, z, gd, out_g.reshape(1, C_DK), gn_g.reshape(1, hw), w_out.astype(BF16), hh, mod)


def _att_layer(hh, mod, norm_g, w_in, q_g, k_g, sink, w_out, L):
    cos, sin = _rope_tables(L, hh.shape[1] - L, HEAD_DIM)
    qa, ka, va, qb, kb, vb = _att_project(hh, mod, norm_g, w_in, q_g, k_g, cos, sin, L)
    oa, ob = _attention(qa, ka, va, qb, kb, vb, sink, L)
    return _att_out(oa, ob, w_out, hh, mod, L)


def _rec_layer(hh, mod, norm_g, w_in, conv_w, a_log, dt_bias, out_g, gn_g, w_out, L):
    cos, sin = _rope_tables(L, hh.shape[1] - L, D_DK)
    qkv, z, qd, kd, vd, gd, gate = _rec_project(hh, mod, norm_g, w_in, a_log, dt_bias, cos, sin, L)
    qc, kc, vc = _rec_conv(qkv, conv_w, L)
    o_scan = _rec_scan(*_rec_intra(qc, kc, vc, qd, kd, vd, gate), L)
    return _rec_out(o_scan, z, gd, out_g, gn_g, w_out, hh, mod, L)


def kernel(x, c, ctx, c_ctx, mod_w, mod_b, norm1_g, norm2_g, att_w_in, att_q_norm, att_k_norm, att_sink, att_w_out,
           rec_w_in, rec_conv_w, rec_a_log, rec_dt_bias, rec_out_norm, rec_gn_g, rec_w_out,
           peer_w_q, peer_sub_keys, peer_u, peer_v, final_norm_g):
    L = x.shape[1]
    depth = mod_w.shape[0]
    mods = _modulation(c, c_ctx, mod_w, mod_b)
    hh = jnp.concatenate([x, ctx], axis=1)
    for layer in range(depth):
        last = layer == depth - 1
        i = layer // 2
        if layer % 2 == 0:
            hh = _att_layer(hh, mods[layer], norm1_g[layer], att_w_in[i], att_q_norm[i], att_k_norm[i], att_sink[i],
                            att_w_out[i], L)
        else:
            hh = _rec_layer(hh, mods[layer], norm1_g[layer], rec_w_in[i], rec_conv_w[i], rec_a_log[i], rec_dt_bias[i],
                            rec_out_norm[i], rec_gn_g[i], rec_w_out[i], L)
        hh = _peer(hh, mods[layer], norm2_g[layer], peer_w_q[layer], peer_sub_keys[layer], peer_u, peer_v, layer,
                   final_norm_g, L, with_ctx=not last, final=last)
    return hh
```

```python
import functools
import math

import numpy as np
import jax
import jax.numpy as jnp
from jax import lax
from jax.experimental import pallas as pl
from jax.experimental.pallas import tpu as pltpu

F32 = jnp.float32
BF16 = jnp.bfloat16

GRID_W = 64
EPS = 1e-6
HEAD_DIM = 64
A_HEADS = 8
A_KV = 2
B_HEADS = 8
B_KV = 2
WINDOW = 128
ROPE_THETA = 10000.0
C_HEADS = 4
C_DK = 128
CONV_W = 5
CHUNK = 64
D_HEADS = 4
D_DK = 128
RET_DECAY_BASE = 5.0
PEER_HEADS = 8
N_KEYS = 128
PEER_TOPK = 16

LANES = 128
BF16_ROWS = 16
VMEM_LIMIT = 56 * 1024 * 1024

ROW_TILE = 256
ATT_TQ = 128
PEER_TM = 256
PEER_EB = 2048
PEER_SUB = 256

_NEG_INF = float("-inf")


def _cparams(sem, flags=None):
    return pltpu.CompilerParams(dimension_semantics=sem, vmem_limit_bytes=VMEM_LIMIT, flags=flags)


def _bdot(a, b):
    return jnp.dot(a.astype(BF16), b.astype(BF16), preferred_element_type=F32)


def _bdot_nt(a, b):
    return lax.dot_general(a.astype(BF16), b.astype(BF16), (((1,), (1,)), ((), ())), preferred_element_type=F32)


def _bdot_tn(a, b):
    return jnp.dot(a.T.astype(BF16), b.astype(BF16), preferred_element_type=F32)


def _split3(a):
    hi = a.astype(BF16)
    r1 = a - hi.astype(F32)
    mid = r1.astype(BF16)
    lo = (r1 - mid.astype(F32)).astype(BF16)
    return hi, mid, lo


def _dot3(a, b):
    a_hi, a_lo, _ = _split3(a)
    b_hi, b_lo, _ = _split3(b)
    return jnp.dot(jnp.concatenate([a_hi, a_lo, a_hi], axis=1), jnp.concatenate([b_hi, b_hi, b_lo], axis=0),
                   preferred_element_type=F32)


def _sigmoid(x):
    return 1.0 / (1.0 + jnp.exp(-x))


def _silu(x):
    return x * _sigmoid(x)


def _norm_mod(x, g, shift, scale):
    r = lax.rsqrt(jnp.mean(x * x, axis=-1, keepdims=True) + EPS)
    return (x * r * g) * (1.0 + scale) + shift


def _rope(x, cos, sin_signed, head_dim):
    quarter = head_dim // 4
    lane = lax.broadcasted_iota(jnp.int32, x.shape, 1)
    first = (lane % (2 * quarter)) < quarter
    partner = jnp.where(first, pltpu.roll(x, LANES - quarter, 1), pltpu.roll(x, quarter, 1))
    return x * cos + partner * sin_signed


def _mod_kernel(c_ref, w_ref, b_ref, o_ref):
    o_ref[0] = _bdot(_silu(c_ref[...]), w_ref[0]) + b_ref[0]


def _modulation(c, c_ctx, mod_w, mod_b):
    depth, d, n = mod_w.shape
    bsz = c.shape[0]
    rows = 16
    cc = jnp.zeros((rows, d), F32).at[:bsz].set(c).at[bsz].set(c_ctx)
    tn = 1536
    out = pl.pallas_call(
        _mod_kernel,
        grid=(depth, n // tn),
        in_specs=[pl.BlockSpec((rows, d), lambda l, j: (0, 0)),
                  pl.BlockSpec((1, d, tn), lambda l, j: (l, 0, j)),
                  pl.BlockSpec((1, 1, tn), lambda l, j: (l, 0, j))],
        out_specs=pl.BlockSpec((1, rows, tn), lambda l, j: (l, 0, j)),
        out_shape=jax.ShapeDtypeStruct((depth, rows, n), F32),
        compiler_params=_cparams(("arbitrary", "arbitrary")),
        name="modulation",
    )(cc, mod_w, mod_b.reshape(depth, 1, n))
    lat = out[:, :bsz].reshape(depth, bsz, 1, 6, d)
    ctx = jnp.broadcast_to(out[:, bsz].reshape(depth, 1, 1, 6, d), (depth, bsz, 1, 6, d))
    return jnp.concatenate([lat, ctx], axis=2)


def _rope_tables(L, LC, head_dim):
    quarter, half = head_dim // 4, head_dim // 2
    freqs = ROPE_THETA ** (-jnp.arange(quarter, dtype=F32) / quarter)
    lane = np.arange(LANES)
    within = lane % head_dim
    use_col = within >= half
    fidx = within % quarter
    sign = np.where((within % half) < quarter, -1.0, 1.0).astype(np.float32)
    t = jnp.arange(L, dtype=jnp.int32)
    row, col = (t // GRID_W).astype(F32), (t % GRID_W).astype(F32)
    pos = jnp.where(use_col[None, :], col[:, None], row[:, None])
    ang = pos * freqs[fidx][None, :]
    cos = jnp.concatenate([jnp.cos(ang), jnp.ones((LC, LANES), F32)], axis=0)
    sin = jnp.concatenate([jnp.sin(ang) * sign[None, :], jnp.zeros((LC, LANES), F32)], axis=0)
    return cos, sin


def _att_proj_kernel(h_ref, mod_ref, g_ref, w_ref, qg_ref, kg_ref, cos_ref, sin_ref, gm_ref,
                     qa_ref, ka_ref, va_ref, qb_ref, kb_ref, vb_ref):
    m = mod_ref[0, 0]
    a = _norm_mod(h_ref[0], g_ref[...], m[0:1], m[1:2])
    o = _bdot(a, w_ref[...])
    cos, sin = cos_ref[...], sin_ref[...]
    gm = gm_ref[...]

    def head_norm(x, gain):
        sq = x * x
        hi = sq.astype(BF16)
        lo = (sq - hi.astype(F32)).astype(BF16)
        ms = jnp.dot(hi, gm, preferred_element_type=F32) + jnp.dot(lo, gm, preferred_element_type=F32)
        return x * lax.rsqrt(ms + EPS) * gain

    def put(ref, first_head, x):
        ref[0, first_head] = x[:, :HEAD_DIM].astype(ref.dtype)
        ref[0, first_head + 1] = x[:, HEAD_DIM:].astype(ref.dtype)

    scale = HEAD_DIM ** -0.5
    qa_w = A_HEADS * HEAD_DIM
    kv_w = A_KV * HEAD_DIM
    off = 0
    for c in range(qa_w // LANES):
        x = o[:, off + c * LANES: off + (c + 1) * LANES]
        put(qa_ref, 2 * c, _rope(head_norm(x, qg_ref[...]), cos, sin, HEAD_DIM) * scale)
    off += qa_w
    put(ka_ref, 0, _rope(head_norm(o[:, off: off + kv_w], kg_ref[...]), cos, sin, HEAD_DIM))
    off += kv_w
    put(va_ref, 0, o[:, off: off + kv_w])
    off += kv_w
    for c in range(qa_w // LANES):
        x = o[:, off + c * LANES: off + (c + 1) * LANES]
        put(qb_ref, 2 * c, _rope(x, cos, sin, HEAD_DIM) * scale)
    off += qa_w
    put(kb_ref, 0, _rope(o[:, off: off + kv_w], cos, sin, HEAD_DIM))
    off += kv_w
    put(vb_ref, 0, o[:, off: off + kv_w])


def _att_project(hh, mod, norm_g, w_in, q_g, k_g, cos, sin, L):
    bsz, lt, d = hh.shape
    tm = ROW_TILE
    n = w_in.shape[1]
    gm = jnp.asarray(np.kron(np.eye(LANES // HEAD_DIM), np.full((HEAD_DIM, HEAD_DIM), 1.0 / HEAD_DIM)), BF16)
    tile2 = lambda v: jnp.tile(v.reshape(1, HEAD_DIM), (1, LANES // HEAD_DIM))
    qshape = jax.ShapeDtypeStruct((bsz, A_HEADS, lt, HEAD_DIM), BF16)
    kshape = jax.ShapeDtypeStruct((bsz, A_KV, lt, HEAD_DIM), BF16)
    qspec = pl.BlockSpec((1, A_HEADS, tm, HEAD_DIM), lambda b, i: (b, 0, i, 0))
    kspec = pl.BlockSpec((1, A_KV, tm, HEAD_DIM), lambda b, i: (b, 0, i, 0))
    const = lambda shape: pl.BlockSpec(shape, lambda b, i: (0,) * len(shape))
    return pl.pallas_call(
        _att_proj_kernel,
        grid=(bsz, lt // tm),
        in_specs=[pl.BlockSpec((1, tm, d), lambda b, i: (b, i, 0)),
                  pl.BlockSpec((1, 1, 6, d), lambda b, i: (b, i // (L // tm), 0, 0)),
                  const((1, d)), const((d, n)), const((1, LANES)), const((1, LANES)),
                  pl.BlockSpec((tm, LANES), lambda b, i: (i, 0)),
                  pl.BlockSpec((tm, LANES), lambda b, i: (i, 0)),
                  const((LANES, LANES))],
        out_specs=[qspec, kspec, kspec, qspec, kspec, kspec],
        out_shape=[qshape, kshape, kshape, qshape, kshape, kshape],
        compiler_params=_cparams(("parallel", "parallel")),
        name="att_project",
    )(hh, mod, norm_g.reshape(1, d), w_in.astype(BF16), tile2(q_g), tile2(k_g), cos, sin, gm)


def _softmax_pv(score_parts, value_parts, extra_logit=None):
    m = functools.reduce(jnp.maximum, [jnp.max(s, axis=-1, keepdims=True) for s in score_parts])
    if extra_logit is not None:
        m = jnp.maximum(m, extra_logit)
    l = 0.0
    acc = 0.0
    for s, v in zip(score_parts, value_parts):
        p = jnp.exp(s - m)
        l = l + jnp.sum(p, axis=-1, keepdims=True)
        acc = acc + jnp.dot(p.astype(BF16), v, preferred_element_type=F32)
    if extra_logit is not None:
        l = l + jnp.exp(extra_logit - m)
    return acc / l


def _att_kernel(sink_ref, qa_ref, ka_ref, va_ref, qb_ref, kb_ref, vb_ref, oa_ref, ob_ref, *, L, LC):
    i = pl.program_id(1)
    tq = ATT_TQ
    group = A_HEADS // A_KV
    band = tq + 2 * WINDOW

    def stacked_q(ref, kvh):
        return ref[0, kvh * group:(kvh + 1) * group].reshape(group * tq, HEAD_DIM)

    def put(ref, kvh, o):
        for g in range(group):
            hd = kvh * group + g
            ref[0, :, hd * HEAD_DIM:(hd + 1) * HEAD_DIM] = o[g * tq:(g + 1) * tq].astype(ref.dtype)

    def sink_col(kvh):
        return jnp.concatenate([jnp.full((tq, 1), sink_ref[kvh * group + g], F32) for g in range(group)], axis=0)

    @pl.when(i < L // tq)
    def _latent():
        for kvh in range(A_KV):
            q = stacked_q(qa_ref, kvh)
            put(oa_ref, kvh, _softmax_pv([_bdot_nt(q, ka_ref[0, kvh])], [va_ref[0, kvh]]))
            q = stacked_q(qb_ref, kvh)
            start = pl.multiple_of(jnp.clip((i - 1) * tq, 0, L - band), tq)
            s_band = _bdot_nt(q, kb_ref[0, kvh, pl.ds(start, band), :])
            qpos = i * tq + (lax.broadcasted_iota(jnp.int32, s_band.shape, 0) % tq)
            kpos = start + lax.broadcasted_iota(jnp.int32, s_band.shape, 1)
            s_band = jnp.where(jnp.abs(qpos - kpos) <= WINDOW, s_band, _NEG_INF)
            s_ctx = _bdot_nt(q, kb_ref[0, kvh, L:L + LC, :])
            put(ob_ref, kvh, _softmax_pv([s_band, s_ctx],
                                         [vb_ref[0, kvh, pl.ds(start, band), :], vb_ref[0, kvh, L:L + LC, :]],
                                         sink_col(kvh)))

    @pl.when(i >= L // tq)
    def _context():
        for kvh in range(A_KV):
            q = stacked_q(qa_ref, kvh)
            put(oa_ref, kvh, _softmax_pv([_bdot_nt(q, ka_ref[0, kvh, L:L + LC, :])], [va_ref[0, kvh, L:L + LC, :]]))
            q = stacked_q(qb_ref, kvh)
            put(ob_ref, kvh, _softmax_pv([_bdot_nt(q, kb_ref[0, kvh, L:L + LC, :])], [vb_ref[0, kvh, L:L + LC, :]],
                                         sink_col(kvh)))


def _attention(qa, ka, va, qb, kb, vb, sink, L):
    bsz, _, lt, _ = qa.shape
    tq = ATT_TQ
    qspec = pl.BlockSpec((1, A_HEADS, tq, HEAD_DIM), lambda b, i: (b, 0, i, 0))
    kspec = pl.BlockSpec((1, A_KV, lt, HEAD_DIM), lambda b, i: (b, 0, 0, 0))
    ospec = pl.BlockSpec((1, tq, A_HEADS * HEAD_DIM), lambda b, i: (b, i, 0))
    oshape = jax.ShapeDtypeStruct((bsz, lt, A_HEADS * HEAD_DIM), BF16)
    return pl.pallas_call(
        functools.partial(_att_kernel, L=L, LC=lt - L),
        grid=(bsz, lt // tq),
        in_specs=[pl.BlockSpec(memory_space=pltpu.SMEM), qspec, kspec, kspec, qspec, kspec, kspec],
        out_specs=[ospec, ospec],
        out_shape=[oshape, oshape],
        compiler_params=_cparams(("parallel", "parallel")),
        name="attention",
    )(sink, qa, ka, va, qb, kb, vb)


def _att_out_kernel(oa_ref, ob_ref, w_ref, h_ref, mod_ref, out_ref):
    half = oa_ref.shape[-1]
    y = (jnp.dot(oa_ref[0], w_ref[:half, :], preferred_element_type=F32)
         + jnp.dot(ob_ref[0], w_ref[half:, :], preferred_element_type=F32))
    out_ref[0] = h_ref[0] + mod_ref[0, 0][2:3] * y


def _att_out(oa, ob, w_out, hh, mod, L):
    bsz, lt, d = hh.shape
    tm = ROW_TILE
    half = oa.shape[-1]
    return pl.pallas_call(
        _att_out_kernel,
        grid=(bsz, lt // tm),
        in_specs=[pl.BlockSpec((1, tm, half), lambda b, i: (b, i, 0)),
                  pl.BlockSpec((1, tm, half), lambda b, i: (b, i, 0)),
                  pl.BlockSpec((2 * half, d), lambda b, i: (0, 0)),
                  pl.BlockSpec((1, tm, d), lambda b, i: (b, i, 0)),
                  pl.BlockSpec((1, 1, 6, d), lambda b, i: (b, i // (L // tm), 0, 0))],
        out_specs=pl.BlockSpec((1, tm, d), lambda b, i: (b, i, 0)),
        out_shape=jax.ShapeDtypeStruct(hh.shape, F32),
        input_output_aliases={3: 0},
        compiler_params=_cparams(("parallel", "parallel")),
        name="att_out",
    )(oa, ob, w_out.astype(BF16), hh, mod)


def _sort_network(n):
    pairs = []
    p = 1
    while p < n:
        k = p
        while k >= 1:
            for j in range(k % p, n - k, 2 * k):
                for i in range(min(k, n - j - k)):
                    if (i + j) // (2 * p) == (i + j + k) // (2 * p):
                        pairs.append((i + j, i + j + k))
            k //= 2
        p *= 2
    return pairs


def _pop_sorted(lists, n, singles=()):
    lists, singles = list(lists), list(singles)
    out = []
    for rnd in range(n):
        head = functools.reduce(jnp.maximum, [lists[0]] + singles)
        m = jnp.max(head, axis=0, keepdims=True)
        out.append(m)
        if rnd == n - 1:
            break
        hit = lists[0] == m
        depth = min(len(lists), n - rnd - 1)
        for d in range(depth):
            nxt = lists[d + 1] if d + 1 < len(lists) else _NEG_INF
            lists[d] = jnp.where(hit, nxt, lists[d])
        singles = [jnp.where(s == m, _NEG_INF, s) for s in singles]
    return out


def _top_values(x, n):
    slabs = [x[r * 8:(r + 1) * 8] for r in range(x.shape[0] // 8)]
    for i, j in _sort_network(len(slabs)):
        slabs[i], slabs[j] = jnp.maximum(slabs[i], slabs[j]), jnp.minimum(slabs[i], slabs[j])
    return _pop_sorted(slabs, n)


def _peer_route_kernel(h_ref, mod_ref, g_ref, wq_ref, keys_ref, at_ref, thr_ref, e0_ref, r1_ref, e1_ref, q_scr):
    m = mod_ref[0, 0]
    a = _norm_mod(h_ref[0], g_ref[...], m[3:4], m[4:5])
    at_ref[0] = a.T.astype(BF16)
    q_scr[...] = jnp.dot(a.astype(BF16), wq_ref[...], preferred_element_type=F32)
    k = PEER_TOPK
    n = k + 1

    def head(hd, carry):
        col = pl.multiple_of(hd * 2 * N_KEYS, 2 * N_KEYS)
        s0 = _bdot_nt(keys_ref[0], q_scr[:, pl.ds(col, N_KEYS)])
        s1 = _bdot_nt(keys_ref[1], q_scr[:, pl.ds(col + N_KEYS, N_KEYS)])
        tokens = s0.shape[1]
        top0 = _top_values(s0, n)
        top1 = _top_values(s1, n)
        first = jnp.concatenate(top0 + [jnp.full((24 - n, tokens), _NEG_INF, F32)], axis=0)
        best = _pop_sorted([first[0:8] + t for t in top1], n, [first[8:16] + top1[0], first[16:24] + top1[0]])
        z = functools.reduce(lambda x, y: x + y, [jnp.exp(b - best[0]) for b in best[:k]])
        tau = 0.5 * (best[k - 1] + best[k])
        r1 = jnp.zeros_like(s1)
        need = jnp.full_like(s0, n + 1.0)
        for b in range(k):
            r1 = jnp.where(s1 >= top1[k - 1 - b], b + 2.0, r1)
            need = jnp.where(s0 >= tau - top1[b], float(n - b), need)
        thr_ref[0, hd] = need
        e0_ref[0, hd] = jnp.exp(s0 - top0[0])
        r1_ref[0, hd] = pltpu.bitcast(r1.astype(BF16), jnp.uint32)
        e1_ref[0, hd] = pltpu.bitcast((jnp.exp(s1 - top1[0]) / z).astype(BF16), jnp.uint32)
        return carry

    lax.fori_loop(0, PEER_HEADS, head, 0, unroll=True)


def _peer_route(hh, mod, norm_g, w_q, sub_keys, L):
    bsz, lt, d = hh.shape
    tm = ROW_TILE
    nq = w_q.shape[1]
    tok = lambda: pl.BlockSpec((1, PEER_HEADS, N_KEYS, tm), lambda b, i: (b, 0, 0, i))
    tshape = jax.ShapeDtypeStruct((bsz, PEER_HEADS, N_KEYS, lt), F32)
    pshape = jax.ShapeDtypeStruct((bsz, PEER_HEADS, N_KEYS // 2, lt), jnp.uint32)
    ptok = pl.BlockSpec((1, PEER_HEADS, N_KEYS // 2, tm), lambda b, i: (b, 0, 0, i))
    return pl.pallas_call(
        _peer_route_kernel,
        grid=(bsz, lt // tm),
        in_specs=[pl.BlockSpec((1, tm, d), lambda b, i: (b, i, 0)),
                  pl.BlockSpec((1, 1, 6, d), lambda b, i: (b, i // (L // tm), 0, 0)),
                  pl.BlockSpec((1, d), lambda b, i: (0, 0)),
                  pl.BlockSpec((d, nq), lambda b, i: (0, 0)),
                  pl.BlockSpec((2, N_KEYS, N_KEYS), lambda b, i: (0, 0, 0))],
        out_specs=[pl.BlockSpec((1, d, tm), lambda b, i: (b, 0, i)), tok(), tok(), ptok, ptok],
        out_shape=[jax.ShapeDtypeStruct((bsz, d, lt), BF16), tshape, tshape, pshape, pshape],
        scratch_shapes=[pltpu.VMEM((tm, nq), F32)],
        compiler_params=_cparams(("parallel", "parallel")),
        name="peer_route",
    )(hh, mod, norm_g.reshape(1, d), w_q.astype(BF16), sub_keys.astype(BF16))


def _gelu(x):
    return 0.5 * x * (1.0 + lax.erf(x * (1.0 / math.sqrt(2.0))))


def _peer_expert_kernel(a_ref, an_ref, thr_ref, e0_ref, r1_ref, e1_ref, u_ref, vt_ref, h_ref, mod_ref, fg_ref,
                        out_ref, yt_scr, wa_scr, at0_scr, at1_scr, *, final):
    e = pl.program_id(1)
    t = pl.program_id(2)
    tm = a_ref.shape[2]
    n_sub = PEER_EB // PEER_SUB
    rows = [slice(q * PEER_SUB, (q + 1) * PEER_SUB) for q in range(n_sub)]

    def activations(dst, src, q):
        words = slice(rows[q].start // 2, rows[q].stop // 2)
        dst[rows[q], :] = jnp.dot(pltpu.bitcast(u_ref[words, :], BF16), src[0], preferred_element_type=F32)

    @pl.when(e == 0)
    def _():
        yt_scr[t] = jnp.zeros(yt_scr.shape[1:], F32)

    @pl.when(t == 0)
    def _():
        for q in range(n_sub):
            activations(at0_scr, a_ref, q)

    def step(act_ref, next_ref):
        y = None
        for q in range(n_sub):
            activations(next_ref, an_ref, q)
            for il in range(PEER_SUB // N_KEYS):
                key0 = q * (PEER_SUB // N_KEYS) + il
                for c in range(tm // LANES):
                    cs = slice(c * LANES, (c + 1) * LANES)
                    row = lambda ref, hd: jnp.broadcast_to(ref[0, hd, key0:key0 + 1, cs], (BF16_ROWS, LANES)).astype(BF16)
                    thr = [row(thr_ref, hd) for hd in range(PEER_HEADS)]
                    e0 = [row(e0_ref, hd) for hd in range(PEER_HEADS)]
                    zero = jnp.zeros((BF16_ROWS, LANES), BF16)
                    for r in range(N_KEYS // BF16_ROWS):
                        ws = slice(r * BF16_ROWS // 2, (r + 1) * BF16_ROWS // 2)
                        gate = None
                        for hd in range(PEER_HEADS):
                            rank = pltpu.bitcast(r1_ref[0, hd, ws, cs], BF16)
                            weight = pltpu.bitcast(e1_ref[0, hd, ws, cs], BF16)
                            part = jnp.where(rank >= thr[hd], weight * e0[hd], zero)
                            gate = part if gate is None else gate + part
                        ars = slice(rows[q].start + il * N_KEYS + r * BF16_ROWS,
                                    rows[q].start + il * N_KEYS + (r + 1) * BF16_ROWS)
                        wa_scr[ars, cs] = gate * _gelu(act_ref[ars, cs]).astype(BF16)
            part = jnp.dot(pltpu.bitcast(vt_ref[:, rows[q]], BF16), wa_scr[rows[q], :], preferred_element_type=F32)
            y = part if y is None else y + part
        yt_scr[t] += y

    @pl.when(t % 2 == 0)
    def _():
        step(at0_scr, at1_scr)

    @pl.when(t % 2 == 1)
    def _():
        step(at1_scr, at0_scr)

    @pl.when(e == pl.num_programs(1) - 1)
    def _():
        hn = h_ref[0] + mod_ref[0, 0][5:6] * yt_scr[t].T
        if final:
            hn = hn * lax.rsqrt(jnp.mean(hn * hn, axis=-1, keepdims=True) + EPS) * fg_ref[...]
        out_ref[0] = hn


def _peer_expert(hh, mod, a, thr, e0, r1, e1, u_bf, vt_bf, final_g, L, *, with_ctx, final):
    bsz, lt, d = hh.shape
    n_exp = vt_bf.shape[1]
    tm = PEER_TM
    n_tiles = (lt if with_ctx else L) // tm
    keys_per_step = PEER_EB // N_KEYS
    n_blocks = n_exp // PEER_EB
    tok = lambda r: pl.BlockSpec((1, PEER_HEADS, r, tm), (lambda b, e, t: (b, 0, e, t)) if r == keys_per_step
                                 else (lambda b, e, t: (b, 0, 0, t)))
    row_spec = pl.BlockSpec((1, tm, d), lambda b, e, t: (b, jnp.where(e == n_blocks - 1, t, 0), 0))
    if final:
        out_shape = jax.ShapeDtypeStruct((bsz, L, d), F32)
        aliases = {}
    else:
        out_shape = jax.ShapeDtypeStruct(hh.shape, F32)
        aliases = {8: 0}
    return pl.pallas_call(
        functools.partial(_peer_expert_kernel, final=final),
        grid=(bsz, n_blocks, n_tiles),
        in_specs=[pl.BlockSpec((1, d, tm), lambda b, e, t: (b, 0, t)),
                  pl.BlockSpec((1, d, tm), lambda b, e, t: (b, 0, jnp.minimum(t + 1, n_tiles - 1))),
                  tok(keys_per_step), tok(keys_per_step), tok(N_KEYS // 2), tok(N_KEYS // 2),
                  pl.BlockSpec((PEER_EB // 2, d), lambda b, e, t: (e, 0)),
                  pl.BlockSpec((d // 2, PEER_EB), lambda b, e, t: (0, e)),
                  row_spec,
                  pl.BlockSpec((1, 1, 6, d), lambda b, e, t: (b, t // (L // tm), 0, 0)),
                  pl.BlockSpec((1, d), lambda b, e, t: (0, 0))],
        out_specs=row_spec,
        out_shape=out_shape,
        input_output_aliases=aliases,
        scratch_shapes=[pltpu.VMEM((n_tiles, d, tm), F32), pltpu.VMEM((PEER_EB, tm), BF16),
                        pltpu.VMEM((PEER_EB, tm), F32), pltpu.VMEM((PEER_EB, tm), F32)],
        compiler_params=_cparams(("parallel", "arbitrary", "arbitrary")),
        name="peer_expert",
    )(a, a, thr, e0, r1, e1, u_bf, vt_bf, hh, mod, final_g.reshape(1, d))


def _pack_kernel(x_ref, o_ref, *, transpose):
    x = x_ref[0]
    if transpose:
        x = x.T
    o_ref[...] = pltpu.bitcast(x.astype(BF16), jnp.uint32)


def _pack_row_pairs(x, layer, *, transpose):
    _, r, c = x.shape
    blk = 512
    if transpose:
        out_shape, out_spec = (c // 2, r), pl.BlockSpec((c // 2, blk), lambda i: (0, i))
    else:
        out_shape, out_spec = (r // 2, c), pl.BlockSpec((blk // 2, c), lambda i: (i, 0))
    return pl.pallas_call(
        functools.partial(_pack_kernel, transpose=transpose),
        grid=(r // blk,),
        in_specs=[pl.BlockSpec((1, blk, c), lambda i: (layer, i, 0))],
        out_specs=out_spec,
        out_shape=jax.ShapeDtypeStruct(out_shape, jnp.uint32),
        compiler_params=_cparams(("parallel",)),
        name="pack_pairs_t" if transpose else "pack_pairs",
    )(x)


def _peer(hh, mod, norm_g, w_q, sub_keys, u_all, v_all, layer, final_g, L, *, with_ctx, final):
    a, thr, e0, r1, e1 = _peer_route(hh, mod, norm_g, w_q, sub_keys, L)
    return _peer_expert(hh, mod, a, thr, e0, r1, e1, _pack_row_pairs(u_all, layer, transpose=False),
                        _pack_row_pairs(v_all, layer, transpose=True), final_g, L, with_ctx=with_ctx, final=final)


REC_SCAN_CHUNKS = 4
REC_CHUNKS_PER_STEP = 2
REC_QKV = C_HEADS * 3 * C_DK
REC_Z = C_HEADS * C_DK
REC_GATES = 4 * C_HEADS
REC_HD = D_HEADS * D_DK


def _rec_proj_kernel(h_ref, mod_ref, g_ref, w_ref, alog_ref, dtb_ref, cos_ref, sin_ref,
                     qkv_ref, z_ref, qd_ref, kd_ref, vd_ref, gd_ref, gate_ref):
    m = mod_ref[0, 0]
    a = _norm_mod(h_ref[0], g_ref[...], m[0:1], m[1:2])
    o = _bdot(a, w_ref[...])
    cos, sin = cos_ref[...], sin_ref[...]
    off = 0
    qkv_ref[0] = o[:, off:off + REC_QKV]
    off += REC_QKV
    z_ref[0] = o[:, off:off + REC_Z]
    off += REC_Z
    for hd in range(D_HEADS):
        qd_ref[0, :, hd * D_DK:(hd + 1) * D_DK] = _rope(o[:, off + hd * D_DK: off + (hd + 1) * D_DK], cos, sin, D_DK)
    off += REC_HD
    for hd in range(D_HEADS):
        kd_ref[0, :, hd * D_DK:(hd + 1) * D_DK] = (
            _rope(o[:, off + hd * D_DK: off + (hd + 1) * D_DK], cos, sin, D_DK) * (D_DK ** -0.5))
    off += REC_HD
    vd_ref[0] = o[:, off:off + REC_HD]
    off += REC_HD
    gd_ref[0] = o[:, off:off + 2 * REC_HD]
    off += 2 * REC_HD
    x = o[:, off:off + LANES]
    lane = lax.broadcasted_iota(jnp.int32, x.shape, 1)
    xb = x + dtb_ref[...]
    softplus = jnp.maximum(xb, 0.0) + jnp.log(1.0 + jnp.exp(-jnp.abs(xb)))
    gate_ref[0] = jnp.where(lane < 2 * C_HEADS, -jnp.exp(alog_ref[...]) * softplus, _sigmoid(x))


def _rec_project(hh, mod, norm_g, w_in, a_log, dt_bias, cos, sin, L):
    bsz, lt, d = hh.shape
    tm = ROW_TILE
    parts = np.cumsum([REC_QKV, REC_Z, REC_GATES, REC_HD, REC_HD, REC_HD])
    qkv_w, z_w, gates_w, qd_w, kd_w, vd_w, gd_w = jnp.split(w_in, [int(p) for p in parts], axis=1)
    w = jnp.concatenate([qkv_w, z_w, qd_w, kd_w, vd_w, gd_w, gates_w, jnp.zeros((d, LANES - REC_GATES), F32)],
                        axis=1).astype(BF16)
    n = w.shape[1]
    pad = lambda p: jnp.zeros((1, LANES), F32).at[0, :2 * C_HEADS].set(p.reshape(-1))
    widths = [REC_QKV, REC_Z, REC_HD, REC_HD, REC_HD, 2 * REC_HD, LANES]
    const = lambda shape: pl.BlockSpec(shape, lambda b, i: (0,) * len(shape))
    return pl.pallas_call(
        _rec_proj_kernel,
        grid=(bsz, lt // tm),
        in_specs=[pl.BlockSpec((1, tm, d), lambda b, i: (b, i, 0)),
                  pl.BlockSpec((1, 1, 6, d), lambda b, i: (b, i // (L // tm), 0, 0)),
                  const((1, d)), const((d, n)), const((1, LANES)), const((1, LANES)),
                  pl.BlockSpec((tm, LANES), lambda b, i: (i, 0)),
                  pl.BlockSpec((tm, LANES), lambda b, i: (i, 0))],
        out_specs=[pl.BlockSpec((1, tm, wd), lambda b, i: (b, i, 0)) for wd in widths],
        out_shape=[jax.ShapeDtypeStruct((bsz, lt, wd), F32) for wd in widths],
        compiler_params=_cparams(("parallel", "parallel")),
        name="rec_project",
    )(hh, mod, norm_g.reshape(1, d), w, pad(a_log), pad(dt_bias), cos, sin)


def _rec_conv_kernel(x_ref, prev_ref, next_ref, w_ref, q_ref, k_ref, v_ref, *, L):
    i = pl.program_id(1)
    tl = x_ref.shape[1]
    n_lat = L // tl
    at_start = (i == 0) | (i == n_lat)
    at_end = (i == n_lat - 1) | (i == pl.num_programs(1) - 1)
    prev = jnp.where(at_start, 0.0, prev_ref[0])
    nxt = jnp.where(at_end, 0.0, next_ref[0])
    xx = jnp.concatenate([prev, x_ref[0], nxt], axis=0)
    n = tl + 16
    acc = 0.0
    for tap in range(CONV_W):
        shift = (CONV_W // 2 - tap) % n
        shifted = xx if shift == 0 else pltpu.roll(xx, shift, 0)
        acc = acc + shifted[8:8 + tl] * w_ref[tap:tap + 1, :]
    y = _silu(acc)
    hw = C_HEADS * C_DK
    for hd in range(C_HEADS):
        def l2(x):
            return x * lax.rsqrt(jnp.sum(x * x, axis=-1, keepdims=True) + EPS)
        sl = slice(hd * C_DK, (hd + 1) * C_DK)
        q_ref[0, :, sl] = l2(y[:, hd * C_DK:(hd + 1) * C_DK]) * (C_DK ** -0.5)
        k_ref[0, :, sl] = l2(y[:, hw + hd * C_DK: hw + (hd + 1) * C_DK])
    v_ref[0] = y[:, 2 * hw:]


def _rec_conv(qkv, conv_w, L):
    bsz, lt, ch = qkv.shape
    tl = ROW_TILE
    hb = tl // 8
    last = lt // 8 - 1
    hw = C_HEADS * C_DK
    return pl.pallas_call(
        functools.partial(_rec_conv_kernel, L=L),
        grid=(bsz, lt // tl),
        in_specs=[pl.BlockSpec((1, tl, ch), lambda b, i: (b, i, 0)),
                  pl.BlockSpec((1, 8, ch), lambda b, i: (b, jnp.maximum(i * hb - 1, 0), 0)),
                  pl.BlockSpec((1, 8, ch), lambda b, i: (b, jnp.minimum((i + 1) * hb, last), 0)),
                  pl.BlockSpec((8, ch), lambda b, i: (0, 0))],
        out_specs=[pl.BlockSpec((1, tl, hw), lambda b, i: (b, i, 0))] * 3,
        out_shape=[jax.ShapeDtypeStruct((bsz, lt, hw), F32)] * 3,
        compiler_params=_cparams(("parallel", "parallel")),
        name="rec_conv",
    )(qkv, qkv, qkv, jnp.zeros((8, ch), F32).at[:CONV_W].set(conv_w))


def _ret_log_gamma(hd):
    return float(np.log1p(-np.exp2(-(RET_DECAY_BASE + hd))))


def _rec_intra_kernel(qc_ref, kc_ref, vc_ref, qd_ref, kd_ref, vd_ref, gate_ref,
                      eq_ref, qk_ref, kt_ref, sol_ref, vr_ref, al_ref):
    c = CHUNK
    ci = lax.broadcasted_iota(jnp.int32, (c, c), 0)
    si = lax.broadcasted_iota(jnp.int32, (c, c), 1)
    pos = lax.broadcasted_iota(jnp.int32, (c, 1), 0).astype(F32)
    ones_row = jnp.ones((1, LANES), F32)
    combos = []
    for cc in range(REC_CHUNKS_PER_STEP):
        _rec_intra_setup(cc, combos, ci, si, pos, ones_row, qc_ref, kc_ref, vc_ref, qd_ref, kd_ref, vd_ref, gate_ref,
                         eq_ref, qk_ref, kt_ref, vr_ref, al_ref)
    _rec_intra_solve(combos, ones_row, eq_ref, qk_ref, kt_ref, sol_ref, al_ref)


def _rec_intra_setup(cc, combos, ci, si, pos, ones_row, qc_ref, kc_ref, vc_ref, qd_ref, kd_ref, vd_ref, gate_ref,
                     eq_ref, qk_ref, kt_ref, vr_ref, al_ref):
    c = CHUNK
    rs = slice(cc * c, (cc + 1) * c)
    gates = gate_ref[0, rs]
    la_parts = _split3(gates)
    for d in range(2):
        incl = (si <= ci) if d == 0 else (si >= ci)
        strict = (si < ci) if d == 0 else (si > ci)
        tri = incl.astype(BF16)
        g_all = jnp.dot(jnp.concatenate([tri] * 3, axis=1), jnp.concatenate(la_parts, axis=0),
                        preferred_element_type=F32)
        g_all_t = g_all.T
        last = c - 1 if d == 0 else 0
        for hd in range(C_HEADS):
            col = d * C_HEADS + hd
            sl = slice(hd * C_DK, (hd + 1) * C_DK)
            q, k, v = qc_ref[0, rs, sl], kc_ref[0, rs, sl], vc_ref[0, rs, sl]
            g_rows = jnp.broadcast_to(g_all[:, col:col + 1], (c, C_DK))
            beta = jnp.broadcast_to(gates[:, 2 * C_HEADS + col: 2 * C_HEADS + col + 1], (c, C_DK))
            grow = g_all_t[col:col + 1, :]
            glast = g_all[last:last + 1, col:col + 1]
            exp_g = jnp.exp(g_rows)
            diff = g_rows[:, :c] - grow
            dec_strict = jnp.exp(jnp.where(strict, diff, _NEG_INF))
            combos.append(dict(
                cc=cc, d=d, hd=hd, q=q, exp_g=exp_g, glast=glast,
                x=-(beta[:, :c] * _bdot_nt(k, k) * dec_strict),
                sol=jnp.concatenate([beta * v, (beta * exp_g) * k], axis=1),
                qk=_bdot_nt(q, k) * jnp.exp(jnp.where(incl, diff, _NEG_INF)),
                kend=k * jnp.exp(glast - g_rows)))
    for d in range(2):
        incl = (si <= ci) if d == 0 else (si >= ci)
        for hd in range(D_HEADS):
            lg = _ret_log_gamma(hd)
            sl = slice(hd * D_DK, (hd + 1) * D_DK)
            q, k, v = qd_ref[0, rs, sl], kd_ref[0, rs, sl], vd_ref[0, rs, sl]
            steps = pos if d == 0 else (c - 1.0) - pos
            dist = (ci - si) if d == 0 else (si - ci)
            dmat = jnp.exp(jnp.where(incl, lg * dist.astype(F32), _NEG_INF))
            qk = _bdot_nt(q, k) * dmat
            eq_ref[0, cc, d, C_HEADS + hd] = (q * jnp.exp(lg * (steps + 1.0))).astype(BF16)
            qk_ref[0, cc, d, C_HEADS + hd] = qk.astype(BF16)
            kt_ref[0, cc, d, C_HEADS + hd] = (k * jnp.exp(lg * ((c - 1.0) - steps))).T.astype(BF16)
            vr_ref[0, cc, d, hd] = v.astype(BF16)
            al_ref[0, cc, d, C_HEADS + hd:C_HEADS + hd + 1, :] = math.exp(lg * c) * ones_row


def _rec_intra_solve(combos, ones_row, eq_ref, qk_ref, kt_ref, sol_ref, al_ref):
    c = CHUNK
    for cb in combos:
        cc, d, hd = cb["cc"], cb["d"], cb["hd"]
        eq_ref[0, cc, d, hd] = (cb["exp_g"] * cb["q"]).astype(BF16)
        qk_ref[0, cc, d, hd] = cb["qk"].astype(BF16)
        kt_ref[0, cc, d, hd] = cb["kend"].T.astype(BF16)
        al_ref[0, cc, d, hd:hd + 1, :] = jnp.exp(cb["glast"]) * ones_row
    levels = int(math.log2(c))
    for lvl in range(levels):
        for cb in combos:
            cb["sol"] = cb["sol"] + _dot3(cb["x"], cb["sol"])
        if lvl < levels - 1:
            for cb in combos:
                cb["x"] = _dot3(cb["x"], cb["x"])
    for cb in combos:
        sol_ref[0, cb["cc"], cb["d"], cb["hd"]] = cb["sol"].astype(BF16)


def _rec_intra(qc, kc, vc, qd, kd, vd, gate):
    bsz, lt, hw = qc.shape
    nc = lt // CHUNK
    nh = C_HEADS + D_HEADS
    per = REC_CHUNKS_PER_STEP
    row = lambda w: pl.BlockSpec((1, per * CHUNK, w), lambda b, n: (b, n, 0))
    lead = lambda *tail: pl.BlockSpec((1, per, 2) + tail, lambda b, n: (b, n, 0) + (0,) * len(tail))
    return pl.pallas_call(
        _rec_intra_kernel,
        grid=(bsz, nc // per),
        in_specs=[row(hw)] * 6 + [row(LANES)],
        out_specs=[lead(nh, CHUNK, C_DK), lead(nh, CHUNK, CHUNK), lead(nh, C_DK, CHUNK), lead(C_HEADS, CHUNK, 2 * C_DK),
                   lead(D_HEADS, CHUNK, C_DK), lead(nh, LANES)],
        out_shape=[jax.ShapeDtypeStruct((bsz, nc, 2, nh, CHUNK, C_DK), BF16),
                   jax.ShapeDtypeStruct((bsz, nc, 2, nh, CHUNK, CHUNK), BF16),
                   jax.ShapeDtypeStruct((bsz, nc, 2, nh, C_DK, CHUNK), BF16),
                   jax.ShapeDtypeStruct((bsz, nc, 2, C_HEADS, CHUNK, 2 * C_DK), BF16),
                   jax.ShapeDtypeStruct((bsz, nc, 2, D_HEADS, CHUNK, C_DK), BF16),
                   jax.ShapeDtypeStruct((bsz, nc, 2, nh, LANES), F32)],
        compiler_params=_cparams(("parallel", "parallel")),
        name="rec_intra",
    )(qc, kc, vc, qd, kd, vd, gate)


def _rec_scan_kernel(eq_ref, qk_ref, kt_ref, sol_ref, vr_ref, al_ref, o_ref, s_scr):
    @pl.when(pl.program_id(2) == 0)
    def _():
        s_scr[...] = jnp.zeros_like(s_scr)

    dot = functools.partial(jnp.dot, preferred_element_type=F32)
    heads = range(C_HEADS + D_HEADS)

    def run(order):
        s = [s_scr[hd] for hd in heads]
        for cc in order:
            sb = [x.astype(BF16) for x in s]
            wks = [dot(sol_ref[0, cc, 0, hd, :, C_DK:], sb[hd]) for hd in range(C_HEADS)]
            cross = [dot(eq_ref[0, cc, 0, hd], sb[hd]) for hd in heads]
            w = [(sol_ref[0, cc, 0, hd, :, :C_DK].astype(F32) - wks[hd]).astype(BF16) for hd in range(C_HEADS)]
            w += [vr_ref[0, cc, 0, hd] for hd in range(D_HEADS)]
            inner = [dot(qk_ref[0, cc, 0, hd], w[hd]) for hd in heads]
            update = [dot(kt_ref[0, cc, 0, hd], w[hd]) for hd in heads]
            for hd in heads:
                o_ref[0, 0, cc * CHUNK:(cc + 1) * CHUNK, hd * C_DK:(hd + 1) * C_DK] = cross[hd] + inner[hd]
            s = [al_ref[0, cc, 0, hd:hd + 1, :] * s[hd] + update[hd] for hd in heads]
        for hd in heads:
            s_scr[hd] = s[hd]

    per = eq_ref.shape[1]

    @pl.when(pl.program_id(1) == 0)
    def _():
        run(range(per))

    @pl.when(pl.program_id(1) == 1)
    def _():
        run(range(per - 1, -1, -1))


def _rec_scan(eq, qk, kt, sol, vr, al, L):
    bsz, nc = eq.shape[:2]
    nh = C_HEADS + D_HEADS
    n_lat = L // CHUNK
    n_ctx = nc - n_lat

    per = REC_SCAN_CHUNKS
    n_lat, n_ctx, nb = n_lat // per, n_ctx // per, nc // per

    def chunk(d, s):
        fwd = jnp.where(s < n_ctx, n_lat + s, s - n_ctx)
        bwd = nb - 1 - s
        return jnp.where(d == 0, fwd, bwd)

    lead = lambda *tail: pl.BlockSpec((1, per, 1) + tail, lambda b, d, s: (b, chunk(d, s), d) + (0,) * len(tail))
    return pl.pallas_call(
        _rec_scan_kernel,
        grid=(bsz, 2, nb),
        in_specs=[lead(nh, CHUNK, C_DK), lead(nh, CHUNK, CHUNK), lead(nh, C_DK, CHUNK), lead(C_HEADS, CHUNK, 2 * C_DK),
                  lead(D_HEADS, CHUNK, C_DK), lead(nh, LANES)],
        out_specs=pl.BlockSpec((1, 1, per * CHUNK, nh * C_DK), lambda b, d, s: (b, d, chunk(d, s), 0)),
        out_shape=jax.ShapeDtypeStruct((bsz, 2, nc * CHUNK, nh * C_DK), F32),
        scratch_shapes=[pltpu.VMEM((nh, C_DK, C_DK), F32)],
        compiler_params=_cparams(("parallel", "parallel", "arbitrary")),
        name="rec_scan",
    )(eq, qk, kt, sol, vr, al)


def _rec_out_kernel(of_ref, ob_ref, z_ref, gd_ref, og_ref, gn_ref, w_ref, h_ref, mod_ref, out_ref):
    hw = C_HEADS * C_DK
    of, ob = of_ref[0, 0], ob_ref[0, 0]
    z, gd = z_ref[0], gd_ref[0]
    parts = []
    for hd in range(C_HEADS):
        sl = slice(hd * C_DK, (hd + 1) * C_DK)
        x = of[:, sl] + ob[:, sl]
        y = x * lax.rsqrt(jnp.mean(x * x, axis=-1, keepdims=True) + EPS) * og_ref[...]
        parts.append(y * _silu(z[:, sl]))
    for hd in range(D_HEADS):
        sl = slice(hd * D_DK, (hd + 1) * D_DK)
        y = 0.0
        for d, o in enumerate((of, ob)):
            x = o[:, hw + hd * D_DK: hw + (hd + 1) * D_DK]
            mu = jnp.mean(x, axis=-1, keepdims=True)
            xc = x - mu
            var = jnp.mean(xc * xc, axis=-1, keepdims=True)
            y = y + xc * lax.rsqrt(var + EPS) * gn_ref[:, sl] * _silu(gd[:, d * hw + hd * D_DK: d * hw + (hd + 1) * D_DK])
        parts.append(y)
    mix = jnp.concatenate(parts, axis=1)
    out_ref[0] = h_ref[0] + mod_ref[0, 0][2:3] * _bdot(mix, w_ref[...])


def _rec_out(o_scan, z, gd, out_g, gn_g, w_out, hh, mod, L):
    bsz, lt, d = hh.shape
    tm = ROW_TILE
    hw = C_HEADS * C_DK
    return pl.pallas_call(
        _rec_out_kernel,
        grid=(bsz, L // tm),
        in_specs=[pl.BlockSpec((1, 1, tm, 2 * hw), lambda b, i: (b, 0, i, 0)),
                  pl.BlockSpec((1, 1, tm, 2 * hw), lambda b, i: (b, 1, i, 0)),
                  pl.BlockSpec((1, tm, hw), lambda b, i: (b, i, 0)),
                  pl.BlockSpec((1, tm, 2 * hw), lambda b, i: (b, i, 0)),
                  pl.BlockSpec((1, C_DK), lambda b, i: (0, 0)),
                  pl.BlockSpec((1, hw), lambda b, i: (0, 0)),
                  pl.BlockSpec((2 * hw, d), lambda b, i: (0, 0)),
                  pl.BlockSpec((1, tm, d), lambda b, i: (b, i, 0)),
                  pl.BlockSpec((1, 1, 6, d), lambda b, i: (b, 0, 0, 0))],
        out_specs=pl.BlockSpec((1, tm, d), lambda b, i: (b, i, 0)),
        out_shape=jax.ShapeDtypeStruct(hh.shape, F32),
        input_output_aliases={7: 0},
        compiler_params=_cparams(("parallel", "parallel")),
        name="rec_out",
    )(o_scan, o_scan, z, gd, out_g.reshape(1, C_DK), gn_g.reshape(1, hw), w_out.astype(BF16), hh, mod)


def _att_layer(hh, mod, norm_g, w_in, q_g, k_g, sink, w_out, L):
    cos, sin = _rope_tables(L, hh.shape[1] - L, HEAD_DIM)
    qa, ka, va, qb, kb, vb = _att_project(hh, mod, norm_g, w_in, q_g, k_g, cos, sin, L)
    oa, ob = _attention(qa, ka, va, qb, kb, vb, sink, L)
    return _att_out(oa, ob, w_out, hh, mod, L)


def _rec_layer(hh, mod, norm_g, w_in, conv_w, a_log, dt_bias, out_g, gn_g, w_out, L):
    cos, sin = _rope_tables(L, hh.shape[1] - L, D_DK)
    qkv, z, qd, kd, vd, gd, gate = _rec_project(hh, mod, norm_g, w_in, a_log, dt_bias, cos, sin, L)
    qc, kc, vc = _rec_conv(qkv, conv_w, L)
    o_scan = _rec_scan(*_rec_intra(qc, kc, vc, qd, kd, vd, gate), L)
    return _rec_out(o_scan, z, gd, out_g, gn_g, w_out, hh, mod, L)


def kernel(x, c, ctx, c_ctx, mod_w, mod_b, norm1_g, norm2_g, att_w_in, att_q_norm, att_k_norm, att_sink, att_w_out,
           rec_w_in, rec_conv_w, rec_a_log, rec_dt_bias, rec_out_norm, rec_gn_g, rec_w_out,
           peer_w_q, peer_sub_keys, peer_u, peer_v, final_norm_g):
    L = x.shape[1]
    depth = mod_w.shape[0]
    mods = _modulation(c, c_ctx, mod_w, mod_b)
    hh = jnp.concatenate([x, ctx], axis=1)
    for layer in range(depth):
        last = layer == depth - 1
        i = layer // 2
        assert layer % 2 == 0 or last, "the recurrent layer is only implemented as the last layer (no context output)"
        if layer % 2 == 0:
            hh = _att_layer(hh, mods[layer], norm1_g[layer], att_w_in[i], att_q_norm[i], att_k_norm[i], att_sink[i],
                            att_w_out[i], L)
        else:
            hh = _rec_layer(hh, mods[layer], norm1_g[layer], rec_w_in[i], rec_conv_w[i], rec_a_log[i], rec_dt_bias[i],
                            rec_out_norm[i], rec_gn_g[i], rec_w_out[i], L)
        hh = _peer(hh, mods[layer], norm2_g[layer], peer_w_q[layer], peer_sub_keys[layer], peer_u, peer_v, layer,
                   final_norm_g, L, with_ctx=not last, final=last)
    return hh
```

```python
import functools
import math

import numpy as np
import jax
import jax.numpy as jnp
from jax import lax
from jax.experimental import pallas as pl
from jax.experimental.pallas import tpu as pltpu

F32 = jnp.float32
BF16 = jnp.bfloat16

GRID_W = 64
EPS = 1e-6
HEAD_DIM = 64
A_HEADS = 8
A_KV = 2
B_HEADS = 8
B_KV = 2
WINDOW = 128
ROPE_THETA = 10000.0
C_HEADS = 4
C_DK = 128
CONV_W = 5
CHUNK = 64
D_HEADS = 4
D_DK = 128
RET_DECAY_BASE = 5.0
PEER_HEADS = 8
N_KEYS = 128
PEER_TOPK = 16

LANES = 128
BF16_ROWS = 16
VMEM_LIMIT = 56 * 1024 * 1024

ROW_TILE = 256
ATT_TQ = 128
PEER_TM = 256
PEER_EB = 2048
PEER_SUB = 256

_NEG_INF = float("-inf")


def _cparams(sem, flags=None):
    return pltpu.CompilerParams(dimension_semantics=sem, vmem_limit_bytes=VMEM_LIMIT, flags=flags)


def _bdot(a, b):
    return jnp.dot(a.astype(BF16), b.astype(BF16), preferred_element_type=F32)


def _bdot_nt(a, b):
    return lax.dot_general(a.astype(BF16), b.astype(BF16), (((1,), (1,)), ((), ())), preferred_element_type=F32)


def _bdot_tn(a, b):
    return jnp.dot(a.T.astype(BF16), b.astype(BF16), preferred_element_type=F32)


def _split3(a):
    hi = a.astype(BF16)
    r1 = a - hi.astype(F32)
    mid = r1.astype(BF16)
    lo = (r1 - mid.astype(F32)).astype(BF16)
    return hi, mid, lo


def _dot3(a, b):
    a_hi, a_lo, _ = _split3(a)
    b_hi, b_lo, _ = _split3(b)
    return jnp.dot(jnp.concatenate([a_hi, a_lo, a_hi], axis=1), jnp.concatenate([b_hi, b_hi, b_lo], axis=0),
                   preferred_element_type=F32)


def _sigmoid(x):
    return 1.0 / (1.0 + jnp.exp(-x))


def _silu(x):
    return x * _sigmoid(x)


def _norm_mod(x, g, shift, scale):
    r = lax.rsqrt(jnp.mean(x * x, axis=-1, keepdims=True) + EPS)
    return (x * r * g) * (1.0 + scale) + shift


def _rope(x, cos, sin_signed, head_dim):
    quarter = head_dim // 4
    lane = lax.broadcasted_iota(jnp.int32, x.shape, 1)
    first = (lane % (2 * quarter)) < quarter
    partner = jnp.where(first, pltpu.roll(x, LANES - quarter, 1), pltpu.roll(x, quarter, 1))
    return x * cos + partner * sin_signed


def _mod_kernel(c_ref, w_ref, b_ref, o_ref):
    o_ref[0] = _bdot(_silu(c_ref[...]), w_ref[0]) + b_ref[0]


def _modulation(c, c_ctx, mod_w, mod_b):
    depth, d, n = mod_w.shape
    bsz = c.shape[0]
    rows = 16
    cc = jnp.zeros((rows, d), F32).at[:bsz].set(c).at[bsz].set(c_ctx)
    tn = 1536
    out = pl.pallas_call(
        _mod_kernel,
        grid=(depth, n // tn),
        in_specs=[pl.BlockSpec((rows, d), lambda l, j: (0, 0)),
                  pl.BlockSpec((1, d, tn), lambda l, j: (l, 0, j)),
                  pl.BlockSpec((1, 1, tn), lambda l, j: (l, 0, j))],
        out_specs=pl.BlockSpec((1, rows, tn), lambda l, j: (l, 0, j)),
        out_shape=jax.ShapeDtypeStruct((depth, rows, n), F32),
        compiler_params=_cparams(("arbitrary", "arbitrary")),
        name="modulation",
    )(cc, mod_w, mod_b.reshape(depth, 1, n))
    lat = out[:, :bsz].reshape(depth, bsz, 1, 6, d)
    ctx = jnp.broadcast_to(out[:, bsz].reshape(depth, 1, 1, 6, d), (depth, bsz, 1, 6, d))
    return jnp.concatenate([lat, ctx], axis=2)


def _rope_tables(L, LC, head_dim):
    quarter, half = head_dim // 4, head_dim // 2
    freqs = ROPE_THETA ** (-jnp.arange(quarter, dtype=F32) / quarter)
    lane = np.arange(LANES)
    within = lane % head_dim
    use_col = within >= half
    fidx = within % quarter
    sign = np.where((within % half) < quarter, -1.0, 1.0).astype(np.float32)
    t = jnp.arange(L, dtype=jnp.int32)
    row, col = (t // GRID_W).astype(F32), (t % GRID_W).astype(F32)
    pos = jnp.where(use_col[None, :], col[:, None], row[:, None])
    ang = pos * freqs[fidx][None, :]
    cos = jnp.concatenate([jnp.cos(ang), jnp.ones((LC, LANES), F32)], axis=0)
    sin = jnp.concatenate([jnp.sin(ang) * sign[None, :], jnp.zeros((LC, LANES), F32)], axis=0)
    return cos, sin


def _att_proj_kernel(lat_ref, ctx_ref, mod_ref, g_ref, w_ref, qg_ref, kg_ref, cos_ref, sin_ref, gm_ref,
                     qa_ref, ka_ref, va_ref, qb_ref, kb_ref, vb_ref, *, n_lat):
    m = mod_ref[0, 0]
    h = jnp.where(pl.program_id(1) < n_lat, lat_ref[0], ctx_ref[0])
    a = _norm_mod(h, g_ref[...], m[0:1], m[1:2])
    o = _bdot(a, w_ref[...])
    cos, sin = cos_ref[...], sin_ref[...]
    gm = gm_ref[...]

    def head_norm(x, gain):
        sq = x * x
        hi = sq.astype(BF16)
        lo = (sq - hi.astype(F32)).astype(BF16)
        ms = jnp.dot(hi, gm, preferred_element_type=F32) + jnp.dot(lo, gm, preferred_element_type=F32)
        return x * lax.rsqrt(ms + EPS) * gain

    def put(ref, first_head, x):
        ref[0, first_head] = x[:, :HEAD_DIM].astype(ref.dtype)
        ref[0, first_head + 1] = x[:, HEAD_DIM:].astype(ref.dtype)

    scale = HEAD_DIM ** -0.5
    qa_w = A_HEADS * HEAD_DIM
    kv_w = A_KV * HEAD_DIM
    off = 0
    for c in range(qa_w // LANES):
        x = o[:, off + c * LANES: off + (c + 1) * LANES]
        put(qa_ref, 2 * c, _rope(head_norm(x, qg_ref[...]), cos, sin, HEAD_DIM) * scale)
    off += qa_w
    put(ka_ref, 0, _rope(head_norm(o[:, off: off + kv_w], kg_ref[...]), cos, sin, HEAD_DIM))
    off += kv_w
    put(va_ref, 0, o[:, off: off + kv_w])
    off += kv_w
    for c in range(qa_w // LANES):
        x = o[:, off + c * LANES: off + (c + 1) * LANES]
        put(qb_ref, 2 * c, _rope(x, cos, sin, HEAD_DIM) * scale)
    off += qa_w
    put(kb_ref, 0, _rope(o[:, off: off + kv_w], cos, sin, HEAD_DIM))
    off += kv_w
    put(vb_ref, 0, o[:, off: off + kv_w])


def _token_specs(L, tm, d, ctx_off):
    n_lat = L // tm
    return [pl.BlockSpec((1, tm, d), lambda b, i: (b, jnp.minimum(i, n_lat - 1), 0)),
            pl.BlockSpec((1, tm, d), lambda b, i: (b, jnp.maximum(i - n_lat, 0) + ctx_off, 0))]


def _att_project(lat, ctx, ctx_off, mod, norm_g, w_in, q_g, k_g, cos, sin, L):
    bsz, _, d = lat.shape
    lt = cos.shape[0]
    tm = ROW_TILE
    n = w_in.shape[1]
    gm = jnp.asarray(np.kron(np.eye(LANES // HEAD_DIM), np.full((HEAD_DIM, HEAD_DIM), 1.0 / HEAD_DIM)), BF16)
    tile2 = lambda v: jnp.tile(v.reshape(1, HEAD_DIM), (1, LANES // HEAD_DIM))
    qshape = jax.ShapeDtypeStruct((bsz, A_HEADS, lt, HEAD_DIM), BF16)
    kshape = jax.ShapeDtypeStruct((bsz, A_KV, lt, HEAD_DIM), BF16)
    qspec = pl.BlockSpec((1, A_HEADS, tm, HEAD_DIM), lambda b, i: (b, 0, i, 0))
    kspec = pl.BlockSpec((1, A_KV, tm, HEAD_DIM), lambda b, i: (b, 0, i, 0))
    const = lambda shape: pl.BlockSpec(shape, lambda b, i: (0,) * len(shape))
    return pl.pallas_call(
        functools.partial(_att_proj_kernel, n_lat=L // tm),
        grid=(bsz, lt // tm),
        in_specs=_token_specs(L, tm, d, ctx_off) + [
                  pl.BlockSpec((1, 1, 6, d), lambda b, i: (b, i // (L // tm), 0, 0)),
                  const((1, d)), const((d, n)), const((1, LANES)), const((1, LANES)),
                  pl.BlockSpec((tm, LANES), lambda b, i: (i, 0)),
                  pl.BlockSpec((tm, LANES), lambda b, i: (i, 0)),
                  const((LANES, LANES))],
        out_specs=[qspec, kspec, kspec, qspec, kspec, kspec],
        out_shape=[qshape, kshape, kshape, qshape, kshape, kshape],
        compiler_params=_cparams(("parallel", "parallel")),
        name="att_project",
    )(lat, ctx, mod, norm_g.reshape(1, d), w_in.astype(BF16), tile2(q_g), tile2(k_g), cos, sin, gm)


def _softmax_pv(score_parts, value_parts, extra_logit=None):
    m = functools.reduce(jnp.maximum, [jnp.max(s, axis=-1, keepdims=True) for s in score_parts])
    if extra_logit is not None:
        m = jnp.maximum(m, extra_logit)
    l = 0.0
    acc = 0.0
    for s, v in zip(score_parts, value_parts):
        p = jnp.exp(s - m)
        l = l + jnp.sum(p, axis=-1, keepdims=True)
        acc = acc + jnp.dot(p.astype(BF16), v, preferred_element_type=F32)
    if extra_logit is not None:
        l = l + jnp.exp(extra_logit - m)
    return acc / l


def _att_kernel(sink_ref, qa_ref, ka_ref, va_ref, qb_ref, kb_ref, vb_ref, oa_ref, ob_ref, *, L, LC):
    i = pl.program_id(1)
    tq = ATT_TQ
    group = A_HEADS // A_KV
    band = tq + 2 * WINDOW

    def stacked_q(ref, kvh):
        return ref[0, kvh * group:(kvh + 1) * group].reshape(group * tq, HEAD_DIM)

    def put(ref, kvh, o):
        for g in range(group):
            hd = kvh * group + g
            ref[0, :, hd * HEAD_DIM:(hd + 1) * HEAD_DIM] = o[g * tq:(g + 1) * tq].astype(ref.dtype)

    def sink_col(kvh):
        return jnp.concatenate([jnp.full((tq, 1), sink_ref[kvh * group + g], F32) for g in range(group)], axis=0)

    @pl.when(i < L // tq)
    def _latent():
        for kvh in range(A_KV):
            q = stacked_q(qa_ref, kvh)
            put(oa_ref, kvh, _softmax_pv([_bdot_nt(q, ka_ref[0, kvh])], [va_ref[0, kvh]]))
            q = stacked_q(qb_ref, kvh)
            start = pl.multiple_of(jnp.clip((i - 1) * tq, 0, L - band), tq)
            s_band = _bdot_nt(q, kb_ref[0, kvh, pl.ds(start, band), :])
            qpos = i * tq + (lax.broadcasted_iota(jnp.int32, s_band.shape, 0) % tq)
            kpos = start + lax.broadcasted_iota(jnp.int32, s_band.shape, 1)
            s_band = jnp.where(jnp.abs(qpos - kpos) <= WINDOW, s_band, _NEG_INF)
            s_ctx = _bdot_nt(q, kb_ref[0, kvh, L:L + LC, :])
            put(ob_ref, kvh, _softmax_pv([s_band, s_ctx],
                                         [vb_ref[0, kvh, pl.ds(start, band), :], vb_ref[0, kvh, L:L + LC, :]],
                                         sink_col(kvh)))

    @pl.when(i >= L // tq)
    def _context():
        for kvh in range(A_KV):
            q = stacked_q(qa_ref, kvh)
            put(oa_ref, kvh, _softmax_pv([_bdot_nt(q, ka_ref[0, kvh, L:L + LC, :])], [va_ref[0, kvh, L:L + LC, :]]))
            q = stacked_q(qb_ref, kvh)
            put(ob_ref, kvh, _softmax_pv([_bdot_nt(q, kb_ref[0, kvh, L:L + LC, :])], [vb_ref[0, kvh, L:L + LC, :]],
                                         sink_col(kvh)))


def _attention(qa, ka, va, qb, kb, vb, sink, L):
    bsz, _, lt, _ = qa.shape
    tq = ATT_TQ
    qspec = pl.BlockSpec((1, A_HEADS, tq, HEAD_DIM), lambda b, i: (b, 0, i, 0))
    kspec = pl.BlockSpec((1, A_KV, lt, HEAD_DIM), lambda b, i: (b, 0, 0, 0))
    ospec = pl.BlockSpec((1, tq, A_HEADS * HEAD_DIM), lambda b, i: (b, i, 0))
    oshape = jax.ShapeDtypeStruct((bsz, lt, A_HEADS * HEAD_DIM), BF16)
    return pl.pallas_call(
        functools.partial(_att_kernel, L=L, LC=lt - L),
        grid=(bsz, lt // tq),
        in_specs=[pl.BlockSpec(memory_space=pltpu.SMEM), qspec, kspec, kspec, qspec, kspec, kspec],
        out_specs=[ospec, ospec],
        out_shape=[oshape, oshape],
        compiler_params=_cparams(("parallel", "parallel")),
        name="attention",
    )(sink, qa, ka, va, qb, kb, vb)


def _att_out_kernel(oa_ref, ob_ref, w_ref, lat_ref, ctx_ref, mod_ref, out_ref, *, n_lat):
    half = oa_ref.shape[-1]
    y = (jnp.dot(oa_ref[0], w_ref[:half, :], preferred_element_type=F32)
         + jnp.dot(ob_ref[0], w_ref[half:, :], preferred_element_type=F32))
    h = jnp.where(pl.program_id(1) < n_lat, lat_ref[0], ctx_ref[0])
    out_ref[0] = h + mod_ref[0, 0][2:3] * y


def _att_out(oa, ob, w_out, lat, ctx, ctx_off, mod, L):
    bsz, lt, half = oa.shape
    d = lat.shape[2]
    tm = ROW_TILE
    return pl.pallas_call(
        functools.partial(_att_out_kernel, n_lat=L // tm),
        grid=(bsz, lt // tm),
        in_specs=[pl.BlockSpec((1, tm, half), lambda b, i: (b, i, 0)),
                  pl.BlockSpec((1, tm, half), lambda b, i: (b, i, 0)),
                  pl.BlockSpec((2 * half, d), lambda b, i: (0, 0))] + _token_specs(L, tm, d, ctx_off) + [
                  pl.BlockSpec((1, 1, 6, d), lambda b, i: (b, i // (L // tm), 0, 0))],
        out_specs=pl.BlockSpec((1, tm, d), lambda b, i: (b, i, 0)),
        out_shape=jax.ShapeDtypeStruct((bsz, lt, d), F32),
        compiler_params=_cparams(("parallel", "parallel")),
        name="att_out",
    )(oa, ob, w_out.astype(BF16), lat, ctx, mod)


def _sort_network(n):
    pairs = []
    p = 1
    while p < n:
        k = p
        while k >= 1:
            for j in range(k % p, n - k, 2 * k):
                for i in range(min(k, n - j - k)):
                    if (i + j) // (2 * p) == (i + j + k) // (2 * p):
                        pairs.append((i + j, i + j + k))
            k //= 2
        p *= 2
    return pairs


def _pop_sorted(lists, n, singles=()):
    lists, singles = list(lists), list(singles)
    out = []
    for rnd in range(n):
        head = functools.reduce(jnp.maximum, [lists[0]] + singles)
        m = jnp.max(head, axis=0, keepdims=True)
        out.append(m)
        if rnd == n - 1:
            break
        hit = lists[0] == m
        depth = min(len(lists), n - rnd - 1)
        for d in range(depth):
            nxt = lists[d + 1] if d + 1 < len(lists) else _NEG_INF
            lists[d] = jnp.where(hit, nxt, lists[d])
        singles = [jnp.where(s == m, _NEG_INF, s) for s in singles]
    return out


def _top_values(x, n):
    slabs = [x[r * 8:(r + 1) * 8] for r in range(x.shape[0] // 8)]
    for i, j in _sort_network(len(slabs)):
        slabs[i], slabs[j] = jnp.maximum(slabs[i], slabs[j]), jnp.minimum(slabs[i], slabs[j])
    return _pop_sorted(slabs, n)


def _peer_route_kernel(h_ref, mod_ref, g_ref, wq_ref, keys_ref, at_ref, thr_ref, e0_ref, r1_ref, e1_ref, q_scr):
    m = mod_ref[0, 0]
    a = _norm_mod(h_ref[0], g_ref[...], m[3:4], m[4:5])
    at_ref[0] = a.T.astype(BF16)
    q_scr[...] = jnp.dot(a.astype(BF16), wq_ref[...], preferred_element_type=F32)
    k = PEER_TOPK
    n = k + 1

    def head(hd, carry):
        col = pl.multiple_of(hd * 2 * N_KEYS, 2 * N_KEYS)
        s0 = _bdot_nt(keys_ref[0], q_scr[:, pl.ds(col, N_KEYS)])
        s1 = _bdot_nt(keys_ref[1], q_scr[:, pl.ds(col + N_KEYS, N_KEYS)])
        tokens = s0.shape[1]
        top0 = _top_values(s0, n)
        top1 = _top_values(s1, n)
        first = jnp.concatenate(top0 + [jnp.full((24 - n, tokens), _NEG_INF, F32)], axis=0)
        best = _pop_sorted([first[0:8] + t for t in top1], n, [first[8:16] + top1[0], first[16:24] + top1[0]])
        z = functools.reduce(lambda x, y: x + y, [jnp.exp(b - best[0]) for b in best[:k]])
        tau = 0.5 * (best[k - 1] + best[k])
        r1 = jnp.zeros_like(s1)
        need = jnp.full_like(s0, n + 1.0)
        for b in range(k):
            r1 = jnp.where(s1 >= top1[k - 1 - b], b + 2.0, r1)
            need = jnp.where(s0 >= tau - top1[b], float(n - b), need)
        thr_ref[0, hd] = need
        e0_ref[0, hd] = jnp.exp(s0 - top0[0])
        r1_ref[0, hd] = pltpu.bitcast(r1.astype(BF16), jnp.uint32)
        e1_ref[0, hd] = pltpu.bitcast((jnp.exp(s1 - top1[0]) / z).astype(BF16), jnp.uint32)
        return carry

    lax.fori_loop(0, PEER_HEADS, head, 0, unroll=True)


def _peer_route(hh, mod, norm_g, w_q, sub_keys, L):
    bsz, lt, d = hh.shape
    tm = ROW_TILE
    nq = w_q.shape[1]
    tok = lambda: pl.BlockSpec((1, PEER_HEADS, N_KEYS, tm), lambda b, i: (b, 0, 0, i))
    tshape = jax.ShapeDtypeStruct((bsz, PEER_HEADS, N_KEYS, lt), F32)
    pshape = jax.ShapeDtypeStruct((bsz, PEER_HEADS, N_KEYS // 2, lt), jnp.uint32)
    ptok = pl.BlockSpec((1, PEER_HEADS, N_KEYS // 2, tm), lambda b, i: (b, 0, 0, i))
    return pl.pallas_call(
        _peer_route_kernel,
        grid=(bsz, lt // tm),
        in_specs=[pl.BlockSpec((1, tm, d), lambda b, i: (b, i, 0)),
                  pl.BlockSpec((1, 1, 6, d), lambda b, i: (b, i // (L // tm), 0, 0)),
                  pl.BlockSpec((1, d), lambda b, i: (0, 0)),
                  pl.BlockSpec((d, nq), lambda b, i: (0, 0)),
                  pl.BlockSpec((2, N_KEYS, N_KEYS), lambda b, i: (0, 0, 0))],
        out_specs=[pl.BlockSpec((1, d, tm), lambda b, i: (b, 0, i)), tok(), tok(), ptok, ptok],
        out_shape=[jax.ShapeDtypeStruct((bsz, d, lt), BF16), tshape, tshape, pshape, pshape],
        scratch_shapes=[pltpu.VMEM((tm, nq), F32)],
        compiler_params=_cparams(("parallel", "parallel")),
        name="peer_route",
    )(hh, mod, norm_g.reshape(1, d), w_q.astype(BF16), sub_keys.astype(BF16))


def _gelu(x):
    return 0.5 * x * (1.0 + lax.erf(x * (1.0 / math.sqrt(2.0))))


def _peer_expert_kernel(a_ref, an_ref, thr_ref, e0_ref, r1_ref, e1_ref, u_ref, vt_ref, h_ref, mod_ref, fg_ref,
                        out_ref, yt_scr, wa_scr, at0_scr, at1_scr, *, final):
    e = pl.program_id(1)
    t = pl.program_id(2)
    tm = a_ref.shape[2]
    n_sub = PEER_EB // PEER_SUB
    rows = [slice(q * PEER_SUB, (q + 1) * PEER_SUB) for q in range(n_sub)]

    def activations(dst, src, q):
        words = slice(rows[q].start // 2, rows[q].stop // 2)
        dst[rows[q], :] = jnp.dot(pltpu.bitcast(u_ref[words, :], BF16), src[0], preferred_element_type=F32)

    @pl.when(e == 0)
    def _():
        yt_scr[t] = jnp.zeros(yt_scr.shape[1:], F32)

    @pl.when(t == 0)
    def _():
        for q in range(n_sub):
            activations(at0_scr, a_ref, q)

    def step(act_ref, next_ref):
        y = None
        for q in range(n_sub):
            activations(next_ref, an_ref, q)
            for il in range(PEER_SUB // N_KEYS):
                key0 = q * (PEER_SUB // N_KEYS) + il
                for c in range(tm // LANES):
                    cs = slice(c * LANES, (c + 1) * LANES)
                    row = lambda ref, hd: jnp.broadcast_to(ref[0, hd, key0:key0 + 1, cs], (BF16_ROWS, LANES)).astype(BF16)
                    thr = [row(thr_ref, hd) for hd in range(PEER_HEADS)]
                    e0 = [row(e0_ref, hd) for hd in range(PEER_HEADS)]
                    zero = jnp.zeros((BF16_ROWS, LANES), BF16)
                    for r in range(N_KEYS // BF16_ROWS):
                        ws = slice(r * BF16_ROWS // 2, (r + 1) * BF16_ROWS // 2)
                        gate = None
                        for hd in range(PEER_HEADS):
                            rank = pltpu.bitcast(r1_ref[0, hd, ws, cs], BF16)
                            weight = pltpu.bitcast(e1_ref[0, hd, ws, cs], BF16)
                            part = jnp.where(rank >= thr[hd], weight * e0[hd], zero)
                            gate = part if gate is None else gate + part
                        ars = slice(rows[q].start + il * N_KEYS + r * BF16_ROWS,
                                    rows[q].start + il * N_KEYS + (r + 1) * BF16_ROWS)
                        wa_scr[ars, cs] = gate * _gelu(act_ref[ars, cs]).astype(BF16)
            part = jnp.dot(pltpu.bitcast(vt_ref[:, rows[q]], BF16), wa_scr[rows[q], :], preferred_element_type=F32)
            y = part if y is None else y + part
        yt_scr[t] += y

    @pl.when(t % 2 == 0)
    def _():
        step(at0_scr, at1_scr)

    @pl.when(t % 2 == 1)
    def _():
        step(at1_scr, at0_scr)

    @pl.when(e == pl.num_programs(1) - 1)
    def _():
        hn = h_ref[0] + mod_ref[0, 0][5:6] * yt_scr[t].T
        if final:
            hn = hn * lax.rsqrt(jnp.mean(hn * hn, axis=-1, keepdims=True) + EPS) * fg_ref[...]
        out_ref[0] = hn


def _peer_expert(hh, mod, a, thr, e0, r1, e1, u_bf, vt_bf, final_g, L, *, with_ctx, final):
    bsz, lt, d = hh.shape
    n_exp = vt_bf.shape[1]
    tm = PEER_TM
    n_tiles = (lt if with_ctx else L) // tm
    keys_per_step = PEER_EB // N_KEYS
    n_blocks = n_exp // PEER_EB
    tok = lambda r: pl.BlockSpec((1, PEER_HEADS, r, tm), (lambda b, e, t: (b, 0, e, t)) if r == keys_per_step
                                 else (lambda b, e, t: (b, 0, 0, t)))
    row_spec = pl.BlockSpec((1, tm, d), lambda b, e, t: (b, jnp.where(e == n_blocks - 1, t, 0), 0))
    if final:
        out_shape = jax.ShapeDtypeStruct((bsz, L, d), F32)
        aliases = {}
    else:
        out_shape = jax.ShapeDtypeStruct(hh.shape, F32)
        aliases = {8: 0}
    return pl.pallas_call(
        functools.partial(_peer_expert_kernel, final=final),
        grid=(bsz, n_blocks, n_tiles),
        in_specs=[pl.BlockSpec((1, d, tm), lambda b, e, t: (b, 0, t)),
                  pl.BlockSpec((1, d, tm), lambda b, e, t: (b, 0, jnp.minimum(t + 1, n_tiles - 1))),
                  tok(keys_per_step), tok(keys_per_step), tok(N_KEYS // 2), tok(N_KEYS // 2),
                  pl.BlockSpec((PEER_EB // 2, d), lambda b, e, t: (e, 0)),
                  pl.BlockSpec((d // 2, PEER_EB), lambda b, e, t: (0, e)),
                  row_spec,
                  pl.BlockSpec((1, 1, 6, d), lambda b, e, t: (b, t // (L // tm), 0, 0)),
                  pl.BlockSpec((1, d), lambda b, e, t: (0, 0))],
        out_specs=row_spec,
        out_shape=out_shape,
        input_output_aliases=aliases,
        scratch_shapes=[pltpu.VMEM((n_tiles, d, tm), F32), pltpu.VMEM((PEER_EB, tm), BF16),
                        pltpu.VMEM((PEER_EB, tm), F32), pltpu.VMEM((PEER_EB, tm), F32)],
        compiler_params=_cparams(("parallel", "arbitrary", "arbitrary")),
        name="peer_expert",
    )(a, a, thr, e0, r1, e1, u_bf, vt_bf, hh, mod, final_g.reshape(1, d))


def _pack_kernel(x_ref, o_ref, *, transpose):
    x = x_ref[0]
    if transpose:
        x = x.T
    o_ref[...] = pltpu.bitcast(x.astype(BF16), jnp.uint32)


def _pack_row_pairs(x, layer, *, transpose):
    _, r, c = x.shape
    blk = 512
    if transpose:
        out_shape, out_spec = (c // 2, r), pl.BlockSpec((c // 2, blk), lambda i: (0, i))
    else:
        out_shape, out_spec = (r // 2, c), pl.BlockSpec((blk // 2, c), lambda i: (i, 0))
    return pl.pallas_call(
        functools.partial(_pack_kernel, transpose=transpose),
        grid=(r // blk,),
        in_specs=[pl.BlockSpec((1, blk, c), lambda i: (layer, i, 0))],
        out_specs=out_spec,
        out_shape=jax.ShapeDtypeStruct(out_shape, jnp.uint32),
        compiler_params=_cparams(("parallel",)),
        name="pack_pairs_t" if transpose else "pack_pairs",
    )(x)


def _peer(hh, mod, norm_g, w_q, sub_keys, u_all, v_all, layer, final_g, L, *, with_ctx, final):
    a, thr, e0, r1, e1 = _peer_route(hh, mod, norm_g, w_q, sub_keys, L)
    return _peer_expert(hh, mod, a, thr, e0, r1, e1, _pack_row_pairs(u_all, layer, transpose=False),
                        _pack_row_pairs(v_all, layer, transpose=True), final_g, L, with_ctx=with_ctx, final=final)


REC_SCAN_CHUNKS = 4
REC_CHUNKS_PER_STEP = 2
REC_QKV = C_HEADS * 3 * C_DK
REC_Z = C_HEADS * C_DK
REC_GATES = 4 * C_HEADS
REC_HD = D_HEADS * D_DK


def _rec_proj_kernel(h_ref, mod_ref, g_ref, w_ref, alog_ref, dtb_ref, cos_ref, sin_ref,
                     qkv_ref, z_ref, qd_ref, kd_ref, vd_ref, gd_ref, gate_ref):
    m = mod_ref[0, 0]
    a = _norm_mod(h_ref[0], g_ref[...], m[0:1], m[1:2])
    o = _bdot(a, w_ref[...])
    cos, sin = cos_ref[...], sin_ref[...]
    off = 0
    qkv_ref[0] = o[:, off:off + REC_QKV]
    off += REC_QKV
    z_ref[0] = o[:, off:off + REC_Z]
    off += REC_Z
    for hd in range(D_HEADS):
        qd_ref[0, :, hd * D_DK:(hd + 1) * D_DK] = _rope(o[:, off + hd * D_DK: off + (hd + 1) * D_DK], cos, sin, D_DK)
    off += REC_HD
    for hd in range(D_HEADS):
        kd_ref[0, :, hd * D_DK:(hd + 1) * D_DK] = (
            _rope(o[:, off + hd * D_DK: off + (hd + 1) * D_DK], cos, sin, D_DK) * (D_DK ** -0.5))
    off += REC_HD
    vd_ref[0] = o[:, off:off + REC_HD]
    off += REC_HD
    gd_ref[0] = o[:, off:off + 2 * REC_HD]
    off += 2 * REC_HD
    x = o[:, off:off + LANES]
    lane = lax.broadcasted_iota(jnp.int32, x.shape, 1)
    xb = x + dtb_ref[...]
    softplus = jnp.maximum(xb, 0.0) + jnp.log(1.0 + jnp.exp(-jnp.abs(xb)))
    gate_ref[0] = jnp.where(lane < 2 * C_HEADS, -jnp.exp(alog_ref[...]) * softplus, _sigmoid(x))


def _rec_project(hh, mod, norm_g, w_in, a_log, dt_bias, cos, sin, L):
    bsz, lt, d = hh.shape
    tm = ROW_TILE
    parts = np.cumsum([REC_QKV, REC_Z, REC_GATES, REC_HD, REC_HD, REC_HD])
    qkv_w, z_w, gates_w, qd_w, kd_w, vd_w, gd_w = jnp.split(w_in, [int(p) for p in parts], axis=1)
    w = jnp.concatenate([qkv_w, z_w, qd_w, kd_w, vd_w, gd_w, gates_w, jnp.zeros((d, LANES - REC_GATES), F32)],
                        axis=1).astype(BF16)
    n = w.shape[1]
    pad = lambda p: jnp.zeros((1, LANES), F32).at[0, :2 * C_HEADS].set(p.reshape(-1))
    widths = [REC_QKV, REC_Z, REC_HD, REC_HD, REC_HD, 2 * REC_HD, LANES]
    const = lambda shape: pl.BlockSpec(shape, lambda b, i: (0,) * len(shape))
    return pl.pallas_call(
        _rec_proj_kernel,
        grid=(bsz, lt // tm),
        in_specs=[pl.BlockSpec((1, tm, d), lambda b, i: (b, i, 0)),
                  pl.BlockSpec((1, 1, 6, d), lambda b, i: (b, i // (L // tm), 0, 0)),
                  const((1, d)), const((d, n)), const((1, LANES)), const((1, LANES)),
                  pl.BlockSpec((tm, LANES), lambda b, i: (i, 0)),
                  pl.BlockSpec((tm, LANES), lambda b, i: (i, 0))],
        out_specs=[pl.BlockSpec((1, tm, wd), lambda b, i: (b, i, 0)) for wd in widths],
        out_shape=[jax.ShapeDtypeStruct((bsz, lt, wd), F32) for wd in widths],
        compiler_params=_cparams(("parallel", "parallel")),
        name="rec_project",
    )(hh, mod, norm_g.reshape(1, d), w, pad(a_log), pad(dt_bias), cos, sin)


def _rec_conv_kernel(x_ref, prev_ref, next_ref, w_ref, q_ref, k_ref, v_ref, *, L):
    i = pl.program_id(1)
    tl = x_ref.shape[1]
    n_lat = L // tl
    at_start = (i == 0) | (i == n_lat)
    at_end = (i == n_lat - 1) | (i == pl.num_programs(1) - 1)
    prev = jnp.where(at_start, 0.0, prev_ref[0])
    nxt = jnp.where(at_end, 0.0, next_ref[0])
    xx = jnp.concatenate([prev, x_ref[0], nxt], axis=0)
    n = tl + 16
    acc = 0.0
    for tap in range(CONV_W):
        shift = (CONV_W // 2 - tap) % n
        shifted = xx if shift == 0 else pltpu.roll(xx, shift, 0)
        acc = acc + shifted[8:8 + tl] * w_ref[tap:tap + 1, :]
    y = _silu(acc)
    hw = C_HEADS * C_DK
    for hd in range(C_HEADS):
        def l2(x):
            return x * lax.rsqrt(jnp.sum(x * x, axis=-1, keepdims=True) + EPS)
        sl = slice(hd * C_DK, (hd + 1) * C_DK)
        q_ref[0, :, sl] = l2(y[:, hd * C_DK:(hd + 1) * C_DK]) * (C_DK ** -0.5)
        k_ref[0, :, sl] = l2(y[:, hw + hd * C_DK: hw + (hd + 1) * C_DK])
    v_ref[0] = y[:, 2 * hw:]


def _rec_conv(qkv, conv_w, L):
    bsz, lt, ch = qkv.shape
    tl = ROW_TILE
    hb = tl // 8
    last = lt // 8 - 1
    hw = C_HEADS * C_DK
    return pl.pallas_call(
        functools.partial(_rec_conv_kernel, L=L),
        grid=(bsz, lt // tl),
        in_specs=[pl.BlockSpec((1, tl, ch), lambda b, i: (b, i, 0)),
                  pl.BlockSpec((1, 8, ch), lambda b, i: (b, jnp.maximum(i * hb - 1, 0), 0)),
                  pl.BlockSpec((1, 8, ch), lambda b, i: (b, jnp.minimum((i + 1) * hb, last), 0)),
                  pl.BlockSpec((8, ch), lambda b, i: (0, 0))],
        out_specs=[pl.BlockSpec((1, tl, hw), lambda b, i: (b, i, 0))] * 3,
        out_shape=[jax.ShapeDtypeStruct((bsz, lt, hw), F32)] * 3,
        compiler_params=_cparams(("parallel", "parallel")),
        name="rec_conv",
    )(qkv, qkv, qkv, jnp.zeros((8, ch), F32).at[:CONV_W].set(conv_w))


def _ret_log_gamma(hd):
    return float(np.log1p(-np.exp2(-(RET_DECAY_BASE + hd))))


def _rec_intra_kernel(qc_ref, kc_ref, vc_ref, qd_ref, kd_ref, vd_ref, gate_ref,
                      eq_ref, qk_ref, kt_ref, sol_ref, vr_ref, al_ref):
    c = CHUNK
    ci = lax.broadcasted_iota(jnp.int32, (c, c), 0)
    si = lax.broadcasted_iota(jnp.int32, (c, c), 1)
    pos = lax.broadcasted_iota(jnp.int32, (c, 1), 0).astype(F32)
    ones_row = jnp.ones((1, LANES), F32)
    combos = []
    for cc in range(REC_CHUNKS_PER_STEP):
        _rec_intra_setup(cc, combos, ci, si, pos, ones_row, qc_ref, kc_ref, vc_ref, qd_ref, kd_ref, vd_ref, gate_ref,
                         eq_ref, qk_ref, kt_ref, vr_ref, al_ref)
    _rec_intra_solve(combos, ones_row, eq_ref, qk_ref, kt_ref, sol_ref, al_ref)


def _rec_intra_setup(cc, combos, ci, si, pos, ones_row, qc_ref, kc_ref, vc_ref, qd_ref, kd_ref, vd_ref, gate_ref,
                     eq_ref, qk_ref, kt_ref, vr_ref, al_ref):
    c = CHUNK
    rs = slice(cc * c, (cc + 1) * c)
    gates = gate_ref[0, rs]
    la_parts = _split3(gates)
    for d in range(2):
        incl = (si <= ci) if d == 0 else (si >= ci)
        strict = (si < ci) if d == 0 else (si > ci)
        tri = incl.astype(BF16)
        g_all = jnp.dot(jnp.concatenate([tri] * 3, axis=1), jnp.concatenate(la_parts, axis=0),
                        preferred_element_type=F32)
        g_all_t = g_all.T
        last = c - 1 if d == 0 else 0
        for hd in range(C_HEADS):
            col = d * C_HEADS + hd
            sl = slice(hd * C_DK, (hd + 1) * C_DK)
            q, k, v = qc_ref[0, rs, sl], kc_ref[0, rs, sl], vc_ref[0, rs, sl]
            g_rows = jnp.broadcast_to(g_all[:, col:col + 1], (c, C_DK))
            beta = jnp.broadcast_to(gates[:, 2 * C_HEADS + col: 2 * C_HEADS + col + 1], (c, C_DK))
            grow = g_all_t[col:col + 1, :]
            glast = g_all[last:last + 1, col:col + 1]
            exp_g = jnp.exp(g_rows)
            diff = g_rows[:, :c] - grow
            dec_strict = jnp.exp(jnp.where(strict, diff, _NEG_INF))
            combos.append(dict(
                cc=cc, d=d, hd=hd, q=q, exp_g=exp_g, glast=glast,
                x=-(beta[:, :c] * _bdot_nt(k, k) * dec_strict),
                sol=jnp.concatenate([beta * v, (beta * exp_g) * k], axis=1),
                qk=_bdot_nt(q, k) * jnp.exp(jnp.where(incl, diff, _NEG_INF)),
                kend=k * jnp.exp(glast - g_rows)))
    for d in range(2):
        incl = (si <= ci) if d == 0 else (si >= ci)
        for hd in range(D_HEADS):
            lg = _ret_log_gamma(hd)
            sl = slice(hd * D_DK, (hd + 1) * D_DK)
            q, k, v = qd_ref[0, rs, sl], kd_ref[0, rs, sl], vd_ref[0, rs, sl]
            steps = pos if d == 0 else (c - 1.0) - pos
            dist = (ci - si) if d == 0 else (si - ci)
            dmat = jnp.exp(jnp.where(incl, lg * dist.astype(F32), _NEG_INF))
            qk = _bdot_nt(q, k) * dmat
            eq_ref[0, cc, d, C_HEADS + hd] = (q * jnp.exp(lg * (steps + 1.0))).astype(BF16)
            qk_ref[0, cc, d, C_HEADS + hd] = qk.astype(BF16)
            kt_ref[0, cc, d, C_HEADS + hd] = (k * jnp.exp(lg * ((c - 1.0) - steps))).T.astype(BF16)
            vr_ref[0, cc, d, hd] = v.astype(BF16)
            al_ref[0, cc, d, C_HEADS + hd:C_HEADS + hd + 1, :] = math.exp(lg * c) * ones_row


def _rec_intra_solve(combos, ones_row, eq_ref, qk_ref, kt_ref, sol_ref, al_ref):
    c = CHUNK
    for cb in combos:
        cc, d, hd = cb["cc"], cb["d"], cb["hd"]
        eq_ref[0, cc, d, hd] = (cb["exp_g"] * cb["q"]).astype(BF16)
        qk_ref[0, cc, d, hd] = cb["qk"].astype(BF16)
        kt_ref[0, cc, d, hd] = cb["kend"].T.astype(BF16)
        al_ref[0, cc, d, hd:hd + 1, :] = jnp.exp(cb["glast"]) * ones_row
    levels = int(math.log2(c))
    for lvl in range(levels):
        for cb in combos:
            cb["sol"] = cb["sol"] + _dot3(cb["x"], cb["sol"])
        if lvl < levels - 1:
            for cb in combos:
                cb["x"] = _dot3(cb["x"], cb["x"])
    for cb in combos:
        sol_ref[0, cb["cc"], cb["d"], cb["hd"]] = cb["sol"].astype(BF16)


def _rec_intra(qc, kc, vc, qd, kd, vd, gate):
    bsz, lt, hw = qc.shape
    nc = lt // CHUNK
    nh = C_HEADS + D_HEADS
    per = REC_CHUNKS_PER_STEP
    row = lambda w: pl.BlockSpec((1, per * CHUNK, w), lambda b, n: (b, n, 0))
    lead = lambda *tail: pl.BlockSpec((1, per, 2) + tail, lambda b, n: (b, n, 0) + (0,) * len(tail))
    return pl.pallas_call(
        _rec_intra_kernel,
        grid=(bsz, nc // per),
        in_specs=[row(hw)] * 6 + [row(LANES)],
        out_specs=[lead(nh, CHUNK, C_DK), lead(nh, CHUNK, CHUNK), lead(nh, C_DK, CHUNK), lead(C_HEADS, CHUNK, 2 * C_DK),
                   lead(D_HEADS, CHUNK, C_DK), lead(nh, LANES)],
        out_shape=[jax.ShapeDtypeStruct((bsz, nc, 2, nh, CHUNK, C_DK), BF16),
                   jax.ShapeDtypeStruct((bsz, nc, 2, nh, CHUNK, CHUNK), BF16),
                   jax.ShapeDtypeStruct((bsz, nc, 2, nh, C_DK, CHUNK), BF16),
                   jax.ShapeDtypeStruct((bsz, nc, 2, C_HEADS, CHUNK, 2 * C_DK), BF16),
                   jax.ShapeDtypeStruct((bsz, nc, 2, D_HEADS, CHUNK, C_DK), BF16),
                   jax.ShapeDtypeStruct((bsz, nc, 2, nh, LANES), F32)],
        compiler_params=_cparams(("parallel", "parallel")),
        name="rec_intra",
    )(qc, kc, vc, qd, kd, vd, gate)


def _rec_scan_kernel(eq_ref, qk_ref, kt_ref, sol_ref, vr_ref, al_ref, o_ref, s_scr):
    @pl.when(pl.program_id(2) == 0)
    def _():
        s_scr[...] = jnp.zeros_like(s_scr)

    dot = functools.partial(jnp.dot, preferred_element_type=F32)
    heads = range(C_HEADS + D_HEADS)

    def run(order):
        s = [s_scr[hd] for hd in heads]
        for cc in order:
            sb = [x.astype(BF16) for x in s]
            wks = [dot(sol_ref[0, cc, 0, hd, :, C_DK:], sb[hd]) for hd in range(C_HEADS)]
            cross = [dot(eq_ref[0, cc, 0, hd], sb[hd]) for hd in heads]
            w = [(sol_ref[0, cc, 0, hd, :, :C_DK].astype(F32) - wks[hd]).astype(BF16) for hd in range(C_HEADS)]
            w += [vr_ref[0, cc, 0, hd] for hd in range(D_HEADS)]
            inner = [dot(qk_ref[0, cc, 0, hd], w[hd]) for hd in heads]
            update = [dot(kt_ref[0, cc, 0, hd], w[hd]) for hd in heads]
            for hd in heads:
                o_ref[0, 0, cc * CHUNK:(cc + 1) * CHUNK, hd * C_DK:(hd + 1) * C_DK] = cross[hd] + inner[hd]
            s = [al_ref[0, cc, 0, hd:hd + 1, :] * s[hd] + update[hd] for hd in heads]
        for hd in heads:
            s_scr[hd] = s[hd]

    per = eq_ref.shape[1]

    @pl.when(pl.program_id(1) == 0)
    def _():
        run(range(per))

    @pl.when(pl.program_id(1) == 1)
    def _():
        run(range(per - 1, -1, -1))


def _rec_scan(eq, qk, kt, sol, vr, al, L):
    bsz, nc = eq.shape[:2]
    nh = C_HEADS + D_HEADS
    n_lat = L // CHUNK
    n_ctx = nc - n_lat

    per = REC_SCAN_CHUNKS
    n_lat, n_ctx, nb = n_lat // per, n_ctx // per, nc // per

    def chunk(d, s):
        fwd = jnp.where(s < n_ctx, n_lat + s, s - n_ctx)
        bwd = nb - 1 - s
        return jnp.where(d == 0, fwd, bwd)

    lead = lambda *tail: pl.BlockSpec((1, per, 1) + tail, lambda b, d, s: (b, chunk(d, s), d) + (0,) * len(tail))
    return pl.pallas_call(
        _rec_scan_kernel,
        grid=(bsz, 2, nb),
        in_specs=[lead(nh, CHUNK, C_DK), lead(nh, CHUNK, CHUNK), lead(nh, C_DK, CHUNK), lead(C_HEADS, CHUNK, 2 * C_DK),
                  lead(D_HEADS, CHUNK, C_DK), lead(nh, LANES)],
        out_specs=pl.BlockSpec((1, 1, per * CHUNK, nh * C_DK), lambda b, d, s: (b, d, chunk(d, s), 0)),
        out_shape=jax.ShapeDtypeStruct((bsz, 2, nc * CHUNK, nh * C_DK), F32),
        scratch_shapes=[pltpu.VMEM((nh, C_DK, C_DK), F32)],
        compiler_params=_cparams(("parallel", "parallel", "arbitrary")),
        name="rec_scan",
    )(eq, qk, kt, sol, vr, al)


def _rec_out_kernel(of_ref, ob_ref, z_ref, gd_ref, og_ref, gn_ref, w_ref, h_ref, mod_ref, out_ref):
    hw = C_HEADS * C_DK
    of, ob = of_ref[0, 0], ob_ref[0, 0]
    z, gd = z_ref[0], gd_ref[0]
    parts = []
    for hd in range(C_HEADS):
        sl = slice(hd * C_DK, (hd + 1) * C_DK)
        x = of[:, sl] + ob[:, sl]
        y = x * lax.rsqrt(jnp.mean(x * x, axis=-1, keepdims=True) + EPS) * og_ref[...]
        parts.append(y * _silu(z[:, sl]))
    for hd in range(D_HEADS):
        sl = slice(hd * D_DK, (hd + 1) * D_DK)
        y = 0.0
        for d, o in enumerate((of, ob)):
            x = o[:, hw + hd * D_DK: hw + (hd + 1) * D_DK]
            mu = jnp.mean(x, axis=-1, keepdims=True)
            xc = x - mu
            var = jnp.mean(xc * xc, axis=-1, keepdims=True)
            y = y + xc * lax.rsqrt(var + EPS) * gn_ref[:, sl] * _silu(gd[:, d * hw + hd * D_DK: d * hw + (hd + 1) * D_DK])
        parts.append(y)
    mix = jnp.concatenate(parts, axis=1)
    out_ref[0] = h_ref[0] + mod_ref[0, 0][2:3] * _bdot(mix, w_ref[...])


def _rec_out(o_scan, z, gd, out_g, gn_g, w_out, hh, mod, L):
    bsz, lt, d = hh.shape
    tm = ROW_TILE
    hw = C_HEADS * C_DK
    return pl.pallas_call(
        _rec_out_kernel,
        grid=(bsz, L // tm),
        in_specs=[pl.BlockSpec((1, 1, tm, 2 * hw), lambda b, i: (b, 0, i, 0)),
                  pl.BlockSpec((1, 1, tm, 2 * hw), lambda b, i: (b, 1, i, 0)),
                  pl.BlockSpec((1, tm, hw), lambda b, i: (b, i, 0)),
                  pl.BlockSpec((1, tm, 2 * hw), lambda b, i: (b, i, 0)),
                  pl.BlockSpec((1, C_DK), lambda b, i: (0, 0)),
                  pl.BlockSpec((1, hw), lambda b, i: (0, 0)),
                  pl.BlockSpec((2 * hw, d), lambda b, i: (0, 0)),
                  pl.BlockSpec((1, tm, d), lambda b, i: (b, i, 0)),
                  pl.BlockSpec((1, 1, 6, d), lambda b, i: (b, 0, 0, 0))],
        out_specs=pl.BlockSpec((1, tm, d), lambda b, i: (b, i, 0)),
        out_shape=jax.ShapeDtypeStruct(hh.shape, F32),
        input_output_aliases={7: 0},
        compiler_params=_cparams(("parallel", "parallel")),
        name="rec_out",
    )(o_scan, o_scan, z, gd, out_g.reshape(1, C_DK), gn_g.reshape(1, hw), w_out.astype(BF16), hh, mod)


def _att_layer(lat, ctx, ctx_off, lc, mod, norm_g, w_in, q_g, k_g, sink, w_out, L):
    cos, sin = _rope_tables(L, lc, HEAD_DIM)
    qa, ka, va, qb, kb, vb = _att_project(lat, ctx, ctx_off, mod, norm_g, w_in, q_g, k_g, cos, sin, L)
    oa, ob = _attention(qa, ka, va, qb, kb, vb, sink, L)
    return _att_out(oa, ob, w_out, lat, ctx, ctx_off, mod, L)


def _rec_layer(hh, mod, norm_g, w_in, conv_w, a_log, dt_bias, out_g, gn_g, w_out, L):
    cos, sin = _rope_tables(L, hh.shape[1] - L, D_DK)
    qkv, z, qd, kd, vd, gd, gate = _rec_project(hh, mod, norm_g, w_in, a_log, dt_bias, cos, sin, L)
    qc, kc, vc = _rec_conv(qkv, conv_w, L)
    o_scan = _rec_scan(*_rec_intra(qc, kc, vc, qd, kd, vd, gate), L)
    return _rec_out(o_scan, z, gd, out_g, gn_g, w_out, hh, mod, L)


def kernel(x, c, ctx, c_ctx, mod_w, mod_b, norm1_g, norm2_g, att_w_in, att_q_norm, att_k_norm, att_sink, att_w_out,
           rec_w_in, rec_conv_w, rec_a_log, rec_dt_bias, rec_out_norm, rec_gn_g, rec_w_out,
           peer_w_q, peer_sub_keys, peer_u, peer_v, final_norm_g):
    L = x.shape[1]
    depth = mod_w.shape[0]
    mods = _modulation(c, c_ctx, mod_w, mod_b)
    lc = ctx.shape[1]
    hh = None
    for layer in range(depth):
        last = layer == depth - 1
        i = layer // 2
        assert layer % 2 == 0 or last, "the recurrent layer is only implemented as the last layer (no context output)"
        if layer % 2 == 0:
            lat, cx, ctx_off = (x, ctx, 0) if layer == 0 else (hh, hh, L // ROW_TILE)
            hh = _att_layer(lat, cx, ctx_off, lc, mods[layer], norm1_g[layer], att_w_in[i], att_q_norm[i],
                            att_k_norm[i], att_sink[i], att_w_out[i], L)
        else:
            hh = _rec_layer(hh, mods[layer], norm1_g[layer], rec_w_in[i], rec_conv_w[i], rec_a_log[i], rec_dt_bias[i],
                            rec_out_norm[i], rec_gn_g[i], rec_w_out[i], L)
        hh = _peer(hh, mods[layer], norm2_g[layer], peer_w_q[layer], peer_sub_keys[layer], peer_u, peer_v, layer,
                   final_norm_g, L, with_ctx=not last, final=last)
    return hh
```

```python
import functools
import math

import numpy as np
import jax
import jax.numpy as jnp
from jax import lax
from jax.experimental import pallas as pl
from jax.experimental.pallas import tpu as pltpu

F32 = jnp.float32
BF16 = jnp.bfloat16

GRID_W = 64
EPS = 1e-6
HEAD_DIM = 64
A_HEADS = 8
A_KV = 2
B_HEADS = 8
B_KV = 2
WINDOW = 128
ROPE_THETA = 10000.0
C_HEADS = 4
C_DK = 128
CONV_W = 5
CHUNK = 64
D_HEADS = 4
D_DK = 128
RET_DECAY_BASE = 5.0
PEER_HEADS = 8
N_KEYS = 128
PEER_TOPK = 16

LANES = 128
BF16_ROWS = 16
VMEM_LIMIT = 56 * 1024 * 1024

ROW_TILE = 256
ATT_TQ = 128
PEER_TM = 256
PEER_EB = 2048
PEER_SUB = 256

_NEG_INF = float("-inf")


def _cparams(sem, flags=None):
    return pltpu.CompilerParams(dimension_semantics=sem, vmem_limit_bytes=VMEM_LIMIT, flags=flags)


def _bdot(a, b):
    return jnp.dot(a.astype(BF16), b.astype(BF16), preferred_element_type=F32)


def _bdot_nt(a, b):
    return lax.dot_general(a.astype(BF16), b.astype(BF16), (((1,), (1,)), ((), ())), preferred_element_type=F32)


def _bdot_tn(a, b):
    return jnp.dot(a.T.astype(BF16), b.astype(BF16), preferred_element_type=F32)


def _split3(a):
    hi = a.astype(BF16)
    r1 = a - hi.astype(F32)
    mid = r1.astype(BF16)
    lo = (r1 - mid.astype(F32)).astype(BF16)
    return hi, mid, lo


def _dot3(a, b):
    a_hi, a_lo, _ = _split3(a)
    b_hi, b_lo, _ = _split3(b)
    return jnp.dot(jnp.concatenate([a_hi, a_lo, a_hi], axis=1), jnp.concatenate([b_hi, b_hi, b_lo], axis=0),
                   preferred_element_type=F32)


def _sigmoid(x):
    return 1.0 / (1.0 + jnp.exp(-x))


def _silu(x):
    return x * _sigmoid(x)


def _norm_mod(x, g, shift, scale):
    r = lax.rsqrt(jnp.mean(x * x, axis=-1, keepdims=True) + EPS)
    return (x * r * g) * (1.0 + scale) + shift


def _rope(x, cos, sin_signed, head_dim):
    quarter = head_dim // 4
    lane = lax.broadcasted_iota(jnp.int32, x.shape, 1)
    first = (lane % (2 * quarter)) < quarter
    partner = jnp.where(first, pltpu.roll(x, LANES - quarter, 1), pltpu.roll(x, quarter, 1))
    return x * cos + partner * sin_signed


def _mod_kernel(c_ref, w_ref, b_ref, o_ref):
    o_ref[0] = _bdot(_silu(c_ref[...]), w_ref[0]) + b_ref[0]


def _modulation(c, c_ctx, mod_w, mod_b):
    depth, d, n = mod_w.shape
    bsz = c.shape[0]
    rows = 16
    cc = jnp.zeros((rows, d), F32).at[:bsz].set(c).at[bsz].set(c_ctx)
    tn = 1536
    out = pl.pallas_call(
        _mod_kernel,
        grid=(depth, n // tn),
        in_specs=[pl.BlockSpec((rows, d), lambda l, j: (0, 0)),
                  pl.BlockSpec((1, d, tn), lambda l, j: (l, 0, j)),
                  pl.BlockSpec((1, 1, tn), lambda l, j: (l, 0, j))],
        out_specs=pl.BlockSpec((1, rows, tn), lambda l, j: (l, 0, j)),
        out_shape=jax.ShapeDtypeStruct((depth, rows, n), F32),
        compiler_params=_cparams(("arbitrary", "arbitrary")),
        name="modulation",
    )(cc, mod_w, mod_b.reshape(depth, 1, n))
    lat = out[:, :bsz].reshape(depth, bsz, 1, 6, d)
    ctx = jnp.broadcast_to(out[:, bsz].reshape(depth, 1, 1, 6, d), (depth, bsz, 1, 6, d))
    return jnp.concatenate([lat, ctx], axis=2)


def _rope_tables(L, LC, head_dim):
    quarter, half = head_dim // 4, head_dim // 2
    freqs = ROPE_THETA ** (-jnp.arange(quarter, dtype=F32) / quarter)
    lane = np.arange(LANES)
    within = lane % head_dim
    use_col = within >= half
    fidx = within % quarter
    sign = np.where((within % half) < quarter, -1.0, 1.0).astype(np.float32)
    t = jnp.arange(L, dtype=jnp.int32)
    row, col = (t // GRID_W).astype(F32), (t % GRID_W).astype(F32)
    pos = jnp.where(use_col[None, :], col[:, None], row[:, None])
    ang = pos * freqs[fidx][None, :]
    cos = jnp.concatenate([jnp.cos(ang), jnp.ones((LC, LANES), F32)], axis=0)
    sin = jnp.concatenate([jnp.sin(ang) * sign[None, :], jnp.zeros((LC, LANES), F32)], axis=0)
    return cos, sin


def _att_proj_kernel(lat_ref, ctx_ref, mod_ref, g_ref, w_ref, qg_ref, kg_ref, cos_ref, sin_ref, gm_ref,
                     qa_ref, ka_ref, va_ref, qb_ref, kb_ref, vb_ref, *, n_lat):
    m = mod_ref[0, 0]
    h = jnp.where(pl.program_id(1) < n_lat, lat_ref[0], ctx_ref[0])
    a = _norm_mod(h, g_ref[...], m[0:1], m[1:2])
    o = _bdot(a, w_ref[...])
    cos, sin = cos_ref[...], sin_ref[...]
    gm = gm_ref[...]

    def head_norm(x, gain):
        sq = x * x
        hi = sq.astype(BF16)
        lo = (sq - hi.astype(F32)).astype(BF16)
        ms = jnp.dot(hi, gm, preferred_element_type=F32) + jnp.dot(lo, gm, preferred_element_type=F32)
        return x * lax.rsqrt(ms + EPS) * gain

    def put(ref, first_head, x):
        ref[0, first_head] = x[:, :HEAD_DIM].astype(ref.dtype)
        ref[0, first_head + 1] = x[:, HEAD_DIM:].astype(ref.dtype)

    scale = HEAD_DIM ** -0.5
    qa_w = A_HEADS * HEAD_DIM
    kv_w = A_KV * HEAD_DIM
    off = 0
    for c in range(qa_w // LANES):
        x = o[:, off + c * LANES: off + (c + 1) * LANES]
        put(qa_ref, 2 * c, _rope(head_norm(x, qg_ref[...]), cos, sin, HEAD_DIM) * scale)
    off += qa_w
    put(ka_ref, 0, _rope(head_norm(o[:, off: off + kv_w], kg_ref[...]), cos, sin, HEAD_DIM))
    off += kv_w
    v2 = o[:, off: off + kv_w]
    lane = lax.broadcasted_iota(jnp.int32, v2.shape, 1)
    ones_col = jnp.where(lane == HEAD_DIM, 1.0, 0.0)
    va_ref[0, 0] = jnp.where(lane < HEAD_DIM, v2, ones_col).astype(va_ref.dtype)
    va_ref[0, 1] = jnp.where(lane < HEAD_DIM, pltpu.roll(v2, HEAD_DIM, 1), ones_col).astype(va_ref.dtype)
    off += kv_w
    for c in range(qa_w // LANES):
        x = o[:, off + c * LANES: off + (c + 1) * LANES]
        put(qb_ref, 2 * c, _rope(x, cos, sin, HEAD_DIM) * scale)
    off += qa_w
    put(kb_ref, 0, _rope(o[:, off: off + kv_w], cos, sin, HEAD_DIM))
    off += kv_w
    put(vb_ref, 0, o[:, off: off + kv_w])


def _token_specs(L, tm, d, ctx_off):
    n_lat = L // tm
    return [pl.BlockSpec((1, tm, d), lambda b, i: (b, jnp.minimum(i, n_lat - 1), 0)),
            pl.BlockSpec((1, tm, d), lambda b, i: (b, jnp.maximum(i - n_lat, 0) + ctx_off, 0))]


def _att_project(lat, ctx, ctx_off, mod, norm_g, w_in, q_g, k_g, cos, sin, L):
    bsz, _, d = lat.shape
    lt = cos.shape[0]
    tm = ROW_TILE
    n = w_in.shape[1]
    gm = jnp.asarray(np.kron(np.eye(LANES // HEAD_DIM), np.full((HEAD_DIM, HEAD_DIM), 1.0 / HEAD_DIM)), BF16)
    tile2 = lambda v: jnp.tile(v.reshape(1, HEAD_DIM), (1, LANES // HEAD_DIM))
    qshape = jax.ShapeDtypeStruct((bsz, A_HEADS, lt, HEAD_DIM), BF16)
    kshape = jax.ShapeDtypeStruct((bsz, A_KV, lt, HEAD_DIM), BF16)
    qspec = pl.BlockSpec((1, A_HEADS, tm, HEAD_DIM), lambda b, i: (b, 0, i, 0))
    kspec = pl.BlockSpec((1, A_KV, tm, HEAD_DIM), lambda b, i: (b, 0, i, 0))
    const = lambda shape: pl.BlockSpec(shape, lambda b, i: (0,) * len(shape))
    return pl.pallas_call(
        functools.partial(_att_proj_kernel, n_lat=L // tm),
        grid=(bsz, lt // tm),
        in_specs=_token_specs(L, tm, d, ctx_off) + [
                  pl.BlockSpec((1, 1, 6, d), lambda b, i: (b, i // (L // tm), 0, 0)),
                  const((1, d)), const((d, n)), const((1, LANES)), const((1, LANES)),
                  pl.BlockSpec((tm, LANES), lambda b, i: (i, 0)),
                  pl.BlockSpec((tm, LANES), lambda b, i: (i, 0)),
                  const((LANES, LANES))],
        out_specs=[qspec, kspec, pl.BlockSpec((1, A_KV, tm, LANES), lambda b, i: (b, 0, i, 0)), qspec, kspec, kspec],
        out_shape=[qshape, kshape, jax.ShapeDtypeStruct((bsz, A_KV, lt, LANES), BF16), qshape, kshape, kshape],
        compiler_params=_cparams(("parallel", "parallel")),
        name="att_project",
    )(lat, ctx, mod, norm_g.reshape(1, d), w_in.astype(BF16), tile2(q_g), tile2(k_g), cos, sin, gm)


def _softmax_pv(score_parts, value_parts, extra_logit=None):
    m = functools.reduce(jnp.maximum, [jnp.max(s, axis=-1, keepdims=True) for s in score_parts])
    if extra_logit is not None:
        m = jnp.maximum(m, extra_logit)
    l = 0.0
    acc = 0.0
    for s, v in zip(score_parts, value_parts):
        p = jnp.exp(s - m)
        l = l + jnp.sum(p, axis=-1, keepdims=True)
        acc = acc + jnp.dot(p.astype(BF16), v, preferred_element_type=F32)
    if extra_logit is not None:
        l = l + jnp.exp(extra_logit - m)
    return acc / l


def _softmax_pv_aug(s, v_aug):
    m = jnp.max(s, axis=-1, keepdims=True)
    acc = jnp.dot(jnp.exp((s - m).astype(BF16)), v_aug, preferred_element_type=F32)
    return acc[:, :HEAD_DIM] / acc[:, HEAD_DIM:HEAD_DIM + 1]


def _att_kernel(sink_ref, qa_ref, ka_ref, va_ref, qb_ref, kb_ref, vb_ref, oa_ref, ob_ref, *, L, LC):
    i = pl.program_id(1)
    tq = ATT_TQ
    group = A_HEADS // A_KV
    band = tq + 2 * WINDOW

    def stacked_q(ref, kvh):
        return ref[0, kvh * group:(kvh + 1) * group].reshape(group * tq, HEAD_DIM)

    def put(ref, kvh, o):
        for g in range(group):
            hd = kvh * group + g
            ref[0, :, hd * HEAD_DIM:(hd + 1) * HEAD_DIM] = o[g * tq:(g + 1) * tq].astype(ref.dtype)

    def sink_col(kvh):
        return jnp.concatenate([jnp.full((tq, 1), sink_ref[kvh * group + g], F32) for g in range(group)], axis=0)

    @pl.when(i < L // tq)
    def _latent():
        for kvh in range(A_KV):
            q = stacked_q(qa_ref, kvh)
            put(oa_ref, kvh, _softmax_pv_aug(_bdot_nt(q, ka_ref[0, kvh]), va_ref[0, kvh]))
            q = stacked_q(qb_ref, kvh)
            start = pl.multiple_of(jnp.clip((i - 1) * tq, 0, L - band), tq)
            s_band = _bdot_nt(q, kb_ref[0, kvh, pl.ds(start, band), :])
            qpos = i * tq + (lax.broadcasted_iota(jnp.int32, s_band.shape, 0) % tq)
            kpos = start + lax.broadcasted_iota(jnp.int32, s_band.shape, 1)
            s_band = jnp.where(jnp.abs(qpos - kpos) <= WINDOW, s_band, _NEG_INF)
            s_ctx = _bdot_nt(q, kb_ref[0, kvh, L:L + LC, :])
            put(ob_ref, kvh, _softmax_pv([s_band, s_ctx],
                                         [vb_ref[0, kvh, pl.ds(start, band), :], vb_ref[0, kvh, L:L + LC, :]],
                                         sink_col(kvh)))

    @pl.when(i >= L // tq)
    def _context():
        for kvh in range(A_KV):
            q = stacked_q(qa_ref, kvh)
            put(oa_ref, kvh, _softmax_pv_aug(_bdot_nt(q, ka_ref[0, kvh, L:L + LC, :]), va_ref[0, kvh, L:L + LC, :]))
            q = stacked_q(qb_ref, kvh)
            put(ob_ref, kvh, _softmax_pv([_bdot_nt(q, kb_ref[0, kvh, L:L + LC, :])], [vb_ref[0, kvh, L:L + LC, :]],
                                         sink_col(kvh)))


def _attention(qa, ka, va, qb, kb, vb, sink, L):
    bsz, _, lt, _ = qa.shape
    tq = ATT_TQ
    qspec = pl.BlockSpec((1, A_HEADS, tq, HEAD_DIM), lambda b, i: (b, 0, i, 0))
    kspec = pl.BlockSpec((1, A_KV, lt, HEAD_DIM), lambda b, i: (b, 0, 0, 0))
    ospec = pl.BlockSpec((1, tq, A_HEADS * HEAD_DIM), lambda b, i: (b, i, 0))
    oshape = jax.ShapeDtypeStruct((bsz, lt, A_HEADS * HEAD_DIM), BF16)
    return pl.pallas_call(
        functools.partial(_att_kernel, L=L, LC=lt - L),
        grid=(bsz, lt // tq),
        in_specs=[pl.BlockSpec(memory_space=pltpu.SMEM), qspec, kspec,
                  pl.BlockSpec((1, A_KV, lt, LANES), lambda b, i: (b, 0, 0, 0)), qspec, kspec, kspec],
        out_specs=[ospec, ospec],
        out_shape=[oshape, oshape],
        compiler_params=_cparams(("parallel", "parallel")),
        name="attention",
    )(sink, qa, ka, va, qb, kb, vb)


def _att_out_kernel(oa_ref, ob_ref, w_ref, lat_ref, ctx_ref, mod_ref, out_ref, *, n_lat):
    half = oa_ref.shape[-1]
    y = (jnp.dot(oa_ref[0], w_ref[:half, :], preferred_element_type=F32)
         + jnp.dot(ob_ref[0], w_ref[half:, :], preferred_element_type=F32))
    h = jnp.where(pl.program_id(1) < n_lat, lat_ref[0], ctx_ref[0])
    out_ref[0] = h + mod_ref[0, 0][2:3] * y


def _att_out(oa, ob, w_out, lat, ctx, ctx_off, mod, L):
    bsz, lt, half = oa.shape
    d = lat.shape[2]
    tm = ROW_TILE
    return pl.pallas_call(
        functools.partial(_att_out_kernel, n_lat=L // tm),
        grid=(bsz, lt // tm),
        in_specs=[pl.BlockSpec((1, tm, half), lambda b, i: (b, i, 0)),
                  pl.BlockSpec((1, tm, half), lambda b, i: (b, i, 0)),
                  pl.BlockSpec((2 * half, d), lambda b, i: (0, 0))] + _token_specs(L, tm, d, ctx_off) + [
                  pl.BlockSpec((1, 1, 6, d), lambda b, i: (b, i // (L // tm), 0, 0))],
        out_specs=pl.BlockSpec((1, tm, d), lambda b, i: (b, i, 0)),
        out_shape=jax.ShapeDtypeStruct((bsz, lt, d), F32),
        compiler_params=_cparams(("parallel", "parallel")),
        name="att_out",
    )(oa, ob, w_out.astype(BF16), lat, ctx, mod)


def _sort_network(n):
    pairs = []
    p = 1
    while p < n:
        k = p
        while k >= 1:
            for j in range(k % p, n - k, 2 * k):
                for i in range(min(k, n - j - k)):
                    if (i + j) // (2 * p) == (i + j + k) // (2 * p):
                        pairs.append((i + j, i + j + k))
            k //= 2
        p *= 2
    return pairs


def _pop_sorted(lists, n, singles=()):
    lists, singles = list(lists), list(singles)
    out = []
    for rnd in range(n):
        head = functools.reduce(jnp.maximum, [lists[0]] + singles)
        m = jnp.max(head, axis=0, keepdims=True)
        out.append(m)
        if rnd == n - 1:
            break
        hit = lists[0] == m
        depth = min(len(lists), n - rnd - 1)
        for d in range(depth):
            nxt = lists[d + 1] if d + 1 < len(lists) else _NEG_INF
            lists[d] = jnp.where(hit, nxt, lists[d])
        singles = [jnp.where(s == m, _NEG_INF, s) for s in singles]
    return out


def _top_values(x, n):
    slabs = [x[r * 8:(r + 1) * 8] for r in range(x.shape[0] // 8)]
    for i, j in _sort_network(len(slabs)):
        slabs[i], slabs[j] = jnp.maximum(slabs[i], slabs[j]), jnp.minimum(slabs[i], slabs[j])
    return _pop_sorted(slabs, n)


def _peer_route_kernel(h_ref, mod_ref, g_ref, wq_ref, keys_ref, at_ref, thr_ref, e0_ref, r1_ref, e1_ref, q_scr):
    m = mod_ref[0, 0]
    a = _norm_mod(h_ref[0], g_ref[...], m[3:4], m[4:5])
    at_ref[0] = a.T.astype(BF16)
    q_scr[...] = jnp.dot(a.astype(BF16), wq_ref[...], preferred_element_type=F32)
    k = PEER_TOPK
    n = k + 1

    def head(hd, carry):
        col = pl.multiple_of(hd * 2 * N_KEYS, 2 * N_KEYS)
        s0 = _bdot_nt(keys_ref[0], q_scr[:, pl.ds(col, N_KEYS)])
        s1 = _bdot_nt(keys_ref[1], q_scr[:, pl.ds(col + N_KEYS, N_KEYS)])
        tokens = s0.shape[1]
        top0 = _top_values(s0, n)
        top1 = _top_values(s1, n)
        first = jnp.concatenate(top0 + [jnp.full((24 - n, tokens), _NEG_INF, F32)], axis=0)
        best = _pop_sorted([first[0:8] + t for t in top1], n, [first[8:16] + top1[0], first[16:24] + top1[0]])
        z = functools.reduce(lambda x, y: x + y, [jnp.exp(b - best[0]) for b in best[:k]])
        tau = 0.5 * (best[k - 1] + best[k])
        r1 = jnp.zeros_like(s1)
        need = jnp.full_like(s0, n + 1.0)
        for b in range(k):
            r1 = jnp.where(s1 >= top1[k - 1 - b], b + 2.0, r1)
            need = jnp.where(s0 >= tau - top1[b], float(n - b), need)
        thr_ref[0, hd] = need
        e0_ref[0, hd] = jnp.exp(s0 - top0[0])
        r1_ref[0, hd] = pltpu.bitcast(r1.astype(BF16), jnp.uint32)
        e1_ref[0, hd] = pltpu.bitcast((jnp.exp(s1 - top1[0]) / z).astype(BF16), jnp.uint32)
        return carry

    lax.fori_loop(0, PEER_HEADS, head, 0, unroll=True)


def _peer_route(hh, mod, norm_g, w_q, sub_keys, L):
    bsz, lt, d = hh.shape
    tm = ROW_TILE
    nq = w_q.shape[1]
    tok = lambda: pl.BlockSpec((1, PEER_HEADS, N_KEYS, tm), lambda b, i: (b, 0, 0, i))
    tshape = jax.ShapeDtypeStruct((bsz, PEER_HEADS, N_KEYS, lt), F32)
    pshape = jax.ShapeDtypeStruct((bsz, PEER_HEADS, N_KEYS // 2, lt), jnp.uint32)
    ptok = pl.BlockSpec((1, PEER_HEADS, N_KEYS // 2, tm), lambda b, i: (b, 0, 0, i))
    return pl.pallas_call(
        _peer_route_kernel,
        grid=(bsz, lt // tm),
        in_specs=[pl.BlockSpec((1, tm, d), lambda b, i: (b, i, 0)),
                  pl.BlockSpec((1, 1, 6, d), lambda b, i: (b, i // (L // tm), 0, 0)),
                  pl.BlockSpec((1, d), lambda b, i: (0, 0)),
                  pl.BlockSpec((d, nq), lambda b, i: (0, 0)),
                  pl.BlockSpec((2, N_KEYS, N_KEYS), lambda b, i: (0, 0, 0))],
        out_specs=[pl.BlockSpec((1, d, tm), lambda b, i: (b, 0, i)), tok(), tok(), ptok, ptok],
        out_shape=[jax.ShapeDtypeStruct((bsz, d, lt), BF16), tshape, tshape, pshape, pshape],
        scratch_shapes=[pltpu.VMEM((tm, nq), F32)],
        compiler_params=_cparams(("parallel", "parallel")),
        name="peer_route",
    )(hh, mod, norm_g.reshape(1, d), w_q.astype(BF16), sub_keys.astype(BF16))


def _gelu(x):
    return 0.5 * x * (1.0 + lax.erf(x * (1.0 / math.sqrt(2.0))))


def _peer_expert_kernel(a_ref, an_ref, thr_ref, e0_ref, r1_ref, e1_ref, u_ref, vt_ref, h_ref, mod_ref, fg_ref,
                        out_ref, yt_scr, wa_scr, at0_scr, at1_scr, *, final):
    e = pl.program_id(1)
    t = pl.program_id(2)
    tm = a_ref.shape[2]
    n_sub = PEER_EB // PEER_SUB
    rows = [slice(q * PEER_SUB, (q + 1) * PEER_SUB) for q in range(n_sub)]

    def activations(dst, src, q):
        words = slice(rows[q].start // 2, rows[q].stop // 2)
        dst[rows[q], :] = jnp.dot(pltpu.bitcast(u_ref[words, :], BF16), src[0], preferred_element_type=F32)

    @pl.when(e == 0)
    def _():
        yt_scr[t] = jnp.zeros(yt_scr.shape[1:], F32)

    @pl.when(t == 0)
    def _():
        for q in range(n_sub):
            activations(at0_scr, a_ref, q)

    def step(act_ref, next_ref):
        y = None
        for q in range(n_sub):
            activations(next_ref, an_ref, q)
            for il in range(PEER_SUB // N_KEYS):
                key0 = q * (PEER_SUB // N_KEYS) + il
                for c in range(tm // LANES):
                    cs = slice(c * LANES, (c + 1) * LANES)
                    row = lambda ref, hd: jnp.broadcast_to(ref[0, hd, key0:key0 + 1, cs], (BF16_ROWS, LANES)).astype(BF16)
                    thr = [row(thr_ref, hd) for hd in range(PEER_HEADS)]
                    e0 = [row(e0_ref, hd) for hd in range(PEER_HEADS)]
                    zero = jnp.zeros((BF16_ROWS, LANES), BF16)
                    for r in range(N_KEYS // BF16_ROWS):
                        ws = slice(r * BF16_ROWS // 2, (r + 1) * BF16_ROWS // 2)
                        gate = None
                        for hd in range(PEER_HEADS):
                            rank = pltpu.bitcast(r1_ref[0, hd, ws, cs], BF16)
                            weight = pltpu.bitcast(e1_ref[0, hd, ws, cs], BF16)
                            part = jnp.where(rank >= thr[hd], weight * e0[hd], zero)
                            gate = part if gate is None else gate + part
                        ars = slice(rows[q].start + il * N_KEYS + r * BF16_ROWS,
                                    rows[q].start + il * N_KEYS + (r + 1) * BF16_ROWS)
                        wa_scr[ars, cs] = gate * _gelu(act_ref[ars, cs]).astype(BF16)
            part = jnp.dot(pltpu.bitcast(vt_ref[:, rows[q]], BF16), wa_scr[rows[q], :], preferred_element_type=F32)
            y = part if y is None else y + part
        yt_scr[t] += y

    @pl.when(t % 2 == 0)
    def _():
        step(at0_scr, at1_scr)

    @pl.when(t % 2 == 1)
    def _():
        step(at1_scr, at0_scr)

    @pl.when(e == pl.num_programs(1) - 1)
    def _():
        hn = h_ref[0] + mod_ref[0, 0][5:6] * yt_scr[t].T
        if final:
            hn = hn * lax.rsqrt(jnp.mean(hn * hn, axis=-1, keepdims=True) + EPS) * fg_ref[...]
        out_ref[0] = hn


def _peer_expert(hh, mod, a, thr, e0, r1, e1, u_bf, vt_bf, final_g, L, *, with_ctx, final):
    bsz, lt, d = hh.shape
    n_exp = vt_bf.shape[1]
    tm = PEER_TM
    n_tiles = (lt if with_ctx else L) // tm
    keys_per_step = PEER_EB // N_KEYS
    n_blocks = n_exp // PEER_EB
    tok = lambda r: pl.BlockSpec((1, PEER_HEADS, r, tm), (lambda b, e, t: (b, 0, e, t)) if r == keys_per_step
                                 else (lambda b, e, t: (b, 0, 0, t)))
    row_spec = pl.BlockSpec((1, tm, d), lambda b, e, t: (b, jnp.where(e == n_blocks - 1, t, 0), 0))
    if final:
        out_shape = jax.ShapeDtypeStruct((bsz, L, d), F32)
        aliases = {}
    else:
        out_shape = jax.ShapeDtypeStruct(hh.shape, F32)
        aliases = {8: 0}
    return pl.pallas_call(
        functools.partial(_peer_expert_kernel, final=final),
        grid=(bsz, n_blocks, n_tiles),
        in_specs=[pl.BlockSpec((1, d, tm), lambda b, e, t: (b, 0, t)),
                  pl.BlockSpec((1, d, tm), lambda b, e, t: (b, 0, jnp.minimum(t + 1, n_tiles - 1))),
                  tok(keys_per_step), tok(keys_per_step), tok(N_KEYS // 2), tok(N_KEYS // 2),
                  pl.BlockSpec((PEER_EB // 2, d), lambda b, e, t: (e, 0)),
                  pl.BlockSpec((d // 2, PEER_EB), lambda b, e, t: (0, e)),
                  row_spec,
                  pl.BlockSpec((1, 1, 6, d), lambda b, e, t: (b, t // (L // tm), 0, 0)),
                  pl.BlockSpec((1, d), lambda b, e, t: (0, 0))],
        out_specs=row_spec,
        out_shape=out_shape,
        input_output_aliases=aliases,
        scratch_shapes=[pltpu.VMEM((n_tiles, d, tm), F32), pltpu.VMEM((PEER_EB, tm), BF16),
                        pltpu.VMEM((PEER_EB, tm), F32), pltpu.VMEM((PEER_EB, tm), F32)],
        compiler_params=_cparams(("parallel", "arbitrary", "arbitrary")),
        name="peer_expert",
    )(a, a, thr, e0, r1, e1, u_bf, vt_bf, hh, mod, final_g.reshape(1, d))


def _pack_kernel(x_ref, o_ref, *, transpose):
    x = x_ref[0]
    if transpose:
        x = x.T
    o_ref[...] = pltpu.bitcast(x.astype(BF16), jnp.uint32)


def _pack_row_pairs(x, layer, *, transpose):
    _, r, c = x.shape
    blk = 512
    if transpose:
        out_shape, out_spec = (c // 2, r), pl.BlockSpec((c // 2, blk), lambda i: (0, i))
    else:
        out_shape, out_spec = (r // 2, c), pl.BlockSpec((blk // 2, c), lambda i: (i, 0))
    return pl.pallas_call(
        functools.partial(_pack_kernel, transpose=transpose),
        grid=(r // blk,),
        in_specs=[pl.BlockSpec((1, blk, c), lambda i: (layer, i, 0))],
        out_specs=out_spec,
        out_shape=jax.ShapeDtypeStruct(out_shape, jnp.uint32),
        compiler_params=_cparams(("parallel",)),
        name="pack_pairs_t" if transpose else "pack_pairs",
    )(x)


def _peer(hh, mod, norm_g, w_q, sub_keys, u_all, v_all, layer, final_g, L, *, with_ctx, final):
    a, thr, e0, r1, e1 = _peer_route(hh, mod, norm_g, w_q, sub_keys, L)
    return _peer_expert(hh, mod, a, thr, e0, r1, e1, _pack_row_pairs(u_all, layer, transpose=False),
                        _pack_row_pairs(v_all, layer, transpose=True), final_g, L, with_ctx=with_ctx, final=final)


REC_SCAN_CHUNKS = 4
REC_CHUNKS_PER_STEP = 2
REC_QKV = C_HEADS * 3 * C_DK
REC_Z = C_HEADS * C_DK
REC_GATES = 4 * C_HEADS
REC_HD = D_HEADS * D_DK


def _rec_proj_kernel(h_ref, mod_ref, g_ref, w_ref, alog_ref, dtb_ref, cos_ref, sin_ref,
                     qkv_ref, z_ref, qd_ref, kd_ref, vd_ref, gd_ref, gate_ref):
    m = mod_ref[0, 0]
    a = _norm_mod(h_ref[0], g_ref[...], m[0:1], m[1:2])
    o = _bdot(a, w_ref[...])
    cos, sin = cos_ref[...], sin_ref[...]
    off = 0
    qkv_ref[0] = o[:, off:off + REC_QKV]
    off += REC_QKV
    z_ref[0] = o[:, off:off + REC_Z]
    off += REC_Z
    for hd in range(D_HEADS):
        qd_ref[0, :, hd * D_DK:(hd + 1) * D_DK] = _rope(o[:, off + hd * D_DK: off + (hd + 1) * D_DK], cos, sin, D_DK)
    off += REC_HD
    for hd in range(D_HEADS):
        kd_ref[0, :, hd * D_DK:(hd + 1) * D_DK] = (
            _rope(o[:, off + hd * D_DK: off + (hd + 1) * D_DK], cos, sin, D_DK) * (D_DK ** -0.5))
    off += REC_HD
    vd_ref[0] = o[:, off:off + REC_HD]
    off += REC_HD
    gd_ref[0] = o[:, off:off + 2 * REC_HD]
    off += 2 * REC_HD
    x = o[:, off:off + LANES]
    lane = lax.broadcasted_iota(jnp.int32, x.shape, 1)
    xb = x + dtb_ref[...]
    softplus = jnp.maximum(xb, 0.0) + jnp.log(1.0 + jnp.exp(-jnp.abs(xb)))
    gate_ref[0] = jnp.where(lane < 2 * C_HEADS, -jnp.exp(alog_ref[...]) * softplus, _sigmoid(x))


def _rec_project(hh, mod, norm_g, w_in, a_log, dt_bias, cos, sin, L):
    bsz, lt, d = hh.shape
    tm = ROW_TILE
    parts = np.cumsum([REC_QKV, REC_Z, REC_GATES, REC_HD, REC_HD, REC_HD])
    qkv_w, z_w, gates_w, qd_w, kd_w, vd_w, gd_w = jnp.split(w_in, [int(p) for p in parts], axis=1)
    w = jnp.concatenate([qkv_w, z_w, qd_w, kd_w, vd_w, gd_w, gates_w, jnp.zeros((d, LANES - REC_GATES), F32)],
                        axis=1).astype(BF16)
    n = w.shape[1]
    pad = lambda p: jnp.zeros((1, LANES), F32).at[0, :2 * C_HEADS].set(p.reshape(-1))
    widths = [REC_QKV, REC_Z, REC_HD, REC_HD, REC_HD, 2 * REC_HD, LANES]
    const = lambda shape: pl.BlockSpec(shape, lambda b, i: (0,) * len(shape))
    return pl.pallas_call(
        _rec_proj_kernel,
        grid=(bsz, lt // tm),
        in_specs=[pl.BlockSpec((1, tm, d), lambda b, i: (b, i, 0)),
                  pl.BlockSpec((1, 1, 6, d), lambda b, i: (b, i // (L // tm), 0, 0)),
                  const((1, d)), const((d, n)), const((1, LANES)), const((1, LANES)),
                  pl.BlockSpec((tm, LANES), lambda b, i: (i, 0)),
                  pl.BlockSpec((tm, LANES), lambda b, i: (i, 0))],
        out_specs=[pl.BlockSpec((1, tm, wd), lambda b, i: (b, i, 0)) for wd in widths],
        out_shape=[jax.ShapeDtypeStruct((bsz, lt, wd), F32) for wd in widths],
        compiler_params=_cparams(("parallel", "parallel")),
        name="rec_project",
    )(hh, mod, norm_g.reshape(1, d), w, pad(a_log), pad(dt_bias), cos, sin)


def _rec_conv_kernel(x_ref, prev_ref, next_ref, w_ref, q_ref, k_ref, v_ref, *, L):
    i = pl.program_id(1)
    tl = x_ref.shape[1]
    n_lat = L // tl
    at_start = (i == 0) | (i == n_lat)
    at_end = (i == n_lat - 1) | (i == pl.num_programs(1) - 1)
    prev = jnp.where(at_start, 0.0, prev_ref[0])
    nxt = jnp.where(at_end, 0.0, next_ref[0])
    xx = jnp.concatenate([prev, x_ref[0], nxt], axis=0)
    n = tl + 16
    acc = 0.0
    for tap in range(CONV_W):
        shift = (CONV_W // 2 - tap) % n
        shifted = xx if shift == 0 else pltpu.roll(xx, shift, 0)
        acc = acc + shifted[8:8 + tl] * w_ref[tap:tap + 1, :]
    y = _silu(acc)
    hw = C_HEADS * C_DK
    for hd in range(C_HEADS):
        def l2(x):
            return x * lax.rsqrt(jnp.sum(x * x, axis=-1, keepdims=True) + EPS)
        sl = slice(hd * C_DK, (hd + 1) * C_DK)
        q_ref[0, :, sl] = l2(y[:, hd * C_DK:(hd + 1) * C_DK]) * (C_DK ** -0.5)
        k_ref[0, :, sl] = l2(y[:, hw + hd * C_DK: hw + (hd + 1) * C_DK])
    v_ref[0] = y[:, 2 * hw:]


def _rec_conv(qkv, conv_w, L):
    bsz, lt, ch = qkv.shape
    tl = ROW_TILE
    hb = tl // 8
    last = lt // 8 - 1
    hw = C_HEADS * C_DK
    return pl.pallas_call(
        functools.partial(_rec_conv_kernel, L=L),
        grid=(bsz, lt // tl),
        in_specs=[pl.BlockSpec((1, tl, ch), lambda b, i: (b, i, 0)),
                  pl.BlockSpec((1, 8, ch), lambda b, i: (b, jnp.maximum(i * hb - 1, 0), 0)),
                  pl.BlockSpec((1, 8, ch), lambda b, i: (b, jnp.minimum((i + 1) * hb, last), 0)),
                  pl.BlockSpec((8, ch), lambda b, i: (0, 0))],
        out_specs=[pl.BlockSpec((1, tl, hw), lambda b, i: (b, i, 0))] * 3,
        out_shape=[jax.ShapeDtypeStruct((bsz, lt, hw), F32)] * 3,
        compiler_params=_cparams(("parallel", "parallel")),
        name="rec_conv",
    )(qkv, qkv, qkv, jnp.zeros((8, ch), F32).at[:CONV_W].set(conv_w))


def _ret_log_gamma(hd):
    return float(np.log1p(-np.exp2(-(RET_DECAY_BASE + hd))))


def _rec_intra_kernel(qc_ref, kc_ref, vc_ref, qd_ref, kd_ref, vd_ref, gate_ref,
                      eq_ref, qk_ref, kt_ref, sol_ref, vr_ref, al_ref):
    c = CHUNK
    ci = lax.broadcasted_iota(jnp.int32, (c, c), 0)
    si = lax.broadcasted_iota(jnp.int32, (c, c), 1)
    pos = lax.broadcasted_iota(jnp.int32, (c, 1), 0).astype(F32)
    ones_row = jnp.ones((1, LANES), F32)
    combos = []
    for cc in range(REC_CHUNKS_PER_STEP):
        _rec_intra_setup(cc, combos, ci, si, pos, ones_row, qc_ref, kc_ref, vc_ref, qd_ref, kd_ref, vd_ref, gate_ref,
                         eq_ref, qk_ref, kt_ref, vr_ref, al_ref)
    _rec_intra_solve(combos, ones_row, eq_ref, qk_ref, kt_ref, sol_ref, al_ref)


def _rec_intra_setup(cc, combos, ci, si, pos, ones_row, qc_ref, kc_ref, vc_ref, qd_ref, kd_ref, vd_ref, gate_ref,
                     eq_ref, qk_ref, kt_ref, vr_ref, al_ref):
    c = CHUNK
    rs = slice(cc * c, (cc + 1) * c)
    gates = gate_ref[0, rs]
    la_parts = _split3(gates)
    for d in range(2):
        incl = (si <= ci) if d == 0 else (si >= ci)
        strict = (si < ci) if d == 0 else (si > ci)
        tri = incl.astype(BF16)
        g_all = jnp.dot(jnp.concatenate([tri] * 3, axis=1), jnp.concatenate(la_parts, axis=0),
                        preferred_element_type=F32)
        g_all_t = g_all.T
        last = c - 1 if d == 0 else 0
        for hd in range(C_HEADS):
            col = d * C_HEADS + hd
            sl = slice(hd * C_DK, (hd + 1) * C_DK)
            q, k, v = qc_ref[0, rs, sl], kc_ref[0, rs, sl], vc_ref[0, rs, sl]
            g_rows = jnp.broadcast_to(g_all[:, col:col + 1], (c, C_DK))
            beta = jnp.broadcast_to(gates[:, 2 * C_HEADS + col: 2 * C_HEADS + col + 1], (c, C_DK))
            grow = g_all_t[col:col + 1, :]
            glast = g_all[last:last + 1, col:col + 1]
            exp_g = jnp.exp(g_rows)
            diff = g_rows[:, :c] - grow
            dec_strict = jnp.exp(jnp.where(strict, diff, _NEG_INF))
            combos.append(dict(
                cc=cc, d=d, hd=hd, q=q, exp_g=exp_g, glast=glast,
                x=-(beta[:, :c] * _bdot_nt(k, k) * dec_strict),
                sol=jnp.concatenate([beta * v, (beta * exp_g) * k], axis=1),
                qk=_bdot_nt(q, k) * jnp.exp(jnp.where(incl, diff, _NEG_INF)),
                kend=k * jnp.exp(glast - g_rows)))
    for d in range(2):
        incl = (si <= ci) if d == 0 else (si >= ci)
        for hd in range(D_HEADS):
            lg = _ret_log_gamma(hd)
            sl = slice(hd * D_DK, (hd + 1) * D_DK)
            q, k, v = qd_ref[0, rs, sl], kd_ref[0, rs, sl], vd_ref[0, rs, sl]
            steps = pos if d == 0 else (c - 1.0) - pos
            dist = (ci - si) if d == 0 else (si - ci)
            dmat = jnp.exp(jnp.where(incl, lg * dist.astype(F32), _NEG_INF))
            qk = _bdot_nt(q, k) * dmat
            eq_ref[0, cc, d, C_HEADS + hd] = (q * jnp.exp(lg * (steps + 1.0))).astype(BF16)
            qk_ref[0, cc, d, C_HEADS + hd] = qk.astype(BF16)
            kt_ref[0, cc, d, C_HEADS + hd] = (k * jnp.exp(lg * ((c - 1.0) - steps))).T.astype(BF16)
            vr_ref[0, cc, d, hd] = v.astype(BF16)
            al_ref[0, cc, d, C_HEADS + hd:C_HEADS + hd + 1, :] = math.exp(lg * c) * ones_row


def _rec_intra_solve(combos, ones_row, eq_ref, qk_ref, kt_ref, sol_ref, al_ref):
    c = CHUNK
    for cb in combos:
        cc, d, hd = cb["cc"], cb["d"], cb["hd"]
        eq_ref[0, cc, d, hd] = (cb["exp_g"] * cb["q"]).astype(BF16)
        qk_ref[0, cc, d, hd] = cb["qk"].astype(BF16)
        kt_ref[0, cc, d, hd] = cb["kend"].T.astype(BF16)
        al_ref[0, cc, d, hd:hd + 1, :] = jnp.exp(cb["glast"]) * ones_row
    levels = int(math.log2(c))
    for lvl in range(levels):
        for cb in combos:
            cb["sol"] = cb["sol"] + _dot3(cb["x"], cb["sol"])
        if lvl < levels - 1:
            for cb in combos:
                cb["x"] = _dot3(cb["x"], cb["x"])
    for cb in combos:
        sol_ref[0, cb["cc"], cb["d"], cb["hd"]] = cb["sol"].astype(BF16)


def _rec_intra(qc, kc, vc, qd, kd, vd, gate):
    bsz, lt, hw = qc.shape
    nc = lt // CHUNK
    nh = C_HEADS + D_HEADS
    per = REC_CHUNKS_PER_STEP
    row = lambda w: pl.BlockSpec((1, per * CHUNK, w), lambda b, n: (b, n, 0))
    lead = lambda *tail: pl.BlockSpec((1, per, 2) + tail, lambda b, n: (b, n, 0) + (0,) * len(tail))
    return pl.pallas_call(
        _rec_intra_kernel,
        grid=(bsz, nc // per),
        in_specs=[row(hw)] * 6 + [row(LANES)],
        out_specs=[lead(nh, CHUNK, C_DK), lead(nh, CHUNK, CHUNK), lead(nh, C_DK, CHUNK), lead(C_HEADS, CHUNK, 2 * C_DK),
                   lead(D_HEADS, CHUNK, C_DK), lead(nh, LANES)],
        out_shape=[jax.ShapeDtypeStruct((bsz, nc, 2, nh, CHUNK, C_DK), BF16),
                   jax.ShapeDtypeStruct((bsz, nc, 2, nh, CHUNK, CHUNK), BF16),
                   jax.ShapeDtypeStruct((bsz, nc, 2, nh, C_DK, CHUNK), BF16),
                   jax.ShapeDtypeStruct((bsz, nc, 2, C_HEADS, CHUNK, 2 * C_DK), BF16),
                   jax.ShapeDtypeStruct((bsz, nc, 2, D_HEADS, CHUNK, C_DK), BF16),
                   jax.ShapeDtypeStruct((bsz, nc, 2, nh, LANES), F32)],
        compiler_params=_cparams(("parallel", "parallel")),
        name="rec_intra",
    )(qc, kc, vc, qd, kd, vd, gate)


def _rec_scan_kernel(eq_ref, qk_ref, kt_ref, sol_ref, vr_ref, al_ref, o_ref, s_scr):
    @pl.when(pl.program_id(2) == 0)
    def _():
        s_scr[...] = jnp.zeros_like(s_scr)

    dot = functools.partial(jnp.dot, preferred_element_type=F32)
    heads = range(C_HEADS + D_HEADS)

    def run(order):
        s = [s_scr[hd] for hd in heads]
        for cc in order:
            sb = [x.astype(BF16) for x in s]
            wks = [dot(sol_ref[0, cc, 0, hd, :, C_DK:], sb[hd]) for hd in range(C_HEADS)]
            cross = [dot(eq_ref[0, cc, 0, hd], sb[hd]) for hd in heads]
            w = [(sol_ref[0, cc, 0, hd, :, :C_DK].astype(F32) - wks[hd]).astype(BF16) for hd in range(C_HEADS)]
            w += [vr_ref[0, cc, 0, hd] for hd in range(D_HEADS)]
            inner = [dot(qk_ref[0, cc, 0, hd], w[hd]) for hd in heads]
            update = [dot(kt_ref[0, cc, 0, hd], w[hd]) for hd in heads]
            for hd in heads:
                o_ref[0, 0, cc * CHUNK:(cc + 1) * CHUNK, hd * C_DK:(hd + 1) * C_DK] = cross[hd] + inner[hd]
            s = [al_ref[0, cc, 0, hd:hd + 1, :] * s[hd] + update[hd] for hd in heads]
        for hd in heads:
            s_scr[hd] = s[hd]

    per = eq_ref.shape[1]

    @pl.when(pl.program_id(1) == 0)
    def _():
        run(range(per))

    @pl.when(pl.program_id(1) == 1)
    def _():
        run(range(per - 1, -1, -1))


def _rec_scan(eq, qk, kt, sol, vr, al, L):
    bsz, nc = eq.shape[:2]
    nh = C_HEADS + D_HEADS
    n_lat = L // CHUNK
    n_ctx = nc - n_lat

    per = REC_SCAN_CHUNKS
    n_lat, n_ctx, nb = n_lat // per, n_ctx // per, nc // per

    def chunk(d, s):
        fwd = jnp.where(s < n_ctx, n_lat + s, s - n_ctx)
        bwd = nb - 1 - s
        return jnp.where(d == 0, fwd, bwd)

    lead = lambda *tail: pl.BlockSpec((1, per, 1) + tail, lambda b, d, s: (b, chunk(d, s), d) + (0,) * len(tail))
    return pl.pallas_call(
        _rec_scan_kernel,
        grid=(bsz, 2, nb),
        in_specs=[lead(nh, CHUNK, C_DK), lead(nh, CHUNK, CHUNK), lead(nh, C_DK, CHUNK), lead(C_HEADS, CHUNK, 2 * C_DK),
                  lead(D_HEADS, CHUNK, C_DK), lead(nh, LANES)],
        out_specs=pl.BlockSpec((1, 1, per * CHUNK, nh * C_DK), lambda b, d, s: (b, d, chunk(d, s), 0)),
        out_shape=jax.ShapeDtypeStruct((bsz, 2, nc * CHUNK, nh * C_DK), F32),
        scratch_shapes=[pltpu.VMEM((nh, C_DK, C_DK), F32)],
        compiler_params=_cparams(("parallel", "parallel", "arbitrary")),
        name="rec_scan",
    )(eq, qk, kt, sol, vr, al)


def _rec_out_kernel(of_ref, ob_ref, z_ref, gd_ref, og_ref, gn_ref, w_ref, h_ref, mod_ref, out_ref):
    hw = C_HEADS * C_DK
    of, ob = of_ref[0, 0], ob_ref[0, 0]
    z, gd = z_ref[0], gd_ref[0]
    parts = []
    for hd in range(C_HEADS):
        sl = slice(hd * C_DK, (hd + 1) * C_DK)
        x = of[:, sl] + ob[:, sl]
        y = x * lax.rsqrt(jnp.mean(x * x, axis=-1, keepdims=True) + EPS) * og_ref[...]
        parts.append(y * _silu(z[:, sl]))
    for hd in range(D_HEADS):
        sl = slice(hd * D_DK, (hd + 1) * D_DK)
        y = 0.0
        for d, o in enumerate((of, ob)):
            x = o[:, hw + hd * D_DK: hw + (hd + 1) * D_DK]
            mu = jnp.mean(x, axis=-1, keepdims=True)
            xc = x - mu
            var = jnp.mean(xc * xc, axis=-1, keepdims=True)
            y = y + xc * lax.rsqrt(var + EPS) * gn_ref[:, sl] * _silu(gd[:, d * hw + hd * D_DK: d * hw + (hd + 1) * D_DK])
        parts.append(y)
    mix = jnp.concatenate(parts, axis=1)
    out_ref[0] = h_ref[0] + mod_ref[0, 0][2:3] * _bdot(mix, w_ref[...])


def _rec_out(o_scan, z, gd, out_g, gn_g, w_out, hh, mod, L):
    bsz, lt, d = hh.shape
    tm = ROW_TILE
    hw = C_HEADS * C_DK
    return pl.pallas_call(
        _rec_out_kernel,
        grid=(bsz, L // tm),
        in_specs=[pl.BlockSpec((1, 1, tm, 2 * hw), lambda b, i: (b, 0, i, 0)),
                  pl.BlockSpec((1, 1, tm, 2 * hw), lambda b, i: (b, 1, i, 0)),
                  pl.BlockSpec((1, tm, hw), lambda b, i: (b, i, 0)),
                  pl.BlockSpec((1, tm, 2 * hw), lambda b, i: (b, i, 0)),
                  pl.BlockSpec((1, C_DK), lambda b, i: (0, 0)),
                  pl.BlockSpec((1, hw), lambda b, i: (0, 0)),
                  pl.BlockSpec((2 * hw, d), lambda b, i: (0, 0)),
                  pl.BlockSpec((1, tm, d), lambda b, i: (b, i, 0)),
                  pl.BlockSpec((1, 1, 6, d), lambda b, i: (b, 0, 0, 0))],
        out_specs=pl.BlockSpec((1, tm, d), lambda b, i: (b, i, 0)),
        out_shape=jax.ShapeDtypeStruct(hh.shape, F32),
        input_output_aliases={7: 0},
        compiler_params=_cparams(("parallel", "parallel")),
        name="rec_out",
    )(o_scan, o_scan, z, gd, out_g.reshape(1, C_DK), gn_g.reshape(1, hw), w_out.astype(BF16), hh, mod)


def _att_layer(lat, ctx, ctx_off, lc, mod, norm_g, w_in, q_g, k_g, sink, w_out, L):
    cos, sin = _rope_tables(L, lc, HEAD_DIM)
    qa, ka, va, qb, kb, vb = _att_project(lat, ctx, ctx_off, mod, norm_g, w_in, q_g, k_g, cos, sin, L)
    oa, ob = _attention(qa, ka, va, qb, kb, vb, sink, L)
    return _att_out(oa, ob, w_out, lat, ctx, ctx_off, mod, L)


def _rec_layer(hh, mod, norm_g, w_in, conv_w, a_log, dt_bias, out_g, gn_g, w_out, L):
    cos, sin = _rope_tables(L, hh.shape[1] - L, D_DK)
    qkv, z, qd, kd, vd, gd, gate = _rec_project(hh, mod, norm_g, w_in, a_log, dt_bias, cos, sin, L)
    qc, kc, vc = _rec_conv(qkv, conv_w, L)
    o_scan = _rec_scan(*_rec_intra(qc, kc, vc, qd, kd, vd, gate), L)
    return _rec_out(o_scan, z, gd, out_g, gn_g, w_out, hh, mod, L)


def kernel(x, c, ctx, c_ctx, mod_w, mod_b, norm1_g, norm2_g, att_w_in, att_q_norm, att_k_norm, att_sink, att_w_out,
           rec_w_in, rec_conv_w, rec_a_log, rec_dt_bias, rec_out_norm, rec_gn_g, rec_w_out,
           peer_w_q, peer_sub_keys, peer_u, peer_v, final_norm_g):
    L = x.shape[1]
    depth = mod_w.shape[0]
    mods = _modulation(c, c_ctx, mod_w, mod_b)
    lc = ctx.shape[1]
    hh = None
    for layer in range(depth):
        last = layer == depth - 1
        i = layer // 2
        assert layer % 2 == 0 or last, "the recurrent layer is only implemented as the last layer (no context output)"
        if layer % 2 == 0:
            lat, cx, ctx_off = (x, ctx, 0) if layer == 0 else (hh, hh, L // ROW_TILE)
            hh = _att_layer(lat, cx, ctx_off, lc, mods[layer], norm1_g[layer], att_w_in[i], att_q_norm[i],
                            att_k_norm[i], att_sink[i], att_w_out[i], L)
        else:
            hh = _rec_layer(hh, mods[layer], norm1_g[layer], rec_w_in[i], rec_conv_w[i], rec_a_log[i], rec_dt_bias[i],
                            rec_out_norm[i], rec_gn_g[i], rec_w_out[i], L)
        hh = _peer(hh, mods[layer], norm2_g[layer], peer_w_q[layer], peer_sub_keys[layer], peer_u, peer_v, layer,
                   final_norm_g, L, with_ctx=not last, final=last)
    return hh
```
